```python
import jax, jax.numpy as jnp
from jax import lax
import numpy as np

D_MODEL = 1024
BATCH = 2
SEQ = 8192
DEPTH = 4
DEC_BATCH = 128
DEC_SEQ = 4
PAST_LEN = 8192
PAGE_SIZE = 128

D_PLE = 256
GLA_HEADS = 4
GLA_DK = 64
GLA_DV = 128
GLA_RANK = 16
GLA_TAU = 16.0
RET_HEADS = 4
RET_DK = 64
RET_DV = 128
RET_THETA = 10000.0
SWA_Q_HEADS = 8
SWA_KV_HEADS = 2
SWA_HEAD_DIM = 64
SWA_GROUP = SWA_Q_HEADS // SWA_KV_HEADS
WINDOW = 128
ROPE_THETA = 500000.0
ROPE_DIM = SWA_HEAD_DIM // 4
D_FF = 2816
CONV_W = 3
CHUNK = 16
N_BRANCH = 3
EPS = 1e-6
NEG_INF = -1e30

GLA_W = GLA_HEADS * GLA_DV
RET_W = RET_HEADS * RET_DV
SWA_W = SWA_Q_HEADS * SWA_HEAD_DIM
D_MIX = GLA_W + RET_W + SWA_W
IN_SIZES = (GLA_HEADS * GLA_DK, GLA_HEADS * GLA_DK, GLA_W, GLA_W, GLA_RANK,
            RET_HEADS * RET_DK, RET_HEADS * RET_DK, RET_W, RET_W,
            SWA_W, SWA_KV_HEADS * SWA_HEAD_DIM, SWA_KV_HEADS * SWA_HEAD_DIM,
            N_BRANCH * D_MODEL)
D_IN = sum(IN_SIZES)

kernel_name = "hybrid_gla_retnet_swa_convffn_step"


def _rmsnorm(x, g):
    xf = x.astype(jnp.float32)
    y = xf * lax.rsqrt(jnp.mean(xf * xf, axis=-1, keepdims=True) + EPS)
    return (y * g.astype(jnp.float32)).astype(x.dtype)


def _rotary(x, pos, inv_freq):
    half = inv_freq.shape[0]
    ang = pos.astype(jnp.float32)[:, None] * inv_freq[None, :]
    cos = jnp.cos(ang)[:, None, :].astype(x.dtype)
    sin = jnp.sin(ang)[:, None, :].astype(x.dtype)
    x1, x2, rest = x[..., :half], x[..., half:2 * half], x[..., 2 * half:]
    return jnp.concatenate([x1 * cos - x2 * sin, x2 * cos + x1 * sin, rest], axis=-1)


def _heads(z, h):
    B, T, _ = z.shape
    return z.reshape(B, T, h, -1).transpose(0, 2, 1, 3)


def _merge_heads(o):
    B, H, T, d = o.shape
    return o.transpose(0, 2, 1, 3).reshape(B, T, H * d)


def _gated_recurrence(q, k, v, log_a, s0):
    out_dtype = q.dtype
    B, H, T, dk = q.shape
    c = CHUNK if T % CHUNK == 0 else T
    n = T // c
    f32 = jnp.float32

    def blk(z):
        return z.astype(f32).reshape(z.shape[:2] + (n, c) + z.shape[3:])

    qc, kc, vc, lc = blk(q), blk(k), blk(v), blk(log_a)
    b = jnp.cumsum(lc, axis=3)
    b_last = b[:, :, :, -1:, :]
    causal = jnp.tril(jnp.ones((c, c), bool))[..., None]
    diff = b[..., :, None, :] - b[..., None, :, :]
    dec = jnp.where(causal, jnp.exp(jnp.where(causal, diff, 0.0)), 0.0)
    if lc.shape[-1] == 1:
        scores = jnp.einsum("bhntd,bhnsd->bhnts", qc, kc) * dec[..., 0]
    else:
        scores = jnp.einsum("bhntd,bhnsd,bhntsd->bhnts", qc, kc, dec)
    o_intra = jnp.einsum("bhnts,bhnsv->bhntv", scores, vc)
    q_dec = qc * jnp.exp(b)
    k_dec = kc * jnp.exp(b_last - b)
    a_chunk = jnp.exp(b_last[:, :, :, 0])

    def step(s, inp):
        qd, kd, vv, ac = inp
        o = jnp.einsum("bhtd,bhdv->bhtv", qd, s)
        s = ac[..., None] * s + jnp.einsum("bhtd,bhtv->bhdv", kd, vv)
        return s, o

    xs = (jnp.moveaxis(q_dec, 2, 0), jnp.moveaxis(k_dec, 2, 0),
          jnp.moveaxis(vc, 2, 0), jnp.moveaxis(a_chunk, 2, 0))
    s_T, o_inter = lax.scan(step, s0.astype(f32), xs)
    o = o_intra + jnp.moveaxis(o_inter, 0, 2)
    return o.reshape(B, H, T, -1).astype(out_dtype), s_T.astype(s0.dtype)


def _sink_attention(q, k, v, mask, sink):
    s = jnp.einsum("bnqhgd,bnkhd->bnhgqk", q, k).astype(jnp.float32) * (SWA_HEAD_DIM ** -0.5)
    s = jnp.where(mask[None, :, None, None], s, NEG_INF)
    sk = sink.astype(jnp.float32)[None, None, :, :, None, None]
    m = jnp.maximum(jnp.max(s, axis=-1, keepdims=True), sk)
    p = jnp.exp(s - m)
    p = p / (jnp.sum(p, axis=-1, keepdims=True) + jnp.exp(sk - m))
    return jnp.einsum("bnhgqk,bnkhd->bnqhgd", p.astype(v.dtype), v)


def _swa_prompt(q, k, v, sink):
    B, T, Hq, D = q.shape
    W = WINDOW
    n = T // W
    qb = q.reshape(B, n, W, SWA_KV_HEADS, SWA_GROUP, D)

    def band(z):
        zp = jnp.concatenate([jnp.zeros_like(z[:, :W]), z[:, :T - W]], axis=1)
        return jnp.concatenate([zp.reshape(B, n, W, SWA_KV_HEADS, D),
                                z.reshape(B, n, W, SWA_KV_HEADS, D)], axis=2)

    i = jnp.arange(W)[:, None]
    m = jnp.arange(2 * W)[None, :]
    rel = i + W - m
    kpos = jnp.arange(n)[:, None, None] * W - W + m[None]
    mask = ((rel >= 0) & (rel < W))[None] & (kpos >= 0)
    o = _sink_attention(qb, band(k), band(v), mask, sink)
    return o.reshape(B, T, Hq * D)


def _swa_sample(q, k_all, v_all, n_past, sink):
    B, T, Hq, D = q.shape
    qb = q.reshape(B, 1, T, SWA_KV_HEADS, SWA_GROUP, D)
    krel = jnp.concatenate([jnp.arange(n_past) - n_past, jnp.arange(T)])
    rel = jnp.arange(T)[:, None] - krel[None, :]
    mask = ((rel >= 0) & (rel < WINDOW))[None]
    o = _sink_attention(qb, k_all[:, None], v_all[:, None], mask, sink)
    return o.reshape(B, T, Hq * D)


def _layer(x, pe, pos, s_gla, s_ret, buf_k, buf_v, buf_conv,
           norm_mix, w_in, w_gla_a, b_gla_a, gla_norm, ret_norm, swa_sink, w_branch, w_out,
           norm_ffn, w_ffn_in, conv_w, conv_b, w_ffn_out, norm_ple, w_ple_gate, w_ple_proj):
    B, T, _ = x.shape
    h = _rmsnorm(x, norm_mix)
    z = h @ w_in
    (gq, gk, gv, gr, ga, rq, rk, rv, rg, sq, sk, sv, zg) = jnp.split(
        z, np.cumsum(IN_SIZES)[:-1].tolist(), axis=-1)

    log_a = jax.nn.log_sigmoid((ga @ w_gla_a + b_gla_a).astype(jnp.float32)) / GLA_TAU
    o_a, s_gla_new = _gated_recurrence(_heads(gq, GLA_HEADS) * (GLA_DK ** -0.5), _heads(gk, GLA_HEADS),
                                       _heads(gv, GLA_HEADS), _heads(log_a, GLA_HEADS), s_gla)
    o_a = _merge_heads(_rmsnorm(o_a, gla_norm)) * jax.nn.silu(gr)

    ret_freq = 1.0 / (RET_THETA ** jnp.linspace(0.0, 1.0, RET_DK // 2))
    qb = _rotary(rq.reshape(B, T, RET_HEADS, RET_DK), pos, ret_freq).transpose(0, 2, 1, 3)
    kb = _rotary(rk.reshape(B, T, RET_HEADS, RET_DK), pos, ret_freq).transpose(0, 2, 1, 3) * (RET_DK ** -0.5)
    log_gamma = jnp.log1p(-jnp.exp2(-5.0 - jnp.arange(RET_HEADS, dtype=jnp.float32)))
    log_a_b = jnp.broadcast_to(log_gamma[None, :, None, None], (B, RET_HEADS, T, 1))
    o_b, s_ret_new = _gated_recurrence(qb, kb, _heads(rv, RET_HEADS), log_a_b, s_ret)
    o_b = _merge_heads(_rmsnorm(o_b, ret_norm[None, :, None, :])) * jax.nn.silu(rg)

    swa_freq = 1.0 / (ROPE_THETA ** (jnp.arange(0, ROPE_DIM, 2, dtype=jnp.float32) / ROPE_DIM))
    qc = _rotary(sq.reshape(B, T, SWA_Q_HEADS, SWA_HEAD_DIM), pos, swa_freq)
    kc = _rotary(sk.reshape(B, T, SWA_KV_HEADS, SWA_HEAD_DIM), pos, swa_freq)
    vc = sv.reshape(B, T, SWA_KV_HEADS, SWA_HEAD_DIM)
    sink = swa_sink.reshape(SWA_KV_HEADS, SWA_GROUP)
    if buf_k is None:
        o_c = _swa_prompt(qc, kc, vc, sink)
        k_all, v_all = kc, vc
    else:
        k_all = jnp.concatenate([buf_k.astype(kc.dtype), kc], axis=1)
        v_all = jnp.concatenate([buf_v.astype(vc.dtype), vc], axis=1)
        o_c = _swa_sample(qc, k_all, v_all, buf_k.shape[1], sink)
    k_keep = k_all[:, -WINDOW:]
    v_keep = v_all[:, -WINDOW:]

    gates = jax.nn.sigmoid(zg.reshape(B, T, N_BRANCH, D_MODEL))
    mixed = (gates[:, :, 0] * (o_a @ w_branch[:GLA_W])
             + gates[:, :, 1] * (o_b @ w_branch[GLA_W:GLA_W + RET_W])
             + gates[:, :, 2] * (o_c @ w_branch[GLA_W + RET_W:]))
    x = x + mixed @ w_out

    h = _rmsnorm(x, norm_ffn)
    a, bb = jnp.split(h @ w_ffn_in, 2, axis=-1)
    a_full = jnp.concatenate([buf_conv.astype(a.dtype), a], axis=1)
    c = conv_b + sum(conv_w[j] * a_full[:, j:j + T] for j in range(CONV_W))
    x = x + (jax.nn.gelu(c) * bb) @ w_ffn_out
    conv_new = a_full[:, -(CONV_W - 1):]

    h = _rmsnorm(x, norm_ple)
    x = x + jax.nn.sigmoid(h @ w_ple_gate) * (pe @ w_ple_proj)
    return x, (s_gla_new, s_ret_new, k_keep, v_keep, conv_new)


def setup_inputs(seed: int = 0) -> dict:
    key = jax.random.key(seed)
    ks = iter(jax.random.split(key, 40))
    f32 = jnp.float32

    def nrm(shape, scale):
        return jax.random.normal(next(ks), shape, f32) * scale

    def gain(shape):
        return 1.0 + 0.05 * jax.random.normal(next(ks), shape, f32)

    w_buf = min(WINDOW, PAST_LEN)
    return {
        "x_prompt": nrm((BATCH, SEQ, D_MODEL), 1.0),
        "x_sample": nrm((DEC_BATCH, DEC_SEQ, D_MODEL), 1.0),
        "state_gla": nrm((DEPTH, DEC_BATCH, GLA_HEADS, GLA_DK, GLA_DV), 0.1),
        "state_ret": nrm((DEPTH, DEC_BATCH, RET_HEADS, RET_DK, RET_DV), 0.1),
        "cache_swa_k": nrm((DEPTH, DEC_BATCH, w_buf, SWA_KV_HEADS, SWA_HEAD_DIM), 1.0),
        "cache_swa_v": nrm((DEPTH, DEC_BATCH, w_buf, SWA_KV_HEADS, SWA_HEAD_DIM), 1.0),
        "state_conv": nrm((DEPTH, DEC_BATCH, CONV_W - 1, D_FF), 1.0),
        "p_prompt": nrm((DEPTH, BATCH, SEQ, D_PLE), 1.0),
        "p_sample": nrm((DEPTH, DEC_BATCH, DEC_SEQ, D_PLE), 1.0),
        "norm_mix": gain((DEPTH, D_MODEL)),
        "w_in": nrm((DEPTH, D_MODEL, D_IN), D_MODEL ** -0.5),
        "w_gla_a": nrm((DEPTH, GLA_RANK, GLA_HEADS * GLA_DK), GLA_RANK ** -0.5),
        "b_gla_a": nrm((DEPTH, GLA_HEADS * GLA_DK), 0.1),
        "gla_norm": gain((DEPTH, GLA_DV)),
        "ret_norm": gain((DEPTH, RET_HEADS, RET_DV)),
        "swa_sink": nrm((DEPTH, SWA_Q_HEADS), 0.5),
        "w_branch": nrm((DEPTH, D_MIX, D_MODEL), D_MIX ** -0.5),
        "w_out": nrm((DEPTH, D_MODEL, D_MODEL), D_MODEL ** -0.5),
        "norm_ffn": gain((DEPTH, D_MODEL)),
        "w_ffn_in": nrm((DEPTH, D_MODEL, 2 * D_FF), D_MODEL ** -0.5),
        "conv_w": nrm((DEPTH, CONV_W, D_FF), CONV_W ** -0.5),
        "conv_b": nrm((DEPTH, D_FF), 0.02),
        "w_ffn_out": nrm((DEPTH, D_FF, D_MODEL), D_FF ** -0.5),
        "norm_ple": gain((DEPTH, D_MODEL)),
        "w_ple_gate": nrm((DEPTH, D_MODEL, D_MODEL), D_MODEL ** -0.5),
        "w_ple_proj": nrm((DEPTH, D_PLE, D_MODEL), D_PLE ** -0.5),
        "norm_final": gain((D_MODEL,)),
    }


def reference(x_prompt, x_sample, state_gla, state_ret, cache_swa_k, cache_swa_v, state_conv,
              p_prompt, p_sample, norm_mix, w_in, w_gla_a, b_gla_a, gla_norm, ret_norm, swa_sink,
              w_branch, w_out, norm_ffn, w_ffn_in, conv_w, conv_b, w_ffn_out,
              norm_ple, w_ple_gate, w_ple_proj, norm_final):
    pos_p = jnp.arange(x_prompt.shape[1])
    pos_s = PAST_LEN + jnp.arange(x_sample.shape[1])
    bp = x_prompt.shape[0]
    dt = x_prompt.dtype
    xp, xs = x_prompt, x_sample
    st_p = [[] for _ in range(5)]
    st_s = [[] for _ in range(5)]
    for i in range(DEPTH):
        w = (norm_mix[i], w_in[i], w_gla_a[i], b_gla_a[i], gla_norm[i], ret_norm[i], swa_sink[i],
             w_branch[i], w_out[i], norm_ffn[i], w_ffn_in[i], conv_w[i], conv_b[i], w_ffn_out[i],
             norm_ple[i], w_ple_gate[i], w_ple_proj[i])
        z_gla = jnp.zeros((bp, GLA_HEADS, GLA_DK, GLA_DV), dt)
        z_ret = jnp.zeros((bp, RET_HEADS, RET_DK, RET_DV), dt)
        z_conv = jnp.zeros((bp, CONV_W - 1, D_FF), dt)
        xp, sp = _layer(xp, p_prompt[i], pos_p, z_gla, z_ret, None, None, z_conv, *w)
        xs, ss = _layer(xs, p_sample[i], pos_s, state_gla[i], state_ret[i],
                        cache_swa_k[i], cache_swa_v[i], state_conv[i], *w)
        for j in range(5):
            st_p[j].append(sp[j])
            st_s[j].append(ss[j])
    y_prompt = _rmsnorm(xp, norm_final)
    y_sample = _rmsnorm(xs, norm_final)
    gla_p, gla_s = jnp.stack(st_p[0]), jnp.stack(st_s[0])
    ret_p, ret_s = jnp.stack(st_p[1]), jnp.stack(st_s[1])
    k_p, k_s = jnp.stack(st_p[2]), jnp.stack(st_s[2])
    v_p, v_s = jnp.stack(st_p[3]), jnp.stack(st_s[3])
    conv_p, conv_s = jnp.stack(st_p[4]), jnp.stack(st_s[4])
    return (y_prompt, y_sample, gla_p, gla_s, ret_p, ret_s, k_p, k_s, v_p, v_s, conv_p, conv_s)
```

```python
import functools
import math

import numpy as np
import jax
import jax.numpy as jnp
from jax import lax
from jax.experimental import pallas as pl
from jax.experimental.pallas import tpu as pltpu

F32 = jnp.float32
BF16 = jnp.bfloat16

D_MODEL = 1024
BATCH = 2
SEQ = 8192
DEPTH = 4
DEC_BATCH = 128
DEC_SEQ = 4
PAST_LEN = 8192
D_PLE = 256
GLA_HEADS = 4
GLA_DK = 64
GLA_DV = 128
GLA_RANK = 16
GLA_TAU = 16.0
RET_HEADS = 4
RET_DK = 64
RET_DV = 128
RET_THETA = 10000.0
SWA_Q_HEADS = 8
SWA_KV_HEADS = 2
SWA_HEAD_DIM = 64
SWA_GROUP = SWA_Q_HEADS // SWA_KV_HEADS
WINDOW = 128
ROPE_THETA = 500000.0
ROPE_DIM = SWA_HEAD_DIM // 4
D_FF = 2816
CONV_W = 3
N_BRANCH = 3
EPS = 1e-6
NEG_INF = -1e30

GLA_W = GLA_HEADS * GLA_DV
RET_W = RET_HEADS * RET_DV
SWA_W = SWA_Q_HEADS * SWA_HEAD_DIM
IN_SIZES = (GLA_HEADS * GLA_DK, GLA_HEADS * GLA_DK, GLA_W, GLA_W, GLA_RANK,
            RET_HEADS * RET_DK, RET_HEADS * RET_DK, RET_W, RET_W,
            SWA_W, SWA_KV_HEADS * SWA_HEAD_DIM, SWA_KV_HEADS * SWA_HEAD_DIM,
            N_BRANCH * D_MODEL)

LANES = 128
VMEM_LIMIT = 52 * 1024 * 1024

C_ZG = 0
C_GQ = 3072
C_GK = 3328
C_GV = 3584
C_GR = 4096
C_RQ = 4608
C_RK = 4864
C_RV = 5120
C_RG = 5632
C_SQ = 6144
C_SK = 6656
C_SV = 6784
C_LA = 6912
W_MAIN = 7040
Z_W = 7168

TT = 128
N_LEVELS = 7
SB = 16


def _dot(a, b):
    return jnp.dot(a, b, preferred_element_type=F32)


def _dot_nt(a, b):
    return lax.dot_general(a, b, (((1,), (1,)), ((), ())), preferred_element_type=F32)


def _dot_tn(a, b):
    return lax.dot_general(a, b, (((0,), (0,)), ((), ())), preferred_element_type=F32)


def _rms(x, g):
    return x * lax.rsqrt(jnp.mean(x * x, axis=-1, keepdims=True) + EPS) * g


def _layer_spec(shape, layer):
    nd = len(shape)
    return pl.BlockSpec((None,) + tuple(shape), lambda *_: (layer,) + (0,) * nd,
                        pipeline_mode=pl.Buffered(1))


def _const_spec(shape):
    nd = len(shape)
    return pl.BlockSpec(tuple(shape), lambda *_: (0,) * nd, pipeline_mode=pl.Buffered(1))


def _params(sem):
    return pltpu.CompilerParams(dimension_semantics=sem, vmem_limit_bytes=VMEM_LIMIT)


def _rope_block(x, cos, sin_s, first):
    half_mask, half = first
    xr = jnp.where(half_mask, pltpu.roll(x, LANES - half, 1), pltpu.roll(x, half, 1))
    return x * cos + xr * sin_s


def _in_kernel(x_ref, g_ref, w_ref, wa_ref, ba_ref, rc_ref, rs_ref, sc_ref, ss_ref, z_ref):
    hb = _rms(x_ref[...], g_ref[...]).astype(BF16)

    def mm(c0, c1):
        return _dot(hb, w_ref[:, c0:c1])

    lane = lax.broadcasted_iota(jnp.int32, (1, LANES), 1) % 64
    ret_first = (lane < RET_DK // 2, RET_DK // 2)
    swa_first = (lane < ROPE_DIM // 2, ROPE_DIM // 2)

    z_ref[:, C_ZG:C_ZG + 3072] = mm(C_ZG, C_ZG + 3072)
    z_ref[:, C_GQ:C_GK] = mm(C_GQ, C_GK) * (GLA_DK ** -0.5)
    z_ref[:, C_GK:C_RQ] = mm(C_GK, C_RQ)
    rc, rs = rc_ref[...], rs_ref[...]
    rq = mm(C_RQ, C_RK)
    rk = mm(C_RK, C_RV)
    for j in range(2):
        sl = slice(LANES * j, LANES * (j + 1))
        z_ref[:, C_RQ + LANES * j:C_RQ + LANES * (j + 1)] = _rope_block(rq[:, sl], rc, rs, ret_first)
        z_ref[:, C_RK + LANES * j:C_RK + LANES * (j + 1)] = (
            _rope_block(rk[:, sl], rc, rs, ret_first) * (RET_DK ** -0.5))
    z_ref[:, C_RV:C_SQ] = mm(C_RV, C_SQ)
    sc, ss = sc_ref[...], ss_ref[...]
    sqk = mm(C_SQ, C_SV)
    for j in range(5):
        sl = slice(LANES * j, LANES * (j + 1))
        z_ref[:, C_SQ + LANES * j:C_SQ + LANES * (j + 1)] = _rope_block(sqk[:, sl], sc, ss, swa_first)
    z_ref[:, C_SV:C_LA] = mm(C_SV, C_LA)
    ga = mm(C_LA, W_MAIN)
    xa = _dot(ga.astype(BF16), wa_ref[...]) + ba_ref[...]
    log_sig = jnp.minimum(xa, 0.0) - jnp.log1p(jnp.exp(-jnp.abs(xa)))
    z_ref[:, C_LA:Z_W] = log_sig * (1.0 / GLA_TAU)


def _in_proj(x, layer, g_mix, w_main, wa, ba, tabs, tm):
    n = x.shape[0]
    rc, rs, sc, ss = tabs
    nt = rc.shape[0] // tm
    tab = pl.BlockSpec((tm, LANES), lambda i: (i % nt, 0))
    return pl.pallas_call(
        _in_kernel,
        grid=(n // tm,),
        in_specs=[pl.BlockSpec((tm, D_MODEL), lambda i: (i, 0)),
                  _layer_spec((1, D_MODEL), layer),
                  _layer_spec((D_MODEL, W_MAIN), layer),
                  _layer_spec((LANES, 256), layer),
                  _layer_spec((1, 256), layer),
                  tab, tab, tab, tab],
        out_specs=pl.BlockSpec((tm, Z_W), lambda i: (i, 0)),
        out_shape=jax.ShapeDtypeStruct((n, Z_W), F32),
        compiler_params=_params(("arbitrary",)),
        name="in_proj",
    )(x, g_mix, w_main, wa, ba, rc, rs, sc, ss)


def _gla_constants():
    t = np.arange(TT)
    g = np.zeros((2 + N_LEVELS, TT, TT), np.float32)
    g[0] = (t[None, :] <= t[:, None])
    g[1] = (t[None, :] > t[:, None])
    m = np.zeros((1 + N_LEVELS, TT, TT), np.float32)
    m[0] = np.eye(TT)
    for lv in range(1, N_LEVELS + 1):
        bs, hf = 2 ** lv, 2 ** (lv - 1)
        bd = (t // bs) * bs + hf - 1
        upper = (t % bs) >= hf
        u = t[None, :]
        g[1 + lv] = np.where(upper[:, None], (u > bd[:, None]) & (u <= t[:, None]),
                             (u > t[:, None]) & (u <= bd[:, None]))
        same = (t[:, None] // bs) == (t[None, :] // bs)
        m[lv] = same & upper[:, None] & (~upper)[None, :]
    return g.reshape(-1, TT), m


def _gla_kernel(q_ref, k_ref, v_ref, gr_ref, la_ref, gm_ref, mk_ref, gn_ref, o_ref, st_ref):
    @pl.when(pl.program_id(1) == 0)
    def _():
        st_ref[...] = jnp.zeros_like(st_ref)

    la = la_ref[...]
    la_hi = la.astype(BF16)
    la_lo = (la - la_hi.astype(F32)).astype(BF16)
    gm = gm_ref[...]
    ex = jnp.exp(_dot(gm, la_hi) + _dot(gm, la_lo))
    q = q_ref[...]
    k = k_ref[...]
    e_b = ex[0:TT]
    a_row = e_b[TT - 1:TT, :]
    qd = (q * e_b).astype(BF16)
    kd = (k * ex[TT:2 * TT]).astype(BF16)
    qb = q.astype(BF16)
    kb = k.astype(BF16)
    ql = [(q * ex[(1 + lv) * TT:(2 + lv) * TT]).astype(BF16) for lv in range(1, N_LEVELS + 1)]
    kl = [(k * ex[(1 + lv) * TT:(2 + lv) * TT]).astype(BF16) for lv in range(1, N_LEVELS + 1)]
    eye = (lax.broadcasted_iota(jnp.int32, (GLA_DK, GLA_DK), 0)
           == lax.broadcasted_iota(jnp.int32, (GLA_DK, GLA_DK), 1)).astype(F32)
    gn = gn_ref[...]
    for h in range(GLA_HEADS):
        hs = slice(GLA_DK * h, GLA_DK * (h + 1))
        vs = slice(GLA_DV * h, GLA_DV * (h + 1))
        a = mk_ref[0] * _dot_nt(qb[:, hs], kb[:, hs])
        for lv in range(N_LEVELS):
            a = a + mk_ref[lv + 1] * _dot_nt(ql[lv][:, hs], kl[lv][:, hs])
        vh = v_ref[:, vs].astype(BF16)
        s = st_ref[0, h]
        o = _dot(a.astype(BF16), vh) + _dot(qd[:, hs], s.astype(BF16))
        a_col = jnp.sum(eye * a_row[:, hs], axis=1, keepdims=True)
        st_ref[0, h] = a_col * s + _dot_tn(kd[:, hs], vh)
        gate = gr_ref[:, vs]
        o_ref[:, vs] = _rms(o, gn[:, vs]) * (gate * jax.nn.sigmoid(gate))


def _gla_prompt(z, gm, mk, gn, nb, t):
    nt = t // TT

    def col(width, c0):
        return pl.BlockSpec((TT, width), lambda b, i: (b * nt + i, c0 // width))

    return pl.pallas_call(
        _gla_kernel,
        grid=(nb, nt),
        in_specs=[col(256, C_GQ), col(256, C_GK), col(512, C_GV), col(512, C_GR), col(256, C_LA),
                  _const_spec(gm.shape), _const_spec(mk.shape), _const_spec((1, GLA_W))],
        out_specs=[pl.BlockSpec((TT, GLA_W), lambda b, i: (b * nt + i, 0)),
                   pl.BlockSpec((1, GLA_HEADS, GLA_DK, GLA_DV), lambda b, i: (b, 0, 0, 0))],
        out_shape=[jax.ShapeDtypeStruct((nb * t, GLA_W), F32),
                   jax.ShapeDtypeStruct((nb, GLA_HEADS, GLA_DK, GLA_DV), F32)],
        compiler_params=_params(("arbitrary", "arbitrary")),
        name="gla_prompt",
    )(z, z, z, z, z, gm, mk, gn)


def _ret_gammas():
    return [1.0 - 2.0 ** (-5.0 - h) for h in range(RET_HEADS)]


def _ret_constants():
    t = np.arange(TT, dtype=np.float64)
    d = np.zeros((RET_HEADS, TT, TT), np.float64)
    qdec = np.zeros((TT, RET_HEADS * RET_DK), np.float64)
    kdec = np.zeros((TT, RET_HEADS * RET_DK), np.float64)
    for h, gam in enumerate(_ret_gammas()):
        diff = t[:, None] - t[None, :]
        d[h] = np.where(diff >= 0, gam ** np.maximum(diff, 0.0), 0.0)
        qdec[:, h * RET_DK:(h + 1) * RET_DK] = (gam ** (t + 1.0))[:, None]
        kdec[:, h * RET_DK:(h + 1) * RET_DK] = (gam ** (TT - 1.0 - t))[:, None]
    return d.astype(np.float32), qdec.astype(np.float32), kdec.astype(np.float32)


def _ret_kernel(q_ref, k_ref, v_ref, gr_ref, d_ref, qdec_ref, kdec_ref, gn_ref, o_ref, st_ref):
    @pl.when(pl.program_id(1) == 0)
    def _():
        st_ref[...] = jnp.zeros_like(st_ref)

    q = q_ref[...]
    k = k_ref[...]
    qb = q.astype(BF16)
    kb = k.astype(BF16)
    qd = (q * qdec_ref[...]).astype(BF16)
    kd = (k * kdec_ref[...]).astype(BF16)
    gn = gn_ref[...]
    for h, gam in enumerate(_ret_gammas()):
        hs = slice(RET_DK * h, RET_DK * (h + 1))
        vs = slice(RET_DV * h, RET_DV * (h + 1))
        a = d_ref[h] * _dot_nt(qb[:, hs], kb[:, hs])
        vh = v_ref[:, vs].astype(BF16)
        s = st_ref[0, h]
        o = _dot(a.astype(BF16), vh) + _dot(qd[:, hs], s.astype(BF16))
        st_ref[0, h] = (gam ** TT) * s + _dot_tn(kd[:, hs], vh)
        gate = gr_ref[:, vs]
        o_ref[:, vs] = _rms(o, gn[:, vs]) * (gate * jax.nn.sigmoid(gate))


def _ret_prompt(z, d, qdec, kdec, gn, nb, t):
    nt = t // TT

    def col(width, c0):
        return pl.BlockSpec((TT, width), lambda b, i: (b * nt + i, c0 // width))

    return pl.pallas_call(
        _ret_kernel,
        grid=(nb, nt),
        in_specs=[col(256, C_RQ), col(256, C_RK), col(512, C_RV), col(512, C_RG),
                  _const_spec(d.shape), _const_spec(qdec.shape), _const_spec(kdec.shape),
                  _const_spec((1, RET_W))],
        out_specs=[pl.BlockSpec((TT, RET_W), lambda b, i: (b * nt + i, 0)),
                   pl.BlockSpec((1, RET_HEADS, RET_DK, RET_DV), lambda b, i: (b, 0, 0, 0))],
        out_shape=[jax.ShapeDtypeStruct((nb * t, RET_W), F32),
                   jax.ShapeDtypeStruct((nb, RET_HEADS, RET_DK, RET_DV), F32)],
        compiler_params=_params(("arbitrary", "arbitrary")),
        name="ret_prompt",
    )(z, z, z, z, d, qdec, kdec, gn)


def _rec_sample_kernel(q_ref, k_ref, v_ref, gr_ref, la_ref, s0_ref, gn_ref, o_ref, s1_ref,
                       qd_scr, kd_scr, a_scr, oi_scr):
    nseq = DEC_BATCH
    row0 = pl.multiple_of(pl.program_id(0) * SB, SB)

    def rows(ref, t):
        return ref[pl.ds(t * nseq + row0, SB), :]

    b = []
    for t in range(DEC_SEQ):
        la_t = rows(la_ref, t)
        b.append(la_t if t == 0 else b[-1] + la_t)
    q = [rows(q_ref, t) for t in range(DEC_SEQ)]
    k = [rows(k_ref, t) for t in range(DEC_SEQ)]
    v = [rows(v_ref, t) for t in range(DEC_SEQ)]
    a_scr[...] = jnp.exp(b[-1])
    for t in range(DEC_SEQ):
        qd_scr[t * SB:(t + 1) * SB, :] = q[t] * jnp.exp(b[t])
        kd_scr[t * SB:(t + 1) * SB, :] = k[t] * jnp.exp(b[-1] - b[t])

    eye = (lax.broadcasted_iota(jnp.int32, (GLA_DK, GLA_DK), 0)
           == lax.broadcasted_iota(jnp.int32, (GLA_DK, GLA_DK), 1)).astype(F32)

    def col(row):
        return jnp.sum(eye * row, axis=1, keepdims=True)

    def per_seq(j, carry):
        a_row = a_scr[pl.ds(j, 1), :]
        kd_rows = [kd_scr[pl.ds(t * SB + j, 1), :] for t in range(DEC_SEQ)]
        qd_rows = [qd_scr[pl.ds(t * SB + j, 1), :] for t in range(DEC_SEQ)]
        v_rows = [v_ref[pl.ds(t * nseq + row0 + j, 1), :] for t in range(DEC_SEQ)]
        for h in range(GLA_HEADS):
            hs = slice(GLA_DK * h, GLA_DK * (h + 1))
            vs = slice(GLA_DV * h, GLA_DV * (h + 1))
            s0 = s0_ref[j, h]
            s1 = col(a_row[:, hs]) * s0
            for t in range(DEC_SEQ):
                s1 = s1 + col(kd_rows[t][:, hs]) * v_rows[t][:, vs]
                oi = jnp.sum(col(qd_rows[t][:, hs]) * s0, axis=0, keepdims=True)
                blk = slice(t * SB, (t + 1) * SB)
                oi_scr[blk, vs] = jnp.where(seq_iota == j, oi, oi_scr[blk, vs])
            s1_ref[j, h] = s1
        return carry

    seq_iota = lax.broadcasted_iota(jnp.int32, (SB, 1), 0)
    oi_scr[...] = jnp.zeros_like(oi_scr)
    lax.fori_loop(0, SB, per_seq, 0)

    gn = gn_ref[...]
    for t in range(DEC_SEQ):
        o = oi_scr[t * SB:(t + 1) * SB, :]
        for s in range(t + 1):
            p = q[t] * k[s] * jnp.exp(b[t] - b[s])
            parts = []
            for h in range(GLA_HEADS):
                hs = slice(GLA_DK * h, GLA_DK * (h + 1))
                vs = slice(GLA_DV * h, GLA_DV * (h + 1))
                parts.append(jnp.sum(p[:, hs], axis=1, keepdims=True) * v[s][:, vs])
            o = o + jnp.concatenate(parts, axis=1)
        gate = rows(gr_ref, t)
        outs = []
        for h in range(GLA_HEADS):
            vs = slice(GLA_DV * h, GLA_DV * (h + 1))
            outs.append(_rms(o[:, vs], gn[:, vs]))
        o_ref[pl.ds(t * nseq + row0, SB), :] = jnp.concatenate(outs, axis=1) * (gate * jax.nn.sigmoid(gate))


def _rec_sample(z, la, s0, layer, gn, cq, ck, cv, cg, c_la, name):
    n = z.shape[0]

    def whole(arr, width, c0):
        return pl.BlockSpec((arr.shape[0], width), lambda j: (0, c0 // width))

    st_in = pl.BlockSpec((None, SB, GLA_HEADS, GLA_DK, GLA_DV), lambda j: (layer, j, 0, 0, 0))
    st_out = pl.BlockSpec((SB, GLA_HEADS, GLA_DK, GLA_DV), lambda j: (j, 0, 0, 0))
    return pl.pallas_call(
        _rec_sample_kernel,
        grid=(DEC_BATCH // SB,),
        in_specs=[whole(z, 256, cq), whole(z, 256, ck), whole(z, 512, cv), whole(z, 512, cg),
                  whole(la, 256, c_la), st_in, _const_spec((1, GLA_W))],
        out_specs=[pl.BlockSpec((n, GLA_W), lambda j: (0, 0)), st_out],
        out_shape=[jax.ShapeDtypeStruct((n, GLA_W), F32),
                   jax.ShapeDtypeStruct(s0.shape[1:], F32)],
        scratch_shapes=[pltpu.VMEM((DEC_SEQ * SB, 256), F32), pltpu.VMEM((DEC_SEQ * SB, 256), F32),
                        pltpu.VMEM((SB, 256), F32), pltpu.VMEM((DEC_SEQ * SB, GLA_W), F32)],
        compiler_params=_params(("arbitrary",)),
        name=name,
    )(z, z, z, z, la, s0, gn)


def _swa_prompt_kernel(sink_ref, q_ref, k_ref, v_ref, o_ref, kp_scr, vp_scr):
    i = pl.program_id(1)

    @pl.when(i == 0)
    def _():
        kp_scr[...] = jnp.zeros_like(kp_scr)
        vp_scr[...] = jnp.zeros_like(vp_scr)

    kc = k_ref[...]
    vc = v_ref[...]
    kband = jnp.concatenate([kp_scr[...], kc], axis=0).astype(BF16)
    vband = jnp.concatenate([vp_scr[...], vc], axis=0).astype(BF16)
    r = lax.broadcasted_iota(jnp.int32, (WINDOW, 2 * WINDOW), 0)
    c = lax.broadcasted_iota(jnp.int32, (WINDOW, 2 * WINDOW), 1)
    first_col = jnp.where(i > 0, 0, WINDOW)
    valid = (c > r) & (c <= r + WINDOW) & (c >= first_col)
    outs = []
    for hq in range(SWA_Q_HEADS):
        g = hq // SWA_GROUP
        ds_ = slice(SWA_HEAD_DIM * g, SWA_HEAD_DIM * (g + 1))
        qh = q_ref[:, SWA_HEAD_DIM * hq:SWA_HEAD_DIM * (hq + 1)].astype(BF16)
        s = _dot_nt(qh, kband[:, ds_]) * (SWA_HEAD_DIM ** -0.5)
        s = jnp.where(valid, s, NEG_INF)
        sk = sink_ref[hq]
        m = jnp.maximum(jnp.max(s, axis=-1, keepdims=True), sk)
        p = jnp.exp(s - m)
        den = jnp.sum(p, axis=-1, keepdims=True) + jnp.exp(sk - m)
        outs.append(_dot(p.astype(BF16), vband[:, ds_]) / den)
    o_ref[...] = jnp.concatenate(outs, axis=1)
    kp_scr[...] = kc
    vp_scr[...] = vc


def _swa_prompt(z, sink, nb, t):
    nt = t // WINDOW

    def col(width, c0):
        return pl.BlockSpec((WINDOW, width), lambda b, i: (b * nt + i, c0 // width))

    return pl.pallas_call(
        _swa_prompt_kernel,
        grid=(nb, nt),
        in_specs=[pl.BlockSpec(memory_space=pltpu.SMEM),
                  col(512, C_SQ), col(128, C_SK), col(128, C_SV)],
        out_specs=pl.BlockSpec((WINDOW, SWA_W), lambda b, i: (b * nt + i, 0)),
        out_shape=jax.ShapeDtypeStruct((nb * t, SWA_W), F32),
        scratch_shapes=[pltpu.VMEM((WINDOW, 128), F32), pltpu.VMEM((WINDOW, 128), F32)],
        compiler_params=_params(("arbitrary", "arbitrary")),
        name="swa_prompt",
    )(sink, z, z, z)


def _swa_sample_masks():
    r = np.arange(DEC_SEQ * SB)
    rt, rb = r // SB, r % SB
    c = np.arange(SB * WINDOW)
    cb, cj = c // WINDOW, c % WINDOW
    m_cache = (rb[:, None] == cb[None, :]) & (cj[None, :] > rt[:, None])
    m_new = (rb[:, None] == rb[None, :]) & (rt[None, :] <= rt[:, None])
    return m_cache.astype(np.float32), m_new.astype(np.float32)


def _swa_sample_kernel(sink_ref, q_ref, k_ref, v_ref, kc_ref, vc_ref, mc_ref, mn_ref, o_ref):
    nseq = DEC_BATCH
    row0 = pl.multiple_of(pl.program_id(0) * SB, SB)

    def gather(ref):
        return jnp.concatenate([ref[pl.ds(t * nseq + row0, SB), :] for t in range(DEC_SEQ)], axis=0)

    q = gather(q_ref).astype(BF16)
    kn = gather(k_ref).astype(BF16)
    vn = gather(v_ref).astype(BF16)
    kc = kc_ref[...].reshape(SB * WINDOW, 128).astype(BF16)
    vc = vc_ref[...].reshape(SB * WINDOW, 128).astype(BF16)
    ok_c = mc_ref[...] > 0.0
    ok_n = mn_ref[...] > 0.0
    outs = []
    for hq in range(SWA_Q_HEADS):
        g = hq // SWA_GROUP
        ds_ = slice(SWA_HEAD_DIM * g, SWA_HEAD_DIM * (g + 1))
        qh = q[:, SWA_HEAD_DIM * hq:SWA_HEAD_DIM * (hq + 1)]
        s1 = jnp.where(ok_c, _dot_nt(qh, kc[:, ds_]) * (SWA_HEAD_DIM ** -0.5), NEG_INF)
        s2 = jnp.where(ok_n, _dot_nt(qh, kn[:, ds_]) * (SWA_HEAD_DIM ** -0.5), NEG_INF)
        sk = sink_ref[hq]
        m = jnp.maximum(jnp.maximum(jnp.max(s1, axis=-1, keepdims=True),
                                    jnp.max(s2, axis=-1, keepdims=True)), sk)
        p1 = jnp.exp(s1 - m)
        p2 = jnp.exp(s2 - m)
        den = jnp.sum(p1, axis=-1, keepdims=True) + jnp.sum(p2, axis=-1, keepdims=True) + jnp.exp(sk - m)
        outs.append((_dot(p1.astype(BF16), vc[:, ds_]) + _dot(p2.astype(BF16), vn[:, ds_])) / den)
    o = jnp.concatenate(outs, axis=1)
    for t in range(DEC_SEQ):
        o_ref[pl.ds(t * nseq + row0, SB), :] = o[t * SB:(t + 1) * SB, :]


def _swa_sample(z, sink, cache_k, cache_v, layer, mc, mn):
    n = z.shape[0]

    def whole(width, c0):
        return pl.BlockSpec((n, width), lambda j: (0, c0 // width))

    cache = pl.BlockSpec((None, SB, WINDOW, 128), lambda j: (layer, j, 0, 0))
    return pl.pallas_call(
        _swa_sample_kernel,
        grid=(DEC_BATCH // SB,),
        in_specs=[pl.BlockSpec(memory_space=pltpu.SMEM),
                  whole(512, C_SQ), whole(128, C_SK), whole(128, C_SV), cache, cache,
                  _const_spec(mc.shape), _const_spec(mn.shape)],
        out_specs=pl.BlockSpec((n, SWA_W), lambda j: (0, 0)),
        out_shape=jax.ShapeDtypeStruct((n, SWA_W), F32),
        compiler_params=_params(("arbitrary",)),
        name="swa_sample",
    )(sink, z, z, z, cache_k, cache_v, mc, mn)


def _merge_kernel(x_ref, oa_ref, ob_ref, oc_ref, zg_ref, wb_ref, wo_ref, y_ref):
    def branch(o_ref, r0, j):
        proj = _dot(o_ref[...].astype(BF16), wb_ref[r0:r0 + 512, :])
        return jax.nn.sigmoid(zg_ref[:, D_MODEL * j:D_MODEL * (j + 1)]) * proj

    mixed = branch(oa_ref, 0, 0) + branch(ob_ref, GLA_W, 1) + branch(oc_ref, GLA_W + RET_W, 2)
    y_ref[...] = x_ref[...] + _dot(mixed.astype(BF16), wo_ref[...])


def _merge(x, oa, ob, oc, z, layer, wb, wo, tm):
    n = x.shape[0]
    tok = lambda w: pl.BlockSpec((tm, w), lambda i: (i, 0))
    return pl.pallas_call(
        _merge_kernel,
        grid=(n // tm,),
        in_specs=[tok(D_MODEL), tok(512), tok(512), tok(512), tok(3 * D_MODEL),
                  _layer_spec((GLA_W + RET_W + SWA_W, D_MODEL), layer),
                  _layer_spec((D_MODEL, D_MODEL), layer)],
        out_specs=tok(D_MODEL),
        out_shape=jax.ShapeDtypeStruct((n, D_MODEL), F32),
        compiler_params=_params(("arbitrary",)),
        name="merge",
    )(x, oa, ob, oc, z, wb, wo)


FF_CHUNK = D_FF // 2


def _ffn_tail(x1, pe_ref, gp_ref, wpg_ref, wpp_ref, gf_ref, y_ref, final):
    hp = _rms(x1, gp_ref[...]).astype(BF16)
    x2 = x1 + jax.nn.sigmoid(_dot(hp, wpg_ref[...])) * _dot(pe_ref[...].astype(BF16), wpp_ref[...])
    y_ref[...] = _rms(x2, gf_ref[...]) if final else x2


def _ffn_prompt_kernel(x_ref, pe_ref, gn_ref, wi_ref, cw_ref, cb_ref, wd_ref, gp_ref, wpg_ref, wpp_ref,
                       gf_ref, y_ref, tail_ref, carry_scr, *, final):
    tm = x_ref.shape[0]

    @pl.when(pl.program_id(1) == 0)
    def _():
        carry_scr[...] = jnp.zeros_like(carry_scr)

    x = x_ref[...]
    hb = _rms(x, gn_ref[...]).astype(BF16)
    row = lax.broadcasted_iota(jnp.int32, (tm, 1), 0)
    acc = x
    for c0 in range(0, D_FF, FF_CHUNK):
        cs = slice(c0, c0 + FF_CHUNK)
        a = _dot(hb, wi_ref[:, c0:c0 + FF_CHUNK])
        bb = _dot(hb, wi_ref[:, D_FF + c0:D_FF + c0 + FF_CHUNK])
        p0 = carry_scr[6:7, cs]
        p1 = carry_scr[7:8, cs]
        a1 = jnp.where(row == 0, p1, pltpu.roll(a, 1, 0))
        a2 = jnp.where(row == 0, p0, jnp.where(row == 1, p1, pltpu.roll(a, 2, 0)))
        conv = cb_ref[:, cs] + cw_ref[0:1, cs] * a2 + cw_ref[1:2, cs] * a1 + cw_ref[2:3, cs] * a
        act = (jax.nn.gelu(conv) * bb).astype(BF16)
        acc = acc + _dot(act, wd_ref[c0:c0 + FF_CHUNK, :])
        carry_scr[:, cs] = a[tm - 8:tm, :]
        tail_ref[:, cs] = a[tm - 8:tm, :]
    _ffn_tail(acc, pe_ref, gp_ref, wpg_ref, wpp_ref, gf_ref, y_ref, final)


def _ffn_sample_kernel(x_ref, pe_ref, st_ref, gn_ref, wi_ref, cw_ref, cb_ref, wd_ref, gp_ref, wpg_ref,
                       wpp_ref, gf_ref, y_ref, tail_ref, *, final):
    ns = DEC_BATCH
    x = x_ref[...]
    hb = _rms(x, gn_ref[...]).astype(BF16)
    acc = x
    for c0 in range(0, D_FF, FF_CHUNK):
        cs = slice(c0, c0 + FF_CHUNK)
        a = _dot(hb, wi_ref[:, c0:c0 + FF_CHUNK])
        bb = _dot(hb, wi_ref[:, D_FF + c0:D_FF + c0 + FF_CHUNK])
        st0 = st_ref[0, :, cs]
        st1 = st_ref[1, :, cs]
        a1 = jnp.concatenate([st1, a[0:3 * ns]], axis=0)
        a2 = jnp.concatenate([st0, st1, a[0:2 * ns]], axis=0)
        conv = cb_ref[:, cs] + cw_ref[0:1, cs] * a2 + cw_ref[1:2, cs] * a1 + cw_ref[2:3, cs] * a
        act = (jax.nn.gelu(conv) * bb).astype(BF16)
        acc = acc + _dot(act, wd_ref[c0:c0 + FF_CHUNK, :])
        tail_ref[:, cs] = a[2 * ns:4 * ns, :]
    _ffn_tail(acc, pe_ref, gp_ref, wpg_ref, wpp_ref, gf_ref, y_ref, final)


def _ffn_weight_specs(layer):
    return [_layer_spec((1, D_MODEL), layer),
            _layer_spec((D_MODEL, 2 * D_FF), layer),
            _layer_spec((CONV_W, D_FF), layer),
            _layer_spec((1, D_FF), layer),
            _layer_spec((D_FF, D_MODEL), layer),
            _layer_spec((1, D_MODEL), layer),
            _layer_spec((D_MODEL, D_MODEL), layer),
            _layer_spec((D_PLE, D_MODEL), layer),
            _const_spec((1, D_MODEL))]


def _ffn_prompt(x, pe, layer, weights, nb, t, tm, final):
    nt = t // tm
    tok = lambda w: pl.BlockSpec((tm, w), lambda b, i: (b * nt + i, 0))
    return pl.pallas_call(
        functools.partial(_ffn_prompt_kernel, final=final),
        grid=(nb, nt),
        in_specs=[tok(D_MODEL),
                  pl.BlockSpec((None, tm, D_PLE), lambda b, i: (layer, b * nt + i, 0))]
        + _ffn_weight_specs(layer),
        out_specs=[tok(D_MODEL), pl.BlockSpec((8, D_FF), lambda b, i: (b, 0))],
        out_shape=[jax.ShapeDtypeStruct((nb * t, D_MODEL), F32),
                   jax.ShapeDtypeStruct((nb * 8, D_FF), F32)],
        scratch_shapes=[pltpu.VMEM((8, D_FF), F32)],
        compiler_params=_params(("arbitrary", "arbitrary")),
        name="ffn_prompt",
    )(x, pe, *weights)


def _ffn_sample(x, pe, st, layer, weights, final):
    n = x.shape[0]
    return pl.pallas_call(
        functools.partial(_ffn_sample_kernel, final=final),
        grid=(1,),
        in_specs=[pl.BlockSpec((n, D_MODEL), lambda i: (0, 0)),
                  pl.BlockSpec((None, n, D_PLE), lambda i: (layer, 0, 0)),
                  pl.BlockSpec((None, CONV_W - 1, DEC_BATCH, D_FF), lambda i: (layer, 0, 0, 0))]
        + _ffn_weight_specs(layer),
        out_specs=[pl.BlockSpec((n, D_MODEL), lambda i: (0, 0)),
                   pl.BlockSpec((n // 2, D_FF), lambda i: (0, 0))],
        out_shape=[jax.ShapeDtypeStruct((n, D_MODEL), F32),
                   jax.ShapeDtypeStruct((n // 2, D_FF), F32)],
        compiler_params=_params(("arbitrary",)),
        name="ffn_sample",
    )(x, pe, st, *weights)


def _rope_tables(pos, inv_freq):
    half = inv_freq.shape[0]
    ang = pos.astype(F32)[:, None] * inv_freq[None, :]
    c, s = jnp.cos(ang), jnp.sin(ang)
    rest = SWA_HEAD_DIM - 2 * half
    cos64 = jnp.concatenate([c, c, jnp.ones((pos.shape[0], rest), F32)], axis=1)
    sin64 = jnp.concatenate([-s, s, jnp.zeros((pos.shape[0], rest), F32)], axis=1)
    return jnp.concatenate([cos64, cos64], axis=1), jnp.concatenate([sin64, sin64], axis=1)


def _pack_w_in(w_in):
    offs = np.cumsum((0,) + IN_SIZES)
    gq, gk, gv, gr, ga, rq, rk, rv, rg, sq, sk, sv, zg = [
        w_in[:, :, offs[i]:offs[i + 1]] for i in range(len(IN_SIZES))]
    ga = jnp.pad(ga, ((0, 0), (0, 0), (0, LANES - GLA_RANK)))
    return jnp.concatenate([zg, gq, gk, gv, gr, rq, rk, rv, rg, sq, sk, sv, ga], axis=2).astype(BF16)


def kernel(x_prompt, x_sample, state_gla, state_ret, cache_swa_k, cache_swa_v, state_conv, p_prompt,
           p_sample, norm_mix, w_in, w_gla_a, b_gla_a, gla_norm, ret_norm, swa_sink, w_branch, w_out,
           norm_ffn, w_ffn_in, conv_w, conv_b, w_ffn_out, norm_ple, w_ple_gate, w_ple_proj, norm_final):
    nb, t, _ = x_prompt.shape
    ns, ts, _ = x_sample.shape
    n_s = ns * ts

    w_main = _pack_w_in(w_in)
    wa = jnp.pad(w_gla_a, ((0, 0), (0, LANES - GLA_RANK), (0, 0))).astype(BF16)
    ba = b_gla_a[:, None, :]
    wb = w_branch.astype(BF16)
    wo = w_out.astype(BF16)
    ffn_weights = (norm_ffn[:, None, :], w_ffn_in.astype(BF16), conv_w, conv_b[:, None, :],
                   w_ffn_out.astype(BF16), norm_ple[:, None, :], w_ple_gate.astype(BF16),
                   w_ple_proj.astype(BF16), norm_final[None, :])
    g_mix = norm_mix[:, None, :]
    gn_gla = jnp.tile(gla_norm, (1, GLA_HEADS))[:, None, :]
    gn_ret = ret_norm.reshape(DEPTH, 1, RET_W)

    ret_freq = 1.0 / (RET_THETA ** jnp.linspace(0.0, 1.0, RET_DK // 2))
    swa_freq = 1.0 / (ROPE_THETA ** (jnp.arange(0, ROPE_DIM, 2, dtype=F32) / ROPE_DIM))
    pos_p = jnp.arange(t)
    pos_s = PAST_LEN + jnp.arange(n_s) // ns
    tabs_p = _rope_tables(pos_p, ret_freq) + _rope_tables(pos_p, swa_freq)
    tabs_s = _rope_tables(pos_s, ret_freq) + _rope_tables(pos_s, swa_freq)

    gm_np, mk_np = _gla_constants()
    gm = jnp.asarray(gm_np, BF16)
    mk = jnp.asarray(mk_np)
    d_np, qdec_np, kdec_np = _ret_constants()
    ret_d, ret_qdec, ret_kdec = jnp.asarray(d_np), jnp.asarray(qdec_np), jnp.asarray(kdec_np)
    log_gamma = jnp.log1p(-jnp.exp2(-5.0 - jnp.arange(RET_HEADS, dtype=F32)))
    la_ret = jnp.broadcast_to(jnp.repeat(log_gamma, RET_DK)[None, :], (n_s, RET_HEADS * RET_DK))
    mc_np, mn_np = _swa_sample_masks()
    swa_mc, swa_mn = jnp.asarray(mc_np), jnp.asarray(mn_np)

    xp = x_prompt.reshape(nb * t, D_MODEL)
    xs = x_sample.transpose(1, 0, 2).reshape(n_s, D_MODEL)
    pe_p = p_prompt.reshape(DEPTH, nb * t, D_PLE)
    pe_s = p_sample.transpose(0, 2, 1, 3).reshape(DEPTH, n_s, D_PLE)
    conv_st = state_conv.transpose(0, 2, 1, 3)
    ck = cache_swa_k.reshape(DEPTH, ns, WINDOW, SWA_KV_HEADS * SWA_HEAD_DIM)
    cv = cache_swa_v.reshape(DEPTH, ns, WINDOW, SWA_KV_HEADS * SWA_HEAD_DIM)

    outs = {k_: [] for k_ in ("gla_p", "gla_s", "ret_p", "ret_s", "k_p", "k_s", "v_p", "v_s", "conv_p", "conv_s")}
    for l in range(DEPTH):
        final = l == DEPTH - 1
        sink = swa_sink[l]

        z = _in_proj(xp, l, g_mix, w_main, wa, ba, tabs_p, 256)
        oa, st_a = _gla_prompt(z, gm, mk, gn_gla[l], nb, t)
        ob, st_b = _ret_prompt(z, ret_d, ret_qdec, ret_kdec, gn_ret[l], nb, t)
        oc = _swa_prompt(z, sink, nb, t)
        xp = _merge(xp, oa, ob, oc, z, l, wb, wo, 512)
        xp, tail = _ffn_prompt(xp, pe_p, l, ffn_weights, nb, t, 256, final)
        z3 = z.reshape(nb, t, Z_W)
        outs["gla_p"].append(st_a)
        outs["ret_p"].append(st_b)
        outs["k_p"].append(z3[:, t - WINDOW:, C_SK:C_SV].reshape(nb, WINDOW, SWA_KV_HEADS, SWA_HEAD_DIM))
        outs["v_p"].append(z3[:, t - WINDOW:, C_SV:C_LA].reshape(nb, WINDOW, SWA_KV_HEADS, SWA_HEAD_DIM))
        outs["conv_p"].append(tail.reshape(nb, 8, D_FF)[:, 8 - (CONV_W - 1):])

        zs = _in_proj(xs, l, g_mix, w_main, wa, ba, tabs_s, 256)
        oa, st_a = _rec_sample(zs, zs, state_gla, l, gn_gla[l], C_GQ, C_GK, C_GV, C_GR, C_LA, "gla_sample")
        ob, st_b = _rec_sample(zs, la_ret, state_ret, l, gn_ret[l], C_RQ, C_RK, C_RV, C_RG, 0, "ret_sample")
        oc = _swa_sample(zs, sink, ck, cv, l, swa_mc, swa_mn)
        xs = _merge(xs, oa, ob, oc, zs, l, wb, wo, 512)
        xs, tail_s = _ffn_sample(xs, pe_s, conv_st, l, ffn_weights, final)
        k_new = zs[:, C_SK:C_SV].reshape(ts, ns, SWA_KV_HEADS, SWA_HEAD_DIM).transpose(1, 0, 2, 3)
        v_new = zs[:, C_SV:C_LA].reshape(ts, ns, SWA_KV_HEADS, SWA_HEAD_DIM).transpose(1, 0, 2, 3)
        outs["gla_s"].append(st_a)
        outs["ret_s"].append(st_b)
        outs["k_s"].append(jnp.concatenate([cache_swa_k[l][:, ts:], k_new], axis=1))
        outs["v_s"].append(jnp.concatenate([cache_swa_v[l][:, ts:], v_new], axis=1))
        outs["conv_s"].append(tail_s.reshape(CONV_W - 1, ns, D_FF).transpose(1, 0, 2))

    y_prompt = xp.reshape(nb, t, D_MODEL)
    y_sample = xs.reshape(ts, ns, D_MODEL).transpose(1, 0, 2)
    st = {k_: jnp.stack(v_) for k_, v_ in outs.items()}
    return (y_prompt, y_sample, st["gla_p"], st["gla_s"], st["ret_p"], st["ret_s"],
            st["k_p"], st["k_s"], st["v_p"], st["v_s"], st["conv_p"], st["conv_s"])
```

```python
import functools
import math

import numpy as np
import jax
import jax.numpy as jnp
from jax import lax
from jax.experimental import pallas as pl
from jax.experimental.pallas import tpu as pltpu

F32 = jnp.float32
BF16 = jnp.bfloat16

D_MODEL = 1024
BATCH = 2
SEQ = 8192
DEPTH = 4
DEC_BATCH = 128
DEC_SEQ = 4
PAST_LEN = 8192
D_PLE = 256
GLA_HEADS = 4
GLA_DK = 64
GLA_DV = 128
GLA_RANK = 16
GLA_TAU = 16.0
RET_HEADS = 4
RET_DK = 64
RET_DV = 128
RET_THETA = 10000.0
SWA_Q_HEADS = 8
SWA_KV_HEADS = 2
SWA_HEAD_DIM = 64
SWA_GROUP = SWA_Q_HEADS // SWA_KV_HEADS
WINDOW = 128
ROPE_THETA = 500000.0
ROPE_DIM = SWA_HEAD_DIM // 4
D_FF = 2816
CONV_W = 3
N_BRANCH = 3
EPS = 1e-6
NEG_INF = -1e30

GLA_W = GLA_HEADS * GLA_DV
RET_W = RET_HEADS * RET_DV
SWA_W = SWA_Q_HEADS * SWA_HEAD_DIM
IN_SIZES = (GLA_HEADS * GLA_DK, GLA_HEADS * GLA_DK, GLA_W, GLA_W, GLA_RANK,
            RET_HEADS * RET_DK, RET_HEADS * RET_DK, RET_W, RET_W,
            SWA_W, SWA_KV_HEADS * SWA_HEAD_DIM, SWA_KV_HEADS * SWA_HEAD_DIM,
            N_BRANCH * D_MODEL)

LANES = 128
VMEM_LIMIT = 52 * 1024 * 1024

C_ZG = 0
C_GQ = 3072
C_GK = 3328
C_GV = 3584
C_GR = 4096
C_RQ = 4608
C_RK = 4864
C_RV = 5120
C_RG = 5632
C_SQ = 6144
C_SK = 6656
C_SV = 6784
C_LA = 6912
W_MAIN = 7040
Z_W = 7168

TT = 128
N_LEVELS = 7
SB = 16


def _dot(a, b):
    return jnp.dot(a, b, preferred_element_type=F32)


def _dot_nt(a, b):
    return lax.dot_general(a, b, (((1,), (1,)), ((), ())), preferred_element_type=F32)


def _dot_tn(a, b):
    return lax.dot_general(a, b, (((0,), (0,)), ((), ())), preferred_element_type=F32)


def _rms(x, g):
    return x * lax.rsqrt(jnp.mean(x * x, axis=-1, keepdims=True) + EPS) * g


def _layer_spec(shape, layer):
    nd = len(shape)
    return pl.BlockSpec((None,) + tuple(shape), lambda *_: (layer,) + (0,) * nd,
                        pipeline_mode=pl.Buffered(1))


def _const_spec(shape):
    nd = len(shape)
    return pl.BlockSpec(tuple(shape), lambda *_: (0,) * nd, pipeline_mode=pl.Buffered(1))


def _params(sem):
    return pltpu.CompilerParams(dimension_semantics=sem, vmem_limit_bytes=VMEM_LIMIT)


def _rope_block(x, cos, sin_s, first):
    half_mask, half = first
    xr = jnp.where(half_mask, pltpu.roll(x, LANES - half, 1), pltpu.roll(x, half, 1))
    return x * cos + xr * sin_s


def _in_kernel(x_ref, g_ref, w_ref, wa_ref, ba_ref, rc_ref, rs_ref, sc_ref, ss_ref, z_ref):
    hb = _rms(x_ref[...], g_ref[...]).astype(BF16)

    def mm(c0, c1):
        return _dot(hb, w_ref[:, c0:c1])

    lane = lax.broadcasted_iota(jnp.int32, (1, LANES), 1) % 64
    ret_first = (lane < RET_DK // 2, RET_DK // 2)
    swa_first = (lane < ROPE_DIM // 2, ROPE_DIM // 2)

    z_ref[:, C_ZG:C_ZG + 3072] = mm(C_ZG, C_ZG + 3072)
    z_ref[:, C_GQ:C_GK] = mm(C_GQ, C_GK) * (GLA_DK ** -0.5)
    z_ref[:, C_GK:C_RQ] = mm(C_GK, C_RQ)
    rc, rs = rc_ref[...], rs_ref[...]
    rq = mm(C_RQ, C_RK)
    rk = mm(C_RK, C_RV)
    for j in range(2):
        sl = slice(LANES * j, LANES * (j + 1))
        z_ref[:, C_RQ + LANES * j:C_RQ + LANES * (j + 1)] = _rope_block(rq[:, sl], rc, rs, ret_first)
        z_ref[:, C_RK + LANES * j:C_RK + LANES * (j + 1)] = (
            _rope_block(rk[:, sl], rc, rs, ret_first) * (RET_DK ** -0.5))
    z_ref[:, C_RV:C_SQ] = mm(C_RV, C_SQ)
    sc, ss = sc_ref[...], ss_ref[...]
    sqk = mm(C_SQ, C_SV)
    for j in range(5):
        sl = slice(LANES * j, LANES * (j + 1))
        z_ref[:, C_SQ + LANES * j:C_SQ + LANES * (j + 1)] = _rope_block(sqk[:, sl], sc, ss, swa_first)
    z_ref[:, C_SV:C_LA] = mm(C_SV, C_LA)
    ga = mm(C_LA, W_MAIN)
    xa = _dot(ga.astype(BF16), wa_ref[...]) + ba_ref[...]
    log_sig = jnp.minimum(xa, 0.0) - jnp.log1p(jnp.exp(-jnp.abs(xa)))
    z_ref[:, C_LA:Z_W] = log_sig * (1.0 / GLA_TAU)


def _in_proj(x, layer, g_mix, w_main, wa, ba, tabs, tm):
    n = x.shape[0]
    rc, rs, sc, ss = tabs
    nt = rc.shape[0] // tm
    tab = pl.BlockSpec((tm, LANES), lambda i: (i % nt, 0))
    return pl.pallas_call(
        _in_kernel,
        grid=(n // tm,),
        in_specs=[pl.BlockSpec((tm, D_MODEL), lambda i: (i, 0)),
                  _layer_spec((1, D_MODEL), layer),
                  _layer_spec((D_MODEL, W_MAIN), layer),
                  _layer_spec((LANES, 256), layer),
                  _layer_spec((1, 256), layer),
                  tab, tab, tab, tab],
        out_specs=pl.BlockSpec((tm, Z_W), lambda i: (i, 0)),
        out_shape=jax.ShapeDtypeStruct((n, Z_W), F32),
        compiler_params=_params(("arbitrary",)),
        name="in_proj",
    )(x, g_mix, w_main, wa, ba, rc, rs, sc, ss)


def _gla_constants():
    t = np.arange(TT)
    g = np.zeros((2 + N_LEVELS, TT, TT), np.float32)
    g[0] = (t[None, :] <= t[:, None])
    g[1] = (t[None, :] > t[:, None])
    m = np.zeros((1 + N_LEVELS, TT, TT), np.float32)
    m[0] = np.eye(TT)
    for lv in range(1, N_LEVELS + 1):
        bs, hf = 2 ** lv, 2 ** (lv - 1)
        bd = (t // bs) * bs + hf - 1
        upper = (t % bs) >= hf
        u = t[None, :]
        g[1 + lv] = np.where(upper[:, None], (u > bd[:, None]) & (u <= t[:, None]),
                             (u > t[:, None]) & (u <= bd[:, None]))
        same = (t[:, None] // bs) == (t[None, :] // bs)
        m[lv] = same & upper[:, None] & (~upper)[None, :]
    return g.reshape(-1, TT), m


def _gla_tile(z_ref, gm_ref, mk_ref, gn_ref, st_ref):
    la = z_ref[:, C_LA:Z_W]
    la_hi = la.astype(BF16)
    la_lo = (la - la_hi.astype(F32)).astype(BF16)
    gm = gm_ref[...]
    ex = jnp.exp(_dot(gm, la_hi) + _dot(gm, la_lo))
    q = z_ref[:, C_GQ:C_GK]
    k = z_ref[:, C_GK:C_GV]
    e_b = ex[0:TT]
    a_row = e_b[TT - 1:TT, :]
    qd = (q * e_b).astype(BF16)
    kd = (k * ex[TT:2 * TT]).astype(BF16)
    qb = q.astype(BF16)
    kb = k.astype(BF16)
    ql = [(q * ex[(1 + lv) * TT:(2 + lv) * TT]).astype(BF16) for lv in range(1, N_LEVELS + 1)]
    kl = [(k * ex[(1 + lv) * TT:(2 + lv) * TT]).astype(BF16) for lv in range(1, N_LEVELS + 1)]
    eye = (lax.broadcasted_iota(jnp.int32, (GLA_DK, GLA_DK), 0)
           == lax.broadcasted_iota(jnp.int32, (GLA_DK, GLA_DK), 1)).astype(F32)
    gn = gn_ref[...]
    outs = []
    for h in range(GLA_HEADS):
        hs = slice(GLA_DK * h, GLA_DK * (h + 1))
        vs = slice(GLA_DV * h, GLA_DV * (h + 1))
        a = mk_ref[0] * _dot_nt(qb[:, hs], kb[:, hs])
        for lv in range(N_LEVELS):
            a = a + mk_ref[lv + 1] * _dot_nt(ql[lv][:, hs], kl[lv][:, hs])
        vh = z_ref[:, C_GV + GLA_DV * h:C_GV + GLA_DV * (h + 1)].astype(BF16)
        s = st_ref[h]
        o = _dot(a.astype(BF16), vh) + _dot(qd[:, hs], s.astype(BF16))
        a_col = jnp.sum(eye * a_row[:, hs], axis=1, keepdims=True)
        st_ref[h] = a_col * s + _dot_tn(kd[:, hs], vh)
        gate = z_ref[:, C_GR + GLA_DV * h:C_GR + GLA_DV * (h + 1)]
        outs.append(_rms(o, gn[:, vs]) * (gate * jax.nn.sigmoid(gate)))
    return outs


def _ret_gammas():
    return [1.0 - 2.0 ** (-5.0 - h) for h in range(RET_HEADS)]


def _ret_constants():
    t = np.arange(TT, dtype=np.float64)
    d = np.zeros((RET_HEADS, TT, TT), np.float64)
    qdec = np.zeros((TT, RET_HEADS * RET_DK), np.float64)
    kdec = np.zeros((TT, RET_HEADS * RET_DK), np.float64)
    for h, gam in enumerate(_ret_gammas()):
        diff = t[:, None] - t[None, :]
        d[h] = np.where(diff >= 0, gam ** np.maximum(diff, 0.0), 0.0)
        qdec[:, h * RET_DK:(h + 1) * RET_DK] = (gam ** (t + 1.0))[:, None]
        kdec[:, h * RET_DK:(h + 1) * RET_DK] = (gam ** (TT - 1.0 - t))[:, None]
    return d.astype(np.float32), qdec.astype(np.float32), kdec.astype(np.float32)


def _ret_tile(z_ref, d_ref, qdec_ref, kdec_ref, gn_ref, st_ref):
    q = z_ref[:, C_RQ:C_RK]
    k = z_ref[:, C_RK:C_RV]
    qb = q.astype(BF16)
    kb = k.astype(BF16)
    qd = (q * qdec_ref[...]).astype(BF16)
    kd = (k * kdec_ref[...]).astype(BF16)
    gn = gn_ref[...]
    outs = []
    for h, gam in enumerate(_ret_gammas()):
        hs = slice(RET_DK * h, RET_DK * (h + 1))
        vs = slice(RET_DV * h, RET_DV * (h + 1))
        a = d_ref[h] * _dot_nt(qb[:, hs], kb[:, hs])
        vh = z_ref[:, C_RV + RET_DV * h:C_RV + RET_DV * (h + 1)].astype(BF16)
        s = st_ref[h]
        o = _dot(a.astype(BF16), vh) + _dot(qd[:, hs], s.astype(BF16))
        st_ref[h] = (gam ** TT) * s + _dot_tn(kd[:, hs], vh)
        gate = z_ref[:, C_RG + RET_DV * h:C_RG + RET_DV * (h + 1)]
        outs.append(_rms(o, gn[:, vs]) * (gate * jax.nn.sigmoid(gate)))
    return outs


def _rec_sample_constants():
    c = np.arange(SB * GLA_DK)
    rep = np.arange(GLA_DK)[:, None] == (c[None, :] % GLA_DK)
    bd = (np.arange(DEC_SEQ * SB)[:, None] % SB) == (c[None, :] // GLA_DK)
    return rep.astype(np.float32), bd.astype(np.float32)


def _rec_sample_kernel(q_ref, k_ref, v_ref, gr_ref, la_ref, s0_ref, gn_ref, rep_ref, bd_ref, o_ref, s1_ref):
    nseq = DEC_BATCH
    row0 = pl.multiple_of(pl.program_id(0) * SB, SB)

    def rows(ref, t):
        return ref[pl.ds(t * nseq + row0, SB), :]

    b = []
    for t in range(DEC_SEQ):
        la_t = rows(la_ref, t)
        b.append(la_t if t == 0 else b[-1] + la_t)
    q = [rows(q_ref, t) for t in range(DEC_SEQ)]
    k = [rows(k_ref, t) for t in range(DEC_SEQ)]
    v = [rows(v_ref, t) for t in range(DEC_SEQ)]
    qd = jnp.concatenate([q[t] * jnp.exp(b[t]) for t in range(DEC_SEQ)], axis=0).astype(BF16)
    kd = jnp.concatenate([k[t] * jnp.exp(b[-1] - b[t]) for t in range(DEC_SEQ)], axis=0).astype(BF16)
    vv = jnp.concatenate(v, axis=0).astype(BF16)
    a = jnp.exp(b[-1])
    a_hi = a.astype(BF16)
    a_r = a - a_hi.astype(F32)
    a_mid = a_r.astype(BF16)
    a_lo = (a_r - a_mid.astype(F32)).astype(BF16)
    rep = rep_ref[...]
    bd = bd_ref[...]
    ones = jnp.ones((SB, GLA_DV), BF16)

    def expand(x, mask):
        return (_dot(x, rep) * mask).astype(BF16)

    oi = []
    for h in range(GLA_HEADS):
        hs = slice(GLA_DK * h, GLA_DK * (h + 1))
        vs = slice(GLA_DV * h, GLA_DV * (h + 1))
        s0 = s0_ref[:, h].reshape(SB * GLA_DK, GLA_DV)
        oi.append(_dot(expand(qd[:, hs], bd), s0.astype(BF16)))
        a_col = (_dot_tn(expand(a_hi[:, hs], bd[0:SB]), ones) + _dot_tn(expand(a_mid[:, hs], bd[0:SB]), ones)
                 + _dot_tn(expand(a_lo[:, hs], bd[0:SB]), ones))
        s1 = a_col * s0 + _dot_tn(expand(kd[:, hs], bd), vv[:, vs])
        s1_ref[:, h] = s1.reshape(SB, GLA_DK, GLA_DV)
    oi = jnp.concatenate(oi, axis=1)

    gn = gn_ref[...]
    for t in range(DEC_SEQ):
        o = oi[t * SB:(t + 1) * SB, :]
        for s in range(t + 1):
            p = q[t] * k[s] * jnp.exp(b[t] - b[s])
            parts = []
            for h in range(GLA_HEADS):
                hs = slice(GLA_DK * h, GLA_DK * (h + 1))
                vs = slice(GLA_DV * h, GLA_DV * (h + 1))
                parts.append(jnp.sum(p[:, hs], axis=1, keepdims=True) * v[s][:, vs])
            o = o + jnp.concatenate(parts, axis=1)
        gate = rows(gr_ref, t)
        outs = []
        for h in range(GLA_HEADS):
            vs = slice(GLA_DV * h, GLA_DV * (h + 1))
            outs.append(_rms(o[:, vs], gn[:, vs]))
        o_ref[pl.ds(t * nseq + row0, SB), :] = jnp.concatenate(outs, axis=1) * (gate * jax.nn.sigmoid(gate))


def _rec_sample(z, la, s0, layer, gn, rep, bd, cq, ck, cv, cg, c_la, name):
    n = z.shape[0]

    def whole(arr, width, c0):
        return pl.BlockSpec((arr.shape[0], width), lambda j: (0, c0 // width))

    st_in = pl.BlockSpec((None, SB, GLA_HEADS, GLA_DK, GLA_DV), lambda j: (layer, j, 0, 0, 0))
    st_out = pl.BlockSpec((SB, GLA_HEADS, GLA_DK, GLA_DV), lambda j: (j, 0, 0, 0))
    return pl.pallas_call(
        _rec_sample_kernel,
        grid=(DEC_BATCH // SB,),
        in_specs=[whole(z, 256, cq), whole(z, 256, ck), whole(z, 512, cv), whole(z, 512, cg),
                  whole(la, 256, c_la), st_in, _const_spec((1, GLA_W)),
                  _const_spec(rep.shape), _const_spec(bd.shape)],
        out_specs=[pl.BlockSpec((n, GLA_W), lambda j: (0, 0)), st_out],
        out_shape=[jax.ShapeDtypeStruct((n, GLA_W), F32),
                   jax.ShapeDtypeStruct(s0.shape[1:], F32)],
        compiler_params=_params(("arbitrary",)),
        name=name,
    )(z, z, z, z, la, s0, gn, rep, bd)


def _swa_tile(z_ref, sink_ref, kp_ref, vp_ref, first_col):
    kc = z_ref[:, C_SK:C_SV].astype(BF16)
    vc = z_ref[:, C_SV:C_LA].astype(BF16)
    kband = jnp.concatenate([kp_ref[...], kc], axis=0)
    vband = jnp.concatenate([vp_ref[...], vc], axis=0)
    rows = SWA_GROUP * WINDOW
    r = lax.broadcasted_iota(jnp.int32, (rows, 2 * WINDOW), 0) & (WINDOW - 1)
    c = lax.broadcasted_iota(jnp.int32, (rows, 2 * WINDOW), 1)
    valid = (c > r) & (c <= r + WINDOW) & (c >= first_col)
    outs = []
    for g in range(SWA_KV_HEADS):
        ds_ = slice(SWA_HEAD_DIM * g, SWA_HEAD_DIM * (g + 1))
        heads = range(SWA_GROUP * g, SWA_GROUP * (g + 1))
        q = jnp.concatenate(
            [z_ref[:, C_SQ + SWA_HEAD_DIM * hq:C_SQ + SWA_HEAD_DIM * (hq + 1)] for hq in heads], axis=0)
        sk = jnp.concatenate([jnp.full((WINDOW, 1), sink_ref[hq], F32) for hq in heads], axis=0)
        s = _dot_nt(q.astype(BF16), kband[:, ds_]) * (SWA_HEAD_DIM ** -0.5)
        s = jnp.where(valid, s, NEG_INF)
        m = jnp.maximum(jnp.max(s, axis=-1, keepdims=True), sk)
        p = jnp.exp(s - m)
        den = jnp.sum(p, axis=-1, keepdims=True) + jnp.exp(sk - m)
        o = _dot(p.astype(BF16), vband[:, ds_]) / den
        outs += [o[WINDOW * j:WINDOW * (j + 1)] for j in range(SWA_GROUP)]
    kp_ref[...] = kc
    vp_ref[...] = vc
    return jnp.concatenate(outs, axis=1)


def _mix_kernel(sink_ref, x_ref, z_ref, gm_ref, mk_ref, d_ref, qdec_ref, kdec_ref, gna_ref, gnb_ref,
                wb_ref, wo_ref, y_ref, sta_ref, stb_ref, kp_scr, vp_scr):
    i = pl.program_id(0)

    @pl.when(i == 0)
    def _():
        sta_ref[...] = jnp.zeros_like(sta_ref)
        stb_ref[...] = jnp.zeros_like(stb_ref)
        kp_scr[...] = jnp.zeros_like(kp_scr)
        vp_scr[...] = jnp.zeros_like(vp_scr)

    first_col = jnp.where(i > 0, 0, WINDOW)
    for b in range(x_ref.shape[0]):
        zb = z_ref.at[b]
        oa = jnp.concatenate(_gla_tile(zb, gm_ref, mk_ref, gna_ref, sta_ref.at[b]), axis=1).astype(BF16)
        ob = jnp.concatenate(_ret_tile(zb, d_ref, qdec_ref, kdec_ref, gnb_ref, stb_ref.at[b]), axis=1).astype(BF16)
        oc = _swa_tile(zb, sink_ref, kp_scr.at[b], vp_scr.at[b], first_col).astype(BF16)
        mixed = (jax.nn.sigmoid(zb[:, C_ZG:C_ZG + D_MODEL]) * _dot(oa, wb_ref[0:GLA_W, :])
                 + jax.nn.sigmoid(zb[:, C_ZG + D_MODEL:C_ZG + 2 * D_MODEL]) * _dot(ob, wb_ref[GLA_W:GLA_W + RET_W, :])
                 + jax.nn.sigmoid(zb[:, C_ZG + 2 * D_MODEL:C_ZG + 3 * D_MODEL]) * _dot(oc, wb_ref[GLA_W + RET_W:, :]))
        y_ref[b] = x_ref[b] + _dot(mixed.astype(BF16), wo_ref[...])


def _mix_prompt(x, z, sink, layer, consts, gna, gnb, wb, wo):
    nb, t, _ = x.shape
    gm, mk, d, qdec, kdec = consts
    st_spec = pl.BlockSpec((nb, GLA_HEADS, GLA_DK, GLA_DV), lambda i: (0, 0, 0, 0))
    st_shape = jax.ShapeDtypeStruct((nb, GLA_HEADS, GLA_DK, GLA_DV), F32)
    return pl.pallas_call(
        _mix_kernel,
        grid=(t // TT,),
        in_specs=[pl.BlockSpec(memory_space=pltpu.SMEM),
                  pl.BlockSpec((nb, TT, D_MODEL), lambda i: (0, i, 0)),
                  pl.BlockSpec((nb, TT, Z_W), lambda i: (0, i, 0)),
                  _const_spec(gm.shape), _const_spec(mk.shape), _const_spec(d.shape),
                  _const_spec(qdec.shape), _const_spec(kdec.shape),
                  _const_spec((1, GLA_W)), _const_spec((1, RET_W)),
                  _layer_spec((GLA_W + RET_W + SWA_W, D_MODEL), layer),
                  _layer_spec((D_MODEL, D_MODEL), layer)],
        out_specs=[pl.BlockSpec((nb, TT, D_MODEL), lambda i: (0, i, 0)), st_spec, st_spec],
        out_shape=[jax.ShapeDtypeStruct((nb, t, D_MODEL), F32), st_shape, st_shape],
        scratch_shapes=[pltpu.VMEM((nb, WINDOW, 128), BF16), pltpu.VMEM((nb, WINDOW, 128), BF16)],
        compiler_params=_params(("arbitrary",)),
        name="mix_prompt",
    )(sink, x, z, gm, mk, d, qdec, kdec, gna, gnb, wb, wo)


def _swa_sample_masks():
    r = np.arange(DEC_SEQ * SB)
    rt, rb = r // SB, r % SB
    c = np.arange(SB * WINDOW)
    cb, cj = c // WINDOW, c % WINDOW
    m_cache = (rb[:, None] == cb[None, :]) & (cj[None, :] > rt[:, None])
    m_new = (rb[:, None] == rb[None, :]) & (rt[None, :] <= rt[:, None])
    return m_cache.astype(np.float32), m_new.astype(np.float32)


def _swa_sample_kernel(sink_ref, q_ref, k_ref, v_ref, kc_ref, vc_ref, mc_ref, mn_ref, o_ref):
    nseq = DEC_BATCH
    row0 = pl.multiple_of(pl.program_id(0) * SB, SB)

    def gather(ref):
        return jnp.concatenate([ref[pl.ds(t * nseq + row0, SB), :] for t in range(DEC_SEQ)], axis=0)

    q = gather(q_ref).astype(BF16)
    kn = gather(k_ref).astype(BF16)
    vn = gather(v_ref).astype(BF16)
    kc = kc_ref[...].reshape(SB * WINDOW, 128).astype(BF16)
    vc = vc_ref[...].reshape(SB * WINDOW, 128).astype(BF16)
    ok_c = mc_ref[...] > 0.0
    ok_n = mn_ref[...] > 0.0
    outs = []
    for hq in range(SWA_Q_HEADS):
        g = hq // SWA_GROUP
        ds_ = slice(SWA_HEAD_DIM * g, SWA_HEAD_DIM * (g + 1))
        qh = q[:, SWA_HEAD_DIM * hq:SWA_HEAD_DIM * (hq + 1)]
        s1 = jnp.where(ok_c, _dot_nt(qh, kc[:, ds_]) * (SWA_HEAD_DIM ** -0.5), NEG_INF)
        s2 = jnp.where(ok_n, _dot_nt(qh, kn[:, ds_]) * (SWA_HEAD_DIM ** -0.5), NEG_INF)
        sk = sink_ref[hq]
        m = jnp.maximum(jnp.maximum(jnp.max(s1, axis=-1, keepdims=True),
                                    jnp.max(s2, axis=-1, keepdims=True)), sk)
        p1 = jnp.exp(s1 - m)
        p2 = jnp.exp(s2 - m)
        den = jnp.sum(p1, axis=-1, keepdims=True) + jnp.sum(p2, axis=-1, keepdims=True) + jnp.exp(sk - m)
        outs.append((_dot(p1.astype(BF16), vc[:, ds_]) + _dot(p2.astype(BF16), vn[:, ds_])) / den)
    o = jnp.concatenate(outs, axis=1)
    for t in range(DEC_SEQ):
        o_ref[pl.ds(t * nseq + row0, SB), :] = o[t * SB:(t + 1) * SB, :]


def _swa_sample(z, sink, cache_k, cache_v, layer, mc, mn):
    n = z.shape[0]

    def whole(width, c0):
        return pl.BlockSpec((n, width), lambda j: (0, c0 // width))

    cache = pl.BlockSpec((None, SB, WINDOW, 128), lambda j: (layer, j, 0, 0))
    return pl.pallas_call(
        _swa_sample_kernel,
        grid=(DEC_BATCH // SB,),
        in_specs=[pl.BlockSpec(memory_space=pltpu.SMEM),
                  whole(512, C_SQ), whole(128, C_SK), whole(128, C_SV), cache, cache,
                  _const_spec(mc.shape), _const_spec(mn.shape)],
        out_specs=pl.BlockSpec((n, SWA_W), lambda j: (0, 0)),
        out_shape=jax.ShapeDtypeStruct((n, SWA_W), F32),
        compiler_params=_params(("arbitrary",)),
        name="swa_sample",
    )(sink, z, z, z, cache_k, cache_v, mc, mn)


def _merge_kernel(x_ref, oa_ref, ob_ref, oc_ref, zg_ref, wb_ref, wo_ref, y_ref):
    def branch(o_ref, r0, j):
        proj = _dot(o_ref[...].astype(BF16), wb_ref[r0:r0 + 512, :])
        return jax.nn.sigmoid(zg_ref[:, D_MODEL * j:D_MODEL * (j + 1)]) * proj

    mixed = branch(oa_ref, 0, 0) + branch(ob_ref, GLA_W, 1) + branch(oc_ref, GLA_W + RET_W, 2)
    y_ref[...] = x_ref[...] + _dot(mixed.astype(BF16), wo_ref[...])


def _merge(x, oa, ob, oc, z, layer, wb, wo, tm):
    n = x.shape[0]
    tok = lambda w: pl.BlockSpec((tm, w), lambda i: (i, 0))
    return pl.pallas_call(
        _merge_kernel,
        grid=(n // tm,),
        in_specs=[tok(D_MODEL), tok(512), tok(512), tok(512), tok(3 * D_MODEL),
                  _layer_spec((GLA_W + RET_W + SWA_W, D_MODEL), layer),
                  _layer_spec((D_MODEL, D_MODEL), layer)],
        out_specs=tok(D_MODEL),
        out_shape=jax.ShapeDtypeStruct((n, D_MODEL), F32),
        compiler_params=_params(("arbitrary",)),
        name="merge",
    )(x, oa, ob, oc, z, wb, wo)


FF_CHUNK = D_FF // 2


def _ffn_tail(x1, pe_ref, gp_ref, wpg_ref, wpp_ref, gf_ref, y_ref, final):
    hp = _rms(x1, gp_ref[...]).astype(BF16)
    x2 = x1 + jax.nn.sigmoid(_dot(hp, wpg_ref[...])) * _dot(pe_ref[...].astype(BF16), wpp_ref[...])
    y_ref[...] = _rms(x2, gf_ref[...]) if final else x2


def _ffn_prompt_kernel(x_ref, pe_ref, gn_ref, wi_ref, cw_ref, cb_ref, wd_ref, gp_ref, wpg_ref, wpp_ref,
                       gf_ref, y_ref, tail_ref, carry_scr, *, final):
    tm = x_ref.shape[0]

    @pl.when(pl.program_id(1) == 0)
    def _():
        carry_scr[...] = jnp.zeros_like(carry_scr)

    x = x_ref[...]
    hb = _rms(x, gn_ref[...]).astype(BF16)
    row = lax.broadcasted_iota(jnp.int32, (tm, 1), 0)
    acc = x
    for c0 in range(0, D_FF, FF_CHUNK):
        cs = slice(c0, c0 + FF_CHUNK)
        a = _dot(hb, wi_ref[:, c0:c0 + FF_CHUNK])
        bb = _dot(hb, wi_ref[:, D_FF + c0:D_FF + c0 + FF_CHUNK])
        p0 = carry_scr[6:7, cs]
        p1 = carry_scr[7:8, cs]
        a1 = jnp.where(row == 0, p1, pltpu.roll(a, 1, 0))
        a2 = jnp.where(row == 0, p0, jnp.where(row == 1, p1, pltpu.roll(a, 2, 0)))
        conv = cb_ref[:, cs] + cw_ref[0:1, cs] * a2 + cw_ref[1:2, cs] * a1 + cw_ref[2:3, cs] * a
        act = (jax.nn.gelu(conv) * bb).astype(BF16)
        acc = acc + _dot(act, wd_ref[c0:c0 + FF_CHUNK, :])
        carry_scr[:, cs] = a[tm - 8:tm, :]
        tail_ref[:, cs] = a[tm - 8:tm, :]
    _ffn_tail(acc, pe_ref, gp_ref, wpg_ref, wpp_ref, gf_ref, y_ref, final)


def _ffn_sample_kernel(x_ref, pe_ref, st_ref, gn_ref, wi_ref, cw_ref, cb_ref, wd_ref, gp_ref, wpg_ref,
                       wpp_ref, gf_ref, y_ref, tail_ref, *, final):
    ns = DEC_BATCH
    x = x_ref[...]
    hb = _rms(x, gn_ref[...]).astype(BF16)
    acc = x
    for c0 in range(0, D_FF, FF_CHUNK):
        cs = slice(c0, c0 + FF_CHUNK)
        a = _dot(hb, wi_ref[:, c0:c0 + FF_CHUNK])
        bb = _dot(hb, wi_ref[:, D_FF + c0:D_FF + c0 + FF_CHUNK])
        st0 = st_ref[0, :, cs]
        st1 = st_ref[1, :, cs]
        a1 = jnp.concatenate([st1, a[0:3 * ns]], axis=0)
        a2 = jnp.concatenate([st0, st1, a[0:2 * ns]], axis=0)
        conv = cb_ref[:, cs] + cw_ref[0:1, cs] * a2 + cw_ref[1:2, cs] * a1 + cw_ref[2:3, cs] * a
        act = (jax.nn.gelu(conv) * bb).astype(BF16)
        acc = acc + _dot(act, wd_ref[c0:c0 + FF_CHUNK, :])
        tail_ref[:, cs] = a[2 * ns:4 * ns, :]
    _ffn_tail(acc, pe_ref, gp_ref, wpg_ref, wpp_ref, gf_ref, y_ref, final)


def _ffn_weight_specs(layer):
    return [_layer_spec((1, D_MODEL), layer),
            _layer_spec((D_MODEL, 2 * D_FF), layer),
            _layer_spec((CONV_W, D_FF), layer),
            _layer_spec((1, D_FF), layer),
            _layer_spec((D_FF, D_MODEL), layer),
            _layer_spec((1, D_MODEL), layer),
            _layer_spec((D_MODEL, D_MODEL), layer),
            _layer_spec((D_PLE, D_MODEL), layer),
            _const_spec((1, D_MODEL))]


def _ffn_prompt(x, pe, layer, weights, nb, t, tm, final):
    nt = t // tm
    tok = lambda w: pl.BlockSpec((tm, w), lambda b, i: (b * nt + i, 0))
    return pl.pallas_call(
        functools.partial(_ffn_prompt_kernel, final=final),
        grid=(nb, nt),
        in_specs=[tok(D_MODEL),
                  pl.BlockSpec((None, tm, D_PLE), lambda b, i: (layer, b * nt + i, 0))]
        + _ffn_weight_specs(layer),
        out_specs=[tok(D_MODEL), pl.BlockSpec((8, D_FF), lambda b, i: (b, 0))],
        out_shape=[jax.ShapeDtypeStruct((nb * t, D_MODEL), F32),
                   jax.ShapeDtypeStruct((nb * 8, D_FF), F32)],
        scratch_shapes=[pltpu.VMEM((8, D_FF), F32)],
        compiler_params=_params(("arbitrary", "arbitrary")),
        name="ffn_prompt",
    )(x, pe, *weights)


def _ffn_sample(x, pe, st, layer, weights, final):
    n = x.shape[0]
    return pl.pallas_call(
        functools.partial(_ffn_sample_kernel, final=final),
        grid=(1,),
        in_specs=[pl.BlockSpec((n, D_MODEL), lambda i: (0, 0)),
                  pl.BlockSpec((None, n, D_PLE), lambda i: (layer, 0, 0)),
                  pl.BlockSpec((None, CONV_W - 1, DEC_BATCH, D_FF), lambda i: (layer, 0, 0, 0))]
        + _ffn_weight_specs(layer),
        out_specs=[pl.BlockSpec((n, D_MODEL), lambda i: (0, 0)),
                   pl.BlockSpec((n // 2, D_FF), lambda i: (0, 0))],
        out_shape=[jax.ShapeDtypeStruct((n, D_MODEL), F32),
                   jax.ShapeDtypeStruct((n // 2, D_FF), F32)],
        compiler_params=_params(("arbitrary",)),
        name="ffn_sample",
    )(x, pe, st, *weights)


def _rope_tables(pos, inv_freq):
    half = inv_freq.shape[0]
    ang = pos.astype(np.float64)[:, None] * inv_freq[None, :]
    c, s = np.cos(ang), np.sin(ang)
    rest = SWA_HEAD_DIM - 2 * half
    cos64 = np.concatenate([c, c, np.ones((pos.shape[0], rest))], axis=1)
    sin64 = np.concatenate([-s, s, np.zeros((pos.shape[0], rest))], axis=1)
    return (jnp.asarray(np.concatenate([cos64, cos64], axis=1), F32),
            jnp.asarray(np.concatenate([sin64, sin64], axis=1), F32))


def _pack_w_in(w_in):
    offs = np.cumsum((0,) + IN_SIZES)
    gq, gk, gv, gr, ga, rq, rk, rv, rg, sq, sk, sv, zg = [
        w_in[:, :, offs[i]:offs[i + 1]] for i in range(len(IN_SIZES))]
    ga = jnp.pad(ga, ((0, 0), (0, 0), (0, LANES - GLA_RANK)))
    return jnp.concatenate([zg, gq, gk, gv, gr, rq, rk, rv, rg, sq, sk, sv, ga], axis=2).astype(BF16)


def kernel(x_prompt, x_sample, state_gla, state_ret, cache_swa_k, cache_swa_v, state_conv, p_prompt,
           p_sample, norm_mix, w_in, w_gla_a, b_gla_a, gla_norm, ret_norm, swa_sink, w_branch, w_out,
           norm_ffn, w_ffn_in, conv_w, conv_b, w_ffn_out, norm_ple, w_ple_gate, w_ple_proj, norm_final):
    nb, t, _ = x_prompt.shape
    ns, ts, _ = x_sample.shape
    n_s = ns * ts

    w_main = _pack_w_in(w_in)
    wa = jnp.pad(w_gla_a, ((0, 0), (0, LANES - GLA_RANK), (0, 0))).astype(BF16)
    ba = b_gla_a[:, None, :]
    wb = w_branch.astype(BF16)
    wo = w_out.astype(BF16)
    ffn_weights = (norm_ffn[:, None, :], w_ffn_in.astype(BF16), conv_w, conv_b[:, None, :],
                   w_ffn_out.astype(BF16), norm_ple[:, None, :], w_ple_gate.astype(BF16),
                   w_ple_proj.astype(BF16), norm_final[None, :])
    g_mix = norm_mix[:, None, :]
    gn_gla = jnp.tile(gla_norm, (1, GLA_HEADS))[:, None, :]
    gn_ret = ret_norm.reshape(DEPTH, 1, RET_W)

    ret_freq = 1.0 / (RET_THETA ** np.linspace(0.0, 1.0, RET_DK // 2))
    swa_freq = 1.0 / (ROPE_THETA ** (np.arange(0, ROPE_DIM, 2, dtype=np.float64) / ROPE_DIM))
    pos_p = np.arange(t)
    pos_s = PAST_LEN + np.arange(n_s) // ns
    tabs_p = _rope_tables(pos_p, ret_freq) + _rope_tables(pos_p, swa_freq)
    tabs_s = _rope_tables(pos_s, ret_freq) + _rope_tables(pos_s, swa_freq)

    gm_np, mk_np = _gla_constants()
    d_np, qdec_np, kdec_np = _ret_constants()
    mix_consts = (jnp.asarray(gm_np, BF16), jnp.asarray(mk_np), jnp.asarray(d_np), jnp.asarray(qdec_np),
                  jnp.asarray(kdec_np))
    log_gamma = np.log1p(-np.exp2(-5.0 - np.arange(RET_HEADS, dtype=np.float64)))
    la_ret = jnp.asarray(np.broadcast_to(np.repeat(log_gamma, RET_DK)[None, :], (n_s, RET_HEADS * RET_DK)), F32)
    mc_np, mn_np = _swa_sample_masks()
    swa_mc, swa_mn = jnp.asarray(mc_np), jnp.asarray(mn_np)
    rep_np, bd_np = _rec_sample_constants()
    rec_rep, rec_bd = jnp.asarray(rep_np, BF16), jnp.asarray(bd_np)

    xp = x_prompt.reshape(nb * t, D_MODEL)
    xs = x_sample.transpose(1, 0, 2).reshape(n_s, D_MODEL)
    pe_p = p_prompt.reshape(DEPTH, nb * t, D_PLE)
    pe_s = p_sample.transpose(0, 2, 1, 3).reshape(DEPTH, n_s, D_PLE)
    conv_st = state_conv.transpose(0, 2, 1, 3)
    ck = cache_swa_k.reshape(DEPTH, ns, WINDOW, SWA_KV_HEADS * SWA_HEAD_DIM)
    cv = cache_swa_v.reshape(DEPTH, ns, WINDOW, SWA_KV_HEADS * SWA_HEAD_DIM)

    outs = {k_: [] for k_ in ("gla_p", "gla_s", "ret_p", "ret_s", "k_p", "k_s", "v_p", "v_s", "conv_p", "conv_s")}
    for l in range(DEPTH):
        final = l == DEPTH - 1
        sink = swa_sink[l]

        z = _in_proj(xp, l, g_mix, w_main, wa, ba, tabs_p, 256)
        z3 = z.reshape(nb, t, Z_W)
        xp3, st_a, st_b = _mix_prompt(xp.reshape(nb, t, D_MODEL), z3, sink, l, mix_consts, gn_gla[l], gn_ret[l],
                                      wb, wo)
        xp, tail = _ffn_prompt(xp3.reshape(nb * t, D_MODEL), pe_p, l, ffn_weights, nb, t, 256, final)
        outs["gla_p"].append(st_a)
        outs["ret_p"].append(st_b)
        outs["k_p"].append(z3[:, t - WINDOW:, C_SK:C_SV].reshape(nb, WINDOW, SWA_KV_HEADS, SWA_HEAD_DIM))
        outs["v_p"].append(z3[:, t - WINDOW:, C_SV:C_LA].reshape(nb, WINDOW, SWA_KV_HEADS, SWA_HEAD_DIM))
        outs["conv_p"].append(tail.reshape(nb, 8, D_FF)[:, 8 - (CONV_W - 1):])

        zs = _in_proj(xs, l, g_mix, w_main, wa, ba, tabs_s, 256)
        oa, st_a = _rec_sample(zs, zs, state_gla, l, gn_gla[l], rec_rep, rec_bd,
                               C_GQ, C_GK, C_GV, C_GR, C_LA, "gla_sample")
        ob, st_b = _rec_sample(zs, la_ret, state_ret, l, gn_ret[l], rec_rep, rec_bd,
                               C_RQ, C_RK, C_RV, C_RG, 0, "ret_sample")
        oc = _swa_sample(zs, sink, ck, cv, l, swa_mc, swa_mn)
        xs = _merge(xs, oa, ob, oc, zs, l, wb, wo, 512)
        xs, tail_s = _ffn_sample(xs, pe_s, conv_st, l, ffn_weights, final)
        k_new = zs[:, C_SK:C_SV].reshape(ts, ns, SWA_KV_HEADS, SWA_HEAD_DIM).transpose(1, 0, 2, 3)
        v_new = zs[:, C_SV:C_LA].reshape(ts, ns, SWA_KV_HEADS, SWA_HEAD_DIM).transpose(1, 0, 2, 3)
        outs["gla_s"].append(st_a)
        outs["ret_s"].append(st_b)
        outs["k_s"].append(jnp.concatenate([cache_swa_k[l][:, ts:], k_new], axis=1))
        outs["v_s"].append(jnp.concatenate([cache_swa_v[l][:, ts:], v_new], axis=1))
        outs["conv_s"].append(tail_s.reshape(CONV_W - 1, ns, D_FF).transpose(1, 0, 2))

    y_prompt = xp.reshape(nb, t, D_MODEL)
    y_sample = xs.reshape(ts, ns, D_MODEL).transpose(1, 0, 2)
    st = {k_: jnp.stack(v_) for k_, v_ in outs.items()}
    return (y_prompt, y_sample, st["gla_p"], st["gla_s"], st["ret_p"], st["ret_s"],
            st["k_p"], st["k_s"], st["v_p"], st["v_s"], st["conv_p"], st["conv_s"])
```

```python
import functools
import math

import numpy as np
import jax
import jax.numpy as jnp
from jax import lax
from jax.experimental import pallas as pl
from jax.experimental.pallas import tpu as pltpu

F32 = jnp.float32
BF16 = jnp.bfloat16

D_MODEL = 1024
BATCH = 2
SEQ = 8192
DEPTH = 4
DEC_BATCH = 128
DEC_SEQ = 4
PAST_LEN = 8192
D_PLE = 256
GLA_HEADS = 4
GLA_DK = 64
GLA_DV = 128
GLA_RANK = 16
GLA_TAU = 16.0
RET_HEADS = 4
RET_DK = 64
RET_DV = 128
RET_THETA = 10000.0
SWA_Q_HEADS = 8
SWA_KV_HEADS = 2
SWA_HEAD_DIM = 64
SWA_GROUP = SWA_Q_HEADS // SWA_KV_HEADS
WINDOW = 128
ROPE_THETA = 500000.0
ROPE_DIM = SWA_HEAD_DIM // 4
D_FF = 2816
CONV_W = 3
N_BRANCH = 3
EPS = 1e-6
NEG_INF = -1e30

GLA_W = GLA_HEADS * GLA_DV
RET_W = RET_HEADS * RET_DV
SWA_W = SWA_Q_HEADS * SWA_HEAD_DIM
IN_SIZES = (GLA_HEADS * GLA_DK, GLA_HEADS * GLA_DK, GLA_W, GLA_W, GLA_RANK,
            RET_HEADS * RET_DK, RET_HEADS * RET_DK, RET_W, RET_W,
            SWA_W, SWA_KV_HEADS * SWA_HEAD_DIM, SWA_KV_HEADS * SWA_HEAD_DIM,
            N_BRANCH * D_MODEL)

LANES = 128
VMEM_LIMIT = 52 * 1024 * 1024

C_ZG = 0
C_GQ = 3072
C_GK = 3328
C_GV = 3584
C_GR = 4096
C_RQ = 4608
C_RK = 4864
C_RV = 5120
C_RG = 5632
C_SQ = 6144
C_SK = 6656
C_SV = 6784
C_LA = 6912
W_MAIN = 7040
Z_W = 7168

TT = 128
N_LEVELS = 7
SB = 16


def _dot(a, b):
    return jnp.dot(a, b, preferred_element_type=F32)


def _dot_nt(a, b):
    return lax.dot_general(a, b, (((1,), (1,)), ((), ())), preferred_element_type=F32)


def _dot_tn(a, b):
    return lax.dot_general(a, b, (((0,), (0,)), ((), ())), preferred_element_type=F32)


def _rms(x, g):
    return x * lax.rsqrt(jnp.mean(x * x, axis=-1, keepdims=True) + EPS) * g


def _layer_spec(shape, layer):
    nd = len(shape)
    return pl.BlockSpec((None,) + tuple(shape), lambda *_: (layer,) + (0,) * nd,
                        pipeline_mode=pl.Buffered(1))


def _const_spec(shape):
    nd = len(shape)
    return pl.BlockSpec(tuple(shape), lambda *_: (0,) * nd, pipeline_mode=pl.Buffered(1))


def _params(sem):
    return pltpu.CompilerParams(dimension_semantics=sem, vmem_limit_bytes=VMEM_LIMIT)


def _rope_block(x, cos, sin_s, first):
    half_mask, half = first
    xr = jnp.where(half_mask, pltpu.roll(x, LANES - half, 1), pltpu.roll(x, half, 1))
    return x * cos + xr * sin_s


def _in_kernel(x_ref, g_ref, w_ref, wa_ref, ba_ref, rc_ref, rs_ref, sc_ref, ss_ref, z_ref):
    hb = _rms(x_ref[...], g_ref[...]).astype(BF16)

    def mm(c0, c1):
        return _dot(hb, w_ref[:, c0:c1])

    lane = lax.broadcasted_iota(jnp.int32, (1, LANES), 1) % 64
    ret_first = (lane < RET_DK // 2, RET_DK // 2)
    swa_first = (lane < ROPE_DIM // 2, ROPE_DIM // 2)

    z_ref[:, C_ZG:C_ZG + 3072] = mm(C_ZG, C_ZG + 3072)
    z_ref[:, C_GQ:C_GK] = mm(C_GQ, C_GK) * (GLA_DK ** -0.5)
    z_ref[:, C_GK:C_RQ] = mm(C_GK, C_RQ)
    rc, rs = rc_ref[...], rs_ref[...]
    rq = mm(C_RQ, C_RK)
    rk = mm(C_RK, C_RV)
    for j in range(2):
        sl = slice(LANES * j, LANES * (j + 1))
        z_ref[:, C_RQ + LANES * j:C_RQ + LANES * (j + 1)] = _rope_block(rq[:, sl], rc, rs, ret_first)
        z_ref[:, C_RK + LANES * j:C_RK + LANES * (j + 1)] = (
            _rope_block(rk[:, sl], rc, rs, ret_first) * (RET_DK ** -0.5))
    z_ref[:, C_RV:C_SQ] = mm(C_RV, C_SQ)
    sc, ss = sc_ref[...], ss_ref[...]
    sqk = mm(C_SQ, C_SV)
    for j in range(5):
        sl = slice(LANES * j, LANES * (j + 1))
        z_ref[:, C_SQ + LANES * j:C_SQ + LANES * (j + 1)] = _rope_block(sqk[:, sl], sc, ss, swa_first)
    z_ref[:, C_SV:C_LA] = mm(C_SV, C_LA)
    ga = mm(C_LA, W_MAIN)
    xa = _dot(ga.astype(BF16), wa_ref[...]) + ba_ref[...]
    log_sig = jnp.minimum(xa, 0.0) - jnp.log1p(jnp.exp(-jnp.abs(xa)))
    z_ref[:, C_LA:Z_W] = log_sig * (1.0 / GLA_TAU)


def _in_proj(x, layer, g_mix, w_main, wa, ba, tabs, tm):
    n = x.shape[0]
    rc, rs, sc, ss = tabs
    nt = rc.shape[0] // tm
    tab = pl.BlockSpec((tm, LANES), lambda i: (i % nt, 0))
    return pl.pallas_call(
        _in_kernel,
        grid=(n // tm,),
        in_specs=[pl.BlockSpec((tm, D_MODEL), lambda i: (i, 0)),
                  _layer_spec((1, D_MODEL), layer),
                  _layer_spec((D_MODEL, W_MAIN), layer),
                  _layer_spec((LANES, 256), layer),
                  _layer_spec((1, 256), layer),
                  tab, tab, tab, tab],
        out_specs=pl.BlockSpec((tm, Z_W), lambda i: (i, 0)),
        out_shape=jax.ShapeDtypeStruct((n, Z_W), F32),
        compiler_params=_params(("arbitrary",)),
        name="in_proj",
    )(x, g_mix, w_main, wa, ba, rc, rs, sc, ss)


def _gla_constants():
    t = np.arange(TT)
    g = np.zeros((2 + N_LEVELS, TT, TT), np.float32)
    g[0] = (t[None, :] <= t[:, None])
    g[1] = (t[None, :] > t[:, None])
    m = np.zeros((1 + N_LEVELS, TT, TT), np.float32)
    m[0] = np.eye(TT)
    for lv in range(1, N_LEVELS + 1):
        bs, hf = 2 ** lv, 2 ** (lv - 1)
        bd = (t // bs) * bs + hf - 1
        upper = (t % bs) >= hf
        u = t[None, :]
        g[1 + lv] = np.where(upper[:, None], (u > bd[:, None]) & (u <= t[:, None]),
                             (u > t[:, None]) & (u <= bd[:, None]))
        same = (t[:, None] // bs) == (t[None, :] // bs)
        m[lv] = same & upper[:, None] & (~upper)[None, :]
    return g.reshape(-1, TT), np.tile(m, (1, 1, GLA_HEADS))


def _head_masks():
    hk = np.arange(GLA_HEADS * TT)[:, None] // TT == np.arange(GLA_HEADS * GLA_DK)[None, :] // GLA_DK
    hv = np.arange(GLA_HEADS * TT)[:, None] // TT == np.arange(GLA_W)[None, :] // GLA_DV
    hs = np.arange(GLA_HEADS * GLA_DK)[:, None] // GLA_DK == np.arange(GLA_W)[None, :] // GLA_DV
    return hk.astype(np.float32), hv.astype(np.float32), hs.astype(np.float32)


def _heads_blockdiag(x, mask):
    return jnp.concatenate([x] * GLA_HEADS, axis=0) * mask


def _col_vector(row):
    n = row.shape[1]
    eye = lax.broadcasted_iota(jnp.int32, (n, n), 0) == lax.broadcasted_iota(jnp.int32, (n, n), 1)
    return jnp.sum(jnp.where(eye, row, 0.0), axis=1, keepdims=True)


def _norm_gate(o, gn, gate):
    outs = [_rms(o[:, GLA_DV * h:GLA_DV * (h + 1)], gn[:, GLA_DV * h:GLA_DV * (h + 1)]) for h in range(GLA_HEADS)]
    return jnp.concatenate(outs, axis=1) * (gate * jax.nn.sigmoid(gate))


def _gla_tile(z_ref, gm_ref, mk_ref, hk_ref, hv_ref, hs_ref, gn_ref, st_ref):
    la = z_ref[:, C_LA:Z_W]
    la_hi = la.astype(BF16)
    la_lo = (la - la_hi.astype(F32)).astype(BF16)
    gm = gm_ref[...]
    ex = jnp.exp(_dot(gm, la_hi) + _dot(gm, la_lo))
    q = z_ref[:, C_GQ:C_GK]
    k = z_ref[:, C_GK:C_GV]
    e_b = ex[0:TT]
    a_row = e_b[TT - 1:TT, :]
    qd = (q * e_b).astype(BF16)
    kd = (k * ex[TT:2 * TT]).astype(BF16)
    qb = q.astype(BF16)
    kb = k.astype(BF16)
    ql = [(q * ex[(1 + lv) * TT:(2 + lv) * TT]).astype(BF16) for lv in range(1, N_LEVELS + 1)]
    kl = [(k * ex[(1 + lv) * TT:(2 + lv) * TT]).astype(BF16) for lv in range(1, N_LEVELS + 1)]
    hk = hk_ref[...]
    yield
    a = mk_ref[0] * _dot_nt(qb, _heads_blockdiag(kb, hk))
    for lv in range(N_LEVELS):
        yield
        a = a + mk_ref[lv + 1] * _dot_nt(ql[lv], _heads_blockdiag(kl[lv], hk))
    yield
    v = z_ref[:, C_GV:C_GR].astype(BF16)
    s = st_ref[...]
    o = _dot(a.astype(BF16), _heads_blockdiag(v, hv_ref[...])) + _dot(qd, s.astype(BF16))
    yield
    st_ref[...] = _col_vector(a_row) * s + hs_ref[...] * _dot_tn(kd, v)
    yield
    return _norm_gate(o, gn_ref[...], z_ref[:, C_GR:C_RQ])


def _ret_gammas():
    return [1.0 - 2.0 ** (-5.0 - h) for h in range(RET_HEADS)]


def _ret_constants():
    t = np.arange(TT, dtype=np.float64)
    d = np.zeros((RET_HEADS, TT, TT), np.float64)
    qdec = np.zeros((TT, RET_HEADS * RET_DK), np.float64)
    kdec = np.zeros((TT, RET_HEADS * RET_DK), np.float64)
    adec = np.zeros((RET_HEADS * RET_DK, LANES), np.float64)
    for h, gam in enumerate(_ret_gammas()):
        diff = t[:, None] - t[None, :]
        d[h] = np.where(diff >= 0, gam ** np.maximum(diff, 0.0), 0.0)
        qdec[:, h * RET_DK:(h + 1) * RET_DK] = (gam ** (t + 1.0))[:, None]
        kdec[:, h * RET_DK:(h + 1) * RET_DK] = (gam ** (TT - 1.0 - t))[:, None]
        adec[h * RET_DK:(h + 1) * RET_DK, :] = gam ** TT
    d = np.concatenate(list(d), axis=1)
    adec = np.tile(adec, (1, RET_W // LANES))
    return d.astype(np.float32), qdec.astype(np.float32), kdec.astype(np.float32), adec.astype(np.float32)


def _ret_tile(z_ref, d_ref, qdec_ref, kdec_ref, adec_ref, hk_ref, hv_ref, hs_ref, gn_ref, st_ref):
    q = z_ref[:, C_RQ:C_RK]
    k = z_ref[:, C_RK:C_RV]
    qd = (q * qdec_ref[...]).astype(BF16)
    kd = (k * kdec_ref[...]).astype(BF16)
    a = d_ref[...] * _dot_nt(q.astype(BF16), _heads_blockdiag(k.astype(BF16), hk_ref[...]))
    yield
    v = z_ref[:, C_RV:C_RG].astype(BF16)
    s = st_ref[...]
    o = _dot(a.astype(BF16), _heads_blockdiag(v, hv_ref[...])) + _dot(qd, s.astype(BF16))
    yield
    st_ref[...] = adec_ref[...] * s + hs_ref[...] * _dot_tn(kd, v)
    yield
    return _norm_gate(o, gn_ref[...], z_ref[:, C_RG:C_SQ])


def _rec_sample_constants():
    c = np.arange(SB * GLA_DK)
    rep = np.arange(GLA_DK)[:, None] == (c[None, :] % GLA_DK)
    bd = (np.arange(DEC_SEQ * SB)[:, None] % SB) == (c[None, :] // GLA_DK)
    return rep.astype(np.float32), bd.astype(np.float32)


def _rec_sample_kernel(q_ref, k_ref, v_ref, gr_ref, la_ref, s0_ref, gn_ref, rep_ref, bd_ref, o_ref, s1_ref):
    nseq = DEC_BATCH
    row0 = pl.multiple_of(pl.program_id(0) * SB, SB)

    def rows(ref, t):
        return ref[pl.ds(t * nseq + row0, SB), :]

    b = []
    for t in range(DEC_SEQ):
        la_t = rows(la_ref, t)
        b.append(la_t if t == 0 else b[-1] + la_t)
    q = [rows(q_ref, t) for t in range(DEC_SEQ)]
    k = [rows(k_ref, t) for t in range(DEC_SEQ)]
    v = [rows(v_ref, t) for t in range(DEC_SEQ)]
    qd = jnp.concatenate([q[t] * jnp.exp(b[t]) for t in range(DEC_SEQ)], axis=0).astype(BF16)
    kd = jnp.concatenate([k[t] * jnp.exp(b[-1] - b[t]) for t in range(DEC_SEQ)], axis=0).astype(BF16)
    vv = jnp.concatenate(v, axis=0).astype(BF16)
    a = jnp.exp(b[-1])
    a_hi = a.astype(BF16)
    a_r = a - a_hi.astype(F32)
    a_mid = a_r.astype(BF16)
    a_lo = (a_r - a_mid.astype(F32)).astype(BF16)
    rep = rep_ref[...]
    bd = bd_ref[...]
    ones = jnp.ones((SB, GLA_DV), BF16)

    def expand(x, mask):
        return (_dot(x, rep) * mask).astype(BF16)

    oi = []
    for h in range(GLA_HEADS):
        hs = slice(GLA_DK * h, GLA_DK * (h + 1))
        vs = slice(GLA_DV * h, GLA_DV * (h + 1))
        s0 = s0_ref[:, h].reshape(SB * GLA_DK, GLA_DV)
        oi.append(_dot(expand(qd[:, hs], bd), s0.astype(BF16)))
        a_col = (_dot_tn(expand(a_hi[:, hs], bd[0:SB]), ones) + _dot_tn(expand(a_mid[:, hs], bd[0:SB]), ones)
                 + _dot_tn(expand(a_lo[:, hs], bd[0:SB]), ones))
        s1 = a_col * s0 + _dot_tn(expand(kd[:, hs], bd), vv[:, vs])
        s1_ref[:, h] = s1.reshape(SB, GLA_DK, GLA_DV)
    oi = jnp.concatenate(oi, axis=1)

    gn = gn_ref[...]
    for t in range(DEC_SEQ):
        o = oi[t * SB:(t + 1) * SB, :]
        for s in range(t + 1):
            p = q[t] * k[s] * jnp.exp(b[t] - b[s])
            parts = []
            for h in range(GLA_HEADS):
                hs = slice(GLA_DK * h, GLA_DK * (h + 1))
                vs = slice(GLA_DV * h, GLA_DV * (h + 1))
                parts.append(jnp.sum(p[:, hs], axis=1, keepdims=True) * v[s][:, vs])
            o = o + jnp.concatenate(parts, axis=1)
        gate = rows(gr_ref, t)
        outs = []
        for h in range(GLA_HEADS):
            vs = slice(GLA_DV * h, GLA_DV * (h + 1))
            outs.append(_rms(o[:, vs], gn[:, vs]))
        o_ref[pl.ds(t * nseq + row0, SB), :] = jnp.concatenate(outs, axis=1) * (gate * jax.nn.sigmoid(gate))


def _rec_sample(z, la, s0, layer, gn, rep, bd, cq, ck, cv, cg, c_la, name):
    n = z.shape[0]

    def whole(arr, width, c0):
        return pl.BlockSpec((arr.shape[0], width), lambda j: (0, c0 // width))

    st_in = pl.BlockSpec((None, SB, GLA_HEADS, GLA_DK, GLA_DV), lambda j: (layer, j, 0, 0, 0))
    st_out = pl.BlockSpec((SB, GLA_HEADS, GLA_DK, GLA_DV), lambda j: (j, 0, 0, 0))
    return pl.pallas_call(
        _rec_sample_kernel,
        grid=(DEC_BATCH // SB,),
        in_specs=[whole(z, 256, cq), whole(z, 256, ck), whole(z, 512, cv), whole(z, 512, cg),
                  whole(la, 256, c_la), st_in, _const_spec((1, GLA_W)),
                  _const_spec(rep.shape), _const_spec(bd.shape)],
        out_specs=[pl.BlockSpec((n, GLA_W), lambda j: (0, 0)), st_out],
        out_shape=[jax.ShapeDtypeStruct((n, GLA_W), F32),
                   jax.ShapeDtypeStruct(s0.shape[1:], F32)],
        compiler_params=_params(("arbitrary",)),
        name=name,
    )(z, z, z, z, la, s0, gn, rep, bd)


def _swa_tile(z_ref, sink_ref, kp_ref, vp_ref, first_col):
    kc = z_ref[:, C_SK:C_SV].astype(BF16)
    vc = z_ref[:, C_SV:C_LA].astype(BF16)
    kband = jnp.concatenate([kp_ref[...], kc], axis=0)
    vband = jnp.concatenate([vp_ref[...], vc], axis=0)
    rows = SWA_GROUP * WINDOW
    r = lax.broadcasted_iota(jnp.int32, (rows, 2 * WINDOW), 0) & (WINDOW - 1)
    c = lax.broadcasted_iota(jnp.int32, (rows, 2 * WINDOW), 1)
    valid = (c > r) & (c <= r + WINDOW) & (c >= first_col)
    groups = range(SWA_KV_HEADS)
    dsl = [slice(SWA_HEAD_DIM * g, SWA_HEAD_DIM * (g + 1)) for g in groups]
    s, sk = [], []
    for g in groups:
        heads = range(SWA_GROUP * g, SWA_GROUP * (g + 1))
        q = jnp.concatenate(
            [z_ref[:, C_SQ + SWA_HEAD_DIM * hq:C_SQ + SWA_HEAD_DIM * (hq + 1)] for hq in heads], axis=0)
        sk.append(jnp.concatenate([jnp.full((WINDOW, 1), sink_ref[hq], F32) for hq in heads], axis=0))
        s.append(_dot_nt(q.astype(BF16), kband[:, dsl[g]]) * (SWA_HEAD_DIM ** -0.5))
    yield
    m, p = [], []
    for g in groups:
        sg = jnp.where(valid, s[g], NEG_INF)
        m.append(jnp.maximum(jnp.max(sg, axis=-1, keepdims=True), sk[g]))
        p.append(jnp.exp(sg - m[g]))
    yield
    outs = []
    for g in groups:
        den = jnp.sum(p[g], axis=-1, keepdims=True) + jnp.exp(sk[g] - m[g])
        o = _dot(p[g].astype(BF16), vband[:, dsl[g]]) / den
        outs += [o[WINDOW * j:WINDOW * (j + 1)] for j in range(SWA_GROUP)]
    kp_ref[...] = kc
    vp_ref[...] = vc
    yield
    return jnp.concatenate(outs, axis=1)


def _gate_stages(z_ref, start):
    for _ in range(start):
        yield
    gates = []
    for j in range(N_BRANCH):
        gates.append(jax.nn.sigmoid(z_ref[:, C_ZG + D_MODEL * j:C_ZG + D_MODEL * (j + 1)]))
        yield
    return gates


def _interleave(stage_fns):
    results = [None] * len(stage_fns)
    live = list(enumerate(stage_fns))
    while live:
        still = []
        for idx, gen in live:
            try:
                next(gen)
                still.append((idx, gen))
            except StopIteration as stop:
                results[idx] = stop.value
        live = still
    return results


def _mix_kernel(sink_ref, x_ref, z_ref, gm_ref, mk_ref, d_ref, qdec_ref, kdec_ref, adec_ref, hk_ref, hv_ref,
                hs_ref, gna_ref, gnb_ref, wb_ref, wo_ref, y_ref, sta_ref, stb_ref, sa_scr, sb_scr, kp_scr, vp_scr):
    i = pl.program_id(0)

    @pl.when(i == 0)
    def _():
        sa_scr[...] = jnp.zeros_like(sa_scr)
        sb_scr[...] = jnp.zeros_like(sb_scr)
        kp_scr[...] = jnp.zeros_like(kp_scr)
        vp_scr[...] = jnp.zeros_like(vp_scr)

    first_col = jnp.where(i > 0, 0, WINDOW)
    nb = x_ref.shape[0]
    stages = []
    for b in range(nb):
        zb = z_ref.at[b]
        stages += [_gla_tile(zb, gm_ref, mk_ref, hk_ref, hv_ref, hs_ref, gna_ref, sa_scr.at[b]),
                   _swa_tile(zb, sink_ref, kp_scr.at[b], vp_scr.at[b], first_col),
                   _ret_tile(zb, d_ref, qdec_ref, kdec_ref, adec_ref, hk_ref, hv_ref, hs_ref, gnb_ref, sb_scr.at[b]),
                   _gate_stages(zb, 4 + N_BRANCH * b)]
    branch = _interleave(stages)
    for b in range(nb):
        for h in range(GLA_HEADS):
            sta_ref[b, h] = sa_scr[b, GLA_DK * h:GLA_DK * (h + 1), GLA_DV * h:GLA_DV * (h + 1)]
            stb_ref[b, h] = sb_scr[b, RET_DK * h:RET_DK * (h + 1), RET_DV * h:RET_DV * (h + 1)]

    def stacked(j):
        return jnp.concatenate([branch[4 * b + j] for b in range(nb)], axis=0).astype(BF16)

    proj = [_dot(stacked(0), wb_ref[0:GLA_W, :]), _dot(stacked(2), wb_ref[GLA_W:GLA_W + RET_W, :]),
            _dot(stacked(1), wb_ref[GLA_W + RET_W:, :])]
    mixed = []
    for b in range(nb):
        rows = slice(TT * b, TT * (b + 1))
        mixed.append(sum(branch[4 * b + 3][j] * proj[j][rows] for j in range(N_BRANCH)))
    out = _dot(jnp.concatenate(mixed, axis=0).astype(BF16), wo_ref[...])
    for b in range(nb):
        y_ref[b] = x_ref[b] + out[TT * b:TT * (b + 1)]


def _mix_prompt(x, z, sink, layer, consts, gna, gnb, wb, wo):
    nb, t, _ = x.shape
    st_spec = pl.BlockSpec((nb, GLA_HEADS, GLA_DK, GLA_DV), lambda i: (0, 0, 0, 0))
    st_shape = jax.ShapeDtypeStruct((nb, GLA_HEADS, GLA_DK, GLA_DV), F32)
    return pl.pallas_call(
        _mix_kernel,
        grid=(t // TT,),
        in_specs=[pl.BlockSpec(memory_space=pltpu.SMEM),
                  pl.BlockSpec((nb, TT, D_MODEL), lambda i: (0, i, 0)),
                  pl.BlockSpec((nb, TT, Z_W), lambda i: (0, i, 0))]
        + [_const_spec(c.shape) for c in consts]
        + [_const_spec((1, GLA_W)), _const_spec((1, RET_W)),
           _layer_spec((GLA_W + RET_W + SWA_W, D_MODEL), layer),
           _layer_spec((D_MODEL, D_MODEL), layer)],
        out_specs=[pl.BlockSpec((nb, TT, D_MODEL), lambda i: (0, i, 0)), st_spec, st_spec],
        out_shape=[jax.ShapeDtypeStruct((nb, t, D_MODEL), F32), st_shape, st_shape],
        scratch_shapes=[pltpu.VMEM((nb, GLA_HEADS * GLA_DK, GLA_W), F32),
                        pltpu.VMEM((nb, RET_HEADS * RET_DK, RET_W), F32),
                        pltpu.VMEM((nb, WINDOW, 128), BF16), pltpu.VMEM((nb, WINDOW, 128), BF16)],
        compiler_params=_params(("arbitrary",)),
        name="mix_prompt",
    )(sink, x, z, *consts, gna, gnb, wb, wo)


def _swa_sample_masks():
    r = np.arange(DEC_SEQ * SB)
    rt, rb = r // SB, r % SB
    c = np.arange(SB * WINDOW)
    cb, cj = c // WINDOW, c % WINDOW
    m_cache = (rb[:, None] == cb[None, :]) & (cj[None, :] > rt[:, None])
    m_new = (rb[:, None] == rb[None, :]) & (rt[None, :] <= rt[:, None])
    return m_cache.astype(np.float32), m_new.astype(np.float32)


def _swa_sample_kernel(sink_ref, q_ref, k_ref, v_ref, kc_ref, vc_ref, mc_ref, mn_ref, o_ref):
    nseq = DEC_BATCH
    row0 = pl.multiple_of(pl.program_id(0) * SB, SB)

    def gather(ref):
        return jnp.concatenate([ref[pl.ds(t * nseq + row0, SB), :] for t in range(DEC_SEQ)], axis=0)

    q = gather(q_ref).astype(BF16)
    kn = gather(k_ref).astype(BF16)
    vn = gather(v_ref).astype(BF16)
    kc = kc_ref[...].reshape(SB * WINDOW, 128).astype(BF16)
    vc = vc_ref[...].reshape(SB * WINDOW, 128).astype(BF16)
    ok_c = mc_ref[...] > 0.0
    ok_n = mn_ref[...] > 0.0
    outs = []
    for hq in range(SWA_Q_HEADS):
        g = hq // SWA_GROUP
        ds_ = slice(SWA_HEAD_DIM * g, SWA_HEAD_DIM * (g + 1))
        qh = q[:, SWA_HEAD_DIM * hq:SWA_HEAD_DIM * (hq + 1)]
        s1 = jnp.where(ok_c, _dot_nt(qh, kc[:, ds_]) * (SWA_HEAD_DIM ** -0.5), NEG_INF)
        s2 = jnp.where(ok_n, _dot_nt(qh, kn[:, ds_]) * (SWA_HEAD_DIM ** -0.5), NEG_INF)
        sk = sink_ref[hq]
        m = jnp.maximum(jnp.maximum(jnp.max(s1, axis=-1, keepdims=True),
                                    jnp.max(s2, axis=-1, keepdims=True)), sk)
        p1 = jnp.exp(s1 - m)
        p2 = jnp.exp(s2 - m)
        den = jnp.sum(p1, axis=-1, keepdims=True) + jnp.sum(p2, axis=-1, keepdims=True) + jnp.exp(sk - m)
        outs.append((_dot(p1.astype(BF16), vc[:, ds_]) + _dot(p2.astype(BF16), vn[:, ds_])) / den)
    o = jnp.concatenate(outs, axis=1)
    for t in range(DEC_SEQ):
        o_ref[pl.ds(t * nseq + row0, SB), :] = o[t * SB:(t + 1) * SB, :]


def _swa_sample(z, sink, cache_k, cache_v, layer, mc, mn):
    n = z.shape[0]

    def whole(width, c0):
        return pl.BlockSpec((n, width), lambda j: (0, c0 // width))

    cache = pl.BlockSpec((None, SB, WINDOW, 128), lambda j: (layer, j, 0, 0))
    return pl.pallas_call(
        _swa_sample_kernel,
        grid=(DEC_BATCH // SB,),
        in_specs=[pl.BlockSpec(memory_space=pltpu.SMEM),
                  whole(512, C_SQ), whole(128, C_SK), whole(128, C_SV), cache, cache,
                  _const_spec(mc.shape), _const_spec(mn.shape)],
        out_specs=pl.BlockSpec((n, SWA_W), lambda j: (0, 0)),
        out_shape=jax.ShapeDtypeStruct((n, SWA_W), F32),
        compiler_params=_params(("arbitrary",)),
        name="swa_sample",
    )(sink, z, z, z, cache_k, cache_v, mc, mn)


def _merge_kernel(x_ref, oa_ref, ob_ref, oc_ref, zg_ref, wb_ref, wo_ref, y_ref):
    def branch(o_ref, r0, j):
        proj = _dot(o_ref[...].astype(BF16), wb_ref[r0:r0 + 512, :])
        return jax.nn.sigmoid(zg_ref[:, D_MODEL * j:D_MODEL * (j + 1)]) * proj

    mixed = branch(oa_ref, 0, 0) + branch(ob_ref, GLA_W, 1) + branch(oc_ref, GLA_W + RET_W, 2)
    y_ref[...] = x_ref[...] + _dot(mixed.astype(BF16), wo_ref[...])


def _merge(x, oa, ob, oc, z, layer, wb, wo, tm):
    n = x.shape[0]
    tok = lambda w: pl.BlockSpec((tm, w), lambda i: (i, 0))
    return pl.pallas_call(
        _merge_kernel,
        grid=(n // tm,),
        in_specs=[tok(D_MODEL), tok(512), tok(512), tok(512), tok(3 * D_MODEL),
                  _layer_spec((GLA_W + RET_W + SWA_W, D_MODEL), layer),
                  _layer_spec((D_MODEL, D_MODEL), layer)],
        out_specs=tok(D_MODEL),
        out_shape=jax.ShapeDtypeStruct((n, D_MODEL), F32),
        compiler_params=_params(("arbitrary",)),
        name="merge",
    )(x, oa, ob, oc, z, wb, wo)


FF_CHUNK = D_FF // 2


def _ffn_tail(x1, pe_ref, gp_ref, wpg_ref, wpp_ref, gf_ref, y_ref, final):
    hp = _rms(x1, gp_ref[...]).astype(BF16)
    x2 = x1 + jax.nn.sigmoid(_dot(hp, wpg_ref[...])) * _dot(pe_ref[...].astype(BF16), wpp_ref[...])
    y_ref[...] = _rms(x2, gf_ref[...]) if final else x2


def _ffn_prompt_kernel(x_ref, pe_ref, gn_ref, wi_ref, cw_ref, cb_ref, wd_ref, gp_ref, wpg_ref, wpp_ref,
                       gf_ref, y_ref, tail_ref, carry_scr, *, final):
    tm = x_ref.shape[0]

    @pl.when(pl.program_id(1) == 0)
    def _():
        carry_scr[...] = jnp.zeros_like(carry_scr)

    x = x_ref[...]
    hb = _rms(x, gn_ref[...]).astype(BF16)
    row = lax.broadcasted_iota(jnp.int32, (tm, 1), 0)
    acc = x
    for c0 in range(0, D_FF, FF_CHUNK):
        cs = slice(c0, c0 + FF_CHUNK)
        a = _dot(hb, wi_ref[:, c0:c0 + FF_CHUNK])
        bb = _dot(hb, wi_ref[:, D_FF + c0:D_FF + c0 + FF_CHUNK])
        p0 = carry_scr[6:7, cs]
        p1 = carry_scr[7:8, cs]
        a1 = jnp.where(row == 0, p1, pltpu.roll(a, 1, 0))
        a2 = jnp.where(row == 0, p0, jnp.where(row == 1, p1, pltpu.roll(a, 2, 0)))
        conv = cb_ref[:, cs] + cw_ref[0:1, cs] * a2 + cw_ref[1:2, cs] * a1 + cw_ref[2:3, cs] * a
        act = (jax.nn.gelu(conv) * bb).astype(BF16)
        acc = acc + _dot(act, wd_ref[c0:c0 + FF_CHUNK, :])
        carry_scr[:, cs] = a[tm - 8:tm, :]
        tail_ref[:, cs] = a[tm - 8:tm, :]
    _ffn_tail(acc, pe_ref, gp_ref, wpg_ref, wpp_ref, gf_ref, y_ref, final)


def _ffn_sample_kernel(x_ref, pe_ref, st_ref, gn_ref, wi_ref, cw_ref, cb_ref, wd_ref, gp_ref, wpg_ref,
                       wpp_ref, gf_ref, y_ref, tail_ref, *, final):
    ns = DEC_BATCH
    x = x_ref[...]
    hb = _rms(x, gn_ref[...]).astype(BF16)
    acc = x
    for c0 in range(0, D_FF, FF_CHUNK):
        cs = slice(c0, c0 + FF_CHUNK)
        a = _dot(hb, wi_ref[:, c0:c0 + FF_CHUNK])
        bb = _dot(hb, wi_ref[:, D_FF + c0:D_FF + c0 + FF_CHUNK])
        st0 = st_ref[0, :, cs]
        st1 = st_ref[1, :, cs]
        a1 = jnp.concatenate([st1, a[0:3 * ns]], axis=0)
        a2 = jnp.concatenate([st0, st1, a[0:2 * ns]], axis=0)
        conv = cb_ref[:, cs] + cw_ref[0:1, cs] * a2 + cw_ref[1:2, cs] * a1 + cw_ref[2:3, cs] * a
        act = (jax.nn.gelu(conv) * bb).astype(BF16)
        acc = acc + _dot(act, wd_ref[c0:c0 + FF_CHUNK, :])
        tail_ref[:, cs] = a[2 * ns:4 * ns, :]
    _ffn_tail(acc, pe_ref, gp_ref, wpg_ref, wpp_ref, gf_ref, y_ref, final)


def _ffn_weight_specs(layer):
    return [_layer_spec((1, D_MODEL), layer),
            _layer_spec((D_MODEL, 2 * D_FF), layer),
            _layer_spec((CONV_W, D_FF), layer),
            _layer_spec((1, D_FF), layer),
            _layer_spec((D_FF, D_MODEL), layer),
            _layer_spec((1, D_MODEL), layer),
            _layer_spec((D_MODEL, D_MODEL), layer),
            _layer_spec((D_PLE, D_MODEL), layer),
            _const_spec((1, D_MODEL))]


def _ffn_prompt(x, pe, layer, weights, nb, t, tm, final):
    nt = t // tm
    tok = lambda w: pl.BlockSpec((tm, w), lambda b, i: (b * nt + i, 0))
    return pl.pallas_call(
        functools.partial(_ffn_prompt_kernel, final=final),
        grid=(nb, nt),
        in_specs=[tok(D_MODEL),
                  pl.BlockSpec((None, tm, D_PLE), lambda b, i: (layer, b * nt + i, 0))]
        + _ffn_weight_specs(layer),
        out_specs=[tok(D_MODEL), pl.BlockSpec((8, D_FF), lambda b, i: (b, 0))],
        out_shape=[jax.ShapeDtypeStruct((nb * t, D_MODEL), F32),
                   jax.ShapeDtypeStruct((nb * 8, D_FF), F32)],
        scratch_shapes=[pltpu.VMEM((8, D_FF), F32)],
        compiler_params=_params(("arbitrary", "arbitrary")),
        name="ffn_prompt",
    )(x, pe, *weights)


def _ffn_sample(x, pe, st, layer, weights, final):
    n = x.shape[0]
    return pl.pallas_call(
        functools.partial(_ffn_sample_kernel, final=final),
        grid=(1,),
        in_specs=[pl.BlockSpec((n, D_MODEL), lambda i: (0, 0)),
                  pl.BlockSpec((None, n, D_PLE), lambda i: (layer, 0, 0)),
                  pl.BlockSpec((None, CONV_W - 1, DEC_BATCH, D_FF), lambda i: (layer, 0, 0, 0))]
        + _ffn_weight_specs(layer),
        out_specs=[pl.BlockSpec((n, D_MODEL), lambda i: (0, 0)),
                   pl.BlockSpec((n // 2, D_FF), lambda i: (0, 0))],
        out_shape=[jax.ShapeDtypeStruct((n, D_MODEL), F32),
                   jax.ShapeDtypeStruct((n // 2, D_FF), F32)],
        compiler_params=_params(("arbitrary",)),
        name="ffn_sample",
    )(x, pe, st, *weights)


def _rope_tables(pos, inv_freq):
    half = inv_freq.shape[0]
    ang = pos.astype(np.float64)[:, None] * inv_freq[None, :]
    c, s = np.cos(ang), np.sin(ang)
    rest = SWA_HEAD_DIM - 2 * half
    cos64 = np.concatenate([c, c, np.ones((pos.shape[0], rest))], axis=1)
    sin64 = np.concatenate([-s, s, np.zeros((pos.shape[0], rest))], axis=1)
    return (jnp.asarray(np.concatenate([cos64, cos64], axis=1), F32),
            jnp.asarray(np.concatenate([sin64, sin64], axis=1), F32))


def _pack_w_in(w_in):
    offs = np.cumsum((0,) + IN_SIZES)
    gq, gk, gv, gr, ga, rq, rk, rv, rg, sq, sk, sv, zg = [
        w_in[:, :, offs[i]:offs[i + 1]] for i in range(len(IN_SIZES))]
    ga = jnp.pad(ga, ((0, 0), (0, 0), (0, LANES - GLA_RANK)))
    return jnp.concatenate([zg, gq, gk, gv, gr, rq, rk, rv, rg, sq, sk, sv, ga], axis=2).astype(BF16)


def kernel(x_prompt, x_sample, state_gla, state_ret, cache_swa_k, cache_swa_v, state_conv, p_prompt,
           p_sample, norm_mix, w_in, w_gla_a, b_gla_a, gla_norm, ret_norm, swa_sink, w_branch, w_out,
           norm_ffn, w_ffn_in, conv_w, conv_b, w_ffn_out, norm_ple, w_ple_gate, w_ple_proj, norm_final):
    nb, t, _ = x_prompt.shape
    ns, ts, _ = x_sample.shape
    n_s = ns * ts

    w_main = _pack_w_in(w_in)
    wa = jnp.pad(w_gla_a, ((0, 0), (0, LANES - GLA_RANK), (0, 0))).astype(BF16)
    ba = b_gla_a[:, None, :]
    wb = w_branch.astype(BF16)
    wo = w_out.astype(BF16)
    ffn_weights = (norm_ffn[:, None, :], w_ffn_in.astype(BF16), conv_w, conv_b[:, None, :],
                   w_ffn_out.astype(BF16), norm_ple[:, None, :], w_ple_gate.astype(BF16),
                   w_ple_proj.astype(BF16), norm_final[None, :])
    g_mix = norm_mix[:, None, :]
    gn_gla = jnp.tile(gla_norm, (1, GLA_HEADS))[:, None, :]
    gn_ret = ret_norm.reshape(DEPTH, 1, RET_W)

    ret_freq = 1.0 / (RET_THETA ** np.linspace(0.0, 1.0, RET_DK // 2))
    swa_freq = 1.0 / (ROPE_THETA ** (np.arange(0, ROPE_DIM, 2, dtype=np.float64) / ROPE_DIM))
    pos_p = np.arange(t)
    pos_s = PAST_LEN + np.arange(n_s) // ns
    tabs_p = _rope_tables(pos_p, ret_freq) + _rope_tables(pos_p, swa_freq)
    tabs_s = _rope_tables(pos_s, ret_freq) + _rope_tables(pos_s, swa_freq)

    gm_np, mk_np = _gla_constants()
    d_np, qdec_np, kdec_np, adec_np = _ret_constants()
    hk_np, hv_np, hs_np = _head_masks()
    mix_consts = (jnp.asarray(gm_np, BF16), jnp.asarray(mk_np), jnp.asarray(d_np), jnp.asarray(qdec_np),
                  jnp.asarray(kdec_np), jnp.asarray(adec_np), jnp.asarray(hk_np, BF16), jnp.asarray(hv_np, BF16),
                  jnp.asarray(hs_np))
    log_gamma = np.log1p(-np.exp2(-5.0 - np.arange(RET_HEADS, dtype=np.float64)))
    la_ret = jnp.asarray(np.broadcast_to(np.repeat(log_gamma, RET_DK)[None, :], (n_s, RET_HEADS * RET_DK)), F32)
    mc_np, mn_np = _swa_sample_masks()
    swa_mc, swa_mn = jnp.asarray(mc_np), jnp.asarray(mn_np)
    rep_np, bd_np = _rec_sample_constants()
    rec_rep, rec_bd = jnp.asarray(rep_np, BF16), jnp.asarray(bd_np)

    xp = x_prompt.reshape(nb * t, D_MODEL)
    xs = x_sample.transpose(1, 0, 2).reshape(n_s, D_MODEL)
    pe_p = p_prompt.reshape(DEPTH, nb * t, D_PLE)
    pe_s = p_sample.transpose(0, 2, 1, 3).reshape(DEPTH, n_s, D_PLE)
    conv_st = state_conv.transpose(0, 2, 1, 3)
    ck = cache_swa_k.reshape(DEPTH, ns, WINDOW, SWA_KV_HEADS * SWA_HEAD_DIM)
    cv = cache_swa_v.reshape(DEPTH, ns, WINDOW, SWA_KV_HEADS * SWA_HEAD_DIM)

    outs = {k_: [] for k_ in ("gla_p", "gla_s", "ret_p", "ret_s", "k_p", "k_s", "v_p", "v_s", "conv_p", "conv_s")}
    for l in range(DEPTH):
        final = l == DEPTH - 1
        sink = swa_sink[l]

        z = _in_proj(xp, l, g_mix, w_main, wa, ba, tabs_p, 256)
        z3 = z.reshape(nb, t, Z_W)
        xp3, st_a, st_b = _mix_prompt(xp.reshape(nb, t, D_MODEL), z3, sink, l, mix_consts, gn_gla[l], gn_ret[l],
                                      wb, wo)
        xp, tail = _ffn_prompt(xp3.reshape(nb * t, D_MODEL), pe_p, l, ffn_weights, nb, t, 256, final)
        outs["gla_p"].append(st_a)
        outs["ret_p"].append(st_b)
        outs["k_p"].append(z3[:, t - WINDOW:, C_SK:C_SV].reshape(nb, WINDOW, SWA_KV_HEADS, SWA_HEAD_DIM))
        outs["v_p"].append(z3[:, t - WINDOW:, C_SV:C_LA].reshape(nb, WINDOW, SWA_KV_HEADS, SWA_HEAD_DIM))
        outs["conv_p"].append(tail.reshape(nb, 8, D_FF)[:, 8 - (CONV_W - 1):])

        zs = _in_proj(xs, l, g_mix, w_main, wa, ba, tabs_s, 256)
        oa, st_a = _rec_sample(zs, zs, state_gla, l, gn_gla[l], rec_rep, rec_bd,
                               C_GQ, C_GK, C_GV, C_GR, C_LA, "gla_sample")
        ob, st_b = _rec_sample(zs, la_ret, state_ret, l, gn_ret[l], rec_rep, rec_bd,
                               C_RQ, C_RK, C_RV, C_RG, 0, "ret_sample")
        oc = _swa_sample(zs, sink, ck, cv, l, swa_mc, swa_mn)
        xs = _merge(xs, oa, ob, oc, zs, l, wb, wo, 512)
        xs, tail_s = _ffn_sample(xs, pe_s, conv_st, l, ffn_weights, final)
        k_new = zs[:, C_SK:C_SV].reshape(ts, ns, SWA_KV_HEADS, SWA_HEAD_DIM).transpose(1, 0, 2, 3)
        v_new = zs[:, C_SV:C_LA].reshape(ts, ns, SWA_KV_HEADS, SWA_HEAD_DIM).transpose(1, 0, 2, 3)
        outs["gla_s"].append(st_a)
        outs["ret_s"].append(st_b)
        outs["k_s"].append(jnp.concatenate([cache_swa_k[l][:, ts:], k_new], axis=1))
        outs["v_s"].append(jnp.concatenate([cache_swa_v[l][:, ts:], v_new], axis=1))
        outs["conv_s"].append(tail_s.reshape(CONV_W - 1, ns, D_FF).transpose(1, 0, 2))

    y_prompt = xp.reshape(nb, t, D_MODEL)
    y_sample = xs.reshape(ts, ns, D_MODEL).transpose(1, 0, 2)
    st = {k_: jnp.stack(v_) for k_, v_ in outs.items()}
    return (y_prompt, y_sample, st["gla_p"], st["gla_s"], st["ret_p"], st["ret_s"],
            st["k_p"], st["k_s"], st["v_p"], st["v_s"], st["conv_p"], st["conv_s"])
```

```python
import functools
import math

import numpy as np
import jax
import jax.numpy as jnp
from jax import lax
from jax.experimental import pallas as pl
from jax.experimental.pallas import tpu as pltpu

F32 = jnp.float32
BF16 = jnp.bfloat16

D_MODEL = 1024
BATCH = 2
SEQ = 8192
DEPTH = 4
DEC_BATCH = 128
DEC_SEQ = 4
PAST_LEN = 8192
D_PLE = 256
GLA_HEADS = 4
GLA_DK = 64
GLA_DV = 128
GLA_RANK = 16
GLA_TAU = 16.0
RET_HEADS = 4
RET_DK = 64
RET_DV = 128
RET_THETA = 10000.0
SWA_Q_HEADS = 8
SWA_KV_HEADS = 2
SWA_HEAD_DIM = 64
SWA_GROUP = SWA_Q_HEADS // SWA_KV_HEADS
WINDOW = 128
ROPE_THETA = 500000.0
ROPE_DIM = SWA_HEAD_DIM // 4
D_FF = 2816
CONV_W = 3
N_BRANCH = 3
EPS = 1e-6
NEG_INF = -1e30

GLA_W = GLA_HEADS * GLA_DV
RET_W = RET_HEADS * RET_DV
SWA_W = SWA_Q_HEADS * SWA_HEAD_DIM
IN_SIZES = (GLA_HEADS * GLA_DK, GLA_HEADS * GLA_DK, GLA_W, GLA_W, GLA_RANK,
            RET_HEADS * RET_DK, RET_HEADS * RET_DK, RET_W, RET_W,
            SWA_W, SWA_KV_HEADS * SWA_HEAD_DIM, SWA_KV_HEADS * SWA_HEAD_DIM,
            N_BRANCH * D_MODEL)

LANES = 128
VMEM_LIMIT = 52 * 1024 * 1024

C_GQ = 0
C_GK = 256
C_GV = 512
C_GR = 1024
C_RQ = 1536
C_RK = 1792
C_RV = 2048
C_RG = 2560
C_SQ = 3072
C_SK = 3584
C_SV = 3712
C_ZG = 3840
C_LA = 6912
W_MAIN = 7040
Z_W = 7168

TT = 128
N_LEVELS = 7
SB = 16


def _dot(a, b):
    return jnp.dot(a, b, preferred_element_type=F32)


def _dot_nt(a, b):
    return lax.dot_general(a, b, (((1,), (1,)), ((), ())), preferred_element_type=F32)


def _dot_tn(a, b):
    return lax.dot_general(a, b, (((0,), (0,)), ((), ())), preferred_element_type=F32)


def _rms(x, g):
    return x * lax.rsqrt(jnp.mean(x * x, axis=-1, keepdims=True) + EPS) * g


def _layer_spec(shape, layer):
    nd = len(shape)
    return pl.BlockSpec((None,) + tuple(shape), lambda *_: (layer,) + (0,) * nd,
                        pipeline_mode=pl.Buffered(1))


def _const_spec(shape):
    nd = len(shape)
    return pl.BlockSpec(tuple(shape), lambda *_: (0,) * nd, pipeline_mode=pl.Buffered(1))


def _params(sem):
    return pltpu.CompilerParams(dimension_semantics=sem, vmem_limit_bytes=VMEM_LIMIT)


def _rope_block(x, cos, sin_s, first):
    half_mask, half = first
    xr = jnp.where(half_mask, pltpu.roll(x, LANES - half, 1), pltpu.roll(x, half, 1))
    return x * cos + xr * sin_s


def _in_kernel(x_ref, g_ref, w_ref, wa_ref, ba_ref, rc_ref, rs_ref, sc_ref, ss_ref, z_ref):
    hb = _rms(x_ref[...], g_ref[...]).astype(BF16)

    def mm(c0, c1):
        return _dot(hb, w_ref[:, c0:c1])

    lane = lax.broadcasted_iota(jnp.int32, (1, LANES), 1) % 64
    ret_first = (lane < RET_DK // 2, RET_DK // 2)
    swa_first = (lane < ROPE_DIM // 2, ROPE_DIM // 2)

    z_ref[:, C_GQ:C_GK] = mm(C_GQ, C_GK) * (GLA_DK ** -0.5)
    z_ref[:, C_GK:C_RQ] = mm(C_GK, C_RQ)
    rc, rs = rc_ref[...], rs_ref[...]
    rq = mm(C_RQ, C_RK)
    rk = mm(C_RK, C_RV)
    for j in range(2):
        sl = slice(LANES * j, LANES * (j + 1))
        z_ref[:, C_RQ + LANES * j:C_RQ + LANES * (j + 1)] = _rope_block(rq[:, sl], rc, rs, ret_first)
        z_ref[:, C_RK + LANES * j:C_RK + LANES * (j + 1)] = (
            _rope_block(rk[:, sl], rc, rs, ret_first) * (RET_DK ** -0.5))
    z_ref[:, C_RV:C_SQ] = mm(C_RV, C_SQ)
    sc, ss = sc_ref[...], ss_ref[...]
    sqk = mm(C_SQ, C_SV)
    for j in range(5):
        sl = slice(LANES * j, LANES * (j + 1))
        z_ref[:, C_SQ + LANES * j:C_SQ + LANES * (j + 1)] = _rope_block(sqk[:, sl], sc, ss, swa_first)
    z_ref[:, C_SV:C_ZG] = mm(C_SV, C_ZG)
    z_ref[:, C_ZG:C_LA] = mm(C_ZG, C_LA)
    ga = mm(C_LA, W_MAIN)
    xa = _dot(ga.astype(BF16), wa_ref[...]) + ba_ref[...]
    log_sig = jnp.minimum(xa, 0.0) - jnp.log1p(jnp.exp(-jnp.abs(xa)))
    z_ref[:, C_LA:Z_W] = log_sig * (1.0 / GLA_TAU)


def _in_proj(x, layer, g_mix, w_main, wa, ba, tabs, tm):
    n = x.shape[0]
    rc, rs, sc, ss = tabs
    nt = rc.shape[0] // tm
    tab = pl.BlockSpec((tm, LANES), lambda i: (i % nt, 0))
    return pl.pallas_call(
        _in_kernel,
        grid=(n // tm,),
        in_specs=[pl.BlockSpec((tm, D_MODEL), lambda i: (i, 0)),
                  _layer_spec((1, D_MODEL), layer),
                  _layer_spec((D_MODEL, W_MAIN), layer),
                  _layer_spec((LANES, 256), layer),
                  _layer_spec((1, 256), layer),
                  tab, tab, tab, tab],
        out_specs=pl.BlockSpec((tm, Z_W), lambda i: (i, 0)),
        out_shape=jax.ShapeDtypeStruct((n, Z_W), F32),
        compiler_params=_params(("arbitrary",)),
        name="in_proj",
    )(x, g_mix, w_main, wa, ba, rc, rs, sc, ss)


def _gla_constants():
    t = np.arange(TT)
    g = np.zeros((2 + N_LEVELS, TT, TT), np.float32)
    g[0] = (t[None, :] <= t[:, None])
    g[1] = (t[None, :] > t[:, None])
    m = np.zeros((1 + N_LEVELS, TT, TT), np.float32)
    m[0] = np.eye(TT)
    for lv in range(1, N_LEVELS + 1):
        bs, hf = 2 ** lv, 2 ** (lv - 1)
        bd = (t // bs) * bs + hf - 1
        upper = (t % bs) >= hf
        u = t[None, :]
        g[1 + lv] = np.where(upper[:, None], (u > bd[:, None]) & (u <= t[:, None]),
                             (u > t[:, None]) & (u <= bd[:, None]))
        same = (t[:, None] // bs) == (t[None, :] // bs)
        m[lv] = same & upper[:, None] & (~upper)[None, :]
    return g.reshape(-1, TT), np.tile(m, (1, 1, GLA_HEADS))


def _head_masks():
    hk = np.arange(GLA_HEADS * TT)[:, None] // TT == np.arange(GLA_HEADS * GLA_DK)[None, :] // GLA_DK
    hv = np.arange(GLA_HEADS * TT)[:, None] // TT == np.arange(GLA_W)[None, :] // GLA_DV
    hs = np.arange(GLA_HEADS * GLA_DK)[:, None] // GLA_DK == np.arange(GLA_W)[None, :] // GLA_DV
    return hk.astype(np.float32), hv.astype(np.float32), hs.astype(np.float32)


def _heads_blockdiag(x, mask):
    return jnp.concatenate([x] * GLA_HEADS, axis=0) * mask


def _col_vector(row):
    n = row.shape[1]
    eye = lax.broadcasted_iota(jnp.int32, (n, n), 0) == lax.broadcasted_iota(jnp.int32, (n, n), 1)
    return jnp.sum(jnp.where(eye, row, 0.0), axis=1, keepdims=True)


def _norm_gate(o, gn, gate):
    outs = [_rms(o[:, GLA_DV * h:GLA_DV * (h + 1)], gn[:, GLA_DV * h:GLA_DV * (h + 1)]) for h in range(GLA_HEADS)]
    return jnp.concatenate(outs, axis=1) * (gate * jax.nn.sigmoid(gate))


def _gla_tile(z_ref, gm_ref, mk_ref, hk_ref, hv_ref, hs_ref, gn_ref, st_ref):
    la = z_ref[:, C_LA:Z_W]
    la_hi = la.astype(BF16)
    la_lo = (la - la_hi.astype(F32)).astype(BF16)
    gm = gm_ref[...]
    ex = jnp.exp(_dot(gm, la_hi) + _dot(gm, la_lo))
    q = z_ref[:, C_GQ:C_GK]
    k = z_ref[:, C_GK:C_GV]
    e_b = ex[0:TT]
    a_row = e_b[TT - 1:TT, :]
    qd = (q * e_b).astype(BF16)
    kd = (k * ex[TT:2 * TT]).astype(BF16)
    qb = q.astype(BF16)
    kb = k.astype(BF16)
    ql = [(q * ex[(1 + lv) * TT:(2 + lv) * TT]).astype(BF16) for lv in range(1, N_LEVELS + 1)]
    kl = [(k * ex[(1 + lv) * TT:(2 + lv) * TT]).astype(BF16) for lv in range(1, N_LEVELS + 1)]
    hk = hk_ref[...]
    yield
    a = mk_ref[0] * _dot_nt(qb, _heads_blockdiag(kb, hk))
    for lv in range(N_LEVELS):
        yield
        a = a + mk_ref[lv + 1] * _dot_nt(ql[lv], _heads_blockdiag(kl[lv], hk))
    yield
    v = z_ref[:, C_GV:C_GR].astype(BF16)
    s = st_ref[...]
    o = _dot(a.astype(BF16), _heads_blockdiag(v, hv_ref[...])) + _dot(qd, s.astype(BF16))
    yield
    st_ref[...] = _col_vector(a_row) * s + hs_ref[...] * _dot_tn(kd, v)
    yield
    return _norm_gate(o, gn_ref[...], z_ref[:, C_GR:C_RQ])


def _ret_gammas():
    return [1.0 - 2.0 ** (-5.0 - h) for h in range(RET_HEADS)]


def _ret_constants():
    t = np.arange(TT, dtype=np.float64)
    d = np.zeros((RET_HEADS, TT, TT), np.float64)
    qdec = np.zeros((TT, RET_HEADS * RET_DK), np.float64)
    kdec = np.zeros((TT, RET_HEADS * RET_DK), np.float64)
    adec = np.zeros((RET_HEADS * RET_DK, LANES), np.float64)
    for h, gam in enumerate(_ret_gammas()):
        diff = t[:, None] - t[None, :]
        d[h] = np.where(diff >= 0, gam ** np.maximum(diff, 0.0), 0.0)
        qdec[:, h * RET_DK:(h + 1) * RET_DK] = (gam ** (t + 1.0))[:, None]
        kdec[:, h * RET_DK:(h + 1) * RET_DK] = (gam ** (TT - 1.0 - t))[:, None]
        adec[h * RET_DK:(h + 1) * RET_DK, :] = gam ** TT
    d = np.concatenate(list(d), axis=1)
    adec = np.tile(adec, (1, RET_W // LANES))
    return d.astype(np.float32), qdec.astype(np.float32), kdec.astype(np.float32), adec.astype(np.float32)


def _ret_tile(z_ref, d_ref, qdec_ref, kdec_ref, adec_ref, hk_ref, hv_ref, hs_ref, gn_ref, st_ref):
    q = z_ref[:, C_RQ:C_RK]
    k = z_ref[:, C_RK:C_RV]
    qd = (q * qdec_ref[...]).astype(BF16)
    kd = (k * kdec_ref[...]).astype(BF16)
    a = d_ref[...] * _dot_nt(q.astype(BF16), _heads_blockdiag(k.astype(BF16), hk_ref[...]))
    yield
    v = z_ref[:, C_RV:C_RG].astype(BF16)
    s = st_ref[...]
    o = _dot(a.astype(BF16), _heads_blockdiag(v, hv_ref[...])) + _dot(qd, s.astype(BF16))
    yield
    st_ref[...] = adec_ref[...] * s + hs_ref[...] * _dot_tn(kd, v)
    yield
    return _norm_gate(o, gn_ref[...], z_ref[:, C_RG:C_SQ])


def _rec_sample_constants():
    c = np.arange(SB * GLA_DK)
    rep = np.arange(GLA_DK)[:, None] == (c[None, :] % GLA_DK)
    bd = (np.arange(DEC_SEQ * SB)[:, None] % SB) == (c[None, :] // GLA_DK)
    return rep.astype(np.float32), bd.astype(np.float32)


def _rec_sample_kernel(q_ref, k_ref, v_ref, gr_ref, la_ref, s0_ref, gn_ref, rep_ref, bd_ref, o_ref, s1_ref):
    nseq = DEC_BATCH
    row0 = pl.multiple_of(pl.program_id(0) * SB, SB)

    def rows(ref, t):
        return ref[pl.ds(t * nseq + row0, SB), :]

    b = []
    for t in range(DEC_SEQ):
        la_t = rows(la_ref, t)
        b.append(la_t if t == 0 else b[-1] + la_t)
    q = [rows(q_ref, t) for t in range(DEC_SEQ)]
    k = [rows(k_ref, t) for t in range(DEC_SEQ)]
    v = [rows(v_ref, t) for t in range(DEC_SEQ)]
    qd = jnp.concatenate([q[t] * jnp.exp(b[t]) for t in range(DEC_SEQ)], axis=0).astype(BF16)
    kd = jnp.concatenate([k[t] * jnp.exp(b[-1] - b[t]) for t in range(DEC_SEQ)], axis=0).astype(BF16)
    vv = jnp.concatenate(v, axis=0).astype(BF16)
    a = jnp.exp(b[-1])
    a_hi = a.astype(BF16)
    a_r = a - a_hi.astype(F32)
    a_mid = a_r.astype(BF16)
    a_lo = (a_r - a_mid.astype(F32)).astype(BF16)
    rep = rep_ref[...]
    bd = bd_ref[...]
    ones = jnp.ones((SB, GLA_DV), BF16)

    def expand(x, mask):
        return (_dot(x, rep) * mask).astype(BF16)

    oi = []
    for h in range(GLA_HEADS):
        hs = slice(GLA_DK * h, GLA_DK * (h + 1))
        vs = slice(GLA_DV * h, GLA_DV * (h + 1))
        s0 = s0_ref[:, h].reshape(SB * GLA_DK, GLA_DV)
        oi.append(_dot(expand(qd[:, hs], bd), s0.astype(BF16)))
        a_col = (_dot_tn(expand(a_hi[:, hs], bd[0:SB]), ones) + _dot_tn(expand(a_mid[:, hs], bd[0:SB]), ones)
                 + _dot_tn(expand(a_lo[:, hs], bd[0:SB]), ones))
        s1 = a_col * s0 + _dot_tn(expand(kd[:, hs], bd), vv[:, vs])
        s1_ref[:, h] = s1.reshape(SB, GLA_DK, GLA_DV)
    oi = jnp.concatenate(oi, axis=1)

    gn = gn_ref[...]
    for t in range(DEC_SEQ):
        o = oi[t * SB:(t + 1) * SB, :]
        for s in range(t + 1):
            p = q[t] * k[s] * jnp.exp(b[t] - b[s])
            parts = []
            for h in range(GLA_HEADS):
                hs = slice(GLA_DK * h, GLA_DK * (h + 1))
                vs = slice(GLA_DV * h, GLA_DV * (h + 1))
                parts.append(jnp.sum(p[:, hs], axis=1, keepdims=True) * v[s][:, vs])
            o = o + jnp.concatenate(parts, axis=1)
        gate = rows(gr_ref, t)
        outs = []
        for h in range(GLA_HEADS):
            vs = slice(GLA_DV * h, GLA_DV * (h + 1))
            outs.append(_rms(o[:, vs], gn[:, vs]))
        o_ref[pl.ds(t * nseq + row0, SB), :] = jnp.concatenate(outs, axis=1) * (gate * jax.nn.sigmoid(gate))


def _rec_sample(z, la, s0, layer, gn, rep, bd, cq, ck, cv, cg, c_la, name):
    n = z.shape[0]

    def whole(arr, width, c0):
        return pl.BlockSpec((arr.shape[0], width), lambda j: (0, c0 // width))

    st_in = pl.BlockSpec((None, SB, GLA_HEADS, GLA_DK, GLA_DV), lambda j: (layer, j, 0, 0, 0))
    st_out = pl.BlockSpec((SB, GLA_HEADS, GLA_DK, GLA_DV), lambda j: (j, 0, 0, 0))
    return pl.pallas_call(
        _rec_sample_kernel,
        grid=(DEC_BATCH // SB,),
        in_specs=[whole(z, 256, cq), whole(z, 256, ck), whole(z, 512, cv), whole(z, 512, cg),
                  whole(la, 256, c_la), st_in, _const_spec((1, GLA_W)),
                  _const_spec(rep.shape), _const_spec(bd.shape)],
        out_specs=[pl.BlockSpec((n, GLA_W), lambda j: (0, 0)), st_out],
        out_shape=[jax.ShapeDtypeStruct((n, GLA_W), F32),
                   jax.ShapeDtypeStruct(s0.shape[1:], F32)],
        compiler_params=_params(("arbitrary",)),
        name=name,
    )(z, z, z, z, la, s0, gn, rep, bd)


def _swa_tile(z_ref, sink_ref, kp_ref, vp_ref, first_col):
    kc = z_ref[:, C_SK:C_SV].astype(BF16)
    vc = z_ref[:, C_SV:C_ZG].astype(BF16)
    kband = jnp.concatenate([kp_ref[...], kc], axis=0)
    vband = jnp.concatenate([vp_ref[...], vc], axis=0)
    rows = SWA_GROUP * WINDOW
    r = lax.broadcasted_iota(jnp.int32, (rows, 2 * WINDOW), 0) & (WINDOW - 1)
    c = lax.broadcasted_iota(jnp.int32, (rows, 2 * WINDOW), 1)
    valid = (c > r) & (c <= r + WINDOW) & (c >= first_col)
    groups = range(SWA_KV_HEADS)
    dsl = [slice(SWA_HEAD_DIM * g, SWA_HEAD_DIM * (g + 1)) for g in groups]
    s, sk = [], []
    for g in groups:
        heads = range(SWA_GROUP * g, SWA_GROUP * (g + 1))
        q = jnp.concatenate(
            [z_ref[:, C_SQ + SWA_HEAD_DIM * hq:C_SQ + SWA_HEAD_DIM * (hq + 1)] for hq in heads], axis=0)
        sk.append(jnp.concatenate([jnp.full((WINDOW, 1), sink_ref[hq], F32) for hq in heads], axis=0))
        s.append(_dot_nt(q.astype(BF16), kband[:, dsl[g]]) * (SWA_HEAD_DIM ** -0.5))
    yield
    m, p = [], []
    for g in groups:
        sg = jnp.where(valid, s[g], NEG_INF)
        m.append(jnp.maximum(jnp.max(sg, axis=-1, keepdims=True), sk[g]))
        p.append(jnp.exp(sg - m[g]))
    yield
    outs = []
    for g in groups:
        den = jnp.sum(p[g], axis=-1, keepdims=True) + jnp.exp(sk[g] - m[g])
        o = _dot(p[g].astype(BF16), vband[:, dsl[g]]) / den
        outs += [o[WINDOW * j:WINDOW * (j + 1)] for j in range(SWA_GROUP)]
    kp_ref[...] = kc
    vp_ref[...] = vc
    yield
    return jnp.concatenate(outs, axis=1)


def _gate_stages(z_ref, start):
    for _ in range(start):
        yield
    gates = []
    for j in range(N_BRANCH):
        gates.append(jax.nn.sigmoid(z_ref[:, C_ZG + D_MODEL * j:C_ZG + D_MODEL * (j + 1)]))
        yield
    return gates


def _interleave(stage_fns):
    results = [None] * len(stage_fns)
    live = list(enumerate(stage_fns))
    while live:
        still = []
        for idx, gen in live:
            try:
                next(gen)
                still.append((idx, gen))
            except StopIteration as stop:
                results[idx] = stop.value
        live = still
    return results


def _mix_kernel(sink_ref, x_ref, z_ref, gm_ref, mk_ref, d_ref, qdec_ref, kdec_ref, adec_ref, hk_ref, hv_ref,
                hs_ref, gna_ref, gnb_ref, wb_ref, wo_ref, y_ref, sta_ref, stb_ref, kv_ref,
                sa_scr, sb_scr, kp_scr, vp_scr):
    i = pl.program_id(0)

    @pl.when(i == 0)
    def _():
        sa_scr[...] = jnp.zeros_like(sa_scr)
        sb_scr[...] = jnp.zeros_like(sb_scr)
        kp_scr[...] = jnp.zeros_like(kp_scr)
        vp_scr[...] = jnp.zeros_like(vp_scr)

    first_col = jnp.where(i > 0, 0, WINDOW)
    nb = x_ref.shape[0]
    stages = []
    for b in range(nb):
        zb = z_ref.at[b]
        stages += [_gla_tile(zb, gm_ref, mk_ref, hk_ref, hv_ref, hs_ref, gna_ref, sa_scr.at[b]),
                   _swa_tile(zb, sink_ref, kp_scr.at[b], vp_scr.at[b], first_col),
                   _ret_tile(zb, d_ref, qdec_ref, kdec_ref, adec_ref, hk_ref, hv_ref, hs_ref, gnb_ref, sb_scr.at[b]),
                   _gate_stages(zb, 4 + N_BRANCH * b)]
    branch = _interleave(stages)
    for b in range(nb):
        for h in range(GLA_HEADS):
            sta_ref[b, h] = sa_scr[b, GLA_DK * h:GLA_DK * (h + 1), GLA_DV * h:GLA_DV * (h + 1)]
            stb_ref[b, h] = sb_scr[b, RET_DK * h:RET_DK * (h + 1), RET_DV * h:RET_DV * (h + 1)]

    def stacked(j):
        return jnp.concatenate([branch[4 * b + j] for b in range(nb)], axis=0).astype(BF16)

    proj = [_dot(stacked(0), wb_ref[0:GLA_W, :]), _dot(stacked(2), wb_ref[GLA_W:GLA_W + RET_W, :]),
            _dot(stacked(1), wb_ref[GLA_W + RET_W:, :])]
    mixed = []
    for b in range(nb):
        rows = slice(TT * b, TT * (b + 1))
        mixed.append(sum(branch[4 * b + 3][j] * proj[j][rows] for j in range(N_BRANCH)))
    out = _dot(jnp.concatenate(mixed, axis=0).astype(BF16), wo_ref[...])
    for b in range(nb):
        y_ref[b] = x_ref[b] + out[TT * b:TT * (b + 1)]
        kv_ref[b] = z_ref[b, :, C_SK:C_ZG]


def _mix_prompt(x, z, sink, layer, consts, gna, gnb, wb, wo):
    nb, t, _ = x.shape
    st_spec = pl.BlockSpec((nb, GLA_HEADS, GLA_DK, GLA_DV), lambda i: (0, 0, 0, 0))
    st_shape = jax.ShapeDtypeStruct((nb, GLA_HEADS, GLA_DK, GLA_DV), F32)
    return pl.pallas_call(
        _mix_kernel,
        grid=(t // TT,),
        in_specs=[pl.BlockSpec(memory_space=pltpu.SMEM),
                  pl.BlockSpec((nb, TT, D_MODEL), lambda i: (0, i, 0)),
                  pl.BlockSpec((nb, TT, Z_W), lambda i: (0, i, 0))]
        + [_const_spec(c.shape) for c in consts]
        + [_const_spec((1, GLA_W)), _const_spec((1, RET_W)),
           _layer_spec((GLA_W + RET_W + SWA_W, D_MODEL), layer),
           _layer_spec((D_MODEL, D_MODEL), layer)],
        out_specs=[pl.BlockSpec((nb, TT, D_MODEL), lambda i: (0, i, 0)), st_spec, st_spec,
                   pl.BlockSpec((nb, WINDOW, 2 * LANES), lambda i: (0, 0, 0))],
        out_shape=[jax.ShapeDtypeStruct((nb, t, D_MODEL), F32), st_shape, st_shape,
                   jax.ShapeDtypeStruct((nb, WINDOW, 2 * LANES), F32)],
        scratch_shapes=[pltpu.VMEM((nb, GLA_HEADS * GLA_DK, GLA_W), F32),
                        pltpu.VMEM((nb, RET_HEADS * RET_DK, RET_W), F32),
                        pltpu.VMEM((nb, WINDOW, 128), BF16), pltpu.VMEM((nb, WINDOW, 128), BF16)],
        compiler_params=_params(("arbitrary",)),
        name="mix_prompt",
    )(sink, x, z, *consts, gna, gnb, wb, wo)


def _swa_sample_masks():
    r = np.arange(DEC_SEQ * SB)
    rt, rb = r // SB, r % SB
    c = np.arange(SB * WINDOW)
    cb, cj = c // WINDOW, c % WINDOW
    m_cache = (rb[:, None] == cb[None, :]) & (cj[None, :] > rt[:, None])
    m_new = (rb[:, None] == rb[None, :]) & (rt[None, :] <= rt[:, None])
    return m_cache.astype(np.float32), m_new.astype(np.float32)


def _swa_sample_kernel(sink_ref, q_ref, k_ref, v_ref, kc_ref, vc_ref, mc_ref, mn_ref, o_ref):
    nseq = DEC_BATCH
    row0 = pl.multiple_of(pl.program_id(0) * SB, SB)

    def gather(ref):
        return jnp.concatenate([ref[pl.ds(t * nseq + row0, SB), :] for t in range(DEC_SEQ)], axis=0)

    q = gather(q_ref).astype(BF16)
    kn = gather(k_ref).astype(BF16)
    vn = gather(v_ref).astype(BF16)
    kc = kc_ref[...].reshape(SB * WINDOW, 128).astype(BF16)
    vc = vc_ref[...].reshape(SB * WINDOW, 128).astype(BF16)
    ok_c = mc_ref[...] > 0.0
    ok_n = mn_ref[...] > 0.0
    outs = []
    for hq in range(SWA_Q_HEADS):
        g = hq // SWA_GROUP
        ds_ = slice(SWA_HEAD_DIM * g, SWA_HEAD_DIM * (g + 1))
        qh = q[:, SWA_HEAD_DIM * hq:SWA_HEAD_DIM * (hq + 1)]
        s1 = jnp.where(ok_c, _dot_nt(qh, kc[:, ds_]) * (SWA_HEAD_DIM ** -0.5), NEG_INF)
        s2 = jnp.where(ok_n, _dot_nt(qh, kn[:, ds_]) * (SWA_HEAD_DIM ** -0.5), NEG_INF)
        sk = sink_ref[hq]
        m = jnp.maximum(jnp.maximum(jnp.max(s1, axis=-1, keepdims=True),
                                    jnp.max(s2, axis=-1, keepdims=True)), sk)
        p1 = jnp.exp(s1 - m)
        p2 = jnp.exp(s2 - m)
        den = jnp.sum(p1, axis=-1, keepdims=True) + jnp.sum(p2, axis=-1, keepdims=True) + jnp.exp(sk - m)
        outs.append((_dot(p1.astype(BF16), vc[:, ds_]) + _dot(p2.astype(BF16), vn[:, ds_])) / den)
    o = jnp.concatenate(outs, axis=1)
    for t in range(DEC_SEQ):
        o_ref[pl.ds(t * nseq + row0, SB), :] = o[t * SB:(t + 1) * SB, :]


def _swa_sample(z, sink, cache_k, cache_v, layer, mc, mn):
    n = z.shape[0]

    def whole(width, c0):
        return pl.BlockSpec((n, width), lambda j: (0, c0 // width))

    cache = pl.BlockSpec((None, SB, WINDOW, 128), lambda j: (layer, j, 0, 0))
    return pl.pallas_call(
        _swa_sample_kernel,
        grid=(DEC_BATCH // SB,),
        in_specs=[pl.BlockSpec(memory_space=pltpu.SMEM),
                  whole(512, C_SQ), whole(128, C_SK), whole(128, C_SV), cache, cache,
                  _const_spec(mc.shape), _const_spec(mn.shape)],
        out_specs=pl.BlockSpec((n, SWA_W), lambda j: (0, 0)),
        out_shape=jax.ShapeDtypeStruct((n, SWA_W), F32),
        compiler_params=_params(("arbitrary",)),
        name="swa_sample",
    )(sink, z, z, z, cache_k, cache_v, mc, mn)


def _merge_kernel(x_ref, oa_ref, ob_ref, oc_ref, z_ref, wb_ref, wo_ref, y_ref):
    def branch(o_ref, r0, j):
        proj = _dot(o_ref[...].astype(BF16), wb_ref[r0:r0 + 512, :])
        return jax.nn.sigmoid(z_ref[:, C_ZG + D_MODEL * j:C_ZG + D_MODEL * (j + 1)]) * proj

    mixed = branch(oa_ref, 0, 0) + branch(ob_ref, GLA_W, 1) + branch(oc_ref, GLA_W + RET_W, 2)
    y_ref[...] = x_ref[...] + _dot(mixed.astype(BF16), wo_ref[...])


def _merge(x, oa, ob, oc, z, layer, wb, wo, tm):
    n = x.shape[0]
    tok = lambda w: pl.BlockSpec((tm, w), lambda i: (i, 0))
    return pl.pallas_call(
        _merge_kernel,
        grid=(n // tm,),
        in_specs=[tok(D_MODEL), tok(512), tok(512), tok(512), tok(Z_W),
                  _layer_spec((GLA_W + RET_W + SWA_W, D_MODEL), layer),
                  _layer_spec((D_MODEL, D_MODEL), layer)],
        out_specs=tok(D_MODEL),
        out_shape=jax.ShapeDtypeStruct((n, D_MODEL), F32),
        compiler_params=_params(("arbitrary",)),
        name="merge",
    )(x, oa, ob, oc, z, wb, wo)


FF_CHUNK = D_FF // 2


def _ffn_tail(x1, pe_ref, gp_ref, wpg_ref, wpp_ref, gf_ref, y_ref, final):
    hp = _rms(x1, gp_ref[...]).astype(BF16)
    x2 = x1 + jax.nn.sigmoid(_dot(hp, wpg_ref[...])) * _dot(pe_ref[...].astype(BF16), wpp_ref[...])
    y_ref[...] = _rms(x2, gf_ref[...]) if final else x2


def _ffn_prompt_kernel(x_ref, pe_ref, gn_ref, wi_ref, cw_ref, cb_ref, wd_ref, gp_ref, wpg_ref, wpp_ref,
                       gf_ref, y_ref, tail_ref, carry_scr, *, final):
    tm = x_ref.shape[0]

    @pl.when(pl.program_id(1) == 0)
    def _():
        carry_scr[...] = jnp.zeros_like(carry_scr)

    x = x_ref[...]
    hb = _rms(x, gn_ref[...]).astype(BF16)
    row = lax.broadcasted_iota(jnp.int32, (tm, 1), 0)
    acc = x
    for c0 in range(0, D_FF, FF_CHUNK):
        cs = slice(c0, c0 + FF_CHUNK)
        a = _dot(hb, wi_ref[:, c0:c0 + FF_CHUNK])
        bb = _dot(hb, wi_ref[:, D_FF + c0:D_FF + c0 + FF_CHUNK])
        p0 = carry_scr[6:7, cs]
        p1 = carry_scr[7:8, cs]
        a1 = jnp.where(row == 0, p1, pltpu.roll(a, 1, 0))
        a2 = jnp.where(row == 0, p0, jnp.where(row == 1, p1, pltpu.roll(a, 2, 0)))
        conv = cb_ref[:, cs] + cw_ref[0:1, cs] * a2 + cw_ref[1:2, cs] * a1 + cw_ref[2:3, cs] * a
        act = (jax.nn.gelu(conv) * bb).astype(BF16)
        acc = acc + _dot(act, wd_ref[c0:c0 + FF_CHUNK, :])
        carry_scr[:, cs] = a[tm - 8:tm, :]
        tail_ref[:, cs] = a[tm - 8:tm, :]
    _ffn_tail(acc, pe_ref, gp_ref, wpg_ref, wpp_ref, gf_ref, y_ref, final)


def _ffn_sample_kernel(x_ref, pe_ref, st_ref, gn_ref, wi_ref, cw_ref, cb_ref, wd_ref, gp_ref, wpg_ref,
                       wpp_ref, gf_ref, y_ref, tail_ref, *, final):
    ns = DEC_BATCH
    x = x_ref[...]
    hb = _rms(x, gn_ref[...]).astype(BF16)
    acc = x
    for c0 in range(0, D_FF, FF_CHUNK):
        cs = slice(c0, c0 + FF_CHUNK)
        a = _dot(hb, wi_ref[:, c0:c0 + FF_CHUNK])
        bb = _dot(hb, wi_ref[:, D_FF + c0:D_FF + c0 + FF_CHUNK])
        st0 = st_ref[0, :, cs]
        st1 = st_ref[1, :, cs]
        a1 = jnp.concatenate([st1, a[0:3 * ns]], axis=0)
        a2 = jnp.concatenate([st0, st1, a[0:2 * ns]], axis=0)
        conv = cb_ref[:, cs] + cw_ref[0:1, cs] * a2 + cw_ref[1:2, cs] * a1 + cw_ref[2:3, cs] * a
        act = (jax.nn.gelu(conv) * bb).astype(BF16)
        acc = acc + _dot(act, wd_ref[c0:c0 + FF_CHUNK, :])
        tail_ref[:, cs] = a[2 * ns:4 * ns, :]
    _ffn_tail(acc, pe_ref, gp_ref, wpg_ref, wpp_ref, gf_ref, y_ref, final)


def _ffn_weight_specs(layer):
    return [_layer_spec((1, D_MODEL), layer),
            _layer_spec((D_MODEL, 2 * D_FF), layer),
            _layer_spec((CONV_W, D_FF), layer),
            _layer_spec((1, D_FF), layer),
            _layer_spec((D_FF, D_MODEL), layer),
            _layer_spec((1, D_MODEL), layer),
            _layer_spec((D_MODEL, D_MODEL), layer),
            _layer_spec((D_PLE, D_MODEL), layer),
            _const_spec((1, D_MODEL))]


def _ffn_prompt(x, pe, layer, weights, nb, t, tm, final):
    nt = t // tm
    tok = lambda w: pl.BlockSpec((tm, w), lambda b, i: (b * nt + i, 0))
    return pl.pallas_call(
        functools.partial(_ffn_prompt_kernel, final=final),
        grid=(nb, nt),
        in_specs=[tok(D_MODEL),
                  pl.BlockSpec((None, tm, D_PLE), lambda b, i: (layer, b * nt + i, 0))]
        + _ffn_weight_specs(layer),
        out_specs=[tok(D_MODEL), pl.BlockSpec((8, D_FF), lambda b, i: (b, 0))],
        out_shape=[jax.ShapeDtypeStruct((nb * t, D_MODEL), F32),
                   jax.ShapeDtypeStruct((nb * 8, D_FF), F32)],
        scratch_shapes=[pltpu.VMEM((8, D_FF), F32)],
        compiler_params=_params(("arbitrary", "arbitrary")),
        name="ffn_prompt",
    )(x, pe, *weights)


def _ffn_sample(x, pe, st, layer, weights, final):
    n = x.shape[0]
    return pl.pallas_call(
        functools.partial(_ffn_sample_kernel, final=final),
        grid=(1,),
        in_specs=[pl.BlockSpec((n, D_MODEL), lambda i: (0, 0)),
                  pl.BlockSpec((None, n, D_PLE), lambda i: (layer, 0, 0)),
                  pl.BlockSpec((None, CONV_W - 1, DEC_BATCH, D_FF), lambda i: (layer, 0, 0, 0))]
        + _ffn_weight_specs(layer),
        out_specs=[pl.BlockSpec((n, D_MODEL), lambda i: (0, 0)),
                   pl.BlockSpec((n // 2, D_FF), lambda i: (0, 0))],
        out_shape=[jax.ShapeDtypeStruct((n, D_MODEL), F32),
                   jax.ShapeDtypeStruct((n // 2, D_FF), F32)],
        compiler_params=_params(("arbitrary",)),
        name="ffn_sample",
    )(x, pe, st, *weights)


def _rope_tables(pos, inv_freq):
    half = inv_freq.shape[0]
    ang = pos.astype(np.float64)[:, None] * inv_freq[None, :]
    c, s = np.cos(ang), np.sin(ang)
    rest = SWA_HEAD_DIM - 2 * half
    cos64 = np.concatenate([c, c, np.ones((pos.shape[0], rest))], axis=1)
    sin64 = np.concatenate([-s, s, np.zeros((pos.shape[0], rest))], axis=1)
    return (jnp.asarray(np.concatenate([cos64, cos64], axis=1), F32),
            jnp.asarray(np.concatenate([sin64, sin64], axis=1), F32))


def _pack_w_in(w_in):
    ga0 = sum(IN_SIZES[:4])
    ga = jnp.pad(w_in[:, :, ga0:ga0 + GLA_RANK], ((0, 0), (0, 0), (0, LANES - GLA_RANK)))
    return jnp.concatenate([w_in[:, :, :ga0], w_in[:, :, ga0 + GLA_RANK:], ga], axis=2).astype(BF16)


def kernel(x_prompt, x_sample, state_gla, state_ret, cache_swa_k, cache_swa_v, state_conv, p_prompt,
           p_sample, norm_mix, w_in, w_gla_a, b_gla_a, gla_norm, ret_norm, swa_sink, w_branch, w_out,
           norm_ffn, w_ffn_in, conv_w, conv_b, w_ffn_out, norm_ple, w_ple_gate, w_ple_proj, norm_final):
    nb, t, _ = x_prompt.shape
    ns, ts, _ = x_sample.shape
    n_s = ns * ts

    w_main = _pack_w_in(w_in)
    wa = jnp.pad(w_gla_a, ((0, 0), (0, LANES - GLA_RANK), (0, 0))).astype(BF16)
    ba = b_gla_a[:, None, :]
    wb = w_branch.astype(BF16)
    wo = w_out.astype(BF16)
    ffn_weights = (norm_ffn[:, None, :], w_ffn_in.astype(BF16), conv_w, conv_b[:, None, :],
                   w_ffn_out.astype(BF16), norm_ple[:, None, :], w_ple_gate.astype(BF16),
                   w_ple_proj.astype(BF16), norm_final[None, :])
    g_mix = norm_mix[:, None, :]
    gn_gla = jnp.tile(gla_norm, (1, GLA_HEADS))[:, None, :]
    gn_ret = ret_norm.reshape(DEPTH, 1, RET_W)

    ret_freq = 1.0 / (RET_THETA ** np.linspace(0.0, 1.0, RET_DK // 2))
    swa_freq = 1.0 / (ROPE_THETA ** (np.arange(0, ROPE_DIM, 2, dtype=np.float64) / ROPE_DIM))
    pos_p = np.arange(t)
    pos_s = PAST_LEN + np.arange(n_s) // ns
    tabs_p = _rope_tables(pos_p, ret_freq) + _rope_tables(pos_p, swa_freq)
    tabs_s = _rope_tables(pos_s, ret_freq) + _rope_tables(pos_s, swa_freq)

    gm_np, mk_np = _gla_constants()
    d_np, qdec_np, kdec_np, adec_np = _ret_constants()
    hk_np, hv_np, hs_np = _head_masks()
    mix_consts = (jnp.asarray(gm_np, BF16), jnp.asarray(mk_np), jnp.asarray(d_np), jnp.asarray(qdec_np),
                  jnp.asarray(kdec_np), jnp.asarray(adec_np), jnp.asarray(hk_np, BF16), jnp.asarray(hv_np, BF16),
                  jnp.asarray(hs_np))
    log_gamma = np.log1p(-np.exp2(-5.0 - np.arange(RET_HEADS, dtype=np.float64)))
    la_ret = jnp.asarray(np.broadcast_to(np.repeat(log_gamma, RET_DK)[None, :], (n_s, RET_HEADS * RET_DK)), F32)
    mc_np, mn_np = _swa_sample_masks()
    swa_mc, swa_mn = jnp.asarray(mc_np), jnp.asarray(mn_np)
    rep_np, bd_np = _rec_sample_constants()
    rec_rep, rec_bd = jnp.asarray(rep_np, BF16), jnp.asarray(bd_np)

    xp = x_prompt.reshape(nb * t, D_MODEL)
    xs = x_sample.transpose(1, 0, 2).reshape(n_s, D_MODEL)
    pe_p = p_prompt.reshape(DEPTH, nb * t, D_PLE)
    pe_s = p_sample.transpose(0, 2, 1, 3).reshape(DEPTH, n_s, D_PLE)
    conv_st = state_conv.transpose(0, 2, 1, 3)
    ck = cache_swa_k.reshape(DEPTH, ns, WINDOW, SWA_KV_HEADS * SWA_HEAD_DIM)
    cv = cache_swa_v.reshape(DEPTH, ns, WINDOW, SWA_KV_HEADS * SWA_HEAD_DIM)

    outs = {k_: [] for k_ in ("gla_p", "gla_s", "ret_p", "ret_s", "kv_p", "kv_s", "conv_p", "conv_s")}
    for l in range(DEPTH):
        final = l == DEPTH - 1
        sink = swa_sink[l]

        z = _in_proj(xp, l, g_mix, w_main, wa, ba, tabs_p, 256)
        z3 = z.reshape(nb, t, Z_W)
        xp3, st_a, st_b, kv_tail = _mix_prompt(xp.reshape(nb, t, D_MODEL), z3, sink, l, mix_consts, gn_gla[l],
                                               gn_ret[l], wb, wo)
        xp, tail = _ffn_prompt(xp3.reshape(nb * t, D_MODEL), pe_p, l, ffn_weights, nb, t, 256, final)
        outs["gla_p"].append(st_a)
        outs["ret_p"].append(st_b)
        outs["kv_p"].append(kv_tail)
        outs["conv_p"].append(tail.reshape(nb, 8, D_FF)[:, 8 - (CONV_W - 1):])

        zs = _in_proj(xs, l, g_mix, w_main, wa, ba, tabs_s, 256)
        oa, st_a = _rec_sample(zs, zs, state_gla, l, gn_gla[l], rec_rep, rec_bd,
                               C_GQ, C_GK, C_GV, C_GR, C_LA, "gla_sample")
        ob, st_b = _rec_sample(zs, la_ret, state_ret, l, gn_ret[l], rec_rep, rec_bd,
                               C_RQ, C_RK, C_RV, C_RG, 0, "ret_sample")
        oc = _swa_sample(zs, sink, ck, cv, l, swa_mc, swa_mn)
        xs = _merge(xs, oa, ob, oc, zs, l, wb, wo, 512)
        xs, tail_s = _ffn_sample(xs, pe_s, conv_st, l, ffn_weights, final)
        outs["gla_s"].append(st_a)
        outs["ret_s"].append(st_b)
        outs["kv_s"].append(zs[:, C_SK:C_ZG])
        outs["conv_s"].append(tail_s.reshape(CONV_W - 1, ns, D_FF).transpose(1, 0, 2))

    y_prompt = xp.reshape(nb, t, D_MODEL)
    y_sample = xs.reshape(ts, ns, D_MODEL).transpose(1, 0, 2)
    st = {k_: jnp.stack(v_) for k_, v_ in outs.items()}

    def kv_heads(a):
        return a.reshape(a.shape[:-1] + (SWA_KV_HEADS, SWA_HEAD_DIM))

    k_p = kv_heads(st["kv_p"][..., :LANES])
    v_p = kv_heads(st["kv_p"][..., LANES:])
    kv_new = st["kv_s"].reshape(DEPTH, ts, ns, 2 * LANES).transpose(0, 2, 1, 3)
    k_s = jnp.concatenate([cache_swa_k[:, :, ts:], kv_heads(kv_new[..., :LANES])], axis=2)
    v_s = jnp.concatenate([cache_swa_v[:, :, ts:], kv_heads(kv_new[..., LANES:])], axis=2)
    return (y_prompt, y_sample, st["gla_p"], st["gla_s"], st["ret_p"], st["ret_s"],
            k_p, k_s, v_p, v_s, st["conv_p"], st["conv_s"])
```

```python
import functools
import math

import numpy as np
import jax
import jax.numpy as jnp
from jax import lax
from jax.experimental import pallas as pl
from jax.experimental.pallas import tpu as pltpu

F32 = jnp.float32
BF16 = jnp.bfloat16

D_MODEL = 1024
BATCH = 2
SEQ = 8192
DEPTH = 4
DEC_BATCH = 128
DEC_SEQ = 4
PAST_LEN = 8192
D_PLE = 256
GLA_HEADS = 4
GLA_DK = 64
GLA_DV = 128
GLA_RANK = 16
GLA_TAU = 16.0
RET_HEADS = 4
RET_DK = 64
RET_DV = 128
RET_THETA = 10000.0
SWA_Q_HEADS = 8
SWA_KV_HEADS = 2
SWA_HEAD_DIM = 64
SWA_GROUP = SWA_Q_HEADS // SWA_KV_HEADS
WINDOW = 128
ROPE_THETA = 500000.0
ROPE_DIM = SWA_HEAD_DIM // 4
D_FF = 2816
CONV_W = 3
N_BRANCH = 3
EPS = 1e-6
NEG_INF = -1e30

GLA_W = GLA_HEADS * GLA_DV
RET_W = RET_HEADS * RET_DV
SWA_W = SWA_Q_HEADS * SWA_HEAD_DIM
IN_SIZES = (GLA_HEADS * GLA_DK, GLA_HEADS * GLA_DK, GLA_W, GLA_W, GLA_RANK,
            RET_HEADS * RET_DK, RET_HEADS * RET_DK, RET_W, RET_W,
            SWA_W, SWA_KV_HEADS * SWA_HEAD_DIM, SWA_KV_HEADS * SWA_HEAD_DIM,
            N_BRANCH * D_MODEL)

LANES = 128
VMEM_LIMIT = 52 * 1024 * 1024

C_GQ = 0
C_GK = 256
C_GV = 512
C_GR = 1024
C_RQ = 1536
C_RK = 1792
C_RV = 2048
C_RG = 2560
C_SQ = 3072
C_SK = 3584
C_SV = 3712
C_ZG = 3840
C_LA = 6912
W_MAIN = 7040
Z_W = 7168

TT = 128
N_LEVELS = 7
SB = 16


def _dot(a, b):
    return jnp.dot(a, b, preferred_element_type=F32)


def _dot_nt(a, b):
    return lax.dot_general(a, b, (((1,), (1,)), ((), ())), preferred_element_type=F32)


def _dot_tn(a, b):
    return lax.dot_general(a, b, (((0,), (0,)), ((), ())), preferred_element_type=F32)


def _rms(x, g):
    return x * lax.rsqrt(jnp.mean(x * x, axis=-1, keepdims=True) + EPS) * g


def _layer_spec(shape, layer):
    nd = len(shape)
    return pl.BlockSpec((None,) + tuple(shape), lambda *_: (layer,) + (0,) * nd,
                        pipeline_mode=pl.Buffered(1))


def _const_spec(shape):
    nd = len(shape)
    return pl.BlockSpec(tuple(shape), lambda *_: (0,) * nd, pipeline_mode=pl.Buffered(1))


def _params(sem):
    return pltpu.CompilerParams(dimension_semantics=sem, vmem_limit_bytes=VMEM_LIMIT)


def _rope_block(x, cos, sin_s, first):
    half_mask, half = first
    xr = jnp.where(half_mask, pltpu.roll(x, LANES - half, 1), pltpu.roll(x, half, 1))
    return x * cos + xr * sin_s


def _in_stages(x, g_ref, w_ref, wa_ref, ba_ref, tabs, z_ref):
    hb = _rms(x, g_ref[...]).astype(BF16)

    def mm(c0, c1):
        return _dot(hb, w_ref[:, c0:c1])

    lane = lax.broadcasted_iota(jnp.int32, (1, LANES), 1) % 64
    ret_first = (lane < RET_DK // 2, RET_DK // 2)
    swa_first = (lane < ROPE_DIM // 2, ROPE_DIM // 2)
    rc, rs, sc, ss = tabs

    ga = mm(C_LA, W_MAIN)
    xa = _dot(ga.astype(BF16), wa_ref[...]) + ba_ref[...]
    log_sig = jnp.minimum(xa, 0.0) - jnp.log1p(jnp.exp(-jnp.abs(xa)))
    z_ref[:, C_LA:Z_W] = log_sig * (1.0 / GLA_TAU)
    z_ref[:, C_GQ:C_GK] = mm(C_GQ, C_GK) * (GLA_DK ** -0.5)
    yield
    z_ref[:, C_GK:C_RQ] = mm(C_GK, C_RQ)
    yield
    rq = mm(C_RQ, C_RK)
    rk = mm(C_RK, C_RV)
    for j in range(2):
        sl = slice(LANES * j, LANES * (j + 1))
        z_ref[:, C_RQ + LANES * j:C_RQ + LANES * (j + 1)] = _rope_block(rq[:, sl], rc, rs, ret_first)
        z_ref[:, C_RK + LANES * j:C_RK + LANES * (j + 1)] = (
            _rope_block(rk[:, sl], rc, rs, ret_first) * (RET_DK ** -0.5))
    yield
    z_ref[:, C_RV:C_SQ] = mm(C_RV, C_SQ)
    yield
    sqk = mm(C_SQ, C_SV)
    for j in range(5):
        sl = slice(LANES * j, LANES * (j + 1))
        z_ref[:, C_SQ + LANES * j:C_SQ + LANES * (j + 1)] = _rope_block(sqk[:, sl], sc, ss, swa_first)
    z_ref[:, C_SV:C_ZG] = mm(C_SV, C_ZG)
    yield
    for j in range(N_BRANCH):
        z_ref[:, C_ZG + D_MODEL * j:C_ZG + D_MODEL * (j + 1)] = mm(C_ZG + D_MODEL * j, C_ZG + D_MODEL * (j + 1))
        yield


def _in_kernel(x_ref, g_ref, w_ref, wa_ref, ba_ref, rc_ref, rs_ref, sc_ref, ss_ref, z_ref):
    tabs = (rc_ref[...], rs_ref[...], sc_ref[...], ss_ref[...])
    for _ in _in_stages(x_ref[...], g_ref, w_ref, wa_ref, ba_ref, tabs, z_ref):
        pass


def _in_proj(x, layer, g_mix, w_main, wa, ba, tabs, tm):
    n = x.shape[0]
    rc, rs, sc, ss = tabs
    nt = rc.shape[0] // tm
    tab = pl.BlockSpec((tm, LANES), lambda i: (i % nt, 0))
    return pl.pallas_call(
        _in_kernel,
        grid=(n // tm,),
        in_specs=[pl.BlockSpec((tm, D_MODEL), lambda i: (i, 0)),
                  _layer_spec((1, D_MODEL), layer),
                  _layer_spec((D_MODEL, W_MAIN), layer),
                  _layer_spec((LANES, 256), layer),
                  _layer_spec((1, 256), layer),
                  tab, tab, tab, tab],
        out_specs=pl.BlockSpec((tm, Z_W), lambda i: (i, 0)),
        out_shape=jax.ShapeDtypeStruct((n, Z_W), F32),
        compiler_params=_params(("arbitrary",)),
        name="in_proj",
    )(x, g_mix, w_main, wa, ba, rc, rs, sc, ss)


def _gla_constants():
    t = np.arange(TT)
    g = np.zeros((2 + N_LEVELS, TT, TT), np.float32)
    g[0] = (t[None, :] <= t[:, None])
    g[1] = (t[None, :] > t[:, None])
    m = np.zeros((1 + N_LEVELS, TT, TT), np.float32)
    m[0] = np.eye(TT)
    for lv in range(1, N_LEVELS + 1):
        bs, hf = 2 ** lv, 2 ** (lv - 1)
        bd = (t // bs) * bs + hf - 1
        upper = (t % bs) >= hf
        u = t[None, :]
        g[1 + lv] = np.where(upper[:, None], (u > bd[:, None]) & (u <= t[:, None]),
                             (u > t[:, None]) & (u <= bd[:, None]))
        same = (t[:, None] // bs) == (t[None, :] // bs)
        m[lv] = same & upper[:, None] & (~upper)[None, :]
    return g.reshape(-1, TT), np.tile(m, (1, 1, GLA_HEADS))


def _head_masks():
    hk = np.arange(GLA_HEADS * TT)[:, None] // TT == np.arange(GLA_HEADS * GLA_DK)[None, :] // GLA_DK
    hv = np.arange(GLA_HEADS * TT)[:, None] // TT == np.arange(GLA_W)[None, :] // GLA_DV
    hs = np.arange(GLA_HEADS * GLA_DK)[:, None] // GLA_DK == np.arange(GLA_W)[None, :] // GLA_DV
    return hk.astype(np.float32), hv.astype(np.float32), hs.astype(np.float32)


def _heads_blockdiag(x, mask):
    return jnp.concatenate([x] * GLA_HEADS, axis=0) * mask


def _col_vector(row):
    n = row.shape[1]
    eye = lax.broadcasted_iota(jnp.int32, (n, n), 0) == lax.broadcasted_iota(jnp.int32, (n, n), 1)
    return jnp.sum(jnp.where(eye, row, 0.0), axis=1, keepdims=True)


def _norm_gate(o, gn, gate):
    outs = [_rms(o[:, GLA_DV * h:GLA_DV * (h + 1)], gn[:, GLA_DV * h:GLA_DV * (h + 1)]) for h in range(GLA_HEADS)]
    return jnp.concatenate(outs, axis=1) * (gate * jax.nn.sigmoid(gate))


def _gla_tile(z_ref, gm_ref, mk_ref, hk_ref, hv_ref, hs_ref, gn_ref, st_ref):
    la = z_ref[:, C_LA:Z_W]
    la_hi = la.astype(BF16)
    la_lo = (la - la_hi.astype(F32)).astype(BF16)
    gm = gm_ref[...]
    ex = jnp.exp(_dot(gm, la_hi) + _dot(gm, la_lo))
    q = z_ref[:, C_GQ:C_GK]
    k = z_ref[:, C_GK:C_GV]
    e_b = ex[0:TT]
    a_row = e_b[TT - 1:TT, :]
    qd = (q * e_b).astype(BF16)
    kd = (k * ex[TT:2 * TT]).astype(BF16)
    qb = q.astype(BF16)
    kb = k.astype(BF16)
    ql = [(q * ex[(1 + lv) * TT:(2 + lv) * TT]).astype(BF16) for lv in range(1, N_LEVELS + 1)]
    kl = [(k * ex[(1 + lv) * TT:(2 + lv) * TT]).astype(BF16) for lv in range(1, N_LEVELS + 1)]
    hk = hk_ref[...]
    yield
    a = mk_ref[0] * _dot_nt(qb, _heads_blockdiag(kb, hk))
    for lv in range(N_LEVELS):
        yield
        a = a + mk_ref[lv + 1] * _dot_nt(ql[lv], _heads_blockdiag(kl[lv], hk))
    yield
    v = z_ref[:, C_GV:C_GR].astype(BF16)
    s = st_ref[...]
    o = _dot(a.astype(BF16), _heads_blockdiag(v, hv_ref[...])) + _dot(qd, s.astype(BF16))
    yield
    st_ref[...] = _col_vector(a_row) * s + hs_ref[...] * _dot_tn(kd, v)
    yield
    return _norm_gate(o, gn_ref[...], z_ref[:, C_GR:C_RQ])


def _ret_gammas():
    return [1.0 - 2.0 ** (-5.0 - h) for h in range(RET_HEADS)]


def _ret_constants():
    t = np.arange(TT, dtype=np.float64)
    d = np.zeros((RET_HEADS, TT, TT), np.float64)
    qdec = np.zeros((TT, RET_HEADS * RET_DK), np.float64)
    kdec = np.zeros((TT, RET_HEADS * RET_DK), np.float64)
    adec = np.zeros((RET_HEADS * RET_DK, LANES), np.float64)
    for h, gam in enumerate(_ret_gammas()):
        diff = t[:, None] - t[None, :]
        d[h] = np.where(diff >= 0, gam ** np.maximum(diff, 0.0), 0.0)
        qdec[:, h * RET_DK:(h + 1) * RET_DK] = (gam ** (t + 1.0))[:, None]
        kdec[:, h * RET_DK:(h + 1) * RET_DK] = (gam ** (TT - 1.0 - t))[:, None]
        adec[h * RET_DK:(h + 1) * RET_DK, :] = gam ** TT
    d = np.concatenate(list(d), axis=1)
    adec = np.tile(adec, (1, RET_W // LANES))
    return d.astype(np.float32), qdec.astype(np.float32), kdec.astype(np.float32), adec.astype(np.float32)


def _ret_tile(z_ref, d_ref, qdec_ref, kdec_ref, adec_ref, hk_ref, hv_ref, hs_ref, gn_ref, st_ref):
    q = z_ref[:, C_RQ:C_RK]
    k = z_ref[:, C_RK:C_RV]
    qd = (q * qdec_ref[...]).astype(BF16)
    kd = (k * kdec_ref[...]).astype(BF16)
    a = d_ref[...] * _dot_nt(q.astype(BF16), _heads_blockdiag(k.astype(BF16), hk_ref[...]))
    yield
    v = z_ref[:, C_RV:C_RG].astype(BF16)
    s = st_ref[...]
    o = _dot(a.astype(BF16), _heads_blockdiag(v, hv_ref[...])) + _dot(qd, s.astype(BF16))
    yield
    st_ref[...] = adec_ref[...] * s + hs_ref[...] * _dot_tn(kd, v)
    yield
    return _norm_gate(o, gn_ref[...], z_ref[:, C_RG:C_SQ])


def _rec_sample_constants():
    c = np.arange(SB * GLA_DK)
    rep = np.arange(GLA_DK)[:, None] == (c[None, :] % GLA_DK)
    bd = (np.arange(DEC_SEQ * SB)[:, None] % SB) == (c[None, :] // GLA_DK)
    return rep.astype(np.float32), bd.astype(np.float32)


def _rec_sample_kernel(q_ref, k_ref, v_ref, gr_ref, la_ref, s0_ref, gn_ref, rep_ref, bd_ref, o_ref, s1_ref):
    nseq = DEC_BATCH
    row0 = pl.multiple_of(pl.program_id(0) * SB, SB)

    def rows(ref, t):
        return ref[pl.ds(t * nseq + row0, SB), :]

    b = []
    for t in range(DEC_SEQ):
        la_t = rows(la_ref, t)
        b.append(la_t if t == 0 else b[-1] + la_t)
    q = [rows(q_ref, t) for t in range(DEC_SEQ)]
    k = [rows(k_ref, t) for t in range(DEC_SEQ)]
    v = [rows(v_ref, t) for t in range(DEC_SEQ)]
    qd = jnp.concatenate([q[t] * jnp.exp(b[t]) for t in range(DEC_SEQ)], axis=0).astype(BF16)
    kd = jnp.concatenate([k[t] * jnp.exp(b[-1] - b[t]) for t in range(DEC_SEQ)], axis=0).astype(BF16)
    vv = jnp.concatenate(v, axis=0).astype(BF16)
    a = jnp.exp(b[-1])
    a_hi = a.astype(BF16)
    a_r = a - a_hi.astype(F32)
    a_mid = a_r.astype(BF16)
    a_lo = (a_r - a_mid.astype(F32)).astype(BF16)
    rep = rep_ref[...]
    bd = bd_ref[...]
    ones = jnp.ones((SB, GLA_DV), BF16)

    def expand(x, mask):
        return (_dot(x, rep) * mask).astype(BF16)

    oi = []
    for h in range(GLA_HEADS):
        hs = slice(GLA_DK * h, GLA_DK * (h + 1))
        vs = slice(GLA_DV * h, GLA_DV * (h + 1))
        s0 = s0_ref[:, h].reshape(SB * GLA_DK, GLA_DV)
        oi.append(_dot(expand(qd[:, hs], bd), s0.astype(BF16)))
        a_col = (_dot_tn(expand(a_hi[:, hs], bd[0:SB]), ones) + _dot_tn(expand(a_mid[:, hs], bd[0:SB]), ones)
                 + _dot_tn(expand(a_lo[:, hs], bd[0:SB]), ones))
        s1 = a_col * s0 + _dot_tn(expand(kd[:, hs], bd), vv[:, vs])
        s1_ref[:, h] = s1.reshape(SB, GLA_DK, GLA_DV)
    oi = jnp.concatenate(oi, axis=1)

    gn = gn_ref[...]
    for t in range(DEC_SEQ):
        o = oi[t * SB:(t + 1) * SB, :]
        for s in range(t + 1):
            p = q[t] * k[s] * jnp.exp(b[t] - b[s])
            parts = []
            for h in range(GLA_HEADS):
                hs = slice(GLA_DK * h, GLA_DK * (h + 1))
                vs = slice(GLA_DV * h, GLA_DV * (h + 1))
                parts.append(jnp.sum(p[:, hs], axis=1, keepdims=True) * v[s][:, vs])
            o = o + jnp.concatenate(parts, axis=1)
        gate = rows(gr_ref, t)
        outs = []
        for h in range(GLA_HEADS):
            vs = slice(GLA_DV * h, GLA_DV * (h + 1))
            outs.append(_rms(o[:, vs], gn[:, vs]))
        o_ref[pl.ds(t * nseq + row0, SB), :] = jnp.concatenate(outs, axis=1) * (gate * jax.nn.sigmoid(gate))


def _rec_sample(z, la, s0, layer, gn, rep, bd, cq, ck, cv, cg, c_la, name):
    n = z.shape[0]

    def whole(arr, width, c0):
        return pl.BlockSpec((arr.shape[0], width), lambda j: (0, c0 // width))

    st_in = pl.BlockSpec((None, SB, GLA_HEADS, GLA_DK, GLA_DV), lambda j: (layer, j, 0, 0, 0))
    st_out = pl.BlockSpec((SB, GLA_HEADS, GLA_DK, GLA_DV), lambda j: (j, 0, 0, 0))
    return pl.pallas_call(
        _rec_sample_kernel,
        grid=(DEC_BATCH // SB,),
        in_specs=[whole(z, 256, cq), whole(z, 256, ck), whole(z, 512, cv), whole(z, 512, cg),
                  whole(la, 256, c_la), st_in, _const_spec((1, GLA_W)),
                  _const_spec(rep.shape), _const_spec(bd.shape)],
        out_specs=[pl.BlockSpec((n, GLA_W), lambda j: (0, 0)), st_out],
        out_shape=[jax.ShapeDtypeStruct((n, GLA_W), F32),
                   jax.ShapeDtypeStruct(s0.shape[1:], F32)],
        compiler_params=_params(("arbitrary",)),
        name=name,
    )(z, z, z, z, la, s0, gn, rep, bd)


def _swa_tile(z_ref, sink_ref, kp_ref, vp_ref, first_col):
    kc = z_ref[:, C_SK:C_SV].astype(BF16)
    vc = z_ref[:, C_SV:C_ZG].astype(BF16)
    kband = jnp.concatenate([kp_ref[...], kc], axis=0)
    vband = jnp.concatenate([vp_ref[...], vc], axis=0)
    rows = SWA_GROUP * WINDOW
    r = lax.broadcasted_iota(jnp.int32, (rows, 2 * WINDOW), 0) & (WINDOW - 1)
    c = lax.broadcasted_iota(jnp.int32, (rows, 2 * WINDOW), 1)
    valid = (c > r) & (c <= r + WINDOW) & (c >= first_col)
    groups = range(SWA_KV_HEADS)
    dsl = [slice(SWA_HEAD_DIM * g, SWA_HEAD_DIM * (g + 1)) for g in groups]
    s, sk = [], []
    for g in groups:
        heads = range(SWA_GROUP * g, SWA_GROUP * (g + 1))
        q = jnp.concatenate(
            [z_ref[:, C_SQ + SWA_HEAD_DIM * hq:C_SQ + SWA_HEAD_DIM * (hq + 1)] for hq in heads], axis=0)
        sk.append(jnp.concatenate([jnp.full((WINDOW, 1), sink_ref[hq], F32) for hq in heads], axis=0))
        s.append(_dot_nt(q.astype(BF16), kband[:, dsl[g]]) * (SWA_HEAD_DIM ** -0.5))
    yield
    m, p = [], []
    for g in groups:
        sg = jnp.where(valid, s[g], NEG_INF)
        m.append(jnp.maximum(jnp.max(sg, axis=-1, keepdims=True), sk[g]))
        p.append(jnp.exp(sg - m[g]))
    yield
    outs = []
    for g in groups:
        den = jnp.sum(p[g], axis=-1, keepdims=True) + jnp.exp(sk[g] - m[g])
        o = _dot(p[g].astype(BF16), vband[:, dsl[g]]) / den
        outs += [o[WINDOW * j:WINDOW * (j + 1)] for j in range(SWA_GROUP)]
    kp_ref[...] = kc
    vp_ref[...] = vc
    yield
    return jnp.concatenate(outs, axis=1)


def _gate_stages(z_ref, start):
    for _ in range(start):
        yield
    gates = []
    for j in range(N_BRANCH):
        gates.append(jax.nn.sigmoid(z_ref[:, C_ZG + D_MODEL * j:C_ZG + D_MODEL * (j + 1)]))
        yield
    return gates


def _interleave(stage_fns):
    results = [None] * len(stage_fns)
    live = list(enumerate(stage_fns))
    while live:
        still = []
        for idx, gen in live:
            try:
                next(gen)
                still.append((idx, gen))
            except StopIteration as stop:
                results[idx] = stop.value
        live = still
    return results


def _mix_kernel(sink_ref, x_ref, xn_ref, tc0, tc1, tc2, tc3, tn0, tn1, tn2, tn3, g_ref, w_ref, wa_ref, ba_ref,
                gm_ref, mk_ref, d_ref, qdec_ref, kdec_ref, adec_ref, hk_ref, hv_ref, hs_ref, gna_ref, gnb_ref,
                wb_ref, wo_ref, y_ref, sta_ref, stb_ref, kv_ref, z_scr, sa_scr, sb_scr, kp_scr, vp_scr):
    i = pl.program_id(0)
    nb = x_ref.shape[0]

    def rows_of(ref3):
        return ref3[...].reshape(nb * TT, ref3.shape[2])

    def tiled(tab_refs):
        return tuple(jnp.concatenate([r[...]] * nb, axis=0) for r in tab_refs)

    @pl.when(i == 0)
    def _():
        sa_scr[...] = jnp.zeros_like(sa_scr)
        sb_scr[...] = jnp.zeros_like(sb_scr)
        kp_scr[...] = jnp.zeros_like(kp_scr)
        vp_scr[...] = jnp.zeros_like(vp_scr)
        for _ in _in_stages(rows_of(x_ref), g_ref, w_ref, wa_ref, ba_ref, tiled((tc0, tc1, tc2, tc3)), z_scr.at[0]):
            pass

    slot = i % 2
    zc = z_scr.at[slot]
    first_col = jnp.where(i > 0, 0, WINDOW)
    stages = [_in_stages(rows_of(xn_ref), g_ref, w_ref, wa_ref, ba_ref, tiled((tn0, tn1, tn2, tn3)),
                         z_scr.at[1 - slot])]
    for b in range(nb):
        zb = zc.at[pl.ds(TT * b, TT)]
        stages += [_gla_tile(zb, gm_ref, mk_ref, hk_ref, hv_ref, hs_ref, gna_ref, sa_scr.at[b]),
                   _swa_tile(zb, sink_ref, kp_scr.at[b], vp_scr.at[b], first_col),
                   _ret_tile(zb, d_ref, qdec_ref, kdec_ref, adec_ref, hk_ref, hv_ref, hs_ref, gnb_ref, sb_scr.at[b]),
                   _gate_stages(zb, 4 + N_BRANCH * b)]
    branch = _interleave(stages)[1:]
    for b in range(nb):
        for h in range(GLA_HEADS):
            sta_ref[b, h] = sa_scr[b, GLA_DK * h:GLA_DK * (h + 1), GLA_DV * h:GLA_DV * (h + 1)]
            stb_ref[b, h] = sb_scr[b, RET_DK * h:RET_DK * (h + 1), RET_DV * h:RET_DV * (h + 1)]

    def stacked(j):
        return jnp.concatenate([branch[4 * b + j] for b in range(nb)], axis=0).astype(BF16)

    proj = [_dot(stacked(0), wb_ref[0:GLA_W, :]), _dot(stacked(2), wb_ref[GLA_W:GLA_W + RET_W, :]),
            _dot(stacked(1), wb_ref[GLA_W + RET_W:, :])]
    mixed = []
    for b in range(nb):
        rows = slice(TT * b, TT * (b + 1))
        mixed.append(sum(branch[4 * b + 3][j] * proj[j][rows] for j in range(N_BRANCH)))
    out = _dot(jnp.concatenate(mixed, axis=0).astype(BF16), wo_ref[...])
    for b in range(nb):
        y_ref[b] = x_ref[b] + out[TT * b:TT * (b + 1)]
        kv_ref[b] = zc[pl.ds(TT * b, TT), C_SK:C_ZG]


MIX_VMEM_LIMIT = 58 * 1024 * 1024


def _mix_prompt(x, sink, layer, g_mix, w_main, wa, ba, tabs, consts, gna, gnb, wb, wo):
    nb, t, _ = x.shape
    nt = t // TT
    st_spec = pl.BlockSpec((nb, GLA_HEADS, GLA_DK, GLA_DV), lambda i: (0, 0, 0, 0))
    st_shape = jax.ShapeDtypeStruct((nb, GLA_HEADS, GLA_DK, GLA_DV), F32)
    nxt = lambda i: jnp.minimum(i + 1, nt - 1)
    tab_cur = pl.BlockSpec((TT, LANES), lambda i: (i, 0))
    tab_nxt = pl.BlockSpec((TT, LANES), lambda i: (nxt(i), 0))
    return pl.pallas_call(
        _mix_kernel,
        grid=(nt,),
        in_specs=[pl.BlockSpec(memory_space=pltpu.SMEM),
                  pl.BlockSpec((nb, TT, D_MODEL), lambda i: (0, i, 0)),
                  pl.BlockSpec((nb, TT, D_MODEL), lambda i: (0, nxt(i), 0))]
        + [tab_cur] * 4 + [tab_nxt] * 4
        + [_layer_spec((1, D_MODEL), layer), _layer_spec((D_MODEL, W_MAIN), layer),
           _layer_spec((LANES, 256), layer), _layer_spec((1, 256), layer)]
        + [_const_spec(c.shape) for c in consts]
        + [_const_spec((1, GLA_W)), _const_spec((1, RET_W)),
           _layer_spec((GLA_W + RET_W + SWA_W, D_MODEL), layer),
           _layer_spec((D_MODEL, D_MODEL), layer)],
        out_specs=[pl.BlockSpec((nb, TT, D_MODEL), lambda i: (0, i, 0)), st_spec, st_spec,
                   pl.BlockSpec((nb, WINDOW, 2 * LANES), lambda i: (0, 0, 0))],
        out_shape=[jax.ShapeDtypeStruct((nb, t, D_MODEL), F32), st_shape, st_shape,
                   jax.ShapeDtypeStruct((nb, WINDOW, 2 * LANES), F32)],
        scratch_shapes=[pltpu.VMEM((2, nb * TT, Z_W), F32),
                        pltpu.VMEM((nb, GLA_HEADS * GLA_DK, GLA_W), F32),
                        pltpu.VMEM((nb, RET_HEADS * RET_DK, RET_W), F32),
                        pltpu.VMEM((nb, WINDOW, 128), BF16), pltpu.VMEM((nb, WINDOW, 128), BF16)],
        compiler_params=pltpu.CompilerParams(dimension_semantics=("arbitrary",), vmem_limit_bytes=MIX_VMEM_LIMIT),
        name="mix_prompt",
    )(sink, x, x, *tabs, *tabs, g_mix, w_main, wa, ba, *consts, gna, gnb, wb, wo)


def _swa_sample_masks():
    r = np.arange(DEC_SEQ * SB)
    rt, rb = r // SB, r % SB
    c = np.arange(SB * WINDOW)
    cb, cj = c // WINDOW, c % WINDOW
    m_cache = (rb[:, None] == cb[None, :]) & (cj[None, :] > rt[:, None])
    m_new = (rb[:, None] == rb[None, :]) & (rt[None, :] <= rt[:, None])
    return m_cache.astype(np.float32), m_new.astype(np.float32)


def _swa_sample_kernel(sink_ref, q_ref, k_ref, v_ref, kc_ref, vc_ref, mc_ref, mn_ref, o_ref):
    nseq = DEC_BATCH
    row0 = pl.multiple_of(pl.program_id(0) * SB, SB)

    def gather(ref):
        return jnp.concatenate([ref[pl.ds(t * nseq + row0, SB), :] for t in range(DEC_SEQ)], axis=0)

    q = gather(q_ref).astype(BF16)
    kn = gather(k_ref).astype(BF16)
    vn = gather(v_ref).astype(BF16)
    kc = kc_ref[...].reshape(SB * WINDOW, 128).astype(BF16)
    vc = vc_ref[...].reshape(SB * WINDOW, 128).astype(BF16)
    ok_c = mc_ref[...] > 0.0
    ok_n = mn_ref[...] > 0.0
    outs = []
    for hq in range(SWA_Q_HEADS):
        g = hq // SWA_GROUP
        ds_ = slice(SWA_HEAD_DIM * g, SWA_HEAD_DIM * (g + 1))
        qh = q[:, SWA_HEAD_DIM * hq:SWA_HEAD_DIM * (hq + 1)]
        s1 = jnp.where(ok_c, _dot_nt(qh, kc[:, ds_]) * (SWA_HEAD_DIM ** -0.5), NEG_INF)
        s2 = jnp.where(ok_n, _dot_nt(qh, kn[:, ds_]) * (SWA_HEAD_DIM ** -0.5), NEG_INF)
        sk = sink_ref[hq]
        m = jnp.maximum(jnp.maximum(jnp.max(s1, axis=-1, keepdims=True),
                                    jnp.max(s2, axis=-1, keepdims=True)), sk)
        p1 = jnp.exp(s1 - m)
        p2 = jnp.exp(s2 - m)
        den = jnp.sum(p1, axis=-1, keepdims=True) + jnp.sum(p2, axis=-1, keepdims=True) + jnp.exp(sk - m)
        outs.append((_dot(p1.astype(BF16), vc[:, ds_]) + _dot(p2.astype(BF16), vn[:, ds_])) / den)
    o = jnp.concatenate(outs, axis=1)
    for t in range(DEC_SEQ):
        o_ref[pl.ds(t * nseq + row0, SB), :] = o[t * SB:(t + 1) * SB, :]


def _swa_sample(z, sink, cache_k, cache_v, layer, mc, mn):
    n = z.shape[0]

    def whole(width, c0):
        return pl.BlockSpec((n, width), lambda j: (0, c0 // width))

    cache = pl.BlockSpec((None, SB, WINDOW, 128), lambda j: (layer, j, 0, 0))
    return pl.pallas_call(
        _swa_sample_kernel,
        grid=(DEC_BATCH // SB,),
        in_specs=[pl.BlockSpec(memory_space=pltpu.SMEM),
                  whole(512, C_SQ), whole(128, C_SK), whole(128, C_SV), cache, cache,
                  _const_spec(mc.shape), _const_spec(mn.shape)],
        out_specs=pl.BlockSpec((n, SWA_W), lambda j: (0, 0)),
        out_shape=jax.ShapeDtypeStruct((n, SWA_W), F32),
        compiler_params=_params(("arbitrary",)),
        name="swa_sample",
    )(sink, z, z, z, cache_k, cache_v, mc, mn)


def _merge_kernel(x_ref, oa_ref, ob_ref, oc_ref, z_ref, wb_ref, wo_ref, y_ref):
    def branch(o_ref, r0, j):
        proj = _dot(o_ref[...].astype(BF16), wb_ref[r0:r0 + 512, :])
        return jax.nn.sigmoid(z_ref[:, C_ZG + D_MODEL * j:C_ZG + D_MODEL * (j + 1)]) * proj

    mixed = branch(oa_ref, 0, 0) + branch(ob_ref, GLA_W, 1) + branch(oc_ref, GLA_W + RET_W, 2)
    y_ref[...] = x_ref[...] + _dot(mixed.astype(BF16), wo_ref[...])


def _merge(x, oa, ob, oc, z, layer, wb, wo, tm):
    n = x.shape[0]
    tok = lambda w: pl.BlockSpec((tm, w), lambda i: (i, 0))
    return pl.pallas_call(
        _merge_kernel,
        grid=(n // tm,),
        in_specs=[tok(D_MODEL), tok(512), tok(512), tok(512), tok(Z_W),
                  _layer_spec((GLA_W + RET_W + SWA_W, D_MODEL), layer),
                  _layer_spec((D_MODEL, D_MODEL), layer)],
        out_specs=tok(D_MODEL),
        out_shape=jax.ShapeDtypeStruct((n, D_MODEL), F32),
        compiler_params=_params(("arbitrary",)),
        name="merge",
    )(x, oa, ob, oc, z, wb, wo)


FF_CHUNK = D_FF // 2


def _ffn_tail(x1, pe_ref, gp_ref, wpg_ref, wpp_ref, gf_ref, y_ref, final):
    hp = _rms(x1, gp_ref[...]).astype(BF16)
    x2 = x1 + jax.nn.sigmoid(_dot(hp, wpg_ref[...])) * _dot(pe_ref[...].astype(BF16), wpp_ref[...])
    y_ref[...] = _rms(x2, gf_ref[...]) if final else x2


def _ffn_prompt_kernel(x_ref, pe_ref, gn_ref, wi_ref, cw_ref, cb_ref, wd_ref, gp_ref, wpg_ref, wpp_ref,
                       gf_ref, y_ref, tail_ref, carry_scr, *, final):
    tm = x_ref.shape[0]

    @pl.when(pl.program_id(1) == 0)
    def _():
        carry_scr[...] = jnp.zeros_like(carry_scr)

    x = x_ref[...]
    hb = _rms(x, gn_ref[...]).astype(BF16)
    row = lax.broadcasted_iota(jnp.int32, (tm, 1), 0)
    acc = x
    for c0 in range(0, D_FF, FF_CHUNK):
        cs = slice(c0, c0 + FF_CHUNK)
        a = _dot(hb, wi_ref[:, c0:c0 + FF_CHUNK])
        bb = _dot(hb, wi_ref[:, D_FF + c0:D_FF + c0 + FF_CHUNK])
        p0 = carry_scr[6:7, cs]
        p1 = carry_scr[7:8, cs]
        a1 = jnp.where(row == 0, p1, pltpu.roll(a, 1, 0))
        a2 = jnp.where(row == 0, p0, jnp.where(row == 1, p1, pltpu.roll(a, 2, 0)))
        conv = cb_ref[:, cs] + cw_ref[0:1, cs] * a2 + cw_ref[1:2, cs] * a1 + cw_ref[2:3, cs] * a
        act = (jax.nn.gelu(conv) * bb).astype(BF16)
        acc = acc + _dot(act, wd_ref[c0:c0 + FF_CHUNK, :])
        carry_scr[:, cs] = a[tm - 8:tm, :]
        tail_ref[:, cs] = a[tm - 8:tm, :]
    _ffn_tail(acc, pe_ref, gp_ref, wpg_ref, wpp_ref, gf_ref, y_ref, final)


def _ffn_sample_kernel(x_ref, pe_ref, st_ref, gn_ref, wi_ref, cw_ref, cb_ref, wd_ref, gp_ref, wpg_ref,
                       wpp_ref, gf_ref, y_ref, tail_ref, *, final):
    ns = DEC_BATCH
    x = x_ref[...]
    hb = _rms(x, gn_ref[...]).astype(BF16)
    acc = x
    for c0 in range(0, D_FF, FF_CHUNK):
        cs = slice(c0, c0 + FF_CHUNK)
        a = _dot(hb, wi_ref[:, c0:c0 + FF_CHUNK])
        bb = _dot(hb, wi_ref[:, D_FF + c0:D_FF + c0 + FF_CHUNK])
        st0 = st_ref[0, :, cs]
        st1 = st_ref[1, :, cs]
        a1 = jnp.concatenate([st1, a[0:3 * ns]], axis=0)
        a2 = jnp.concatenate([st0, st1, a[0:2 * ns]], axis=0)
        conv = cb_ref[:, cs] + cw_ref[0:1, cs] * a2 + cw_ref[1:2, cs] * a1 + cw_ref[2:3, cs] * a
        act = (jax.nn.gelu(conv) * bb).astype(BF16)
        acc = acc + _dot(act, wd_ref[c0:c0 + FF_CHUNK, :])
        tail_ref[:, cs] = a[2 * ns:4 * ns, :]
    _ffn_tail(acc, pe_ref, gp_ref, wpg_ref, wpp_ref, gf_ref, y_ref, final)


def _ffn_weight_specs(layer):
    return [_layer_spec((1, D_MODEL), layer),
            _layer_spec((D_MODEL, 2 * D_FF), layer),
            _layer_spec((CONV_W, D_FF), layer),
            _layer_spec((1, D_FF), layer),
            _layer_spec((D_FF, D_MODEL), layer),
            _layer_spec((1, D_MODEL), layer),
            _layer_spec((D_MODEL, D_MODEL), layer),
            _layer_spec((D_PLE, D_MODEL), layer),
            _const_spec((1, D_MODEL))]


def _ffn_prompt(x, pe, layer, weights, nb, t, tm, final):
    nt = t // tm
    tok = lambda w: pl.BlockSpec((tm, w), lambda b, i: (b * nt + i, 0))
    return pl.pallas_call(
        functools.partial(_ffn_prompt_kernel, final=final),
        grid=(nb, nt),
        in_specs=[tok(D_MODEL),
                  pl.BlockSpec((None, tm, D_PLE), lambda b, i: (layer, b * nt + i, 0))]
        + _ffn_weight_specs(layer),
        out_specs=[tok(D_MODEL), pl.BlockSpec((8, D_FF), lambda b, i: (b, 0))],
        out_shape=[jax.ShapeDtypeStruct((nb * t, D_MODEL), F32),
                   jax.ShapeDtypeStruct((nb * 8, D_FF), F32)],
        scratch_shapes=[pltpu.VMEM((8, D_FF), F32)],
        compiler_params=_params(("arbitrary", "arbitrary")),
        name="ffn_prompt",
    )(x, pe, *weights)


def _ffn_sample(x, pe, st, layer, weights, final):
    n = x.shape[0]
    return pl.pallas_call(
        functools.partial(_ffn_sample_kernel, final=final),
        grid=(1,),
        in_specs=[pl.BlockSpec((n, D_MODEL), lambda i: (0, 0)),
                  pl.BlockSpec((None, n, D_PLE), lambda i: (layer, 0, 0)),
                  pl.BlockSpec((None, CONV_W - 1, DEC_BATCH, D_FF), lambda i: (layer, 0, 0, 0))]
        + _ffn_weight_specs(layer),
        out_specs=[pl.BlockSpec((n, D_MODEL), lambda i: (0, 0)),
                   pl.BlockSpec((n // 2, D_FF), lambda i: (0, 0))],
        out_shape=[jax.ShapeDtypeStruct((n, D_MODEL), F32),
                   jax.ShapeDtypeStruct((n // 2, D_FF), F32)],
        compiler_params=_params(("arbitrary",)),
        name="ffn_sample",
    )(x, pe, st, *weights)


def _rope_tables(pos, inv_freq):
    half = inv_freq.shape[0]
    ang = pos.astype(np.float64)[:, None] * inv_freq[None, :]
    c, s = np.cos(ang), np.sin(ang)
    rest = SWA_HEAD_DIM - 2 * half
    cos64 = np.concatenate([c, c, np.ones((pos.shape[0], rest))], axis=1)
    sin64 = np.concatenate([-s, s, np.zeros((pos.shape[0], rest))], axis=1)
    return (jnp.asarray(np.concatenate([cos64, cos64], axis=1), F32),
            jnp.asarray(np.concatenate([sin64, sin64], axis=1), F32))


def _pack_w_in(w_in):
    ga0 = sum(IN_SIZES[:4])
    ga = jnp.pad(w_in[:, :, ga0:ga0 + GLA_RANK], ((0, 0), (0, 0), (0, LANES - GLA_RANK)))
    return jnp.concatenate([w_in[:, :, :ga0], w_in[:, :, ga0 + GLA_RANK:], ga], axis=2).astype(BF16)


def kernel(x_prompt, x_sample, state_gla, state_ret, cache_swa_k, cache_swa_v, state_conv, p_prompt,
           p_sample, norm_mix, w_in, w_gla_a, b_gla_a, gla_norm, ret_norm, swa_sink, w_branch, w_out,
           norm_ffn, w_ffn_in, conv_w, conv_b, w_ffn_out, norm_ple, w_ple_gate, w_ple_proj, norm_final):
    nb, t, _ = x_prompt.shape
    ns, ts, _ = x_sample.shape
    n_s = ns * ts

    w_main = _pack_w_in(w_in)
    wa = jnp.pad(w_gla_a, ((0, 0), (0, LANES - GLA_RANK), (0, 0))).astype(BF16)
    ba = b_gla_a[:, None, :]
    wb = w_branch.astype(BF16)
    wo = w_out.astype(BF16)
    ffn_weights = (norm_ffn[:, None, :], w_ffn_in.astype(BF16), conv_w, conv_b[:, None, :],
                   w_ffn_out.astype(BF16), norm_ple[:, None, :], w_ple_gate.astype(BF16),
                   w_ple_proj.astype(BF16), norm_final[None, :])
    g_mix = norm_mix[:, None, :]
    gn_gla = jnp.tile(gla_norm, (1, GLA_HEADS))[:, None, :]
    gn_ret = ret_norm.reshape(DEPTH, 1, RET_W)

    ret_freq = 1.0 / (RET_THETA ** np.linspace(0.0, 1.0, RET_DK // 2))
    swa_freq = 1.0 / (ROPE_THETA ** (np.arange(0, ROPE_DIM, 2, dtype=np.float64) / ROPE_DIM))
    pos_p = np.arange(t)
    pos_s = PAST_LEN + np.arange(n_s) // ns
    tabs_p = _rope_tables(pos_p, ret_freq) + _rope_tables(pos_p, swa_freq)
    tabs_s = _rope_tables(pos_s, ret_freq) + _rope_tables(pos_s, swa_freq)

    gm_np, mk_np = _gla_constants()
    d_np, qdec_np, kdec_np, adec_np = _ret_constants()
    hk_np, hv_np, hs_np = _head_masks()
    mix_consts = (jnp.asarray(gm_np, BF16), jnp.asarray(mk_np), jnp.asarray(d_np), jnp.asarray(qdec_np),
                  jnp.asarray(kdec_np), jnp.asarray(adec_np), jnp.asarray(hk_np, BF16), jnp.asarray(hv_np, BF16),
                  jnp.asarray(hs_np))
    log_gamma = np.log1p(-np.exp2(-5.0 - np.arange(RET_HEADS, dtype=np.float64)))
    la_ret = jnp.asarray(np.broadcast_to(np.repeat(log_gamma, RET_DK)[None, :], (n_s, RET_HEADS * RET_DK)), F32)
    mc_np, mn_np = _swa_sample_masks()
    swa_mc, swa_mn = jnp.asarray(mc_np), jnp.asarray(mn_np)
    rep_np, bd_np = _rec_sample_constants()
    rec_rep, rec_bd = jnp.asarray(rep_np, BF16), jnp.asarray(bd_np)

    xp = x_prompt.reshape(nb * t, D_MODEL)
    xs = x_sample.transpose(1, 0, 2).reshape(n_s, D_MODEL)
    pe_p = p_prompt.reshape(DEPTH, nb * t, D_PLE)
    pe_s = p_sample.transpose(0, 2, 1, 3).reshape(DEPTH, n_s, D_PLE)
    conv_st = state_conv.transpose(0, 2, 1, 3)
    ck = cache_swa_k.reshape(DEPTH, ns, WINDOW, SWA_KV_HEADS * SWA_HEAD_DIM)
    cv = cache_swa_v.reshape(DEPTH, ns, WINDOW, SWA_KV_HEADS * SWA_HEAD_DIM)

    outs = {k_: [] for k_ in ("gla_p", "gla_s", "ret_p", "ret_s", "kv_p", "kv_s", "conv_p", "conv_s")}
    for l in range(DEPTH):
        final = l == DEPTH - 1
        sink = swa_sink[l]

        xp3, st_a, st_b, kv_tail = _mix_prompt(xp.reshape(nb, t, D_MODEL), sink, l, g_mix, w_main, wa, ba, tabs_p,
                                               mix_consts, gn_gla[l], gn_ret[l], wb, wo)
        xp, tail = _ffn_prompt(xp3.reshape(nb * t, D_MODEL), pe_p, l, ffn_weights, nb, t, 256, final)
        outs["gla_p"].append(st_a)
        outs["ret_p"].append(st_b)
        outs["kv_p"].append(kv_tail)
        outs["conv_p"].append(tail.reshape(nb, 8, D_FF)[:, 8 - (CONV_W - 1):])

        zs = _in_proj(xs, l, g_mix, w_main, wa, ba, tabs_s, 256)
        oa, st_a = _rec_sample(zs, zs, state_gla, l, gn_gla[l], rec_rep, rec_bd,
                               C_GQ, C_GK, C_GV, C_GR, C_LA, "gla_sample")
        ob, st_b = _rec_sample(zs, la_ret, state_ret, l, gn_ret[l], rec_rep, rec_bd,
                               C_RQ, C_RK, C_RV, C_RG, 0, "ret_sample")
        oc = _swa_sample(zs, sink, ck, cv, l, swa_mc, swa_mn)
        xs = _merge(xs, oa, ob, oc, zs, l, wb, wo, 512)
        xs, tail_s = _ffn_sample(xs, pe_s, conv_st, l, ffn_weights, final)
        outs["gla_s"].append(st_a)
        outs["ret_s"].append(st_b)
        outs["kv_s"].append(zs[:, C_SK:C_ZG])
        outs["conv_s"].append(tail_s.reshape(CONV_W - 1, ns, D_FF).transpose(1, 0, 2))

    y_prompt = xp.reshape(nb, t, D_MODEL)
    y_sample = xs.reshape(ts, ns, D_MODEL).transpose(1, 0, 2)
    st = {k_: jnp.stack(v_) for k_, v_ in outs.items()}

    def kv_heads(a):
        return a.reshape(a.shape[:-1] + (SWA_KV_HEADS, SWA_HEAD_DIM))

    k_p = kv_heads(st["kv_p"][..., :LANES])
    v_p = kv_heads(st["kv_p"][..., LANES:])
    kv_new = st["kv_s"].reshape(DEPTH, ts, ns, 2 * LANES).transpose(0, 2, 1, 3)
    k_s = jnp.concatenate([cache_swa_k[:, :, ts:], kv_heads(kv_new[..., :LANES])], axis=2)
    v_s = jnp.concatenate([cache_swa_v[:, :, ts:], kv_heads(kv_new[..., LANES:])], axis=2)
    return (y_prompt, y_sample, st["gla_p"], st["gla_s"], st["ret_p"], st["ret_s"],
            k_p, k_s, v_p, v_s, st["conv_p"], st["conv_s"])
```

```python
import functools
import math

import numpy as np
import jax
import jax.numpy as jnp
from jax import lax
from jax.experimental import pallas as pl
from jax.experimental.pallas import tpu as pltpu

F32 = jnp.float32
BF16 = jnp.bfloat16

D_MODEL = 1024
BATCH = 2
SEQ = 8192
DEPTH = 4
DEC_BATCH = 128
DEC_SEQ = 4
PAST_LEN = 8192
D_PLE = 256
GLA_HEADS = 4
GLA_DK = 64
GLA_DV = 128
GLA_RANK = 16
GLA_TAU = 16.0
RET_HEADS = 4
RET_DK = 64
RET_DV = 128
RET_THETA = 10000.0
SWA_Q_HEADS = 8
SWA_KV_HEADS = 2
SWA_HEAD_DIM = 64
SWA_GROUP = SWA_Q_HEADS // SWA_KV_HEADS
WINDOW = 128
ROPE_THETA = 500000.0
ROPE_DIM = SWA_HEAD_DIM // 4
D_FF = 2816
CONV_W = 3
N_BRANCH = 3
EPS = 1e-6
NEG_INF = -1e30

GLA_W = GLA_HEADS * GLA_DV
RET_W = RET_HEADS * RET_DV
SWA_W = SWA_Q_HEADS * SWA_HEAD_DIM
IN_SIZES = (GLA_HEADS * GLA_DK, GLA_HEADS * GLA_DK, GLA_W, GLA_W, GLA_RANK,
            RET_HEADS * RET_DK, RET_HEADS * RET_DK, RET_W, RET_W,
            SWA_W, SWA_KV_HEADS * SWA_HEAD_DIM, SWA_KV_HEADS * SWA_HEAD_DIM,
            N_BRANCH * D_MODEL)

LANES = 128
VMEM_LIMIT = 52 * 1024 * 1024

C_GQ = 0
C_GK = 256
C_GV = 512
C_GR = 1024
C_RQ = 1536
C_RK = 1792
C_RV = 2048
C_RG = 2560
C_SQ = 3072
C_SK = 3584
C_SV = 3712
C_ZG = 3840
C_LA = 6912
W_MAIN = 7040
Z_W = 7168

TT = 128
N_LEVELS = 7
SB = 16


def _dot(a, b):
    return jnp.dot(a, b, preferred_element_type=F32)


def _dot_nt(a, b):
    return lax.dot_general(a, b, (((1,), (1,)), ((), ())), preferred_element_type=F32)


def _dot_tn(a, b):
    return lax.dot_general(a, b, (((0,), (0,)), ((), ())), preferred_element_type=F32)


def _rms(x, g):
    return x * lax.rsqrt(jnp.mean(x * x, axis=-1, keepdims=True) + EPS) * g


def _layer_spec(shape, layer):
    nd = len(shape)
    return pl.BlockSpec((None,) + tuple(shape), lambda *_: (layer,) + (0,) * nd,
                        pipeline_mode=pl.Buffered(1))


def _const_spec(shape):
    nd = len(shape)
    return pl.BlockSpec(tuple(shape), lambda *_: (0,) * nd, pipeline_mode=pl.Buffered(1))


def _params(sem):
    return pltpu.CompilerParams(dimension_semantics=sem, vmem_limit_bytes=VMEM_LIMIT)


def _rope_block(x, cos, sin_s, first):
    half_mask, half = first
    xr = jnp.where(half_mask, pltpu.roll(x, LANES - half, 1), pltpu.roll(x, half, 1))
    return x * cos + xr * sin_s


def _in_stages(x, g_ref, w_ref, wa_ref, ba_ref, tabs, z_ref):
    hb = _rms(x, g_ref[...]).astype(BF16)

    def mm(c0, c1):
        return _dot(hb, w_ref[:, c0:c1])

    lane = lax.broadcasted_iota(jnp.int32, (1, LANES), 1) % 64
    ret_first = (lane < RET_DK // 2, RET_DK // 2)
    swa_first = (lane < ROPE_DIM // 2, ROPE_DIM // 2)
    rc, rs, sc, ss = tabs

    ga = mm(C_LA, W_MAIN)
    xa = _dot(ga.astype(BF16), wa_ref[...]) + ba_ref[...]
    log_sig = jnp.minimum(xa, 0.0) - jnp.log1p(jnp.exp(-jnp.abs(xa)))
    z_ref[:, C_LA:Z_W] = log_sig * (1.0 / GLA_TAU)
    z_ref[:, C_GQ:C_GK] = mm(C_GQ, C_GK) * (GLA_DK ** -0.5)
    yield
    z_ref[:, C_GK:C_RQ] = mm(C_GK, C_RQ)
    yield
    rq = mm(C_RQ, C_RK)
    rk = mm(C_RK, C_RV)
    for j in range(2):
        sl = slice(LANES * j, LANES * (j + 1))
        z_ref[:, C_RQ + LANES * j:C_RQ + LANES * (j + 1)] = _rope_block(rq[:, sl], rc, rs, ret_first)
        z_ref[:, C_RK + LANES * j:C_RK + LANES * (j + 1)] = (
            _rope_block(rk[:, sl], rc, rs, ret_first) * (RET_DK ** -0.5))
    yield
    z_ref[:, C_RV:C_SQ] = mm(C_RV, C_SQ)
    yield
    sqk = mm(C_SQ, C_SV)
    for j in range(5):
        sl = slice(LANES * j, LANES * (j + 1))
        z_ref[:, C_SQ + LANES * j:C_SQ + LANES * (j + 1)] = _rope_block(sqk[:, sl], sc, ss, swa_first)
    z_ref[:, C_SV:C_ZG] = mm(C_SV, C_ZG)
    yield
    for j in range(N_BRANCH):
        z_ref[:, C_ZG + D_MODEL * j:C_ZG + D_MODEL * (j + 1)] = mm(C_ZG + D_MODEL * j, C_ZG + D_MODEL * (j + 1))
        yield


def _in_kernel(x_ref, g_ref, w_ref, wa_ref, ba_ref, rc_ref, rs_ref, sc_ref, ss_ref, z_ref):
    tabs = (rc_ref[...], rs_ref[...], sc_ref[...], ss_ref[...])
    for _ in _in_stages(x_ref[...], g_ref, w_ref, wa_ref, ba_ref, tabs, z_ref):
        pass


def _in_proj(x, layer, g_mix, w_main, wa, ba, tabs, tm):
    n = x.shape[0]
    rc, rs, sc, ss = tabs
    nt = rc.shape[0] // tm
    tab = pl.BlockSpec((tm, LANES), lambda i: (i % nt, 0))
    return pl.pallas_call(
        _in_kernel,
        grid=(n // tm,),
        in_specs=[pl.BlockSpec((tm, D_MODEL), lambda i: (i, 0)),
                  _layer_spec((1, D_MODEL), layer),
                  _layer_spec((D_MODEL, W_MAIN), layer),
                  _layer_spec((LANES, 256), layer),
                  _layer_spec((1, 256), layer),
                  tab, tab, tab, tab],
        out_specs=pl.BlockSpec((tm, Z_W), lambda i: (i, 0)),
        out_shape=jax.ShapeDtypeStruct((n, Z_W), F32),
        compiler_params=_params(("arbitrary",)),
        name="in_proj",
    )(x, g_mix, w_main, wa, ba, rc, rs, sc, ss)


def _gla_constants():
    t = np.arange(TT)
    g = np.zeros((2 + N_LEVELS, TT, TT), np.float32)
    g[0] = (t[None, :] <= t[:, None])
    g[1] = (t[None, :] > t[:, None])
    m = np.zeros((1 + N_LEVELS, TT, TT), np.float32)
    m[0] = np.eye(TT)
    for lv in range(1, N_LEVELS + 1):
        bs, hf = 2 ** lv, 2 ** (lv - 1)
        bd = (t // bs) * bs + hf - 1
        upper = (t % bs) >= hf
        u = t[None, :]
        g[1 + lv] = np.where(upper[:, None], (u > bd[:, None]) & (u <= t[:, None]),
                             (u > t[:, None]) & (u <= bd[:, None]))
        same = (t[:, None] // bs) == (t[None, :] // bs)
        m[lv] = same & upper[:, None] & (~upper)[None, :]
    return g.reshape(-1, TT), np.tile(m, (1, 1, GLA_HEADS))


def _head_masks():
    hk = np.arange(GLA_HEADS * TT)[:, None] // TT == np.arange(GLA_HEADS * GLA_DK)[None, :] // GLA_DK
    hv = np.arange(GLA_HEADS * TT)[:, None] // TT == np.arange(GLA_W)[None, :] // GLA_DV
    hs = np.arange(GLA_HEADS * GLA_DK)[:, None] // GLA_DK == np.arange(GLA_W)[None, :] // GLA_DV
    return hk.astype(np.float32), hv.astype(np.float32), hs.astype(np.float32)


def _heads_blockdiag(x, mask):
    return jnp.concatenate([x] * GLA_HEADS, axis=0) * mask


def _col_vector(row):
    n = row.shape[1]
    eye = lax.broadcasted_iota(jnp.int32, (n, n), 0) == lax.broadcasted_iota(jnp.int32, (n, n), 1)
    return jnp.sum(jnp.where(eye, row, 0.0), axis=1, keepdims=True)


def _norm_gate(o, gn, gate):
    outs = [_rms(o[:, GLA_DV * h:GLA_DV * (h + 1)], gn[:, GLA_DV * h:GLA_DV * (h + 1)]) for h in range(GLA_HEADS)]
    return jnp.concatenate(outs, axis=1) * (gate * jax.nn.sigmoid(gate))


def _gla_tile(z_ref, gm_ref, mk_ref, hk_ref, hv_ref, hs_ref, gn_ref, st_ref):
    la = z_ref[:, C_LA:Z_W]
    la_hi = la.astype(BF16)
    la_lo = (la - la_hi.astype(F32)).astype(BF16)
    gm = gm_ref[...]
    ex = jnp.exp(_dot(gm, la_hi) + _dot(gm, la_lo))
    q = z_ref[:, C_GQ:C_GK]
    k = z_ref[:, C_GK:C_GV]
    e_b = ex[0:TT]
    a_row = e_b[TT - 1:TT, :]
    qd = (q * e_b).astype(BF16)
    kd = (k * ex[TT:2 * TT]).astype(BF16)
    qb = q.astype(BF16)
    kb = k.astype(BF16)
    ql = [(q * ex[(1 + lv) * TT:(2 + lv) * TT]).astype(BF16) for lv in range(1, N_LEVELS + 1)]
    kl = [(k * ex[(1 + lv) * TT:(2 + lv) * TT]).astype(BF16) for lv in range(1, N_LEVELS + 1)]
    hk = hk_ref[...]
    yield
    a = mk_ref[0] * _dot_nt(qb, _heads_blockdiag(kb, hk))
    for lv in range(N_LEVELS):
        yield
        a = a + mk_ref[lv + 1] * _dot_nt(ql[lv], _heads_blockdiag(kl[lv], hk))
    yield
    v = z_ref[:, C_GV:C_GR].astype(BF16)
    s = st_ref[...]
    o = _dot(a.astype(BF16), _heads_blockdiag(v, hv_ref[...])) + _dot(qd, s.astype(BF16))
    yield
    st_ref[...] = _col_vector(a_row) * s + hs_ref[...] * _dot_tn(kd, v)
    yield
    return _norm_gate(o, gn_ref[...], z_ref[:, C_GR:C_RQ])


def _ret_gammas():
    return [1.0 - 2.0 ** (-5.0 - h) for h in range(RET_HEADS)]


def _ret_constants():
    t = np.arange(TT, dtype=np.float64)
    d = np.zeros((RET_HEADS, TT, TT), np.float64)
    qdec = np.zeros((TT, RET_HEADS * RET_DK), np.float64)
    kdec = np.zeros((TT, RET_HEADS * RET_DK), np.float64)
    adec = np.zeros((RET_HEADS * RET_DK, LANES), np.float64)
    for h, gam in enumerate(_ret_gammas()):
        diff = t[:, None] - t[None, :]
        d[h] = np.where(diff >= 0, gam ** np.maximum(diff, 0.0), 0.0)
        qdec[:, h * RET_DK:(h + 1) * RET_DK] = (gam ** (t + 1.0))[:, None]
        kdec[:, h * RET_DK:(h + 1) * RET_DK] = (gam ** (TT - 1.0 - t))[:, None]
        adec[h * RET_DK:(h + 1) * RET_DK, :] = gam ** TT
    d = np.concatenate(list(d), axis=1)
    adec = np.tile(adec, (1, RET_W // LANES))
    return d.astype(np.float32), qdec.astype(np.float32), kdec.astype(np.float32), adec.astype(np.float32)


def _ret_tile(z_ref, d_ref, qdec_ref, kdec_ref, adec_ref, hk_ref, hv_ref, hs_ref, gn_ref, st_ref):
    q = z_ref[:, C_RQ:C_RK]
    k = z_ref[:, C_RK:C_RV]
    qd = (q * qdec_ref[...]).astype(BF16)
    kd = (k * kdec_ref[...]).astype(BF16)
    a = d_ref[...] * _dot_nt(q.astype(BF16), _heads_blockdiag(k.astype(BF16), hk_ref[...]))
    yield
    v = z_ref[:, C_RV:C_RG].astype(BF16)
    s = st_ref[...]
    o = _dot(a.astype(BF16), _heads_blockdiag(v, hv_ref[...])) + _dot(qd, s.astype(BF16))
    yield
    st_ref[...] = adec_ref[...] * s + hs_ref[...] * _dot_tn(kd, v)
    yield
    return _norm_gate(o, gn_ref[...], z_ref[:, C_RG:C_SQ])


def _rec_sample_constants():
    c = np.arange(SB * GLA_DK)
    rep = np.arange(GLA_DK)[:, None] == (c[None, :] % GLA_DK)
    bd = (np.arange(DEC_SEQ * SB)[:, None] % SB) == (c[None, :] // GLA_DK)
    return rep.astype(np.float32), bd.astype(np.float32)


def _block_rows(ref, row0, c0, c1):
    return [ref[pl.ds(t * DEC_BATCH + row0, SB), c0:c1] for t in range(DEC_SEQ)]


def _rec_sample_stages(q, k, v, gate, la, s0_ref, s1_ref, gn_ref, rep_ref, bd_ref):
    b = []
    for t in range(DEC_SEQ):
        b.append(la[t] if t == 0 else b[-1] + la[t])
    qd = jnp.concatenate([q[t] * jnp.exp(b[t]) for t in range(DEC_SEQ)], axis=0).astype(BF16)
    kd = jnp.concatenate([k[t] * jnp.exp(b[-1] - b[t]) for t in range(DEC_SEQ)], axis=0).astype(BF16)
    vv = jnp.concatenate(v, axis=0).astype(BF16)
    a = jnp.exp(b[-1])
    a_hi = a.astype(BF16)
    a_r = a - a_hi.astype(F32)
    a_mid = a_r.astype(BF16)
    a_lo = (a_r - a_mid.astype(F32)).astype(BF16)
    rep = rep_ref[...]
    bd = bd_ref[...]
    ones = jnp.ones((SB, GLA_DV), BF16)

    def expand(x, mask):
        return (_dot(x, rep) * mask).astype(BF16)

    yield
    oi = []
    for h in range(GLA_HEADS):
        hs = slice(GLA_DK * h, GLA_DK * (h + 1))
        vs = slice(GLA_DV * h, GLA_DV * (h + 1))
        s0 = s0_ref[:, h].reshape(SB * GLA_DK, GLA_DV)
        oi.append(_dot(expand(qd[:, hs], bd), s0.astype(BF16)))
        a_col = (_dot_tn(expand(a_hi[:, hs], bd[0:SB]), ones) + _dot_tn(expand(a_mid[:, hs], bd[0:SB]), ones)
                 + _dot_tn(expand(a_lo[:, hs], bd[0:SB]), ones))
        s1 = a_col * s0 + _dot_tn(expand(kd[:, hs], bd), vv[:, vs])
        s1_ref[:, h] = s1.reshape(SB, GLA_DK, GLA_DV)
        yield
    oi = jnp.concatenate(oi, axis=1)

    gn = gn_ref[...]
    outs = []
    for t in range(DEC_SEQ):
        o = oi[t * SB:(t + 1) * SB, :]
        for s in range(t + 1):
            p = q[t] * k[s] * jnp.exp(b[t] - b[s])
            parts = []
            for h in range(GLA_HEADS):
                hs = slice(GLA_DK * h, GLA_DK * (h + 1))
                vs = slice(GLA_DV * h, GLA_DV * (h + 1))
                parts.append(jnp.sum(p[:, hs], axis=1, keepdims=True) * v[s][:, vs])
            o = o + jnp.concatenate(parts, axis=1)
        outs.append(_norm_gate(o, gn, gate[t]))
        yield
    return jnp.concatenate(outs, axis=0)


def _swa_tile(z_ref, sink_ref, kp_ref, vp_ref, first_col):
    kc = z_ref[:, C_SK:C_SV].astype(BF16)
    vc = z_ref[:, C_SV:C_ZG].astype(BF16)
    kband = jnp.concatenate([kp_ref[...], kc], axis=0)
    vband = jnp.concatenate([vp_ref[...], vc], axis=0)
    rows = SWA_GROUP * WINDOW
    r = lax.broadcasted_iota(jnp.int32, (rows, 2 * WINDOW), 0) & (WINDOW - 1)
    c = lax.broadcasted_iota(jnp.int32, (rows, 2 * WINDOW), 1)
    valid = (c > r) & (c <= r + WINDOW) & (c >= first_col)
    groups = range(SWA_KV_HEADS)
    dsl = [slice(SWA_HEAD_DIM * g, SWA_HEAD_DIM * (g + 1)) for g in groups]
    s, sk = [], []
    for g in groups:
        heads = range(SWA_GROUP * g, SWA_GROUP * (g + 1))
        q = jnp.concatenate(
            [z_ref[:, C_SQ + SWA_HEAD_DIM * hq:C_SQ + SWA_HEAD_DIM * (hq + 1)] for hq in heads], axis=0)
        sk.append(jnp.concatenate([jnp.full((WINDOW, 1), sink_ref[hq], F32) for hq in heads], axis=0))
        s.append(_dot_nt(q.astype(BF16), kband[:, dsl[g]]) * (SWA_HEAD_DIM ** -0.5))
    yield
    m, p = [], []
    for g in groups:
        sg = jnp.where(valid, s[g], NEG_INF)
        m.append(jnp.maximum(jnp.max(sg, axis=-1, keepdims=True), sk[g]))
        p.append(jnp.exp(sg - m[g]))
    yield
    outs = []
    for g in groups:
        den = jnp.sum(p[g], axis=-1, keepdims=True) + jnp.exp(sk[g] - m[g])
        o = _dot(p[g].astype(BF16), vband[:, dsl[g]]) / den
        outs += [o[WINDOW * j:WINDOW * (j + 1)] for j in range(SWA_GROUP)]
    kp_ref[...] = kc
    vp_ref[...] = vc
    yield
    return jnp.concatenate(outs, axis=1)


def _gate_stages(z_ref, start):
    for _ in range(start):
        yield
    gates = []
    for j in range(N_BRANCH):
        gates.append(jax.nn.sigmoid(z_ref[:, C_ZG + D_MODEL * j:C_ZG + D_MODEL * (j + 1)]))
        yield
    return gates


def _interleave(stage_fns):
    results = [None] * len(stage_fns)
    live = list(enumerate(stage_fns))
    while live:
        still = []
        for idx, gen in live:
            try:
                next(gen)
                still.append((idx, gen))
            except StopIteration as stop:
                results[idx] = stop.value
        live = still
    return results


def _mix_kernel(sink_ref, x_ref, xn_ref, tc0, tc1, tc2, tc3, tn0, tn1, tn2, tn3, g_ref, w_ref, wa_ref, ba_ref,
                gm_ref, mk_ref, d_ref, qdec_ref, kdec_ref, adec_ref, hk_ref, hv_ref, hs_ref, gna_ref, gnb_ref,
                wb_ref, wo_ref, y_ref, sta_ref, stb_ref, kv_ref, z_scr, sa_scr, sb_scr, kp_scr, vp_scr):
    i = pl.program_id(0)
    nb = x_ref.shape[0]

    def rows_of(ref3):
        return ref3[...].reshape(nb * TT, ref3.shape[2])

    def tiled(tab_refs):
        return tuple(jnp.concatenate([r[...]] * nb, axis=0) for r in tab_refs)

    @pl.when(i == 0)
    def _():
        sa_scr[...] = jnp.zeros_like(sa_scr)
        sb_scr[...] = jnp.zeros_like(sb_scr)
        kp_scr[...] = jnp.zeros_like(kp_scr)
        vp_scr[...] = jnp.zeros_like(vp_scr)
        for _ in _in_stages(rows_of(x_ref), g_ref, w_ref, wa_ref, ba_ref, tiled((tc0, tc1, tc2, tc3)), z_scr.at[0]):
            pass

    slot = i % 2
    zc = z_scr.at[slot]
    first_col = jnp.where(i > 0, 0, WINDOW)
    stages = [_in_stages(rows_of(xn_ref), g_ref, w_ref, wa_ref, ba_ref, tiled((tn0, tn1, tn2, tn3)),
                         z_scr.at[1 - slot])]
    for b in range(nb):
        zb = zc.at[pl.ds(TT * b, TT)]
        stages += [_gla_tile(zb, gm_ref, mk_ref, hk_ref, hv_ref, hs_ref, gna_ref, sa_scr.at[b]),
                   _swa_tile(zb, sink_ref, kp_scr.at[b], vp_scr.at[b], first_col),
                   _ret_tile(zb, d_ref, qdec_ref, kdec_ref, adec_ref, hk_ref, hv_ref, hs_ref, gnb_ref, sb_scr.at[b]),
                   _gate_stages(zb, 4 + N_BRANCH * b)]
    branch = _interleave(stages)[1:]
    for b in range(nb):
        for h in range(GLA_HEADS):
            sta_ref[b, h] = sa_scr[b, GLA_DK * h:GLA_DK * (h + 1), GLA_DV * h:GLA_DV * (h + 1)]
            stb_ref[b, h] = sb_scr[b, RET_DK * h:RET_DK * (h + 1), RET_DV * h:RET_DV * (h + 1)]

    def stacked(j):
        return jnp.concatenate([branch[4 * b + j] for b in range(nb)], axis=0).astype(BF16)

    proj = [_dot(stacked(0), wb_ref[0:GLA_W, :]), _dot(stacked(2), wb_ref[GLA_W:GLA_W + RET_W, :]),
            _dot(stacked(1), wb_ref[GLA_W + RET_W:, :])]
    mixed = []
    for b in range(nb):
        rows = slice(TT * b, TT * (b + 1))
        mixed.append(sum(branch[4 * b + 3][j] * proj[j][rows] for j in range(N_BRANCH)))
    out = _dot(jnp.concatenate(mixed, axis=0).astype(BF16), wo_ref[...])
    for b in range(nb):
        y_ref[b] = x_ref[b] + out[TT * b:TT * (b + 1)]
        kv_ref[b] = zc[pl.ds(TT * b, TT), C_SK:C_ZG]


MIX_VMEM_LIMIT = 58 * 1024 * 1024


def _mix_prompt(x, sink, layer, g_mix, w_main, wa, ba, tabs, consts, gna, gnb, wb, wo):
    nb, t, _ = x.shape
    nt = t // TT
    st_spec = pl.BlockSpec((nb, GLA_HEADS, GLA_DK, GLA_DV), lambda i: (0, 0, 0, 0))
    st_shape = jax.ShapeDtypeStruct((nb, GLA_HEADS, GLA_DK, GLA_DV), F32)
    nxt = lambda i: jnp.minimum(i + 1, nt - 1)
    tab_cur = pl.BlockSpec((TT, LANES), lambda i: (i, 0))
    tab_nxt = pl.BlockSpec((TT, LANES), lambda i: (nxt(i), 0))
    return pl.pallas_call(
        _mix_kernel,
        grid=(nt,),
        in_specs=[pl.BlockSpec(memory_space=pltpu.SMEM),
                  pl.BlockSpec((nb, TT, D_MODEL), lambda i: (0, i, 0)),
                  pl.BlockSpec((nb, TT, D_MODEL), lambda i: (0, nxt(i), 0))]
        + [tab_cur] * 4 + [tab_nxt] * 4
        + [_layer_spec((1, D_MODEL), layer), _layer_spec((D_MODEL, W_MAIN), layer),
           _layer_spec((LANES, 256), layer), _layer_spec((1, 256), layer)]
        + [_const_spec(c.shape) for c in consts]
        + [_const_spec((1, GLA_W)), _const_spec((1, RET_W)),
           _layer_spec((GLA_W + RET_W + SWA_W, D_MODEL), layer),
           _layer_spec((D_MODEL, D_MODEL), layer)],
        out_specs=[pl.BlockSpec((nb, TT, D_MODEL), lambda i: (0, i, 0)), st_spec, st_spec,
                   pl.BlockSpec((nb, WINDOW, 2 * LANES), lambda i: (0, 0, 0))],
        out_shape=[jax.ShapeDtypeStruct((nb, t, D_MODEL), F32), st_shape, st_shape,
                   jax.ShapeDtypeStruct((nb, WINDOW, 2 * LANES), F32)],
        scratch_shapes=[pltpu.VMEM((2, nb * TT, Z_W), F32),
                        pltpu.VMEM((nb, GLA_HEADS * GLA_DK, GLA_W), F32),
                        pltpu.VMEM((nb, RET_HEADS * RET_DK, RET_W), F32),
                        pltpu.VMEM((nb, WINDOW, 128), BF16), pltpu.VMEM((nb, WINDOW, 128), BF16)],
        compiler_params=pltpu.CompilerParams(dimension_semantics=("arbitrary",), vmem_limit_bytes=MIX_VMEM_LIMIT),
        name="mix_prompt",
    )(sink, x, x, *tabs, *tabs, g_mix, w_main, wa, ba, *consts, gna, gnb, wb, wo)


def _swa_sample_masks():
    r = np.arange(DEC_SEQ * SB)
    rt, rb = r // SB, r % SB
    c = np.arange(SB * WINDOW)
    cb, cj = c // WINDOW, c % WINDOW
    m_cache = (rb[:, None] == cb[None, :]) & (cj[None, :] > rt[:, None])
    m_new = (rb[:, None] == rb[None, :]) & (rt[None, :] <= rt[:, None])
    return m_cache.astype(np.float32), m_new.astype(np.float32)


def _swa_sample_stages(sink_ref, q, kn, vn, kc_ref, vc_ref, mc_ref, mn_ref):
    q = q.astype(BF16)
    kn = kn.astype(BF16)
    vn = vn.astype(BF16)
    kc = kc_ref[...].reshape(SB * WINDOW, 128).astype(BF16)
    vc = vc_ref[...].reshape(SB * WINDOW, 128).astype(BF16)
    ok_c = mc_ref[...] > 0.0
    ok_n = mn_ref[...] > 0.0
    outs = []
    for hq in range(SWA_Q_HEADS):
        g = hq // SWA_GROUP
        ds_ = slice(SWA_HEAD_DIM * g, SWA_HEAD_DIM * (g + 1))
        qh = q[:, SWA_HEAD_DIM * hq:SWA_HEAD_DIM * (hq + 1)]
        s1 = jnp.where(ok_c, _dot_nt(qh, kc[:, ds_]) * (SWA_HEAD_DIM ** -0.5), NEG_INF)
        s2 = jnp.where(ok_n, _dot_nt(qh, kn[:, ds_]) * (SWA_HEAD_DIM ** -0.5), NEG_INF)
        sk = sink_ref[hq]
        m = jnp.maximum(jnp.maximum(jnp.max(s1, axis=-1, keepdims=True),
                                    jnp.max(s2, axis=-1, keepdims=True)), sk)
        p1 = jnp.exp(s1 - m)
        p2 = jnp.exp(s2 - m)
        den = jnp.sum(p1, axis=-1, keepdims=True) + jnp.sum(p2, axis=-1, keepdims=True) + jnp.exp(sk - m)
        outs.append((_dot(p1.astype(BF16), vc[:, ds_]) + _dot(p2.astype(BF16), vn[:, ds_])) / den)
        if hq % 2 == 1:
            yield
    return jnp.concatenate(outs, axis=1)


def _mix_sample_kernel(sink_ref, x_ref, z_ref, lar_ref, sa0_ref, sb0_ref, kc_ref, vc_ref, gna_ref, gnb_ref,
                       rep_ref, bd_ref, mc_ref, mn_ref, wb_ref, wo_ref, y_ref, sa1_ref, sb1_ref):
    row0 = pl.multiple_of(pl.program_id(0) * SB, SB)

    def rows(c0, c1):
        return _block_rows(z_ref, row0, c0, c1)

    def stacked(c0, c1):
        return jnp.concatenate(rows(c0, c1), axis=0)

    la_ret = [lar_ref[...]] * DEC_SEQ
    oa, ob, oc = _interleave([
        _rec_sample_stages(rows(C_GQ, C_GK), rows(C_GK, C_GV), rows(C_GV, C_GR), rows(C_GR, C_RQ),
                           rows(C_LA, Z_W), sa0_ref, sa1_ref, gna_ref, rep_ref, bd_ref),
        _rec_sample_stages(rows(C_RQ, C_RK), rows(C_RK, C_RV), rows(C_RV, C_RG), rows(C_RG, C_SQ),
                           la_ret, sb0_ref, sb1_ref, gnb_ref, rep_ref, bd_ref),
        _swa_sample_stages(sink_ref, stacked(C_SQ, C_SK), stacked(C_SK, C_SV), stacked(C_SV, C_ZG),
                           kc_ref, vc_ref, mc_ref, mn_ref)])
    mixed = sum(jax.nn.sigmoid(stacked(C_ZG + D_MODEL * j, C_ZG + D_MODEL * (j + 1)))
                * _dot(o.astype(BF16), wb_ref[512 * j:512 * (j + 1), :])
                for j, o in enumerate((oa, ob, oc)))
    out = _dot(mixed.astype(BF16), wo_ref[...])
    for t in range(DEC_SEQ):
        r = pl.ds(t * DEC_BATCH + row0, SB)
        y_ref[r, :] = x_ref[r, :] + out[t * SB:(t + 1) * SB, :]


def _mix_sample(x, z, sink, la_ret, state_gla, state_ret, cache_k, cache_v, layer, gna, gnb, consts, wb, wo):
    n = x.shape[0]
    rep, bd, mc, mn = consts
    whole = lambda w: pl.BlockSpec((n, w), lambda j: (0, 0), pipeline_mode=pl.Buffered(1))
    st_in = pl.BlockSpec((None, SB, GLA_HEADS, GLA_DK, GLA_DV), lambda j: (layer, j, 0, 0, 0))
    st_out = pl.BlockSpec((SB, GLA_HEADS, GLA_DK, GLA_DV), lambda j: (j, 0, 0, 0))
    cache = pl.BlockSpec((None, SB, WINDOW, 128), lambda j: (layer, j, 0, 0))
    st_shape = jax.ShapeDtypeStruct(state_gla.shape[1:], F32)
    return pl.pallas_call(
        _mix_sample_kernel,
        grid=(DEC_BATCH // SB,),
        in_specs=[pl.BlockSpec(memory_space=pltpu.SMEM), whole(D_MODEL), whole(Z_W),
                  _const_spec(la_ret.shape), st_in, st_in, cache, cache,
                  _const_spec((1, GLA_W)), _const_spec((1, RET_W)),
                  _const_spec(rep.shape), _const_spec(bd.shape), _const_spec(mc.shape), _const_spec(mn.shape),
                  _layer_spec((GLA_W + RET_W + SWA_W, D_MODEL), layer),
                  _layer_spec((D_MODEL, D_MODEL), layer)],
        out_specs=[pl.BlockSpec((n, D_MODEL), lambda j: (0, 0)), st_out, st_out],
        out_shape=[jax.ShapeDtypeStruct((n, D_MODEL), F32), st_shape, st_shape],
        compiler_params=_params(("arbitrary",)),
        name="mix_sample",
    )(sink, x, z, la_ret, state_gla, state_ret, cache_k, cache_v, gna, gnb, rep, bd, mc, mn, wb, wo)


FF_CHUNK = D_FF // 2


def _ffn_tail(x1, pe_ref, gp_ref, wpg_ref, wpp_ref, gf_ref, y_ref, final):
    hp = _rms(x1, gp_ref[...]).astype(BF16)
    x2 = x1 + jax.nn.sigmoid(_dot(hp, wpg_ref[...])) * _dot(pe_ref[...].astype(BF16), wpp_ref[...])
    y_ref[...] = _rms(x2, gf_ref[...]) if final else x2


def _ffn_prompt_kernel(x_ref, pe_ref, gn_ref, wi_ref, cw_ref, cb_ref, wd_ref, gp_ref, wpg_ref, wpp_ref,
                       gf_ref, y_ref, tail_ref, carry_scr, *, final):
    tm = x_ref.shape[0]

    @pl.when(pl.program_id(1) == 0)
    def _():
        carry_scr[...] = jnp.zeros_like(carry_scr)

    x = x_ref[...]
    hb = _rms(x, gn_ref[...]).astype(BF16)
    row = lax.broadcasted_iota(jnp.int32, (tm, 1), 0)
    acc = x
    for c0 in range(0, D_FF, FF_CHUNK):
        cs = slice(c0, c0 + FF_CHUNK)
        a = _dot(hb, wi_ref[:, c0:c0 + FF_CHUNK])
        bb = _dot(hb, wi_ref[:, D_FF + c0:D_FF + c0 + FF_CHUNK])
        p0 = carry_scr[6:7, cs]
        p1 = carry_scr[7:8, cs]
        a1 = jnp.where(row == 0, p1, pltpu.roll(a, 1, 0))
        a2 = jnp.where(row == 0, p0, jnp.where(row == 1, p1, pltpu.roll(a, 2, 0)))
        conv = cb_ref[:, cs] + cw_ref[0:1, cs] * a2 + cw_ref[1:2, cs] * a1 + cw_ref[2:3, cs] * a
        act = (jax.nn.gelu(conv) * bb).astype(BF16)
        acc = acc + _dot(act, wd_ref[c0:c0 + FF_CHUNK, :])
        carry_scr[:, cs] = a[tm - 8:tm, :]
        tail_ref[:, cs] = a[tm - 8:tm, :]
    _ffn_tail(acc, pe_ref, gp_ref, wpg_ref, wpp_ref, gf_ref, y_ref, final)


def _ffn_sample_kernel(x_ref, pe_ref, st_ref, gn_ref, wi_ref, cw_ref, cb_ref, wd_ref, gp_ref, wpg_ref,
                       wpp_ref, gf_ref, y_ref, tail_ref, *, final):
    ns = DEC_BATCH
    x = x_ref[...]
    hb = _rms(x, gn_ref[...]).astype(BF16)
    acc = x
    for c0 in range(0, D_FF, FF_CHUNK):
        cs = slice(c0, c0 + FF_CHUNK)
        a = _dot(hb, wi_ref[:, c0:c0 + FF_CHUNK])
        bb = _dot(hb, wi_ref[:, D_FF + c0:D_FF + c0 + FF_CHUNK])
        st0 = st_ref[0, :, cs]
        st1 = st_ref[1, :, cs]
        a1 = jnp.concatenate([st1, a[0:3 * ns]], axis=0)
        a2 = jnp.concatenate([st0, st1, a[0:2 * ns]], axis=0)
        conv = cb_ref[:, cs] + cw_ref[0:1, cs] * a2 + cw_ref[1:2, cs] * a1 + cw_ref[2:3, cs] * a
        act = (jax.nn.gelu(conv) * bb).astype(BF16)
        acc = acc + _dot(act, wd_ref[c0:c0 + FF_CHUNK, :])
        tail_ref[:, cs] = a[2 * ns:4 * ns, :]
    _ffn_tail(acc, pe_ref, gp_ref, wpg_ref, wpp_ref, gf_ref, y_ref, final)


def _ffn_weight_specs(layer):
    return [_layer_spec((1, D_MODEL), layer),
            _layer_spec((D_MODEL, 2 * D_FF), layer),
            _layer_spec((CONV_W, D_FF), layer),
            _layer_spec((1, D_FF), layer),
            _layer_spec((D_FF, D_MODEL), layer),
            _layer_spec((1, D_MODEL), layer),
            _layer_spec((D_MODEL, D_MODEL), layer),
            _layer_spec((D_PLE, D_MODEL), layer),
            _const_spec((1, D_MODEL))]


def _ffn_prompt(x, pe, layer, weights, nb, t, tm, final):
    nt = t // tm
    tok = lambda w: pl.BlockSpec((tm, w), lambda b, i: (b * nt + i, 0))
    return pl.pallas_call(
        functools.partial(_ffn_prompt_kernel, final=final),
        grid=(nb, nt),
        in_specs=[tok(D_MODEL),
                  pl.BlockSpec((None, tm, D_PLE), lambda b, i: (layer, b * nt + i, 0))]
        + _ffn_weight_specs(layer),
        out_specs=[tok(D_MODEL), pl.BlockSpec((8, D_FF), lambda b, i: (b, 0))],
        out_shape=[jax.ShapeDtypeStruct((nb * t, D_MODEL), F32),
                   jax.ShapeDtypeStruct((nb * 8, D_FF), F32)],
        scratch_shapes=[pltpu.VMEM((8, D_FF), F32)],
        compiler_params=_params(("arbitrary", "arbitrary")),
        name="ffn_prompt",
    )(x, pe, *weights)


def _ffn_sample(x, pe, st, layer, weights, final):
    n = x.shape[0]
    return pl.pallas_call(
        functools.partial(_ffn_sample_kernel, final=final),
        grid=(1,),
        in_specs=[pl.BlockSpec((n, D_MODEL), lambda i: (0, 0)),
                  pl.BlockSpec((None, n, D_PLE), lambda i: (layer, 0, 0)),
                  pl.BlockSpec((None, CONV_W - 1, DEC_BATCH, D_FF), lambda i: (layer, 0, 0, 0))]
        + _ffn_weight_specs(layer),
        out_specs=[pl.BlockSpec((n, D_MODEL), lambda i: (0, 0)),
                   pl.BlockSpec((n // 2, D_FF), lambda i: (0, 0))],
        out_shape=[jax.ShapeDtypeStruct((n, D_MODEL), F32),
                   jax.ShapeDtypeStruct((n // 2, D_FF), F32)],
        compiler_params=_params(("arbitrary",)),
        name="ffn_sample",
    )(x, pe, st, *weights)


def _rope_tables(pos, inv_freq):
    half = inv_freq.shape[0]
    ang = pos.astype(np.float64)[:, None] * inv_freq[None, :]
    c, s = np.cos(ang), np.sin(ang)
    rest = SWA_HEAD_DIM - 2 * half
    cos64 = np.concatenate([c, c, np.ones((pos.shape[0], rest))], axis=1)
    sin64 = np.concatenate([-s, s, np.zeros((pos.shape[0], rest))], axis=1)
    return (jnp.asarray(np.concatenate([cos64, cos64], axis=1), F32),
            jnp.asarray(np.concatenate([sin64, sin64], axis=1), F32))


def _pack_w_in(w_in):
    ga0 = sum(IN_SIZES[:4])
    ga = jnp.pad(w_in[:, :, ga0:ga0 + GLA_RANK], ((0, 0), (0, 0), (0, LANES - GLA_RANK)))
    return jnp.concatenate([w_in[:, :, :ga0], w_in[:, :, ga0 + GLA_RANK:], ga], axis=2).astype(BF16)


def kernel(x_prompt, x_sample, state_gla, state_ret, cache_swa_k, cache_swa_v, state_conv, p_prompt,
           p_sample, norm_mix, w_in, w_gla_a, b_gla_a, gla_norm, ret_norm, swa_sink, w_branch, w_out,
           norm_ffn, w_ffn_in, conv_w, conv_b, w_ffn_out, norm_ple, w_ple_gate, w_ple_proj, norm_final):
    nb, t, _ = x_prompt.shape
    ns, ts, _ = x_sample.shape
    n_s = ns * ts

    w_main = _pack_w_in(w_in)
    wa = jnp.pad(w_gla_a, ((0, 0), (0, LANES - GLA_RANK), (0, 0))).astype(BF16)
    ba = b_gla_a[:, None, :]
    wb = w_branch.astype(BF16)
    wo = w_out.astype(BF16)
    ffn_weights = (norm_ffn[:, None, :], w_ffn_in.astype(BF16), conv_w, conv_b[:, None, :],
                   w_ffn_out.astype(BF16), norm_ple[:, None, :], w_ple_gate.astype(BF16),
                   w_ple_proj.astype(BF16), norm_final[None, :])
    g_mix = norm_mix[:, None, :]
    gn_gla = jnp.tile(gla_norm, (1, GLA_HEADS))[:, None, :]
    gn_ret = ret_norm.reshape(DEPTH, 1, RET_W)

    ret_freq = 1.0 / (RET_THETA ** np.linspace(0.0, 1.0, RET_DK // 2))
    swa_freq = 1.0 / (ROPE_THETA ** (np.arange(0, ROPE_DIM, 2, dtype=np.float64) / ROPE_DIM))
    pos_p = np.arange(t)
    pos_s = PAST_LEN + np.arange(n_s) // ns
    tabs_p = _rope_tables(pos_p, ret_freq) + _rope_tables(pos_p, swa_freq)
    tabs_s = _rope_tables(pos_s, ret_freq) + _rope_tables(pos_s, swa_freq)

    gm_np, mk_np = _gla_constants()
    d_np, qdec_np, kdec_np, adec_np = _ret_constants()
    hk_np, hv_np, hs_np = _head_masks()
    mix_consts = (jnp.asarray(gm_np, BF16), jnp.asarray(mk_np), jnp.asarray(d_np), jnp.asarray(qdec_np),
                  jnp.asarray(kdec_np), jnp.asarray(adec_np), jnp.asarray(hk_np, BF16), jnp.asarray(hv_np, BF16),
                  jnp.asarray(hs_np))
    log_gamma = np.log1p(-np.exp2(-5.0 - np.arange(RET_HEADS, dtype=np.float64)))
    la_ret = jnp.asarray(np.broadcast_to(np.repeat(log_gamma, RET_DK)[None, :], (SB, RET_HEADS * RET_DK)), F32)
    mc_np, mn_np = _swa_sample_masks()
    rep_np, bd_np = _rec_sample_constants()
    sample_consts = (jnp.asarray(rep_np, BF16), jnp.asarray(bd_np), jnp.asarray(mc_np), jnp.asarray(mn_np))

    xp = x_prompt.reshape(nb * t, D_MODEL)
    xs = x_sample.transpose(1, 0, 2).reshape(n_s, D_MODEL)
    pe_p = p_prompt.reshape(DEPTH, nb * t, D_PLE)
    pe_s = p_sample.transpose(0, 2, 1, 3).reshape(DEPTH, n_s, D_PLE)
    conv_st = state_conv.transpose(0, 2, 1, 3)
    ck = cache_swa_k.reshape(DEPTH, ns, WINDOW, SWA_KV_HEADS * SWA_HEAD_DIM)
    cv = cache_swa_v.reshape(DEPTH, ns, WINDOW, SWA_KV_HEADS * SWA_HEAD_DIM)

    outs = {k_: [] for k_ in ("gla_p", "gla_s", "ret_p", "ret_s", "kv_p", "kv_s", "conv_p", "conv_s")}
    for l in range(DEPTH):
        final = l == DEPTH - 1
        sink = swa_sink[l]

        xp3, st_a, st_b, kv_tail = _mix_prompt(xp.reshape(nb, t, D_MODEL), sink, l, g_mix, w_main, wa, ba, tabs_p,
                                               mix_consts, gn_gla[l], gn_ret[l], wb, wo)
        xp, tail = _ffn_prompt(xp3.reshape(nb * t, D_MODEL), pe_p, l, ffn_weights, nb, t, 256, final)
        outs["gla_p"].append(st_a)
        outs["ret_p"].append(st_b)
        outs["kv_p"].append(kv_tail)
        outs["conv_p"].append(tail.reshape(nb, 8, D_FF)[:, 8 - (CONV_W - 1):])

        zs = _in_proj(xs, l, g_mix, w_main, wa, ba, tabs_s, 256)
        xs, st_a, st_b = _mix_sample(xs, zs, sink, la_ret, state_gla, state_ret, ck, cv, l, gn_gla[l], gn_ret[l],
                                     sample_consts, wb, wo)
        xs, tail_s = _ffn_sample(xs, pe_s, conv_st, l, ffn_weights, final)
        outs["gla_s"].append(st_a)
        outs["ret_s"].append(st_b)
        outs["kv_s"].append(zs[:, C_SK:C_ZG])
        outs["conv_s"].append(tail_s.reshape(CONV_W - 1, ns, D_FF).transpose(1, 0, 2))

    y_prompt = xp.reshape(nb, t, D_MODEL)
    y_sample = xs.reshape(ts, ns, D_MODEL).transpose(1, 0, 2)
    st = {k_: jnp.stack(v_) for k_, v_ in outs.items()}

    def kv_heads(a):
        return a.reshape(a.shape[:-1] + (SWA_KV_HEADS, SWA_HEAD_DIM))

    k_p = kv_heads(st["kv_p"][..., :LANES])
    v_p = kv_heads(st["kv_p"][..., LANES:])
    kv_new = st["kv_s"].reshape(DEPTH, ts, ns, 2 * LANES).transpose(0, 2, 1, 3)
    k_s = jnp.concatenate([cache_swa_k[:, :, ts:], kv_heads(kv_new[..., :LANES])], axis=2)
    v_s = jnp.concatenate([cache_swa_v[:, :, ts:], kv_heads(kv_new[..., LANES:])], axis=2)
    return (y_prompt, y_sample, st["gla_p"], st["gla_s"], st["ret_p"], st["ret_s"],
            k_p, k_s, v_p, v_s, st["conv_p"], st["conv_s"])
```

```python
import functools
import math

import numpy as np
import jax
import jax.numpy as jnp
from jax import lax
from jax.experimental import pallas as pl
from jax.experimental.pallas import tpu as pltpu

F32 = jnp.float32
BF16 = jnp.bfloat16

D_MODEL = 1024
BATCH = 2
SEQ = 8192
DEPTH = 4
DEC_BATCH = 128
DEC_SEQ = 4
PAST_LEN = 8192
D_PLE = 256
GLA_HEADS = 4
GLA_DK = 64
GLA_DV = 128
GLA_RANK = 16
GLA_TAU = 16.0
RET_HEADS = 4
RET_DK = 64
RET_DV = 128
RET_THETA = 10000.0
SWA_Q_HEADS = 8
SWA_KV_HEADS = 2
SWA_HEAD_DIM = 64
SWA_GROUP = SWA_Q_HEADS // SWA_KV_HEADS
WINDOW = 128
ROPE_THETA = 500000.0
ROPE_DIM = SWA_HEAD_DIM // 4
D_FF = 2816
CONV_W = 3
N_BRANCH = 3
EPS = 1e-6
NEG_INF = -1e30

GLA_W = GLA_HEADS * GLA_DV
RET_W = RET_HEADS * RET_DV
SWA_W = SWA_Q_HEADS * SWA_HEAD_DIM
IN_SIZES = (GLA_HEADS * GLA_DK, GLA_HEADS * GLA_DK, GLA_W, GLA_W, GLA_RANK,
            RET_HEADS * RET_DK, RET_HEADS * RET_DK, RET_W, RET_W,
            SWA_W, SWA_KV_HEADS * SWA_HEAD_DIM, SWA_KV_HEADS * SWA_HEAD_DIM,
            N_BRANCH * D_MODEL)

LANES = 128
VMEM_LIMIT = 52 * 1024 * 1024

C_GQ = 0
C_GK = 256
C_GV = 512
C_GR = 1024
C_RQ = 1536
C_RK = 1792
C_RV = 2048
C_RG = 2560
C_SQ = 3072
C_SK = 3584
C_SV = 3712
C_LA = 3840
C_ZG = 4096
Z_W = 7168
W_ZG = 3840
W_GA = 6912
W_MAIN = 7040

TT = 128
N_LEVELS = 7
SB = 16


def _dot(a, b):
    return jnp.dot(a, b, preferred_element_type=F32)


def _dot_nt(a, b):
    return lax.dot_general(a, b, (((1,), (1,)), ((), ())), preferred_element_type=F32)


def _dot_tn(a, b):
    return lax.dot_general(a, b, (((0,), (0,)), ((), ())), preferred_element_type=F32)


def _rms(x, g):
    return x * lax.rsqrt(jnp.mean(x * x, axis=-1, keepdims=True) + EPS) * g


def _layer_spec(shape, layer):
    nd = len(shape)
    return pl.BlockSpec((None,) + tuple(shape), lambda *_: (layer,) + (0,) * nd,
                        pipeline_mode=pl.Buffered(1))


def _const_spec(shape):
    nd = len(shape)
    return pl.BlockSpec(tuple(shape), lambda *_: (0,) * nd, pipeline_mode=pl.Buffered(1))


def _params(sem):
    return pltpu.CompilerParams(dimension_semantics=sem, vmem_limit_bytes=VMEM_LIMIT)


def _rope_block(x, cos, sin_s, first):
    half_mask, half = first
    xr = jnp.where(half_mask, pltpu.roll(x, LANES - half, 1), pltpu.roll(x, half, 1))
    return x * cos + xr * sin_s


def _gate_logits(hb, w_ref, j):
    return _dot(hb, w_ref[:, W_ZG + D_MODEL * j:W_ZG + D_MODEL * (j + 1)])


def _in_stages(x, g_ref, w_ref, wa_ref, ba_ref, tabs, z_ref, with_gates):
    hb = _rms(x, g_ref[...]).astype(BF16)

    def mm(c0, c1):
        return _dot(hb, w_ref[:, c0:c1])

    lane = lax.broadcasted_iota(jnp.int32, (1, LANES), 1) % 64
    ret_first = (lane < RET_DK // 2, RET_DK // 2)
    swa_first = (lane < ROPE_DIM // 2, ROPE_DIM // 2)
    rc, rs, sc, ss = tabs

    ga = mm(W_GA, W_MAIN)
    xa = _dot(ga.astype(BF16), wa_ref[...]) + ba_ref[...]
    log_sig = jnp.minimum(xa, 0.0) - jnp.log1p(jnp.exp(-jnp.abs(xa)))
    z_ref[:, C_LA:C_ZG] = log_sig * (1.0 / GLA_TAU)
    z_ref[:, C_GQ:C_GK] = mm(C_GQ, C_GK) * (GLA_DK ** -0.5)
    yield
    z_ref[:, C_GK:C_RQ] = mm(C_GK, C_RQ)
    yield
    rq = mm(C_RQ, C_RK)
    rk = mm(C_RK, C_RV)
    for j in range(2):
        sl = slice(LANES * j, LANES * (j + 1))
        z_ref[:, C_RQ + LANES * j:C_RQ + LANES * (j + 1)] = _rope_block(rq[:, sl], rc, rs, ret_first)
        z_ref[:, C_RK + LANES * j:C_RK + LANES * (j + 1)] = (
            _rope_block(rk[:, sl], rc, rs, ret_first) * (RET_DK ** -0.5))
    yield
    z_ref[:, C_RV:C_SQ] = mm(C_RV, C_SQ)
    yield
    sqk = mm(C_SQ, C_SV)
    for j in range(5):
        sl = slice(LANES * j, LANES * (j + 1))
        z_ref[:, C_SQ + LANES * j:C_SQ + LANES * (j + 1)] = _rope_block(sqk[:, sl], sc, ss, swa_first)
    z_ref[:, C_SV:C_LA] = mm(C_SV, C_LA)
    yield
    if with_gates:
        for j in range(N_BRANCH):
            z_ref[:, C_ZG + D_MODEL * j:C_ZG + D_MODEL * (j + 1)] = _gate_logits(hb, w_ref, j)
            yield


def _in_kernel(x_ref, g_ref, w_ref, wa_ref, ba_ref, rc_ref, rs_ref, sc_ref, ss_ref, z_ref):
    tabs = (rc_ref[...], rs_ref[...], sc_ref[...], ss_ref[...])
    for _ in _in_stages(x_ref[...], g_ref, w_ref, wa_ref, ba_ref, tabs, z_ref, True):
        pass


def _in_proj(x, layer, g_mix, w_main, wa, ba, tabs, tm):
    n = x.shape[0]
    rc, rs, sc, ss = tabs
    nt = rc.shape[0] // tm
    tab = pl.BlockSpec((tm, LANES), lambda i: (i % nt, 0))
    return pl.pallas_call(
        _in_kernel,
        grid=(n // tm,),
        in_specs=[pl.BlockSpec((tm, D_MODEL), lambda i: (i, 0)),
                  _layer_spec((1, D_MODEL), layer),
                  _layer_spec((D_MODEL, W_MAIN), layer),
                  _layer_spec((LANES, 256), layer),
                  _layer_spec((1, 256), layer),
                  tab, tab, tab, tab],
        out_specs=pl.BlockSpec((tm, Z_W), lambda i: (i, 0)),
        out_shape=jax.ShapeDtypeStruct((n, Z_W), F32),
        compiler_params=_params(("arbitrary",)),
        name="in_proj",
    )(x, g_mix, w_main, wa, ba, rc, rs, sc, ss)


def _gla_constants():
    t = np.arange(TT)
    g = np.zeros((2 + N_LEVELS, TT, TT), np.float32)
    g[0] = (t[None, :] <= t[:, None])
    g[1] = (t[None, :] > t[:, None])
    m = np.zeros((1 + N_LEVELS, TT, TT), np.float32)
    m[0] = np.eye(TT)
    for lv in range(1, N_LEVELS + 1):
        bs, hf = 2 ** lv, 2 ** (lv - 1)
        bd = (t // bs) * bs + hf - 1
        upper = (t % bs) >= hf
        u = t[None, :]
        g[1 + lv] = np.where(upper[:, None], (u > bd[:, None]) & (u <= t[:, None]),
                             (u > t[:, None]) & (u <= bd[:, None]))
        same = (t[:, None] // bs) == (t[None, :] // bs)
        m[lv] = same & upper[:, None] & (~upper)[None, :]
    return g.reshape(-1, TT), np.tile(m, (1, 1, GLA_HEADS))


def _head_masks():
    hk = np.arange(GLA_HEADS * TT)[:, None] // TT == np.arange(GLA_HEADS * GLA_DK)[None, :] // GLA_DK
    hv = np.arange(GLA_HEADS * TT)[:, None] // TT == np.arange(GLA_W)[None, :] // GLA_DV
    hs = np.arange(GLA_HEADS * GLA_DK)[:, None] // GLA_DK == np.arange(GLA_W)[None, :] // GLA_DV
    return hk.astype(np.float32), hv.astype(np.float32), hs.astype(np.float32)


def _heads_blockdiag(x, mask):
    return jnp.concatenate([x] * GLA_HEADS, axis=0) * mask


def _col_vector(row):
    n = row.shape[1]
    eye = lax.broadcasted_iota(jnp.int32, (n, n), 0) == lax.broadcasted_iota(jnp.int32, (n, n), 1)
    return jnp.sum(jnp.where(eye, row, 0.0), axis=1, keepdims=True)


def _norm_gate(o, gn, gate):
    outs = [_rms(o[:, GLA_DV * h:GLA_DV * (h + 1)], gn[:, GLA_DV * h:GLA_DV * (h + 1)]) for h in range(GLA_HEADS)]
    return jnp.concatenate(outs, axis=1) * (gate * jax.nn.sigmoid(gate))


def _gla_tile(z_ref, gm_ref, mk_ref, hk_ref, hv_ref, hs_ref, gn_ref, st_ref):
    la = z_ref[:, C_LA:C_ZG]
    la_hi = la.astype(BF16)
    la_lo = (la - la_hi.astype(F32)).astype(BF16)
    gm = gm_ref[...]
    ex = jnp.exp(_dot(gm, la_hi) + _dot(gm, la_lo))
    q = z_ref[:, C_GQ:C_GK]
    k = z_ref[:, C_GK:C_GV]
    e_b = ex[0:TT]
    a_row = e_b[TT - 1:TT, :]
    qd = (q * e_b).astype(BF16)
    kd = (k * ex[TT:2 * TT]).astype(BF16)
    qb = q.astype(BF16)
    kb = k.astype(BF16)
    ql = [(q * ex[(1 + lv) * TT:(2 + lv) * TT]).astype(BF16) for lv in range(1, N_LEVELS + 1)]
    kl = [(k * ex[(1 + lv) * TT:(2 + lv) * TT]).astype(BF16) for lv in range(1, N_LEVELS + 1)]
    hk = hk_ref[...]
    yield
    a = mk_ref[0] * _dot_nt(qb, _heads_blockdiag(kb, hk))
    for lv in range(N_LEVELS):
        yield
        a = a + mk_ref[lv + 1] * _dot_nt(ql[lv], _heads_blockdiag(kl[lv], hk))
    yield
    v = z_ref[:, C_GV:C_GR].astype(BF16)
    s = st_ref[...]
    o = _dot(a.astype(BF16), _heads_blockdiag(v, hv_ref[...])) + _dot(qd, s.astype(BF16))
    yield
    st_ref[...] = _col_vector(a_row) * s + hs_ref[...] * _dot_tn(kd, v)
    yield
    return _norm_gate(o, gn_ref[...], z_ref[:, C_GR:C_RQ])


def _ret_gammas():
    return [1.0 - 2.0 ** (-5.0 - h) for h in range(RET_HEADS)]


def _ret_constants():
    t = np.arange(TT, dtype=np.float64)
    d = np.zeros((RET_HEADS, TT, TT), np.float64)
    qdec = np.zeros((TT, RET_HEADS * RET_DK), np.float64)
    kdec = np.zeros((TT, RET_HEADS * RET_DK), np.float64)
    adec = np.zeros((RET_HEADS * RET_DK, LANES), np.float64)
    for h, gam in enumerate(_ret_gammas()):
        diff = t[:, None] - t[None, :]
        d[h] = np.where(diff >= 0, gam ** np.maximum(diff, 0.0), 0.0)
        qdec[:, h * RET_DK:(h + 1) * RET_DK] = (gam ** (t + 1.0))[:, None]
        kdec[:, h * RET_DK:(h + 1) * RET_DK] = (gam ** (TT - 1.0 - t))[:, None]
        adec[h * RET_DK:(h + 1) * RET_DK, :] = gam ** TT
    d = np.concatenate(list(d), axis=1)
    adec = np.tile(adec, (1, RET_W // LANES))
    return d.astype(np.float32), qdec.astype(np.float32), kdec.astype(np.float32), adec.astype(np.float32)


def _ret_tile(z_ref, d_ref, qdec_ref, kdec_ref, adec_ref, hk_ref, hv_ref, hs_ref, gn_ref, st_ref):
    q = z_ref[:, C_RQ:C_RK]
    k = z_ref[:, C_RK:C_RV]
    qd = (q * qdec_ref[...]).astype(BF16)
    kd = (k * kdec_ref[...]).astype(BF16)
    a = d_ref[...] * _dot_nt(q.astype(BF16), _heads_blockdiag(k.astype(BF16), hk_ref[...]))
    yield
    v = z_ref[:, C_RV:C_RG].astype(BF16)
    s = st_ref[...]
    o = _dot(a.astype(BF16), _heads_blockdiag(v, hv_ref[...])) + _dot(qd, s.astype(BF16))
    yield
    st_ref[...] = adec_ref[...] * s + hs_ref[...] * _dot_tn(kd, v)
    yield
    return _norm_gate(o, gn_ref[...], z_ref[:, C_RG:C_SQ])


def _rec_sample_constants():
    c = np.arange(SB * GLA_DK)
    rep = np.arange(GLA_DK)[:, None] == (c[None, :] % GLA_DK)
    bd = (np.arange(DEC_SEQ * SB)[:, None] % SB) == (c[None, :] // GLA_DK)
    return rep.astype(np.float32), bd.astype(np.float32)


def _block_rows(ref, row0, c0, c1):
    return [ref[pl.ds(t * DEC_BATCH + row0, SB), c0:c1] for t in range(DEC_SEQ)]


def _rec_sample_stages(q, k, v, gate, la, s0_ref, s1_ref, gn_ref, rep_ref, bd_ref):
    b = []
    for t in range(DEC_SEQ):
        b.append(la[t] if t == 0 else b[-1] + la[t])
    qd = jnp.concatenate([q[t] * jnp.exp(b[t]) for t in range(DEC_SEQ)], axis=0).astype(BF16)
    kd = jnp.concatenate([k[t] * jnp.exp(b[-1] - b[t]) for t in range(DEC_SEQ)], axis=0).astype(BF16)
    vv = jnp.concatenate(v, axis=0).astype(BF16)
    a = jnp.exp(b[-1])
    a_hi = a.astype(BF16)
    a_r = a - a_hi.astype(F32)
    a_mid = a_r.astype(BF16)
    a_lo = (a_r - a_mid.astype(F32)).astype(BF16)
    rep = rep_ref[...]
    bd = bd_ref[...]
    ones = jnp.ones((SB, GLA_DV), BF16)

    def expand(x, mask):
        return (_dot(x, rep) * mask).astype(BF16)

    yield
    oi = []
    for h in range(GLA_HEADS):
        hs = slice(GLA_DK * h, GLA_DK * (h + 1))
        vs = slice(GLA_DV * h, GLA_DV * (h + 1))
        s0 = s0_ref[:, h].reshape(SB * GLA_DK, GLA_DV)
        oi.append(_dot(expand(qd[:, hs], bd), s0.astype(BF16)))
        a_col = (_dot_tn(expand(a_hi[:, hs], bd[0:SB]), ones) + _dot_tn(expand(a_mid[:, hs], bd[0:SB]), ones)
                 + _dot_tn(expand(a_lo[:, hs], bd[0:SB]), ones))
        s1 = a_col * s0 + _dot_tn(expand(kd[:, hs], bd), vv[:, vs])
        s1_ref[:, h] = s1.reshape(SB, GLA_DK, GLA_DV)
        yield
    oi = jnp.concatenate(oi, axis=1)

    gn = gn_ref[...]
    outs = []
    for t in range(DEC_SEQ):
        o = oi[t * SB:(t + 1) * SB, :]
        for s in range(t + 1):
            p = q[t] * k[s] * jnp.exp(b[t] - b[s])
            parts = []
            for h in range(GLA_HEADS):
                hs = slice(GLA_DK * h, GLA_DK * (h + 1))
                vs = slice(GLA_DV * h, GLA_DV * (h + 1))
                parts.append(jnp.sum(p[:, hs], axis=1, keepdims=True) * v[s][:, vs])
            o = o + jnp.concatenate(parts, axis=1)
        outs.append(_norm_gate(o, gn, gate[t]))
        yield
    return jnp.concatenate(outs, axis=0)


def _swa_tile(z_ref, sink_ref, kp_ref, vp_ref, first_col):
    kc = z_ref[:, C_SK:C_SV].astype(BF16)
    vc = z_ref[:, C_SV:C_LA].astype(BF16)
    kband = jnp.concatenate([kp_ref[...], kc], axis=0)
    vband = jnp.concatenate([vp_ref[...], vc], axis=0)
    rows = SWA_GROUP * WINDOW
    r = lax.broadcasted_iota(jnp.int32, (rows, 2 * WINDOW), 0) & (WINDOW - 1)
    c = lax.broadcasted_iota(jnp.int32, (rows, 2 * WINDOW), 1)
    valid = (c > r) & (c <= r + WINDOW) & (c >= first_col)
    groups = range(SWA_KV_HEADS)
    dsl = [slice(SWA_HEAD_DIM * g, SWA_HEAD_DIM * (g + 1)) for g in groups]
    s, sk = [], []
    for g in groups:
        heads = range(SWA_GROUP * g, SWA_GROUP * (g + 1))
        q = jnp.concatenate(
            [z_ref[:, C_SQ + SWA_HEAD_DIM * hq:C_SQ + SWA_HEAD_DIM * (hq + 1)] for hq in heads], axis=0)
        sk.append(jnp.concatenate([jnp.full((WINDOW, 1), sink_ref[hq], F32) for hq in heads], axis=0))
        s.append(_dot_nt(q.astype(BF16), kband[:, dsl[g]]) * (SWA_HEAD_DIM ** -0.5))
    yield
    m, p = [], []
    for g in groups:
        sg = jnp.where(valid, s[g], NEG_INF)
        m.append(jnp.maximum(jnp.max(sg, axis=-1, keepdims=True), sk[g]))
        p.append(jnp.exp(sg - m[g]))
    yield
    outs = []
    for g in groups:
        den = jnp.sum(p[g], axis=-1, keepdims=True) + jnp.exp(sk[g] - m[g])
        o = _dot(p[g].astype(BF16), vband[:, dsl[g]]) / den
        outs += [o[WINDOW * j:WINDOW * (j + 1)] for j in range(SWA_GROUP)]
    kp_ref[...] = kc
    vp_ref[...] = vc
    yield
    return jnp.concatenate(outs, axis=1)


def _interleave(stage_fns):
    results = [None] * len(stage_fns)
    live = list(enumerate(stage_fns))
    while live:
        still = []
        for idx, gen in live:
            try:
                next(gen)
                still.append((idx, gen))
            except StopIteration as stop:
                results[idx] = stop.value
        live = still
    return results


def _merge_stages(pend_ref, x_ref, g_ref, w_ref, wb_ref, wo_ref, y_ref):
    nb = x_ref.shape[0]
    hb = _rms(x_ref[...].reshape(nb * TT, D_MODEL), g_ref[...]).astype(BF16)
    gates = []
    for j in range(N_BRANCH):
        gates.append(jax.nn.sigmoid(_gate_logits(hb, w_ref, j)))
        yield
    mixed = None
    for j in range(N_BRANCH):
        term = gates[j] * _dot(pend_ref[:, GLA_W * j:GLA_W * (j + 1)], wb_ref[GLA_W * j:GLA_W * (j + 1), :])
        mixed = term if mixed is None else mixed + term
        yield
    out = _dot(mixed.astype(BF16), wo_ref[...])
    for b in range(x_ref.shape[0]):
        y_ref[b] = x_ref[b] + out[TT * b:TT * (b + 1)]
    yield


def _mix_kernel(sink_ref, xp_ref, xn_ref, tp0, tp1, tp2, tp3, tn0, tn1, tn2, tn3, g_ref, w_ref, wa_ref, ba_ref,
                gm_ref, mk_ref, d_ref, qdec_ref, kdec_ref, adec_ref, hk_ref, hv_ref, hs_ref, gna_ref, gnb_ref,
                wb_ref, wo_ref, y_ref, sta_ref, stb_ref, kv_ref, z_scr, pend_scr, sa_scr, sb_scr, kp_scr, vp_scr,
                *, nt):
    i = pl.program_id(0)
    nb = xp_ref.shape[0]

    def rows_of(ref3):
        return ref3[...].reshape(nb * TT, ref3.shape[2])

    def tiled(tab_refs):
        return tuple(jnp.concatenate([r[...]] * nb, axis=0) for r in tab_refs)

    @pl.when(i == 0)
    def _():
        sa_scr[...] = jnp.zeros_like(sa_scr)
        sb_scr[...] = jnp.zeros_like(sb_scr)
        kp_scr[...] = jnp.zeros_like(kp_scr)
        vp_scr[...] = jnp.zeros_like(vp_scr)
        pend_scr[...] = jnp.zeros_like(pend_scr)
        for _ in _in_stages(rows_of(xp_ref), g_ref, w_ref, wa_ref, ba_ref, tiled((tp0, tp1, tp2, tp3)),
                            z_scr.at[0], False):
            pass

    slot = i % 2
    zc = z_scr.at[slot]
    first_col = jnp.where(i > 0, 0, WINDOW)
    stages = [_merge_stages(pend_scr, xp_ref, g_ref, w_ref, wb_ref, wo_ref, y_ref),
              _in_stages(rows_of(xn_ref), g_ref, w_ref, wa_ref, ba_ref, tiled((tn0, tn1, tn2, tn3)),
                         z_scr.at[1 - slot], False)]
    for b in range(nb):
        zb = zc.at[pl.ds(TT * b, TT)]
        stages += [_gla_tile(zb, gm_ref, mk_ref, hk_ref, hv_ref, hs_ref, gna_ref, sa_scr.at[b]),
                   _swa_tile(zb, sink_ref, kp_scr.at[b], vp_scr.at[b], first_col),
                   _ret_tile(zb, d_ref, qdec_ref, kdec_ref, adec_ref, hk_ref, hv_ref, hs_ref, gnb_ref, sb_scr.at[b])]
    branch = _interleave(stages)[2:]
    pend = [jnp.concatenate([branch[3 * b + k] for b in range(nb)], axis=0).astype(BF16) for k in (0, 2, 1)]

    @pl.when(i < nt)
    def _():
        for j in range(N_BRANCH):
            pend_scr[:, GLA_W * j:GLA_W * (j + 1)] = pend[j]
        for b in range(nb):
            for h in range(GLA_HEADS):
                sta_ref[b, h] = sa_scr[b, GLA_DK * h:GLA_DK * (h + 1), GLA_DV * h:GLA_DV * (h + 1)]
                stb_ref[b, h] = sb_scr[b, RET_DK * h:RET_DK * (h + 1), RET_DV * h:RET_DV * (h + 1)]
            kv_ref[b] = zc[pl.ds(TT * b, TT), C_SK:C_LA]


MIX_VMEM_LIMIT = 60 * 1024 * 1024


def _mix_prompt(x, sink, layer, g_mix, w_main, wa, ba, tabs, consts, gna, gnb, wb, wo):
    nb, t, _ = x.shape
    nt = t // TT
    st_spec = pl.BlockSpec((nb, GLA_HEADS, GLA_DK, GLA_DV), lambda i: (0, 0, 0, 0))
    st_shape = jax.ShapeDtypeStruct((nb, GLA_HEADS, GLA_DK, GLA_DV), F32)
    nxt = lambda i: jnp.minimum(i + 1, nt - 1)
    prv = lambda i: jnp.maximum(i - 1, 0)
    tab_prv = pl.BlockSpec((TT, LANES), lambda i: (prv(i), 0))
    tab_nxt = pl.BlockSpec((TT, LANES), lambda i: (nxt(i), 0))
    return pl.pallas_call(
        functools.partial(_mix_kernel, nt=nt),
        grid=(nt + 1,),
        in_specs=[pl.BlockSpec(memory_space=pltpu.SMEM),
                  pl.BlockSpec((nb, TT, D_MODEL), lambda i: (0, prv(i), 0)),
                  pl.BlockSpec((nb, TT, D_MODEL), lambda i: (0, nxt(i), 0))]
        + [tab_prv] * 4 + [tab_nxt] * 4
        + [_layer_spec((1, D_MODEL), layer), _layer_spec((D_MODEL, W_MAIN), layer),
           _layer_spec((LANES, 256), layer), _layer_spec((1, 256), layer)]
        + [_const_spec(c.shape) for c in consts]
        + [_const_spec((1, GLA_W)), _const_spec((1, RET_W)),
           _layer_spec((GLA_W + RET_W + SWA_W, D_MODEL), layer),
           _layer_spec((D_MODEL, D_MODEL), layer)],
        out_specs=[pl.BlockSpec((nb, TT, D_MODEL), lambda i: (0, prv(i), 0)), st_spec, st_spec,
                   pl.BlockSpec((nb, WINDOW, 2 * LANES), lambda i: (0, 0, 0))],
        out_shape=[jax.ShapeDtypeStruct((nb, t, D_MODEL), F32), st_shape, st_shape,
                   jax.ShapeDtypeStruct((nb, WINDOW, 2 * LANES), F32)],
        scratch_shapes=[pltpu.VMEM((2, nb * TT, C_ZG), F32),
                        pltpu.VMEM((nb * TT, GLA_W + RET_W + SWA_W), BF16),
                        pltpu.VMEM((nb, GLA_HEADS * GLA_DK, GLA_W), F32),
                        pltpu.VMEM((nb, RET_HEADS * RET_DK, RET_W), F32),
                        pltpu.VMEM((nb, WINDOW, 128), BF16), pltpu.VMEM((nb, WINDOW, 128), BF16)],
        compiler_params=pltpu.CompilerParams(dimension_semantics=("arbitrary",), vmem_limit_bytes=MIX_VMEM_LIMIT),
        name="mix_prompt",
    )(sink, x, x, *tabs, *tabs, g_mix, w_main, wa, ba, *consts, gna, gnb, wb, wo)


def _swa_sample_masks():
    r = np.arange(DEC_SEQ * SB)
    rt, rb = r // SB, r % SB
    c = np.arange(SB * WINDOW)
    cb, cj = c // WINDOW, c % WINDOW
    m_cache = (rb[:, None] == cb[None, :]) & (cj[None, :] > rt[:, None])
    m_new = (rb[:, None] == rb[None, :]) & (rt[None, :] <= rt[:, None])
    return m_cache.astype(np.float32), m_new.astype(np.float32)


def _swa_sample_stages(sink_ref, q, kn, vn, kc_ref, vc_ref, mc_ref, mn_ref):
    q = q.astype(BF16)
    kn = kn.astype(BF16)
    vn = vn.astype(BF16)
    kc = kc_ref[...].reshape(SB * WINDOW, 128).astype(BF16)
    vc = vc_ref[...].reshape(SB * WINDOW, 128).astype(BF16)
    ok_c = mc_ref[...] > 0.0
    ok_n = mn_ref[...] > 0.0
    outs = []
    for hq in range(SWA_Q_HEADS):
        g = hq // SWA_GROUP
        ds_ = slice(SWA_HEAD_DIM * g, SWA_HEAD_DIM * (g + 1))
        qh = q[:, SWA_HEAD_DIM * hq:SWA_HEAD_DIM * (hq + 1)]
        s1 = jnp.where(ok_c, _dot_nt(qh, kc[:, ds_]) * (SWA_HEAD_DIM ** -0.5), NEG_INF)
        s2 = jnp.where(ok_n, _dot_nt(qh, kn[:, ds_]) * (SWA_HEAD_DIM ** -0.5), NEG_INF)
        sk = sink_ref[hq]
        m = jnp.maximum(jnp.maximum(jnp.max(s1, axis=-1, keepdims=True),
                                    jnp.max(s2, axis=-1, keepdims=True)), sk)
        p1 = jnp.exp(s1 - m)
        p2 = jnp.exp(s2 - m)
        den = jnp.sum(p1, axis=-1, keepdims=True) + jnp.sum(p2, axis=-1, keepdims=True) + jnp.exp(sk - m)
        outs.append((_dot(p1.astype(BF16), vc[:, ds_]) + _dot(p2.astype(BF16), vn[:, ds_])) / den)
        if hq % 2 == 1:
            yield
    return jnp.concatenate(outs, axis=1)


def _mix_sample_kernel(sink_ref, x_ref, z_ref, lar_ref, sa0_ref, sb0_ref, kc_ref, vc_ref, gna_ref, gnb_ref,
                       rep_ref, bd_ref, mc_ref, mn_ref, wb_ref, wo_ref, y_ref, sa1_ref, sb1_ref):
    row0 = pl.multiple_of(pl.program_id(0) * SB, SB)

    def rows(c0, c1):
        return _block_rows(z_ref, row0, c0, c1)

    def stacked(c0, c1):
        return jnp.concatenate(rows(c0, c1), axis=0)

    la_ret = [lar_ref[...]] * DEC_SEQ
    oa, ob, oc = _interleave([
        _rec_sample_stages(rows(C_GQ, C_GK), rows(C_GK, C_GV), rows(C_GV, C_GR), rows(C_GR, C_RQ),
                           rows(C_LA, C_ZG), sa0_ref, sa1_ref, gna_ref, rep_ref, bd_ref),
        _rec_sample_stages(rows(C_RQ, C_RK), rows(C_RK, C_RV), rows(C_RV, C_RG), rows(C_RG, C_SQ),
                           la_ret, sb0_ref, sb1_ref, gnb_ref, rep_ref, bd_ref),
        _swa_sample_stages(sink_ref, stacked(C_SQ, C_SK), stacked(C_SK, C_SV), stacked(C_SV, C_LA),
                           kc_ref, vc_ref, mc_ref, mn_ref)])
    mixed = sum(jax.nn.sigmoid(stacked(C_ZG + D_MODEL * j, C_ZG + D_MODEL * (j + 1)))
                * _dot(o.astype(BF16), wb_ref[512 * j:512 * (j + 1), :])
                for j, o in enumerate((oa, ob, oc)))
    out = _dot(mixed.astype(BF16), wo_ref[...])
    for t in range(DEC_SEQ):
        r = pl.ds(t * DEC_BATCH + row0, SB)
        y_ref[r, :] = x_ref[r, :] + out[t * SB:(t + 1) * SB, :]


def _mix_sample(x, z, sink, la_ret, state_gla, state_ret, cache_k, cache_v, layer, gna, gnb, consts, wb, wo):
    n = x.shape[0]
    rep, bd, mc, mn = consts
    whole = lambda w: pl.BlockSpec((n, w), lambda j: (0, 0), pipeline_mode=pl.Buffered(1))
    st_in = pl.BlockSpec((None, SB, GLA_HEADS, GLA_DK, GLA_DV), lambda j: (layer, j, 0, 0, 0))
    st_out = pl.BlockSpec((SB, GLA_HEADS, GLA_DK, GLA_DV), lambda j: (j, 0, 0, 0))
    cache = pl.BlockSpec((None, SB, WINDOW, 128), lambda j: (layer, j, 0, 0))
    st_shape = jax.ShapeDtypeStruct(state_gla.shape[1:], F32)
    return pl.pallas_call(
        _mix_sample_kernel,
        grid=(DEC_BATCH // SB,),
        in_specs=[pl.BlockSpec(memory_space=pltpu.SMEM), whole(D_MODEL), whole(Z_W),
                  _const_spec(la_ret.shape), st_in, st_in, cache, cache,
                  _const_spec((1, GLA_W)), _const_spec((1, RET_W)),
                  _const_spec(rep.shape), _const_spec(bd.shape), _const_spec(mc.shape), _const_spec(mn.shape),
                  _layer_spec((GLA_W + RET_W + SWA_W, D_MODEL), layer),
                  _layer_spec((D_MODEL, D_MODEL), layer)],
        out_specs=[pl.BlockSpec((n, D_MODEL), lambda j: (0, 0)), st_out, st_out],
        out_shape=[jax.ShapeDtypeStruct((n, D_MODEL), F32), st_shape, st_shape],
        compiler_params=_params(("arbitrary",)),
        name="mix_sample",
    )(sink, x, z, la_ret, state_gla, state_ret, cache_k, cache_v, gna, gnb, rep, bd, mc, mn, wb, wo)


FF_CHUNK = D_FF // 2


def _ffn_tail(x1, pe_ref, gp_ref, wpg_ref, wpp_ref, gf_ref, y_ref, final):
    hp = _rms(x1, gp_ref[...]).astype(BF16)
    x2 = x1 + jax.nn.sigmoid(_dot(hp, wpg_ref[...])) * _dot(pe_ref[...].astype(BF16), wpp_ref[...])
    y_ref[...] = _rms(x2, gf_ref[...]) if final else x2


def _ffn_prompt_kernel(x_ref, pe_ref, gn_ref, wi_ref, cw_ref, cb_ref, wd_ref, gp_ref, wpg_ref, wpp_ref,
                       gf_ref, y_ref, tail_ref, carry_scr, *, final):
    tm = x_ref.shape[0]

    @pl.when(pl.program_id(1) == 0)
    def _():
        carry_scr[...] = jnp.zeros_like(carry_scr)

    x = x_ref[...]
    hb = _rms(x, gn_ref[...]).astype(BF16)
    row = lax.broadcasted_iota(jnp.int32, (tm, 1), 0)
    acc = x
    for c0 in range(0, D_FF, FF_CHUNK):
        cs = slice(c0, c0 + FF_CHUNK)
        a = _dot(hb, wi_ref[:, c0:c0 + FF_CHUNK])
        bb = _dot(hb, wi_ref[:, D_FF + c0:D_FF + c0 + FF_CHUNK])
        p0 = carry_scr[6:7, cs]
        p1 = carry_scr[7:8, cs]
        a1 = jnp.where(row == 0, p1, pltpu.roll(a, 1, 0))
        a2 = jnp.where(row == 0, p0, jnp.where(row == 1, p1, pltpu.roll(a, 2, 0)))
        conv = cb_ref[:, cs] + cw_ref[0:1, cs] * a2 + cw_ref[1:2, cs] * a1 + cw_ref[2:3, cs] * a
        act = (jax.nn.gelu(conv) * bb).astype(BF16)
        acc = acc + _dot(act, wd_ref[c0:c0 + FF_CHUNK, :])
        carry_scr[:, cs] = a[tm - 8:tm, :]
        tail_ref[:, cs] = a[tm - 8:tm, :]
    _ffn_tail(acc, pe_ref, gp_ref, wpg_ref, wpp_ref, gf_ref, y_ref, final)


def _ffn_sample_kernel(x_ref, pe_ref, st_ref, gn_ref, wi_ref, cw_ref, cb_ref, wd_ref, gp_ref, wpg_ref,
                       wpp_ref, gf_ref, y_ref, tail_ref, *, final):
    ns = DEC_BATCH
    x = x_ref[...]
    hb = _rms(x, gn_ref[...]).astype(BF16)
    acc = x
    for c0 in range(0, D_FF, FF_CHUNK):
        cs = slice(c0, c0 + FF_CHUNK)
        a = _dot(hb, wi_ref[:, c0:c0 + FF_CHUNK])
        bb = _dot(hb, wi_ref[:, D_FF + c0:D_FF + c0 + FF_CHUNK])
        st0 = st_ref[0, :, cs]
        st1 = st_ref[1, :, cs]
        a1 = jnp.concatenate([st1, a[0:3 * ns]], axis=0)
        a2 = jnp.concatenate([st0, st1, a[0:2 * ns]], axis=0)
        conv = cb_ref[:, cs] + cw_ref[0:1, cs] * a2 + cw_ref[1:2, cs] * a1 + cw_ref[2:3, cs] * a
        act = (jax.nn.gelu(conv) * bb).astype(BF16)
        acc = acc + _dot(act, wd_ref[c0:c0 + FF_CHUNK, :])
        tail_ref[:, cs] = a[2 * ns:4 * ns, :]
    _ffn_tail(acc, pe_ref, gp_ref, wpg_ref, wpp_ref, gf_ref, y_ref, final)


def _ffn_weight_specs(layer):
    return [_layer_spec((1, D_MODEL), layer),
            _layer_spec((D_MODEL, 2 * D_FF), layer),
            _layer_spec((CONV_W, D_FF), layer),
            _layer_spec((1, D_FF), layer),
            _layer_spec((D_FF, D_MODEL), layer),
            _layer_spec((1, D_MODEL), layer),
            _layer_spec((D_MODEL, D_MODEL), layer),
            _layer_spec((D_PLE, D_MODEL), layer),
            _const_spec((1, D_MODEL))]


def _ffn_prompt(x, pe, layer, weights, nb, t, tm, final):
    nt = t // tm
    tok = lambda w: pl.BlockSpec((tm, w), lambda b, i: (b * nt + i, 0))
    return pl.pallas_call(
        functools.partial(_ffn_prompt_kernel, final=final),
        grid=(nb, nt),
        in_specs=[tok(D_MODEL),
                  pl.BlockSpec((None, tm, D_PLE), lambda b, i: (layer, b * nt + i, 0))]
        + _ffn_weight_specs(layer),
        out_specs=[tok(D_MODEL), pl.BlockSpec((8, D_FF), lambda b, i: (b, 0))],
        out_shape=[jax.ShapeDtypeStruct((nb * t, D_MODEL), F32),
                   jax.ShapeDtypeStruct((nb * 8, D_FF), F32)],
        scratch_shapes=[pltpu.VMEM((8, D_FF), F32)],
        compiler_params=_params(("arbitrary", "arbitrary")),
        name="ffn_prompt",
    )(x, pe, *weights)


def _ffn_sample(x, pe, st, layer, weights, final):
    n = x.shape[0]
    return pl.pallas_call(
        functools.partial(_ffn_sample_kernel, final=final),
        grid=(1,),
        in_specs=[pl.BlockSpec((n, D_MODEL), lambda i: (0, 0)),
                  pl.BlockSpec((None, n, D_PLE), lambda i: (layer, 0, 0)),
                  pl.BlockSpec((None, CONV_W - 1, DEC_BATCH, D_FF), lambda i: (layer, 0, 0, 0))]
        + _ffn_weight_specs(layer),
        out_specs=[pl.BlockSpec((n, D_MODEL), lambda i: (0, 0)),
                   pl.BlockSpec((n // 2, D_FF), lambda i: (0, 0))],
        out_shape=[jax.ShapeDtypeStruct((n, D_MODEL), F32),
                   jax.ShapeDtypeStruct((n // 2, D_FF), F32)],
        compiler_params=_params(("arbitrary",)),
        name="ffn_sample",
    )(x, pe, st, *weights)


def _rope_tables(pos, inv_freq):
    half = inv_freq.shape[0]
    ang = pos.astype(np.float64)[:, None] * inv_freq[None, :]
    c, s = np.cos(ang), np.sin(ang)
    rest = SWA_HEAD_DIM - 2 * half
    cos64 = np.concatenate([c, c, np.ones((pos.shape[0], rest))], axis=1)
    sin64 = np.concatenate([-s, s, np.zeros((pos.shape[0], rest))], axis=1)
    return (jnp.asarray(np.concatenate([cos64, cos64], axis=1), F32),
            jnp.asarray(np.concatenate([sin64, sin64], axis=1), F32))


def _pack_w_in(w_in):
    ga0 = sum(IN_SIZES[:4])
    ga = jnp.pad(w_in[:, :, ga0:ga0 + GLA_RANK], ((0, 0), (0, 0), (0, LANES - GLA_RANK)))
    return jnp.concatenate([w_in[:, :, :ga0], w_in[:, :, ga0 + GLA_RANK:], ga], axis=2).astype(BF16)


def kernel(x_prompt, x_sample, state_gla, state_ret, cache_swa_k, cache_swa_v, state_conv, p_prompt,
           p_sample, norm_mix, w_in, w_gla_a, b_gla_a, gla_norm, ret_norm, swa_sink, w_branch, w_out,
           norm_ffn, w_ffn_in, conv_w, conv_b, w_ffn_out, norm_ple, w_ple_gate, w_ple_proj, norm_final):
    nb, t, _ = x_prompt.shape
    ns, ts, _ = x_sample.shape
    n_s = ns * ts

    w_main = _pack_w_in(w_in)
    wa = jnp.pad(w_gla_a, ((0, 0), (0, LANES - GLA_RANK), (0, 0))).astype(BF16)
    ba = b_gla_a[:, None, :]
    wb = w_branch.astype(BF16)
    wo = w_out.astype(BF16)
    ffn_weights = (norm_ffn[:, None, :], w_ffn_in.astype(BF16), conv_w, conv_b[:, None, :],
                   w_ffn_out.astype(BF16), norm_ple[:, None, :], w_ple_gate.astype(BF16),
                   w_ple_proj.astype(BF16), norm_final[None, :])
    g_mix = norm_mix[:, None, :]
    gn_gla = jnp.tile(gla_norm, (1, GLA_HEADS))[:, None, :]
    gn_ret = ret_norm.reshape(DEPTH, 1, RET_W)

    ret_freq = 1.0 / (RET_THETA ** np.linspace(0.0, 1.0, RET_DK // 2))
    swa_freq = 1.0 / (ROPE_THETA ** (np.arange(0, ROPE_DIM, 2, dtype=np.float64) / ROPE_DIM))
    pos_p = np.arange(t)
    pos_s = PAST_LEN + np.arange(n_s) // ns
    tabs_p = _rope_tables(pos_p, ret_freq) + _rope_tables(pos_p, swa_freq)
    tabs_s = _rope_tables(pos_s, ret_freq) + _rope_tables(pos_s, swa_freq)

    gm_np, mk_np = _gla_constants()
    d_np, qdec_np, kdec_np, adec_np = _ret_constants()
    hk_np, hv_np, hs_np = _head_masks()
    mix_consts = (jnp.asarray(gm_np, BF16), jnp.asarray(mk_np), jnp.asarray(d_np), jnp.asarray(qdec_np),
                  jnp.asarray(kdec_np), jnp.asarray(adec_np), jnp.asarray(hk_np, BF16), jnp.asarray(hv_np, BF16),
                  jnp.asarray(hs_np))
    log_gamma = np.log1p(-np.exp2(-5.0 - np.arange(RET_HEADS, dtype=np.float64)))
    la_ret = jnp.asarray(np.broadcast_to(np.repeat(log_gamma, RET_DK)[None, :], (SB, RET_HEADS * RET_DK)), F32)
    mc_np, mn_np = _swa_sample_masks()
    rep_np, bd_np = _rec_sample_constants()
    sample_consts = (jnp.asarray(rep_np, BF16), jnp.asarray(bd_np), jnp.asarray(mc_np), jnp.asarray(mn_np))

    xp = x_prompt.reshape(nb * t, D_MODEL)
    xs = x_sample.transpose(1, 0, 2).reshape(n_s, D_MODEL)
    pe_p = p_prompt.reshape(DEPTH, nb * t, D_PLE)
    pe_s = p_sample.transpose(0, 2, 1, 3).reshape(DEPTH, n_s, D_PLE)
    conv_st = state_conv.transpose(0, 2, 1, 3)
    ck = cache_swa_k.reshape(DEPTH, ns, WINDOW, SWA_KV_HEADS * SWA_HEAD_DIM)
    cv = cache_swa_v.reshape(DEPTH, ns, WINDOW, SWA_KV_HEADS * SWA_HEAD_DIM)

    outs = {k_: [] for k_ in ("gla_p", "gla_s", "ret_p", "ret_s", "kv_p", "kv_s", "conv_p", "conv_s")}
    for l in range(DEPTH):
        final = l == DEPTH - 1
        sink = swa_sink[l]

        xp3, st_a, st_b, kv_tail = _mix_prompt(xp.reshape(nb, t, D_MODEL), sink, l, g_mix, w_main, wa, ba, tabs_p,
                                               mix_consts, gn_gla[l], gn_ret[l], wb, wo)
        xp, tail = _ffn_prompt(xp3.reshape(nb * t, D_MODEL), pe_p, l, ffn_weights, nb, t, 256, final)
        outs["gla_p"].append(st_a)
        outs["ret_p"].append(st_b)
        outs["kv_p"].append(kv_tail)
        outs["conv_p"].append(tail.reshape(nb, 8, D_FF)[:, 8 - (CONV_W - 1):])

        zs = _in_proj(xs, l, g_mix, w_main, wa, ba, tabs_s, 256)
        xs, st_a, st_b = _mix_sample(xs, zs, sink, la_ret, state_gla, state_ret, ck, cv, l, gn_gla[l], gn_ret[l],
                                     sample_consts, wb, wo)
        xs, tail_s = _ffn_sample(xs, pe_s, conv_st, l, ffn_weights, final)
        outs["gla_s"].append(st_a)
        outs["ret_s"].append(st_b)
        outs["kv_s"].append(zs[:, C_SK:C_LA])
        outs["conv_s"].append(tail_s.reshape(CONV_W - 1, ns, D_FF).transpose(1, 0, 2))

    y_prompt = xp.reshape(nb, t, D_MODEL)
    y_sample = xs.reshape(ts, ns, D_MODEL).transpose(1, 0, 2)
    st = {k_: jnp.stack(v_) for k_, v_ in outs.items()}

    def kv_heads(a):
        return a.reshape(a.shape[:-1] + (SWA_KV_HEADS, SWA_HEAD_DIM))

    k_p = kv_heads(st["kv_p"][..., :LANES])
    v_p = kv_heads(st["kv_p"][..., LANES:])
    kv_new = st["kv_s"].reshape(DEPTH, ts, ns, 2 * LANES).transpose(0, 2, 1, 3)
    k_s = jnp.concatenate([cache_swa_k[:, :, ts:], kv_heads(kv_new[..., :LANES])], axis=2)
    v_s = jnp.concatenate([cache_swa_v[:, :, ts:], kv_heads(kv_new[..., LANES:])], axis=2)
    return (y_prompt, y_sample, st["gla_p"], st["gla_s"], st["ret_p"], st["ret_s"],
            k_p, k_s, v_p, v_s, st["conv_p"], st["conv_s"])
```

```python
import functools
import math

import numpy as np
import jax
import jax.numpy as jnp
from jax import lax
from jax.experimental import pallas as pl
from jax.experimental.pallas import tpu as pltpu

F32 = jnp.float32
BF16 = jnp.bfloat16

D_MODEL = 1024
BATCH = 2
SEQ = 8192
DEPTH = 4
DEC_BATCH = 128
DEC_SEQ = 4
PAST_LEN = 8192
D_PLE = 256
GLA_HEADS = 4
GLA_DK = 64
GLA_DV = 128
GLA_RANK = 16
GLA_TAU = 16.0
RET_HEADS = 4
RET_DK = 64
RET_DV = 128
RET_THETA = 10000.0
SWA_Q_HEADS = 8
SWA_KV_HEADS = 2
SWA_HEAD_DIM = 64
SWA_GROUP = SWA_Q_HEADS // SWA_KV_HEADS
WINDOW = 128
ROPE_THETA = 500000.0
ROPE_DIM = SWA_HEAD_DIM // 4
D_FF = 2816
CONV_W = 3
N_BRANCH = 3
EPS = 1e-6
NEG_INF = -1e30

GLA_W = GLA_HEADS * GLA_DV
RET_W = RET_HEADS * RET_DV
SWA_W = SWA_Q_HEADS * SWA_HEAD_DIM
IN_SIZES = (GLA_HEADS * GLA_DK, GLA_HEADS * GLA_DK, GLA_W, GLA_W, GLA_RANK,
            RET_HEADS * RET_DK, RET_HEADS * RET_DK, RET_W, RET_W,
            SWA_W, SWA_KV_HEADS * SWA_HEAD_DIM, SWA_KV_HEADS * SWA_HEAD_DIM,
            N_BRANCH * D_MODEL)

LANES = 128
VMEM_LIMIT = 52 * 1024 * 1024

C_GQ = 0
C_GK = 256
C_GV = 512
C_GR = 1024
C_RQ = 1536
C_RK = 1792
C_RV = 2048
C_RG = 2560
C_SQ = 3072
C_SK = 3584
C_SV = 3712
C_LA = 3840
C_ZG = 4096
Z_W = 7168
W_ZG = 3840
W_GA = 6912
W_MAIN = 7040

TT = 128
N_LEVELS = 7
SB = 16


def _dot(a, b):
    return jnp.dot(a, b, preferred_element_type=F32)


def _dot_nt(a, b):
    return lax.dot_general(a, b, (((1,), (1,)), ((), ())), preferred_element_type=F32)


def _dot_tn(a, b):
    return lax.dot_general(a, b, (((0,), (0,)), ((), ())), preferred_element_type=F32)


def _rms(x, g):
    return x * lax.rsqrt(jnp.mean(x * x, axis=-1, keepdims=True) + EPS) * g


def _layer_spec(shape, layer):
    nd = len(shape)
    return pl.BlockSpec((None,) + tuple(shape), lambda *_: (layer,) + (0,) * nd,
                        pipeline_mode=pl.Buffered(1))


def _const_spec(shape):
    nd = len(shape)
    return pl.BlockSpec(tuple(shape), lambda *_: (0,) * nd, pipeline_mode=pl.Buffered(1))


def _params(sem):
    return pltpu.CompilerParams(dimension_semantics=sem, vmem_limit_bytes=VMEM_LIMIT)


def _rope_block(x, cos, sin_s, first):
    half_mask, half = first
    xr = jnp.where(half_mask, pltpu.roll(x, LANES - half, 1), pltpu.roll(x, half, 1))
    return x * cos + xr * sin_s


def _gate_logits(hb, w_ref, j):
    return _dot(hb, w_ref[:, W_ZG + D_MODEL * j:W_ZG + D_MODEL * (j + 1)])


def _in_stages(x, g_ref, w_ref, wa_ref, ba_ref, tabs, z_ref, with_gates):
    hb = _rms(x, g_ref[...]).astype(BF16)

    def mm(c0, c1):
        return _dot(hb, w_ref[:, c0:c1])

    lane = lax.broadcasted_iota(jnp.int32, (1, LANES), 1) % 64
    ret_first = (lane < RET_DK // 2, RET_DK // 2)
    swa_first = (lane < ROPE_DIM // 2, ROPE_DIM // 2)
    rc, rs, sc, ss = tabs

    ga = mm(W_GA, W_MAIN)
    xa = _dot(ga.astype(BF16), wa_ref[...]) + ba_ref[...]
    log_sig = jnp.minimum(xa, 0.0) - jnp.log1p(jnp.exp(-jnp.abs(xa)))
    z_ref[:, C_LA:C_ZG] = log_sig * (1.0 / GLA_TAU)
    z_ref[:, C_GQ:C_GK] = mm(C_GQ, C_GK) * (GLA_DK ** -0.5)
    yield
    z_ref[:, C_GK:C_RQ] = mm(C_GK, C_RQ)
    yield
    rq = mm(C_RQ, C_RK)
    rk = mm(C_RK, C_RV)
    for j in range(2):
        sl = slice(LANES * j, LANES * (j + 1))
        z_ref[:, C_RQ + LANES * j:C_RQ + LANES * (j + 1)] = _rope_block(rq[:, sl], rc, rs, ret_first)
        z_ref[:, C_RK + LANES * j:C_RK + LANES * (j + 1)] = (
            _rope_block(rk[:, sl], rc, rs, ret_first) * (RET_DK ** -0.5))
    yield
    z_ref[:, C_RV:C_SQ] = mm(C_RV, C_SQ)
    yield
    sqkv = mm(C_SQ, C_LA)
    for j in range(5):
        sl = slice(LANES * j, LANES * (j + 1))
        z_ref[:, C_SQ + LANES * j:C_SQ + LANES * (j + 1)] = _rope_block(sqkv[:, sl], sc, ss, swa_first)
    z_ref[:, C_SV:C_LA] = sqkv[:, C_SV - C_SQ:]
    yield
    if with_gates:
        for j in range(N_BRANCH):
            z_ref[:, C_ZG + D_MODEL * j:C_ZG + D_MODEL * (j + 1)] = _gate_logits(hb, w_ref, j)
            yield


def _in_kernel(x_ref, g_ref, w_ref, wa_ref, ba_ref, rc_ref, rs_ref, sc_ref, ss_ref, z_ref):
    tabs = (rc_ref[...], rs_ref[...], sc_ref[...], ss_ref[...])
    for _ in _in_stages(x_ref[...], g_ref, w_ref, wa_ref, ba_ref, tabs, z_ref, True):
        pass


def _in_proj(x, layer, g_mix, w_main, wa, ba, tabs, tm):
    n = x.shape[0]
    rc, rs, sc, ss = tabs
    nt = rc.shape[0] // tm
    tab = pl.BlockSpec((tm, LANES), lambda i: (i % nt, 0))
    return pl.pallas_call(
        _in_kernel,
        grid=(n // tm,),
        in_specs=[pl.BlockSpec((tm, D_MODEL), lambda i: (i, 0)),
                  _layer_spec((1, D_MODEL), layer),
                  _layer_spec((D_MODEL, W_MAIN), layer),
                  _layer_spec((LANES, 256), layer),
                  _layer_spec((1, 256), layer),
                  tab, tab, tab, tab],
        out_specs=pl.BlockSpec((tm, Z_W), lambda i: (i, 0)),
        out_shape=jax.ShapeDtypeStruct((n, Z_W), F32),
        compiler_params=_params(("arbitrary",)),
        name="in_proj",
    )(x, g_mix, w_main, wa, ba, rc, rs, sc, ss)


def _gla_constants():
    t = np.arange(TT)
    g = np.zeros((2 + N_LEVELS, TT, TT), np.float32)
    g[0] = (t[None, :] <= t[:, None])
    g[1] = (t[None, :] > t[:, None])
    m = np.zeros((1 + N_LEVELS, TT, TT), np.float32)
    m[0] = np.eye(TT)
    for lv in range(1, N_LEVELS + 1):
        bs, hf = 2 ** lv, 2 ** (lv - 1)
        bd = (t // bs) * bs + hf - 1
        upper = (t % bs) >= hf
        u = t[None, :]
        g[1 + lv] = np.where(upper[:, None], (u > bd[:, None]) & (u <= t[:, None]),
                             (u > t[:, None]) & (u <= bd[:, None]))
        same = (t[:, None] // bs) == (t[None, :] // bs)
        m[lv] = same & upper[:, None] & (~upper)[None, :]
    return g.reshape(-1, TT), np.tile(m, (1, 1, GLA_HEADS))


def _head_masks():
    hk = np.arange(GLA_HEADS * TT)[:, None] // TT == np.arange(GLA_HEADS * GLA_DK)[None, :] // GLA_DK
    hv = np.arange(GLA_HEADS * TT)[:, None] // TT == np.arange(GLA_W)[None, :] // GLA_DV
    hs = np.arange(GLA_HEADS * GLA_DK)[:, None] // GLA_DK == np.arange(GLA_W)[None, :] // GLA_DV
    return hk.astype(np.float32), hv.astype(np.float32), hs.astype(np.float32)


def _heads_blockdiag(x, mask):
    return jnp.concatenate([x] * GLA_HEADS, axis=0) * mask


def _col_vector(row):
    n = row.shape[1]
    eye = lax.broadcasted_iota(jnp.int32, (n, n), 0) == lax.broadcasted_iota(jnp.int32, (n, n), 1)
    return jnp.sum(jnp.where(eye, row, 0.0), axis=1, keepdims=True)


def _norm_gate(o, gn, gate):
    outs = [_rms(o[:, GLA_DV * h:GLA_DV * (h + 1)], gn[:, GLA_DV * h:GLA_DV * (h + 1)]) for h in range(GLA_HEADS)]
    return jnp.concatenate(outs, axis=1) * (gate * jax.nn.sigmoid(gate))


def _gla_tile(z_ref, gm_ref, mk_ref, hk_ref, hv_ref, hs_ref, gn_ref, st_ref):
    la = z_ref[:, C_LA:C_ZG]
    la_hi = la.astype(BF16)
    la_lo = (la - la_hi.astype(F32)).astype(BF16)
    gm = gm_ref[...]
    ex = jnp.exp(_dot(gm, la_hi) + _dot(gm, la_lo))
    q = z_ref[:, C_GQ:C_GK]
    k = z_ref[:, C_GK:C_GV]
    e_b = ex[0:TT]
    a_row = e_b[TT - 1:TT, :]
    qd = (q * e_b).astype(BF16)
    kd = (k * ex[TT:2 * TT]).astype(BF16)
    qb = q.astype(BF16)
    kb = k.astype(BF16)
    ql = [(q * ex[(1 + lv) * TT:(2 + lv) * TT]).astype(BF16) for lv in range(1, N_LEVELS + 1)]
    kl = [(k * ex[(1 + lv) * TT:(2 + lv) * TT]).astype(BF16) for lv in range(1, N_LEVELS + 1)]
    hk = hk_ref[...]
    yield
    a = mk_ref[0] * _dot_nt(qb, _heads_blockdiag(kb, hk))
    for lv in range(N_LEVELS):
        yield
        a = a + mk_ref[lv + 1] * _dot_nt(ql[lv], _heads_blockdiag(kl[lv], hk))
    yield
    v = z_ref[:, C_GV:C_GR].astype(BF16)
    s = st_ref[...]
    o = _dot(a.astype(BF16), _heads_blockdiag(v, hv_ref[...])) + _dot(qd, s.astype(BF16))
    yield
    st_ref[...] = _col_vector(a_row) * s + hs_ref[...] * _dot_tn(kd, v)
    yield
    return _norm_gate(o, gn_ref[...], z_ref[:, C_GR:C_RQ])


def _ret_gammas():
    return [1.0 - 2.0 ** (-5.0 - h) for h in range(RET_HEADS)]


def _ret_constants():
    t = np.arange(TT, dtype=np.float64)
    d = np.zeros((RET_HEADS, TT, TT), np.float64)
    qdec = np.zeros((TT, RET_HEADS * RET_DK), np.float64)
    kdec = np.zeros((TT, RET_HEADS * RET_DK), np.float64)
    adec = np.zeros((RET_HEADS * RET_DK, LANES), np.float64)
    for h, gam in enumerate(_ret_gammas()):
        diff = t[:, None] - t[None, :]
        d[h] = np.where(diff >= 0, gam ** np.maximum(diff, 0.0), 0.0)
        qdec[:, h * RET_DK:(h + 1) * RET_DK] = (gam ** (t + 1.0))[:, None]
        kdec[:, h * RET_DK:(h + 1) * RET_DK] = (gam ** (TT - 1.0 - t))[:, None]
        adec[h * RET_DK:(h + 1) * RET_DK, :] = gam ** TT
    d = np.concatenate(list(d), axis=1)
    adec = np.tile(adec, (1, RET_W // LANES))
    return d.astype(np.float32), qdec.astype(np.float32), kdec.astype(np.float32), adec.astype(np.float32)


def _ret_tile(z_ref, d_ref, qdec_ref, kdec_ref, adec_ref, hk_ref, hv_ref, hs_ref, gn_ref, st_ref):
    q = z_ref[:, C_RQ:C_RK]
    k = z_ref[:, C_RK:C_RV]
    qd = (q * qdec_ref[...]).astype(BF16)
    kd = (k * kdec_ref[...]).astype(BF16)
    a = d_ref[...] * _dot_nt(q.astype(BF16), _heads_blockdiag(k.astype(BF16), hk_ref[...]))
    yield
    v = z_ref[:, C_RV:C_RG].astype(BF16)
    s = st_ref[...]
    o = _dot(a.astype(BF16), _heads_blockdiag(v, hv_ref[...])) + _dot(qd, s.astype(BF16))
    yield
    st_ref[...] = adec_ref[...] * s + hs_ref[...] * _dot_tn(kd, v)
    yield
    return _norm_gate(o, gn_ref[...], z_ref[:, C_RG:C_SQ])


def _rec_sample_constants():
    c = np.arange(SB * GLA_DK)
    rep = np.arange(GLA_DK)[:, None] == (c[None, :] % GLA_DK)
    bd = (np.arange(DEC_SEQ * SB)[:, None] % SB) == (c[None, :] // GLA_DK)
    return rep.astype(np.float32), bd.astype(np.float32)


def _block_rows(ref, row0, c0, c1):
    return [ref[pl.ds(t * DEC_BATCH + row0, SB), c0:c1] for t in range(DEC_SEQ)]


def _rec_sample_stages(q, k, v, gate, la, s0_ref, s1_ref, gn_ref, rep_ref, bd_ref):
    b = []
    for t in range(DEC_SEQ):
        b.append(la[t] if t == 0 else b[-1] + la[t])
    qd = jnp.concatenate([q[t] * jnp.exp(b[t]) for t in range(DEC_SEQ)], axis=0).astype(BF16)
    kd = jnp.concatenate([k[t] * jnp.exp(b[-1] - b[t]) for t in range(DEC_SEQ)], axis=0).astype(BF16)
    vv = jnp.concatenate(v, axis=0).astype(BF16)
    a = jnp.exp(b[-1])
    a_hi = a.astype(BF16)
    a_r = a - a_hi.astype(F32)
    a_mid = a_r.astype(BF16)
    a_lo = (a_r - a_mid.astype(F32)).astype(BF16)
    rep = rep_ref[...]
    bd = bd_ref[...]
    ones = jnp.ones((SB, GLA_DV), BF16)

    def expand(x, mask):
        return (_dot(x, rep) * mask).astype(BF16)

    yield
    oi = []
    for h in range(GLA_HEADS):
        hs = slice(GLA_DK * h, GLA_DK * (h + 1))
        vs = slice(GLA_DV * h, GLA_DV * (h + 1))
        s0 = s0_ref[:, h].reshape(SB * GLA_DK, GLA_DV)
        oi.append(_dot(expand(qd[:, hs], bd), s0.astype(BF16)))
        a_col = (_dot_tn(expand(a_hi[:, hs], bd[0:SB]), ones) + _dot_tn(expand(a_mid[:, hs], bd[0:SB]), ones)
                 + _dot_tn(expand(a_lo[:, hs], bd[0:SB]), ones))
        s1 = a_col * s0 + _dot_tn(expand(kd[:, hs], bd), vv[:, vs])
        s1_ref[:, h] = s1.reshape(SB, GLA_DK, GLA_DV)
        yield
    oi = jnp.concatenate(oi, axis=1)

    gn = gn_ref[...]
    outs = []
    for t in range(DEC_SEQ):
        o = oi[t * SB:(t + 1) * SB, :]
        for s in range(t + 1):
            p = q[t] * k[s] * jnp.exp(b[t] - b[s])
            parts = []
            for h in range(GLA_HEADS):
                hs = slice(GLA_DK * h, GLA_DK * (h + 1))
                vs = slice(GLA_DV * h, GLA_DV * (h + 1))
                parts.append(jnp.sum(p[:, hs], axis=1, keepdims=True) * v[s][:, vs])
            o = o + jnp.concatenate(parts, axis=1)
        outs.append(_norm_gate(o, gn, gate[t]))
        yield
    return jnp.concatenate(outs, axis=0)


def _swa_tile(z_ref, sink_ref, kp_ref, vp_ref, first_col):
    kc = z_ref[:, C_SK:C_SV].astype(BF16)
    vc = z_ref[:, C_SV:C_LA].astype(BF16)
    kband = jnp.concatenate([kp_ref[...], kc], axis=0)
    vband = jnp.concatenate([vp_ref[...], vc], axis=0)
    rows = SWA_GROUP * WINDOW
    r = lax.broadcasted_iota(jnp.int32, (rows, 2 * WINDOW), 0) & (WINDOW - 1)
    c = lax.broadcasted_iota(jnp.int32, (rows, 2 * WINDOW), 1)
    valid = (c > r) & (c <= r + WINDOW) & (c >= first_col)
    groups = range(SWA_KV_HEADS)
    dsl = [slice(SWA_HEAD_DIM * g, SWA_HEAD_DIM * (g + 1)) for g in groups]
    s, sk = [], []
    for g in groups:
        heads = range(SWA_GROUP * g, SWA_GROUP * (g + 1))
        q = jnp.concatenate(
            [z_ref[:, C_SQ + SWA_HEAD_DIM * hq:C_SQ + SWA_HEAD_DIM * (hq + 1)] for hq in heads], axis=0)
        sk.append(jnp.concatenate([jnp.full((WINDOW, 1), sink_ref[hq], F32) for hq in heads], axis=0))
        s.append(_dot_nt(q.astype(BF16), kband[:, dsl[g]]) * (SWA_HEAD_DIM ** -0.5))
    yield
    m, p = [], []
    for g in groups:
        sg = jnp.where(valid, s[g], NEG_INF)
        m.append(jnp.maximum(jnp.max(sg, axis=-1, keepdims=True), sk[g]))
        p.append(jnp.exp(sg - m[g]))
    yield
    outs = []
    for g in groups:
        den = jnp.sum(p[g], axis=-1, keepdims=True) + jnp.exp(sk[g] - m[g])
        o = _dot(p[g].astype(BF16), vband[:, dsl[g]]) / den
        outs += [o[WINDOW * j:WINDOW * (j + 1)] for j in range(SWA_GROUP)]
    yield
    return jnp.concatenate(outs, axis=1), kc, vc


def _interleave(stage_fns):
    results = [None] * len(stage_fns)
    live = list(enumerate(stage_fns))
    while live:
        still = []
        for idx, gen in live:
            try:
                next(gen)
                still.append((idx, gen))
            except StopIteration as stop:
                results[idx] = stop.value
        live = still
    return results


def _merge_stages(pend_ref, x_ref, g_ref, w_ref, wb_ref, wo_ref, y_ref):
    nb = x_ref.shape[0]
    hb = _rms(x_ref[...].reshape(nb * TT, D_MODEL), g_ref[...]).astype(BF16)
    gates = []
    for j in range(N_BRANCH):
        gates.append(jax.nn.sigmoid(_gate_logits(hb, w_ref, j)))
        yield
    mixed = None
    for j in range(N_BRANCH):
        term = gates[j] * _dot(pend_ref[:, GLA_W * j:GLA_W * (j + 1)], wb_ref[GLA_W * j:GLA_W * (j + 1), :])
        mixed = term if mixed is None else mixed + term
        yield
    out = _dot(mixed.astype(BF16), wo_ref[...])
    for b in range(x_ref.shape[0]):
        y_ref[b] = x_ref[b] + out[TT * b:TT * (b + 1)]
    yield


def _mix_kernel(sink_ref, xp_ref, xn_ref, tp0, tp1, tp2, tp3, tn0, tn1, tn2, tn3, g_ref, w_ref, wa_ref, ba_ref,
                gm_ref, mk_ref, d_ref, qdec_ref, kdec_ref, adec_ref, hk_ref, hv_ref, hs_ref, gna_ref, gnb_ref,
                wb_ref, wo_ref, y_ref, sta_ref, stb_ref, kv_ref, z_scr, pend_scr, sa_scr, sb_scr, kp_scr, vp_scr,
                *, nt):
    i = pl.program_id(0)
    nb = xp_ref.shape[0]

    def rows_of(ref3):
        return ref3[...].reshape(nb * TT, ref3.shape[2])

    def tiled(tab_refs):
        return tuple(jnp.concatenate([r[...]] * nb, axis=0) for r in tab_refs)

    @pl.when(i == 0)
    def _():
        sa_scr[...] = jnp.zeros_like(sa_scr)
        sb_scr[...] = jnp.zeros_like(sb_scr)
        kp_scr[...] = jnp.zeros_like(kp_scr)
        vp_scr[...] = jnp.zeros_like(vp_scr)
        pend_scr[...] = jnp.zeros_like(pend_scr)
        for _ in _in_stages(rows_of(xp_ref), g_ref, w_ref, wa_ref, ba_ref, tiled((tp0, tp1, tp2, tp3)),
                            z_scr.at[0], False):
            pass

    slot = i % 2
    zc = z_scr.at[slot]
    first_col = jnp.where(i > 0, 0, WINDOW)
    stages = [_merge_stages(pend_scr, xp_ref, g_ref, w_ref, wb_ref, wo_ref, y_ref),
              _in_stages(rows_of(xn_ref), g_ref, w_ref, wa_ref, ba_ref, tiled((tn0, tn1, tn2, tn3)),
                         z_scr.at[1 - slot], False)]
    for b in range(nb):
        zb = zc.at[pl.ds(TT * b, TT)]
        stages += [_gla_tile(zb, gm_ref, mk_ref, hk_ref, hv_ref, hs_ref, gna_ref, sa_scr.at[b]),
                   _swa_tile(zb, sink_ref, kp_scr.at[b], vp_scr.at[b], first_col),
                   _ret_tile(zb, d_ref, qdec_ref, kdec_ref, adec_ref, hk_ref, hv_ref, hs_ref, gnb_ref, sb_scr.at[b])]
    branch = _interleave(stages)[2:]
    swa = [branch[3 * b + 1] for b in range(nb)]
    outs = [[branch[3 * b] for b in range(nb)], [branch[3 * b + 2] for b in range(nb)], [s[0] for s in swa]]
    pend = [jnp.concatenate(o, axis=0).astype(BF16) for o in outs]

    @pl.when(i < nt)
    def _():
        for j in range(N_BRANCH):
            pend_scr[:, GLA_W * j:GLA_W * (j + 1)] = pend[j]
        for b in range(nb):
            kp_scr[b] = swa[b][1]
            vp_scr[b] = swa[b][2]
        for b in range(nb):
            for h in range(GLA_HEADS):
                sta_ref[b, h] = sa_scr[b, GLA_DK * h:GLA_DK * (h + 1), GLA_DV * h:GLA_DV * (h + 1)]
                stb_ref[b, h] = sb_scr[b, RET_DK * h:RET_DK * (h + 1), RET_DV * h:RET_DV * (h + 1)]
            kv_ref[b] = zc[pl.ds(TT * b, TT), C_SK:C_LA]


MIX_VMEM_LIMIT = 60 * 1024 * 1024


def _mix_prompt(x, sink, layer, g_mix, w_main, wa, ba, tabs, consts, gna, gnb, wb, wo):
    nb, t, _ = x.shape
    nt = t // TT
    st_spec = pl.BlockSpec((nb, GLA_HEADS, GLA_DK, GLA_DV), lambda i: (0, 0, 0, 0))
    st_shape = jax.ShapeDtypeStruct((nb, GLA_HEADS, GLA_DK, GLA_DV), F32)
    nxt = lambda i: jnp.minimum(i + 1, nt - 1)
    prv = lambda i: jnp.maximum(i - 1, 0)
    tab_prv = pl.BlockSpec((TT, LANES), lambda i: (prv(i), 0))
    tab_nxt = pl.BlockSpec((TT, LANES), lambda i: (nxt(i), 0))
    return pl.pallas_call(
        functools.partial(_mix_kernel, nt=nt),
        grid=(nt + 1,),
        in_specs=[pl.BlockSpec(memory_space=pltpu.SMEM),
                  pl.BlockSpec((nb, TT, D_MODEL), lambda i: (0, prv(i), 0)),
                  pl.BlockSpec((nb, TT, D_MODEL), lambda i: (0, nxt(i), 0))]
        + [tab_prv] * 4 + [tab_nxt] * 4
        + [_layer_spec((1, D_MODEL), layer), _layer_spec((D_MODEL, W_MAIN), layer),
           _layer_spec((LANES, 256), layer), _layer_spec((1, 256), layer)]
        + [_const_spec(c.shape) for c in consts]
        + [_const_spec((1, GLA_W)), _const_spec((1, RET_W)),
           _layer_spec((GLA_W + RET_W + SWA_W, D_MODEL), layer),
           _layer_spec((D_MODEL, D_MODEL), layer)],
        out_specs=[pl.BlockSpec((nb, TT, D_MODEL), lambda i: (0, prv(i), 0)), st_spec, st_spec,
                   pl.BlockSpec((nb, WINDOW, 2 * LANES), lambda i: (0, 0, 0))],
        out_shape=[jax.ShapeDtypeStruct((nb, t, D_MODEL), F32), st_shape, st_shape,
                   jax.ShapeDtypeStruct((nb, WINDOW, 2 * LANES), F32)],
        scratch_shapes=[pltpu.VMEM((2, nb * TT, C_ZG), F32),
                        pltpu.VMEM((nb * TT, GLA_W + RET_W + SWA_W), BF16),
                        pltpu.VMEM((nb, GLA_HEADS * GLA_DK, GLA_W), F32),
                        pltpu.VMEM((nb, RET_HEADS * RET_DK, RET_W), F32),
                        pltpu.VMEM((nb, WINDOW, 128), BF16), pltpu.VMEM((nb, WINDOW, 128), BF16)],
        compiler_params=pltpu.CompilerParams(dimension_semantics=("arbitrary",), vmem_limit_bytes=MIX_VMEM_LIMIT),
        name="mix_prompt",
    )(sink, x, x, *tabs, *tabs, g_mix, w_main, wa, ba, *consts, gna, gnb, wb, wo)


def _swa_sample_masks():
    r = np.arange(DEC_SEQ * SB)
    rt, rb = r // SB, r % SB
    c = np.arange(SB * WINDOW)
    cb, cj = c // WINDOW, c % WINDOW
    m_cache = (rb[:, None] == cb[None, :]) & (cj[None, :] > rt[:, None])
    m_new = (rb[:, None] == rb[None, :]) & (rt[None, :] <= rt[:, None])
    return m_cache.astype(np.float32), m_new.astype(np.float32)


def _swa_sample_stages(sink_ref, q, kn, vn, kc_ref, vc_ref, mc_ref, mn_ref):
    q = q.astype(BF16)
    kn = kn.astype(BF16)
    vn = vn.astype(BF16)
    kc = kc_ref[...].reshape(SB * WINDOW, 128).astype(BF16)
    vc = vc_ref[...].reshape(SB * WINDOW, 128).astype(BF16)
    ok_c = mc_ref[...] > 0.0
    ok_n = mn_ref[...] > 0.0
    outs = []
    for hq in range(SWA_Q_HEADS):
        g = hq // SWA_GROUP
        ds_ = slice(SWA_HEAD_DIM * g, SWA_HEAD_DIM * (g + 1))
        qh = q[:, SWA_HEAD_DIM * hq:SWA_HEAD_DIM * (hq + 1)]
        s1 = jnp.where(ok_c, _dot_nt(qh, kc[:, ds_]) * (SWA_HEAD_DIM ** -0.5), NEG_INF)
        s2 = jnp.where(ok_n, _dot_nt(qh, kn[:, ds_]) * (SWA_HEAD_DIM ** -0.5), NEG_INF)
        sk = sink_ref[hq]
        m = jnp.maximum(jnp.maximum(jnp.max(s1, axis=-1, keepdims=True),
                                    jnp.max(s2, axis=-1, keepdims=True)), sk)
        p1 = jnp.exp(s1 - m)
        p2 = jnp.exp(s2 - m)
        den = jnp.sum(p1, axis=-1, keepdims=True) + jnp.sum(p2, axis=-1, keepdims=True) + jnp.exp(sk - m)
        outs.append((_dot(p1.astype(BF16), vc[:, ds_]) + _dot(p2.astype(BF16), vn[:, ds_])) / den)
        if hq % 2 == 1:
            yield
    return jnp.concatenate(outs, axis=1)


def _mix_sample_kernel(sink_ref, x_ref, z_ref, lar_ref, sa0_ref, sb0_ref, kc_ref, vc_ref, gna_ref, gnb_ref,
                       rep_ref, bd_ref, mc_ref, mn_ref, wb_ref, wo_ref, y_ref, sa1_ref, sb1_ref):
    row0 = pl.multiple_of(pl.program_id(0) * SB, SB)

    def rows(c0, c1):
        return _block_rows(z_ref, row0, c0, c1)

    def stacked(c0, c1):
        return jnp.concatenate(rows(c0, c1), axis=0)

    la_ret = [lar_ref[...]] * DEC_SEQ
    oa, ob, oc = _interleave([
        _rec_sample_stages(rows(C_GQ, C_GK), rows(C_GK, C_GV), rows(C_GV, C_GR), rows(C_GR, C_RQ),
                           rows(C_LA, C_ZG), sa0_ref, sa1_ref, gna_ref, rep_ref, bd_ref),
        _rec_sample_stages(rows(C_RQ, C_RK), rows(C_RK, C_RV), rows(C_RV, C_RG), rows(C_RG, C_SQ),
                           la_ret, sb0_ref, sb1_ref, gnb_ref, rep_ref, bd_ref),
        _swa_sample_stages(sink_ref, stacked(C_SQ, C_SK), stacked(C_SK, C_SV), stacked(C_SV, C_LA),
                           kc_ref, vc_ref, mc_ref, mn_ref)])
    mixed = sum(jax.nn.sigmoid(stacked(C_ZG + D_MODEL * j, C_ZG + D_MODEL * (j + 1)))
                * _dot(o.astype(BF16), wb_ref[512 * j:512 * (j + 1), :])
                for j, o in enumerate((oa, ob, oc)))
    out = _dot(mixed.astype(BF16), wo_ref[...])
    for t in range(DEC_SEQ):
        r = pl.ds(t * DEC_BATCH + row0, SB)
        y_ref[r, :] = x_ref[r, :] + out[t * SB:(t + 1) * SB, :]


def _mix_sample(x, z, sink, la_ret, state_gla, state_ret, cache_k, cache_v, layer, gna, gnb, consts, wb, wo):
    n = x.shape[0]
    rep, bd, mc, mn = consts
    whole = lambda w: pl.BlockSpec((n, w), lambda j: (0, 0), pipeline_mode=pl.Buffered(1))
    st_in = pl.BlockSpec((None, SB, GLA_HEADS, GLA_DK, GLA_DV), lambda j: (layer, j, 0, 0, 0))
    st_out = pl.BlockSpec((SB, GLA_HEADS, GLA_DK, GLA_DV), lambda j: (j, 0, 0, 0))
    cache = pl.BlockSpec((None, SB, WINDOW, 128), lambda j: (layer, j, 0, 0))
    st_shape = jax.ShapeDtypeStruct(state_gla.shape[1:], F32)
    return pl.pallas_call(
        _mix_sample_kernel,
        grid=(DEC_BATCH // SB,),
        in_specs=[pl.BlockSpec(memory_space=pltpu.SMEM), whole(D_MODEL), whole(Z_W),
                  _const_spec(la_ret.shape), st_in, st_in, cache, cache,
                  _const_spec((1, GLA_W)), _const_spec((1, RET_W)),
                  _const_spec(rep.shape), _const_spec(bd.shape), _const_spec(mc.shape), _const_spec(mn.shape),
                  _layer_spec((GLA_W + RET_W + SWA_W, D_MODEL), layer),
                  _layer_spec((D_MODEL, D_MODEL), layer)],
        out_specs=[pl.BlockSpec((n, D_MODEL), lambda j: (0, 0)), st_out, st_out],
        out_shape=[jax.ShapeDtypeStruct((n, D_MODEL), F32), st_shape, st_shape],
        compiler_params=_params(("arbitrary",)),
        name="mix_sample",
    )(sink, x, z, la_ret, state_gla, state_ret, cache_k, cache_v, gna, gnb, rep, bd, mc, mn, wb, wo)


MXU_N = 256
FF_CHUNKS = ((0, 6 * MXU_N), (6 * MXU_N, D_FF))


def _ffn_tail(x1, pe_ref, gp_ref, wpg_ref, wpp_ref, gf_ref, y_ref, final):
    hp = _rms(x1, gp_ref[...]).astype(BF16)
    x2 = x1 + jax.nn.sigmoid(_dot(hp, wpg_ref[...])) * _dot(pe_ref[...].astype(BF16), wpp_ref[...])
    y_ref[...] = _rms(x2, gf_ref[...]) if final else x2


def _ple_stages(x1, pe_ref, gp_ref, wpg_ref, wpp_ref, gf_ref, y_ref, final):
    hp = _rms(x1, gp_ref[...]).astype(BF16)
    gate = jax.nn.sigmoid(_dot(hp, wpg_ref[...]))
    yield
    x2 = x1 + gate * _dot(pe_ref[...].astype(BF16), wpp_ref[...])
    y_ref[...] = _rms(x2, gf_ref[...]) if final else x2
    yield


def _ffn_stages(x, carry, first, gn_ref, wi_ref, cw_ref, cb_ref, wd_ref, keep):
    tm = x.shape[0]
    hb = _rms(x, gn_ref[...]).astype(BF16)
    row = lax.broadcasted_iota(jnp.int32, (tm, 1), 0)
    acc = x
    for c0, c1 in FF_CHUNKS:
        cs = slice(c0, c1)
        a = _dot(hb, wi_ref[:, c0:c1])
        bb = _dot(hb, wi_ref[:, D_FF + c0:D_FF + c1])
        yield
        p0 = carry[6:7, cs] * first
        p1 = carry[7:8, cs] * first
        a1 = jnp.where(row == 0, p1, pltpu.roll(a, 1, 0))
        a2 = jnp.where(row == 0, p0, jnp.where(row == 1, p1, pltpu.roll(a, 2, 0)))
        conv = cb_ref[:, cs] + cw_ref[0:1, cs] * a2 + cw_ref[1:2, cs] * a1 + cw_ref[2:3, cs] * a
        act = (jax.nn.gelu(conv) * bb).astype(BF16)
        keep.append(a[tm - 8:tm, :])
        yield
        acc = acc + _dot(act, wd_ref[c0:c1, :])
        yield
    return acc


def _ffn_prompt_kernel(x_ref, pe_ref, gn_ref, wi_ref, cw_ref, cb_ref, wd_ref, gp_ref, wpg_ref, wpp_ref,
                       gf_ref, y_ref, tail_ref, carry_scr, x1_scr, *, final, nt, n_tiles):
    g = pl.program_id(0)

    @pl.when(g == 0)
    def _():
        carry_scr[...] = jnp.zeros_like(carry_scr)
        x1_scr[...] = jnp.zeros_like(x1_scr)

    keep = []
    first = jnp.where((g % nt) == 0, 0.0, 1.0)
    _, acc = _interleave([
        _ple_stages(x1_scr[...], pe_ref, gp_ref, wpg_ref, wpp_ref, gf_ref, y_ref, final),
        _ffn_stages(x_ref[...], carry_scr[...], first, gn_ref, wi_ref, cw_ref, cb_ref, wd_ref, keep)])

    @pl.when(g < n_tiles)
    def _():
        x1_scr[...] = acc
        last_rows = jnp.concatenate(keep, axis=1)
        carry_scr[...] = last_rows
        tail_ref[...] = last_rows


def _ffn_sample_kernel(x_ref, pe_ref, st_ref, gn_ref, wi_ref, cw_ref, cb_ref, wd_ref, gp_ref, wpg_ref,
                       wpp_ref, gf_ref, y_ref, tail_ref, *, final):
    ns = DEC_BATCH
    x = x_ref[...]
    hb = _rms(x, gn_ref[...]).astype(BF16)
    acc = x
    for c0, c1 in FF_CHUNKS:
        cs = slice(c0, c1)
        a = _dot(hb, wi_ref[:, c0:c1])
        bb = _dot(hb, wi_ref[:, D_FF + c0:D_FF + c1])
        st0 = st_ref[0, :, cs]
        st1 = st_ref[1, :, cs]
        a1 = jnp.concatenate([st1, a[0:3 * ns]], axis=0)
        a2 = jnp.concatenate([st0, st1, a[0:2 * ns]], axis=0)
        conv = cb_ref[:, cs] + cw_ref[0:1, cs] * a2 + cw_ref[1:2, cs] * a1 + cw_ref[2:3, cs] * a
        act = (jax.nn.gelu(conv) * bb).astype(BF16)
        acc = acc + _dot(act, wd_ref[c0:c1, :])
        tail_ref[:, cs] = a[2 * ns:4 * ns, :]
    _ffn_tail(acc, pe_ref, gp_ref, wpg_ref, wpp_ref, gf_ref, y_ref, final)


def _ffn_weight_specs(layer):
    return [_layer_spec((1, D_MODEL), layer),
            _layer_spec((D_MODEL, 2 * D_FF), layer),
            _layer_spec((CONV_W, D_FF), layer),
            _layer_spec((1, D_FF), layer),
            _layer_spec((D_FF, D_MODEL), layer),
            _layer_spec((1, D_MODEL), layer),
            _layer_spec((D_MODEL, D_MODEL), layer),
            _layer_spec((D_PLE, D_MODEL), layer),
            _const_spec((1, D_MODEL))]


def _ffn_prompt(x, pe, layer, weights, nb, t, tm, final):
    nt = t // tm
    n_tiles = nb * nt
    cur = lambda g: jnp.minimum(g, n_tiles - 1)
    prv = lambda g: jnp.maximum(g - 1, 0)
    return pl.pallas_call(
        functools.partial(_ffn_prompt_kernel, final=final, nt=nt, n_tiles=n_tiles),
        grid=(n_tiles + 1,),
        in_specs=[pl.BlockSpec((tm, D_MODEL), lambda g: (cur(g), 0)),
                  pl.BlockSpec((None, tm, D_PLE), lambda g: (layer, prv(g), 0))]
        + _ffn_weight_specs(layer),
        out_specs=[pl.BlockSpec((tm, D_MODEL), lambda g: (prv(g), 0)),
                   pl.BlockSpec((8, D_FF), lambda g: (cur(g) // nt, 0))],
        out_shape=[jax.ShapeDtypeStruct((nb * t, D_MODEL), F32),
                   jax.ShapeDtypeStruct((nb * 8, D_FF), F32)],
        scratch_shapes=[pltpu.VMEM((8, D_FF), F32), pltpu.VMEM((tm, D_MODEL), F32)],
        compiler_params=_params(("arbitrary",)),
        name="ffn_prompt",
    )(x, pe, *weights)


def _ffn_sample(x, pe, st, layer, weights, final):
    n = x.shape[0]
    return pl.pallas_call(
        functools.partial(_ffn_sample_kernel, final=final),
        grid=(1,),
        in_specs=[pl.BlockSpec((n, D_MODEL), lambda i: (0, 0)),
                  pl.BlockSpec((None, n, D_PLE), lambda i: (layer, 0, 0)),
                  pl.BlockSpec((None, CONV_W - 1, DEC_BATCH, D_FF), lambda i: (layer, 0, 0, 0))]
        + _ffn_weight_specs(layer),
        out_specs=[pl.BlockSpec((n, D_MODEL), lambda i: (0, 0)),
                   pl.BlockSpec((n // 2, D_FF), lambda i: (0, 0))],
        out_shape=[jax.ShapeDtypeStruct((n, D_MODEL), F32),
                   jax.ShapeDtypeStruct((n // 2, D_FF), F32)],
        compiler_params=_params(("arbitrary",)),
        name="ffn_sample",
    )(x, pe, st, *weights)


def _rope_tables(pos, inv_freq):
    half = inv_freq.shape[0]
    ang = pos.astype(np.float64)[:, None] * inv_freq[None, :]
    c, s = np.cos(ang), np.sin(ang)
    rest = SWA_HEAD_DIM - 2 * half
    cos64 = np.concatenate([c, c, np.ones((pos.shape[0], rest))], axis=1)
    sin64 = np.concatenate([-s, s, np.zeros((pos.shape[0], rest))], axis=1)
    return (jnp.asarray(np.concatenate([cos64, cos64], axis=1), F32),
            jnp.asarray(np.concatenate([sin64, sin64], axis=1), F32))


def _pack_w_in(w_in):
    ga0 = sum(IN_SIZES[:4])
    ga = jnp.pad(w_in[:, :, ga0:ga0 + GLA_RANK], ((0, 0), (0, 0), (0, LANES - GLA_RANK)))
    return jnp.concatenate([w_in[:, :, :ga0], w_in[:, :, ga0 + GLA_RANK:], ga], axis=2).astype(BF16)


def kernel(x_prompt, x_sample, state_gla, state_ret, cache_swa_k, cache_swa_v, state_conv, p_prompt,
           p_sample, norm_mix, w_in, w_gla_a, b_gla_a, gla_norm, ret_norm, swa_sink, w_branch, w_out,
           norm_ffn, w_ffn_in, conv_w, conv_b, w_ffn_out, norm_ple, w_ple_gate, w_ple_proj, norm_final):
    nb, t, _ = x_prompt.shape
    ns, ts, _ = x_sample.shape
    n_s = ns * ts

    w_main = _pack_w_in(w_in)
    wa = jnp.pad(w_gla_a, ((0, 0), (0, LANES - GLA_RANK), (0, 0))).astype(BF16)
    ba = b_gla_a[:, None, :]
    wb = w_branch.astype(BF16)
    wo = w_out.astype(BF16)
    ffn_weights = (norm_ffn[:, None, :], w_ffn_in.astype(BF16), conv_w, conv_b[:, None, :],
                   w_ffn_out.astype(BF16), norm_ple[:, None, :], w_ple_gate.astype(BF16),
                   w_ple_proj.astype(BF16), norm_final[None, :])
    g_mix = norm_mix[:, None, :]
    gn_gla = jnp.tile(gla_norm, (1, GLA_HEADS))[:, None, :]
    gn_ret = ret_norm.reshape(DEPTH, 1, RET_W)

    ret_freq = 1.0 / (RET_THETA ** np.linspace(0.0, 1.0, RET_DK // 2))
    swa_freq = 1.0 / (ROPE_THETA ** (np.arange(0, ROPE_DIM, 2, dtype=np.float64) / ROPE_DIM))
    pos_p = np.arange(t)
    pos_s = PAST_LEN + np.arange(n_s) // ns
    tabs_p = _rope_tables(pos_p, ret_freq) + _rope_tables(pos_p, swa_freq)
    tabs_s = _rope_tables(pos_s, ret_freq) + _rope_tables(pos_s, swa_freq)

    gm_np, mk_np = _gla_constants()
    d_np, qdec_np, kdec_np, adec_np = _ret_constants()
    hk_np, hv_np, hs_np = _head_masks()
    mix_consts = (jnp.asarray(gm_np, BF16), jnp.asarray(mk_np), jnp.asarray(d_np), jnp.asarray(qdec_np),
                  jnp.asarray(kdec_np), jnp.asarray(adec_np), jnp.asarray(hk_np, BF16), jnp.asarray(hv_np, BF16),
                  jnp.asarray(hs_np))
    log_gamma = np.log1p(-np.exp2(-5.0 - np.arange(RET_HEADS, dtype=np.float64)))
    la_ret = jnp.asarray(np.broadcast_to(np.repeat(log_gamma, RET_DK)[None, :], (SB, RET_HEADS * RET_DK)), F32)
    mc_np, mn_np = _swa_sample_masks()
    rep_np, bd_np = _rec_sample_constants()
    sample_consts = (jnp.asarray(rep_np, BF16), jnp.asarray(bd_np), jnp.asarray(mc_np), jnp.asarray(mn_np))

    xp = x_prompt.reshape(nb * t, D_MODEL)
    xs = x_sample.transpose(1, 0, 2).reshape(n_s, D_MODEL)
    pe_p = p_prompt.reshape(DEPTH, nb * t, D_PLE)
    pe_s = p_sample.transpose(0, 2, 1, 3).reshape(DEPTH, n_s, D_PLE)
    conv_st = state_conv.transpose(0, 2, 1, 3)
    ck = cache_swa_k.reshape(DEPTH, ns, WINDOW, SWA_KV_HEADS * SWA_HEAD_DIM)
    cv = cache_swa_v.reshape(DEPTH, ns, WINDOW, SWA_KV_HEADS * SWA_HEAD_DIM)

    outs = {k_: [] for k_ in ("gla_p", "gla_s", "ret_p", "ret_s", "kv_p", "kv_s", "conv_p", "conv_s")}
    for l in range(DEPTH):
        final = l == DEPTH - 1
        sink = swa_sink[l]

        xp3, st_a, st_b, kv_tail = _mix_prompt(xp.reshape(nb, t, D_MODEL), sink, l, g_mix, w_main, wa, ba, tabs_p,
                                               mix_consts, gn_gla[l], gn_ret[l], wb, wo)
        xp, tail = _ffn_prompt(xp3.reshape(nb * t, D_MODEL), pe_p, l, ffn_weights, nb, t, 256, final)
        outs["gla_p"].append(st_a)
        outs["ret_p"].append(st_b)
        outs["kv_p"].append(kv_tail)
        outs["conv_p"].append(tail.reshape(nb, 8, D_FF)[:, 8 - (CONV_W - 1):])

        zs = _in_proj(xs, l, g_mix, w_main, wa, ba, tabs_s, 256)
        xs, st_a, st_b = _mix_sample(xs, zs, sink, la_ret, state_gla, state_ret, ck, cv, l, gn_gla[l], gn_ret[l],
                                     sample_consts, wb, wo)
        xs, tail_s = _ffn_sample(xs, pe_s, conv_st, l, ffn_weights, final)
        outs["gla_s"].append(st_a)
        outs["ret_s"].append(st_b)
        outs["kv_s"].append(zs[:, C_SK:C_LA])
        outs["conv_s"].append(tail_s.reshape(CONV_W - 1, ns, D_FF).transpose(1, 0, 2))

    y_prompt = xp.reshape(nb, t, D_MODEL)
    y_sample = xs.reshape(ts, ns, D_MODEL).transpose(1, 0, 2)
    st = {k_: jnp.stack(v_) for k_, v_ in outs.items()}

    def kv_heads(a):
        return a.reshape(a.shape[:-1] + (SWA_KV_HEADS, SWA_HEAD_DIM))

    k_p = kv_heads(st["kv_p"][..., :LANES])
    v_p = kv_heads(st["kv_p"][..., LANES:])
    kv_new = st["kv_s"].reshape(DEPTH, ts, ns, 2 * LANES).transpose(0, 2, 1, 3)
    k_s = jnp.concatenate([cache_swa_k[:, :, ts:], kv_heads(kv_new[..., :LANES])], axis=2)
    v_s = jnp.concatenate([cache_swa_v[:, :, ts:], kv_heads(kv_new[..., LANES:])], axis=2)
    return (y_prompt, y_sample, st["gla_p"], st["gla_s"], st["ret_p"], st["ret_s"],
            k_p, k_s, v_p, v_s, st["conv_p"], st["conv_s"])
```

```python
import functools
import math

import numpy as np
import jax
import jax.numpy as jnp
from jax import lax
from jax.experimental import pallas as pl
from jax.experimental.pallas import tpu as pltpu

F32 = jnp.float32
BF16 = jnp.bfloat16

D_MODEL = 1024
BATCH = 2
SEQ = 8192
DEPTH = 4
DEC_BATCH = 128
DEC_SEQ = 4
PAST_LEN = 8192
D_PLE = 256
GLA_HEADS = 4
GLA_DK = 64
GLA_DV = 128
GLA_RANK = 16
GLA_TAU = 16.0
RET_HEADS = 4
RET_DK = 64
RET_DV = 128
RET_THETA = 10000.0
SWA_Q_HEADS = 8
SWA_KV_HEADS = 2
SWA_HEAD_DIM = 64
SWA_GROUP = SWA_Q_HEADS // SWA_KV_HEADS
WINDOW = 128
ROPE_THETA = 500000.0
ROPE_DIM = SWA_HEAD_DIM // 4
D_FF = 2816
CONV_W = 3
N_BRANCH = 3
EPS = 1e-6
NEG_INF = -1e30

GLA_W = GLA_HEADS * GLA_DV
RET_W = RET_HEADS * RET_DV
SWA_W = SWA_Q_HEADS * SWA_HEAD_DIM
IN_SIZES = (GLA_HEADS * GLA_DK, GLA_HEADS * GLA_DK, GLA_W, GLA_W, GLA_RANK,
            RET_HEADS * RET_DK, RET_HEADS * RET_DK, RET_W, RET_W,
            SWA_W, SWA_KV_HEADS * SWA_HEAD_DIM, SWA_KV_HEADS * SWA_HEAD_DIM,
            N_BRANCH * D_MODEL)

LANES = 128
VMEM_LIMIT = 52 * 1024 * 1024

C_GQ = 0
C_GK = 256
C_GV = 512
C_GR = 1024
C_RQ = 1536
C_RK = 1792
C_RV = 2048
C_RG = 2560
C_SQ = 3072
C_SK = 3584
C_SV = 3712
C_LA = 3840
C_ZG = 4096
Z_W = 7168
W_ZG = 3840
W_GA = 6912
W_MAIN = 7040

TT = 128
N_LEVELS = 7
SB = 16


def _dot(a, b):
    return jnp.dot(a, b, preferred_element_type=F32)


def _dot_nt(a, b):
    return lax.dot_general(a, b, (((1,), (1,)), ((), ())), preferred_element_type=F32)


def _dot_tn(a, b):
    return lax.dot_general(a, b, (((0,), (0,)), ((), ())), preferred_element_type=F32)


def _rms(x, g):
    return x * lax.rsqrt(jnp.mean(x * x, axis=-1, keepdims=True) + EPS) * g


def _layer_spec(shape, layer):
    nd = len(shape)
    return pl.BlockSpec((None,) + tuple(shape), lambda *_: (layer,) + (0,) * nd,
                        pipeline_mode=pl.Buffered(1))


def _const_spec(shape):
    nd = len(shape)
    return pl.BlockSpec(tuple(shape), lambda *_: (0,) * nd, pipeline_mode=pl.Buffered(1))


def _params(sem):
    return pltpu.CompilerParams(dimension_semantics=sem, vmem_limit_bytes=VMEM_LIMIT)


def _rope_block(x, cos, sin_s, first):
    half_mask, half = first
    xr = jnp.where(half_mask, pltpu.roll(x, LANES - half, 1), pltpu.roll(x, half, 1))
    return x * cos + xr * sin_s


def _gate_logits(hb, w_ref, j):
    return _dot(hb, w_ref[:, W_ZG + D_MODEL * j:W_ZG + D_MODEL * (j + 1)])


def _in_stages(x, g_ref, w_ref, wa_ref, ba_ref, tabs, z_ref, with_gates):
    hb = _rms(x, g_ref[...]).astype(BF16)

    def mm(c0, c1):
        return _dot(hb, w_ref[:, c0:c1])

    lane = lax.broadcasted_iota(jnp.int32, (1, LANES), 1) % 64
    ret_first = (lane < RET_DK // 2, RET_DK // 2)
    swa_first = (lane < ROPE_DIM // 2, ROPE_DIM // 2)
    rc, rs, sc, ss = tabs

    ga = mm(W_GA, W_MAIN)
    xa = _dot(ga.astype(BF16), wa_ref[...]) + ba_ref[...]
    log_sig = jnp.minimum(xa, 0.0) - jnp.log1p(jnp.exp(-jnp.abs(xa)))
    z_ref[:, C_LA:C_ZG] = log_sig * (1.0 / GLA_TAU)
    z_ref[:, C_GQ:C_GK] = mm(C_GQ, C_GK) * (GLA_DK ** -0.5)
    yield
    z_ref[:, C_GK:C_RQ] = mm(C_GK, C_RQ)
    yield
    rq = mm(C_RQ, C_RK)
    rk = mm(C_RK, C_RV)
    for j in range(2):
        sl = slice(LANES * j, LANES * (j + 1))
        z_ref[:, C_RQ + LANES * j:C_RQ + LANES * (j + 1)] = _rope_block(rq[:, sl], rc, rs, ret_first)
        z_ref[:, C_RK + LANES * j:C_RK + LANES * (j + 1)] = (
            _rope_block(rk[:, sl], rc, rs, ret_first) * (RET_DK ** -0.5))
    yield
    z_ref[:, C_RV:C_SQ] = mm(C_RV, C_SQ)
    yield
    sqkv = mm(C_SQ, C_LA)
    for j in range(5):
        sl = slice(LANES * j, LANES * (j + 1))
        z_ref[:, C_SQ + LANES * j:C_SQ + LANES * (j + 1)] = _rope_block(sqkv[:, sl], sc, ss, swa_first)
    z_ref[:, C_SV:C_LA] = sqkv[:, C_SV - C_SQ:]
    yield
    if with_gates:
        for j in range(N_BRANCH):
            z_ref[:, C_ZG + D_MODEL * j:C_ZG + D_MODEL * (j + 1)] = _gate_logits(hb, w_ref, j)
            yield


def _in_kernel(x_ref, g_ref, w_ref, wa_ref, ba_ref, rc_ref, rs_ref, sc_ref, ss_ref, z_ref):
    tabs = (rc_ref[...], rs_ref[...], sc_ref[...], ss_ref[...])
    for _ in _in_stages(x_ref[...], g_ref, w_ref, wa_ref, ba_ref, tabs, z_ref, True):
        pass


def _in_proj(x, layer, g_mix, w_main, wa, ba, tabs, tm):
    n = x.shape[0]
    rc, rs, sc, ss = tabs
    nt = rc.shape[0] // tm
    tab = pl.BlockSpec((tm, LANES), lambda i: (i % nt, 0))
    return pl.pallas_call(
        _in_kernel,
        grid=(n // tm,),
        in_specs=[pl.BlockSpec((tm, D_MODEL), lambda i: (i, 0)),
                  _layer_spec((1, D_MODEL), layer),
                  _layer_spec((D_MODEL, W_MAIN), layer),
                  _layer_spec((LANES, 256), layer),
                  _layer_spec((1, 256), layer),
                  tab, tab, tab, tab],
        out_specs=pl.BlockSpec((tm, Z_W), lambda i: (i, 0)),
        out_shape=jax.ShapeDtypeStruct((n, Z_W), F32),
        compiler_params=_params(("arbitrary",)),
        name="in_proj",
    )(x, g_mix, w_main, wa, ba, rc, rs, sc, ss)


def _gla_constants():
    t = np.arange(TT)
    g = np.zeros((2 + N_LEVELS, TT, TT), np.float32)
    g[0] = (t[None, :] <= t[:, None])
    g[1] = (t[None, :] > t[:, None])
    m = np.zeros((1 + N_LEVELS, TT, TT), np.float32)
    m[0] = np.eye(TT)
    for lv in range(1, N_LEVELS + 1):
        bs, hf = 2 ** lv, 2 ** (lv - 1)
        bd = (t // bs) * bs + hf - 1
        upper = (t % bs) >= hf
        u = t[None, :]
        g[1 + lv] = np.where(upper[:, None], (u > bd[:, None]) & (u <= t[:, None]),
                             (u > t[:, None]) & (u <= bd[:, None]))
        same = (t[:, None] // bs) == (t[None, :] // bs)
        m[lv] = same & upper[:, None] & (~upper)[None, :]
    return g.reshape(-1, TT), np.tile(m, (1, 1, GLA_HEADS))


def _head_masks():
    hk = np.arange(GLA_HEADS * TT)[:, None] // TT == np.arange(GLA_HEADS * GLA_DK)[None, :] // GLA_DK
    hv = np.arange(GLA_HEADS * TT)[:, None] // TT == np.arange(GLA_W)[None, :] // GLA_DV
    hs = np.arange(GLA_HEADS * GLA_DK)[:, None] // GLA_DK == np.arange(GLA_W)[None, :] // GLA_DV
    return hk.astype(np.float32), hv.astype(np.float32), hs.astype(np.float32)


def _heads_blockdiag(x, mask):
    return jnp.concatenate([x] * GLA_HEADS, axis=0) * mask


def _col_vector(row):
    n = row.shape[1]
    eye = lax.broadcasted_iota(jnp.int32, (n, n), 0) == lax.broadcasted_iota(jnp.int32, (n, n), 1)
    return jnp.sum(jnp.where(eye, row, 0.0), axis=1, keepdims=True)


def _norm_gate(o, gn, gate):
    outs = [_rms(o[:, GLA_DV * h:GLA_DV * (h + 1)], gn[:, GLA_DV * h:GLA_DV * (h + 1)]) for h in range(GLA_HEADS)]
    return jnp.concatenate(outs, axis=1) * (gate * jax.nn.sigmoid(gate))


def _gla_tile(z_ref, gm_ref, mk_ref, hk_ref, hv_ref, hs_ref, gn_ref, st_ref):
    la = z_ref[:, C_LA:C_ZG]
    la_hi = la.astype(BF16)
    la_lo = (la - la_hi.astype(F32)).astype(BF16)
    gm = gm_ref[...]
    ex = jnp.exp(_dot(gm, la_hi) + _dot(gm, la_lo))
    q = z_ref[:, C_GQ:C_GK]
    k = z_ref[:, C_GK:C_GV]
    e_b = ex[0:TT]
    a_row = e_b[TT - 1:TT, :]
    qd = (q * e_b).astype(BF16)
    kd = (k * ex[TT:2 * TT]).astype(BF16)
    qb = q.astype(BF16)
    kb = k.astype(BF16)
    ql = [(q * ex[(1 + lv) * TT:(2 + lv) * TT]).astype(BF16) for lv in range(1, N_LEVELS + 1)]
    kl = [(k * ex[(1 + lv) * TT:(2 + lv) * TT]).astype(BF16) for lv in range(1, N_LEVELS + 1)]
    hk = hk_ref[...]
    yield
    a = mk_ref[0] * _dot_nt(qb, _heads_blockdiag(kb, hk))
    for lv in range(N_LEVELS):
        yield
        a = a + mk_ref[lv + 1] * _dot_nt(ql[lv], _heads_blockdiag(kl[lv], hk))
    yield
    v = z_ref[:, C_GV:C_GR].astype(BF16)
    s = st_ref[...]
    o = _dot(a.astype(BF16), _heads_blockdiag(v, hv_ref[...])) + _dot(qd, s.astype(BF16))
    yield
    st_ref[...] = _col_vector(a_row) * s + hs_ref[...] * _dot_tn(kd, v)
    yield
    return _norm_gate(o, gn_ref[...], z_ref[:, C_GR:C_RQ])


def _ret_gammas():
    return [1.0 - 2.0 ** (-5.0 - h) for h in range(RET_HEADS)]


def _ret_constants():
    t = np.arange(TT, dtype=np.float64)
    d = np.zeros((RET_HEADS, TT, TT), np.float64)
    qdec = np.zeros((TT, RET_HEADS * RET_DK), np.float64)
    kdec = np.zeros((TT, RET_HEADS * RET_DK), np.float64)
    adec = np.zeros((RET_HEADS * RET_DK, LANES), np.float64)
    for h, gam in enumerate(_ret_gammas()):
        diff = t[:, None] - t[None, :]
        d[h] = np.where(diff >= 0, gam ** np.maximum(diff, 0.0), 0.0)
        qdec[:, h * RET_DK:(h + 1) * RET_DK] = (gam ** (t + 1.0))[:, None]
        kdec[:, h * RET_DK:(h + 1) * RET_DK] = (gam ** (TT - 1.0 - t))[:, None]
        adec[h * RET_DK:(h + 1) * RET_DK, :] = gam ** TT
    d = np.concatenate(list(d), axis=1)
    adec = np.tile(adec, (1, RET_W // LANES))
    return d.astype(np.float32), qdec.astype(np.float32), kdec.astype(np.float32), adec.astype(np.float32)


def _ret_tile(z_ref, d_ref, qdec_ref, kdec_ref, adec_ref, hk_ref, hv_ref, hs_ref, gn_ref, st_ref):
    q = z_ref[:, C_RQ:C_RK]
    k = z_ref[:, C_RK:C_RV]
    qd = (q * qdec_ref[...]).astype(BF16)
    kd = (k * kdec_ref[...]).astype(BF16)
    a = d_ref[...] * _dot_nt(q.astype(BF16), _heads_blockdiag(k.astype(BF16), hk_ref[...]))
    yield
    v = z_ref[:, C_RV:C_RG].astype(BF16)
    s = st_ref[...]
    o = _dot(a.astype(BF16), _heads_blockdiag(v, hv_ref[...])) + _dot(qd, s.astype(BF16))
    yield
    st_ref[...] = adec_ref[...] * s + hs_ref[...] * _dot_tn(kd, v)
    yield
    return _norm_gate(o, gn_ref[...], z_ref[:, C_RG:C_SQ])


def _rec_sample_constants():
    c = np.arange(SB * GLA_DK)
    rep = np.arange(GLA_DK)[:, None] == (c[None, :] % GLA_DK)
    bd = (np.arange(DEC_SEQ * SB)[:, None] % SB) == (c[None, :] // GLA_DK)
    return rep.astype(np.float32), bd.astype(np.float32)


def _block_rows(ref, row0, c0, c1):
    return [ref[pl.ds(t * DEC_BATCH + row0, SB), c0:c1] for t in range(DEC_SEQ)]


def _rec_sample_stages(q, k, v, gate, la, s0_ref, s1_ref, gn_ref, rep_ref, bd_ref):
    b = []
    for t in range(DEC_SEQ):
        b.append(la[t] if t == 0 else b[-1] + la[t])
    qd = jnp.concatenate([q[t] * jnp.exp(b[t]) for t in range(DEC_SEQ)], axis=0).astype(BF16)
    kd = jnp.concatenate([k[t] * jnp.exp(b[-1] - b[t]) for t in range(DEC_SEQ)], axis=0).astype(BF16)
    vv = jnp.concatenate(v, axis=0).astype(BF16)
    a = jnp.exp(b[-1])
    a_hi = a.astype(BF16)
    a_r = a - a_hi.astype(F32)
    a_mid = a_r.astype(BF16)
    a_lo = (a_r - a_mid.astype(F32)).astype(BF16)
    rep = rep_ref[...]
    bd = bd_ref[...]
    ones = jnp.ones((SB, GLA_DV), BF16)

    def expand(x, mask):
        return (_dot(x, rep) * mask).astype(BF16)

    yield
    oi = []
    for h in range(GLA_HEADS):
        hs = slice(GLA_DK * h, GLA_DK * (h + 1))
        vs = slice(GLA_DV * h, GLA_DV * (h + 1))
        s0 = s0_ref[:, h].reshape(SB * GLA_DK, GLA_DV)
        oi.append(_dot(expand(qd[:, hs], bd), s0.astype(BF16)))
        a_col = (_dot_tn(expand(a_hi[:, hs], bd[0:SB]), ones) + _dot_tn(expand(a_mid[:, hs], bd[0:SB]), ones)
                 + _dot_tn(expand(a_lo[:, hs], bd[0:SB]), ones))
        s1 = a_col * s0 + _dot_tn(expand(kd[:, hs], bd), vv[:, vs])
        s1_ref[:, h] = s1.reshape(SB, GLA_DK, GLA_DV)
        yield
    oi = jnp.concatenate(oi, axis=1)

    gn = gn_ref[...]
    outs = []
    for t in range(DEC_SEQ):
        o = oi[t * SB:(t + 1) * SB, :]
        for s in range(t + 1):
            p = q[t] * k[s] * jnp.exp(b[t] - b[s])
            parts = []
            for h in range(GLA_HEADS):
                hs = slice(GLA_DK * h, GLA_DK * (h + 1))
                vs = slice(GLA_DV * h, GLA_DV * (h + 1))
                parts.append(jnp.sum(p[:, hs], axis=1, keepdims=True) * v[s][:, vs])
            o = o + jnp.concatenate(parts, axis=1)
        outs.append(_norm_gate(o, gn, gate[t]))
        yield
    return jnp.concatenate(outs, axis=0)


def _swa_tile(z_ref, sink_ref, kp_ref, vp_ref, first_col):
    kc = z_ref[:, C_SK:C_SV].astype(BF16)
    vc = z_ref[:, C_SV:C_LA].astype(BF16)
    kband = jnp.concatenate([kp_ref[...], kc], axis=0)
    vband = jnp.concatenate([vp_ref[...], vc], axis=0)
    rows = SWA_GROUP * WINDOW
    r = lax.broadcasted_iota(jnp.int32, (rows, 2 * WINDOW), 0) & (WINDOW - 1)
    c = lax.broadcasted_iota(jnp.int32, (rows, 2 * WINDOW), 1)
    valid = (c > r) & (c <= r + WINDOW) & (c >= first_col)
    groups = range(SWA_KV_HEADS)
    dsl = [slice(SWA_HEAD_DIM * g, SWA_HEAD_DIM * (g + 1)) for g in groups]
    s, sk = [], []
    for g in groups:
        heads = range(SWA_GROUP * g, SWA_GROUP * (g + 1))
        q = jnp.concatenate(
            [z_ref[:, C_SQ + SWA_HEAD_DIM * hq:C_SQ + SWA_HEAD_DIM * (hq + 1)] for hq in heads], axis=0)
        sk.append(jnp.concatenate([jnp.full((WINDOW, 1), sink_ref[hq], F32) for hq in heads], axis=0))
        s.append(_dot_nt(q.astype(BF16), kband[:, dsl[g]]) * (SWA_HEAD_DIM ** -0.5))
    yield
    m, p = [], []
    for g in groups:
        sg = jnp.where(valid, s[g], NEG_INF)
        m.append(jnp.maximum(jnp.max(sg, axis=-1, keepdims=True), sk[g]))
        p.append(jnp.exp(sg - m[g]))
    yield
    outs = []
    for g in groups:
        den = jnp.sum(p[g], axis=-1, keepdims=True) + jnp.exp(sk[g] - m[g])
        o = _dot(p[g].astype(BF16), vband[:, dsl[g]]) / den
        outs += [o[WINDOW * j:WINDOW * (j + 1)] for j in range(SWA_GROUP)]
    yield
    return jnp.concatenate(outs, axis=1), kc, vc


def _interleave(stage_fns):
    results = [None] * len(stage_fns)
    live = list(enumerate(stage_fns))
    while live:
        still = []
        for idx, gen in live:
            try:
                next(gen)
                still.append((idx, gen))
            except StopIteration as stop:
                results[idx] = stop.value
        live = still
    return results


def _merge_stages(pend_ref, x_ref, g_ref, w_ref, wb_ref, wo_ref, y_ref):
    nb = x_ref.shape[0]
    hb = _rms(x_ref[...].reshape(nb * TT, D_MODEL), g_ref[...]).astype(BF16)
    gates = []
    for j in range(N_BRANCH):
        gates.append(jax.nn.sigmoid(_gate_logits(hb, w_ref, j)))
        yield
    mixed = None
    for j in range(N_BRANCH):
        term = gates[j] * _dot(pend_ref[:, GLA_W * j:GLA_W * (j + 1)], wb_ref[GLA_W * j:GLA_W * (j + 1), :])
        mixed = term if mixed is None else mixed + term
        yield
    out = _dot(mixed.astype(BF16), wo_ref[...])
    for b in range(x_ref.shape[0]):
        y_ref[b] = x_ref[b] + out[TT * b:TT * (b + 1)]
    yield


def _mix_kernel(sink_ref, xp_ref, xn_ref, tp0, tp1, tp2, tp3, tn0, tn1, tn2, tn3, g_ref, w_ref, wa_ref, ba_ref,
                gm_ref, mk_ref, d_ref, qdec_ref, kdec_ref, adec_ref, hk_ref, hv_ref, hs_ref, gna_ref, gnb_ref,
                wb_ref, wo_ref, y_ref, sta_ref, stb_ref, kv_ref, z_scr, pend_scr, sa_scr, sb_scr, kp_scr, vp_scr,
                *, nt):
    i = pl.program_id(0)
    nb = xp_ref.shape[0]

    def rows_of(ref3):
        return ref3[...].reshape(nb * TT, ref3.shape[2])

    def tiled(tab_refs):
        return tuple(jnp.concatenate([r[...]] * nb, axis=0) for r in tab_refs)

    @pl.when(i == 0)
    def _():
        sa_scr[...] = jnp.zeros_like(sa_scr)
        sb_scr[...] = jnp.zeros_like(sb_scr)
        kp_scr[...] = jnp.zeros_like(kp_scr)
        vp_scr[...] = jnp.zeros_like(vp_scr)
        pend_scr[...] = jnp.zeros_like(pend_scr)
        for _ in _in_stages(rows_of(xp_ref), g_ref, w_ref, wa_ref, ba_ref, tiled((tp0, tp1, tp2, tp3)),
                            z_scr.at[0], False):
            pass

    slot = i % 2
    zc = z_scr.at[slot]
    first_col = jnp.where(i > 0, 0, WINDOW)
    stages = [_merge_stages(pend_scr, xp_ref, g_ref, w_ref, wb_ref, wo_ref, y_ref),
              _in_stages(rows_of(xn_ref), g_ref, w_ref, wa_ref, ba_ref, tiled((tn0, tn1, tn2, tn3)),
                         z_scr.at[1 - slot], False)]
    for b in range(nb):
        zb = zc.at[pl.ds(TT * b, TT)]
        stages += [_gla_tile(zb, gm_ref, mk_ref, hk_ref, hv_ref, hs_ref, gna_ref, sa_scr.at[b]),
                   _swa_tile(zb, sink_ref, kp_scr.at[b], vp_scr.at[b], first_col),
                   _ret_tile(zb, d_ref, qdec_ref, kdec_ref, adec_ref, hk_ref, hv_ref, hs_ref, gnb_ref, sb_scr.at[b])]
    branch = _interleave(stages)[2:]
    swa = [branch[3 * b + 1] for b in range(nb)]
    outs = [[branch[3 * b] for b in range(nb)], [branch[3 * b + 2] for b in range(nb)], [s[0] for s in swa]]
    pend = [jnp.concatenate(o, axis=0).astype(BF16) for o in outs]

    @pl.when(i < nt)
    def _():
        for j in range(N_BRANCH):
            pend_scr[:, GLA_W * j:GLA_W * (j + 1)] = pend[j]
        for b in range(nb):
            kp_scr[b] = swa[b][1]
            vp_scr[b] = swa[b][2]
        for b in range(nb):
            for h in range(GLA_HEADS):
                sta_ref[b, h] = sa_scr[b, GLA_DK * h:GLA_DK * (h + 1), GLA_DV * h:GLA_DV * (h + 1)]
                stb_ref[b, h] = sb_scr[b, RET_DK * h:RET_DK * (h + 1), RET_DV * h:RET_DV * (h + 1)]
            kv_ref[b] = zc[pl.ds(TT * b, TT), C_SK:C_LA]


MIX_VMEM_LIMIT = 60 * 1024 * 1024


def _mix_prompt(x, sink, layer, g_mix, w_main, wa, ba, tabs, consts, gna, gnb, wb, wo):
    nb, t, _ = x.shape
    nt = t // TT
    st_spec = pl.BlockSpec((nb, GLA_HEADS, GLA_DK, GLA_DV), lambda i: (0, 0, 0, 0))
    st_shape = jax.ShapeDtypeStruct((nb, GLA_HEADS, GLA_DK, GLA_DV), F32)
    nxt = lambda i: jnp.minimum(i + 1, nt - 1)
    prv = lambda i: jnp.maximum(i - 1, 0)
    tab_prv = pl.BlockSpec((TT, LANES), lambda i: (prv(i), 0))
    tab_nxt = pl.BlockSpec((TT, LANES), lambda i: (nxt(i), 0))
    return pl.pallas_call(
        functools.partial(_mix_kernel, nt=nt),
        grid=(nt + 1,),
        in_specs=[pl.BlockSpec(memory_space=pltpu.SMEM),
                  pl.BlockSpec((nb, TT, D_MODEL), lambda i: (0, prv(i), 0)),
                  pl.BlockSpec((nb, TT, D_MODEL), lambda i: (0, nxt(i), 0))]
        + [tab_prv] * 4 + [tab_nxt] * 4
        + [_layer_spec((1, D_MODEL), layer), _layer_spec((D_MODEL, W_MAIN), layer),
           _layer_spec((LANES, 256), layer), _layer_spec((1, 256), layer)]
        + [_const_spec(c.shape) for c in consts]
        + [_const_spec((1, GLA_W)), _const_spec((1, RET_W)),
           _layer_spec((GLA_W + RET_W + SWA_W, D_MODEL), layer),
           _layer_spec((D_MODEL, D_MODEL), layer)],
        out_specs=[pl.BlockSpec((nb, TT, D_MODEL), lambda i: (0, prv(i), 0)), st_spec, st_spec,
                   pl.BlockSpec((nb, WINDOW, 2 * LANES), lambda i: (0, 0, 0))],
        out_shape=[jax.ShapeDtypeStruct((nb, t, D_MODEL), F32), st_shape, st_shape,
                   jax.ShapeDtypeStruct((nb, WINDOW, 2 * LANES), F32)],
        scratch_shapes=[pltpu.VMEM((2, nb * TT, C_ZG), F32),
                        pltpu.VMEM((nb * TT, GLA_W + RET_W + SWA_W), BF16),
                        pltpu.VMEM((nb, GLA_HEADS * GLA_DK, GLA_W), F32),
                        pltpu.VMEM((nb, RET_HEADS * RET_DK, RET_W), F32),
                        pltpu.VMEM((nb, WINDOW, 128), BF16), pltpu.VMEM((nb, WINDOW, 128), BF16)],
        compiler_params=pltpu.CompilerParams(dimension_semantics=("arbitrary",), vmem_limit_bytes=MIX_VMEM_LIMIT),
        name="mix_prompt",
    )(sink, x, x, *tabs, *tabs, g_mix, w_main, wa, ba, *consts, gna, gnb, wb, wo)


def _swa_sample_masks():
    r = np.arange(DEC_SEQ * SB)
    rt, rb = r // SB, r % SB
    c = np.arange(SB * WINDOW)
    cb, cj = c // WINDOW, c % WINDOW
    m_cache = (rb[:, None] == cb[None, :]) & (cj[None, :] > rt[:, None])
    m_new = (rb[:, None] == rb[None, :]) & (rt[None, :] <= rt[:, None])
    return m_cache.astype(np.float32), m_new.astype(np.float32)


def _swa_sample_stages(sink_ref, q, kn, vn, kc_ref, vc_ref, mc_ref, mn_ref):
    q = q.astype(BF16)
    kn = kn.astype(BF16)
    vn = vn.astype(BF16)
    kc = kc_ref[...].reshape(SB * WINDOW, 128).astype(BF16)
    vc = vc_ref[...].reshape(SB * WINDOW, 128).astype(BF16)
    ok_c = mc_ref[...] > 0.0
    ok_n = mn_ref[...] > 0.0
    outs = []
    for hq in range(SWA_Q_HEADS):
        g = hq // SWA_GROUP
        ds_ = slice(SWA_HEAD_DIM * g, SWA_HEAD_DIM * (g + 1))
        qh = q[:, SWA_HEAD_DIM * hq:SWA_HEAD_DIM * (hq + 1)]
        s1 = jnp.where(ok_c, _dot_nt(qh, kc[:, ds_]) * (SWA_HEAD_DIM ** -0.5), NEG_INF)
        s2 = jnp.where(ok_n, _dot_nt(qh, kn[:, ds_]) * (SWA_HEAD_DIM ** -0.5), NEG_INF)
        sk = sink_ref[hq]
        m = jnp.maximum(jnp.maximum(jnp.max(s1, axis=-1, keepdims=True),
                                    jnp.max(s2, axis=-1, keepdims=True)), sk)
        p1 = jnp.exp(s1 - m)
        p2 = jnp.exp(s2 - m)
        den = jnp.sum(p1, axis=-1, keepdims=True) + jnp.sum(p2, axis=-1, keepdims=True) + jnp.exp(sk - m)
        outs.append((_dot(p1.astype(BF16), vc[:, ds_]) + _dot(p2.astype(BF16), vn[:, ds_])) / den)
        if hq % 2 == 1:
            yield
    return jnp.concatenate(outs, axis=1)


def _mix_sample_kernel(sink_ref, x_ref, z_ref, lar_ref, sa0_ref, sb0_ref, kc_ref, vc_ref, gna_ref, gnb_ref,
                       rep_ref, bd_ref, mc_ref, mn_ref, wb_ref, wo_ref, y_ref, sa1_ref, sb1_ref):
    row0 = pl.multiple_of(pl.program_id(0) * SB, SB)

    def rows(c0, c1):
        return _block_rows(z_ref, row0, c0, c1)

    def stacked(c0, c1):
        return jnp.concatenate(rows(c0, c1), axis=0)

    la_ret = [lar_ref[...]] * DEC_SEQ
    oa, ob, oc = _interleave([
        _rec_sample_stages(rows(C_GQ, C_GK), rows(C_GK, C_GV), rows(C_GV, C_GR), rows(C_GR, C_RQ),
                           rows(C_LA, C_ZG), sa0_ref, sa1_ref, gna_ref, rep_ref, bd_ref),
        _rec_sample_stages(rows(C_RQ, C_RK), rows(C_RK, C_RV), rows(C_RV, C_RG), rows(C_RG, C_SQ),
                           la_ret, sb0_ref, sb1_ref, gnb_ref, rep_ref, bd_ref),
        _swa_sample_stages(sink_ref, stacked(C_SQ, C_SK), stacked(C_SK, C_SV), stacked(C_SV, C_LA),
                           kc_ref, vc_ref, mc_ref, mn_ref)])
    mixed = sum(jax.nn.sigmoid(stacked(C_ZG + D_MODEL * j, C_ZG + D_MODEL * (j + 1)))
                * _dot(o.astype(BF16), wb_ref[512 * j:512 * (j + 1), :])
                for j, o in enumerate((oa, ob, oc)))
    out = _dot(mixed.astype(BF16), wo_ref[...])
    for t in range(DEC_SEQ):
        r = pl.ds(t * DEC_BATCH + row0, SB)
        y_ref[r, :] = x_ref[r, :] + out[t * SB:(t + 1) * SB, :]


def _mix_sample(x, z, sink, la_ret, state_gla, state_ret, cache_k, cache_v, layer, gna, gnb, consts, wb, wo):
    n = x.shape[0]
    rep, bd, mc, mn = consts
    whole = lambda w: pl.BlockSpec((n, w), lambda j: (0, 0), pipeline_mode=pl.Buffered(1))
    st_in = pl.BlockSpec((None, SB, GLA_HEADS, GLA_DK, GLA_DV), lambda j: (layer, j, 0, 0, 0))
    st_out = pl.BlockSpec((SB, GLA_HEADS, GLA_DK, GLA_DV), lambda j: (j, 0, 0, 0))
    cache = pl.BlockSpec((None, SB, WINDOW, 128), lambda j: (layer, j, 0, 0))
    st_shape = jax.ShapeDtypeStruct(state_gla.shape[1:], F32)
    return pl.pallas_call(
        _mix_sample_kernel,
        grid=(DEC_BATCH // SB,),
        in_specs=[pl.BlockSpec(memory_space=pltpu.SMEM), whole(D_MODEL), whole(Z_W),
                  _const_spec(la_ret.shape), st_in, st_in, cache, cache,
                  _const_spec((1, GLA_W)), _const_spec((1, RET_W)),
                  _const_spec(rep.shape), _const_spec(bd.shape), _const_spec(mc.shape), _const_spec(mn.shape),
                  _layer_spec((GLA_W + RET_W + SWA_W, D_MODEL), layer),
                  _layer_spec((D_MODEL, D_MODEL), layer)],
        out_specs=[pl.BlockSpec((n, D_MODEL), lambda j: (0, 0)), st_out, st_out],
        out_shape=[jax.ShapeDtypeStruct((n, D_MODEL), F32), st_shape, st_shape],
        compiler_params=_params(("arbitrary",)),
        name="mix_sample",
    )(sink, x, z, la_ret, state_gla, state_ret, cache_k, cache_v, gna, gnb, rep, bd, mc, mn, wb, wo)


MXU_N = 256
FF_CHUNKS = ((0, 6 * MXU_N), (6 * MXU_N, D_FF))


def _ffn_tail(x1, pe_ref, gp_ref, wpg_ref, wpp_ref, gf_ref, y_ref, final):
    hp = _rms(x1, gp_ref[...]).astype(BF16)
    x2 = x1 + jax.nn.sigmoid(_dot(hp, wpg_ref[...])) * _dot(pe_ref[...].astype(BF16), wpp_ref[...])
    y_ref[...] = _rms(x2, gf_ref[...]) if final else x2


def _ffn_stages(x, carry, first, gn_ref, wi_ref, cw_ref, cb_ref, wd_ref, keep):
    tm = x.shape[0]
    hb = _rms(x, gn_ref[...]).astype(BF16)
    row = lax.broadcasted_iota(jnp.int32, (tm, 1), 0)
    acc = x
    for c0, c1 in FF_CHUNKS:
        cs = slice(c0, c1)
        a = _dot(hb, wi_ref[:, c0:c1])
        bb = _dot(hb, wi_ref[:, D_FF + c0:D_FF + c1])
        yield
        p0 = carry[6:7, cs] * first
        p1 = carry[7:8, cs] * first
        a1 = jnp.where(row == 0, p1, pltpu.roll(a, 1, 0))
        a2 = jnp.where(row == 0, p0, jnp.where(row == 1, p1, pltpu.roll(a, 2, 0)))
        conv = cb_ref[:, cs] + cw_ref[0:1, cs] * a2 + cw_ref[1:2, cs] * a1 + cw_ref[2:3, cs] * a
        act = (jax.nn.gelu(conv) * bb).astype(BF16)
        keep.append(a[tm - 8:tm, :])
        yield
        acc = acc + _dot(act, wd_ref[c0:c1, :])
        yield
    return acc


def _ffn_prompt_kernel(x_ref, pe_ref, gn_ref, wi_ref, cw_ref, cb_ref, wd_ref, gp_ref, wpg_ref, wpp_ref,
                       gf_ref, y_ref, tail_ref, carry_scr, *, final, nt):
    i = pl.program_id(1)

    @pl.when(i == 0)
    def _():
        carry_scr[...] = jnp.zeros_like(carry_scr)

    keep = []
    (acc,) = _interleave([_ffn_stages(x_ref[...], carry_scr[...], 1.0, gn_ref, wi_ref, cw_ref, cb_ref, wd_ref, keep)])
    _ffn_tail(acc, pe_ref, gp_ref, wpg_ref, wpp_ref, gf_ref, y_ref, final)
    last_rows = jnp.concatenate(keep, axis=1)
    tail_ref[...] = last_rows

    @pl.when(i < nt - 1)
    def _():
        carry_scr[...] = last_rows


def _ffn_sample_kernel(x_ref, pe_ref, st_ref, gn_ref, wi_ref, cw_ref, cb_ref, wd_ref, gp_ref, wpg_ref,
                       wpp_ref, gf_ref, y_ref, tail_ref, *, final):
    ns = DEC_BATCH
    x = x_ref[...]
    hb = _rms(x, gn_ref[...]).astype(BF16)
    acc = x
    for c0, c1 in FF_CHUNKS:
        cs = slice(c0, c1)
        a = _dot(hb, wi_ref[:, c0:c1])
        bb = _dot(hb, wi_ref[:, D_FF + c0:D_FF + c1])
        st0 = st_ref[0, :, cs]
        st1 = st_ref[1, :, cs]
        a1 = jnp.concatenate([st1, a[0:3 * ns]], axis=0)
        a2 = jnp.concatenate([st0, st1, a[0:2 * ns]], axis=0)
        conv = cb_ref[:, cs] + cw_ref[0:1, cs] * a2 + cw_ref[1:2, cs] * a1 + cw_ref[2:3, cs] * a
        act = (jax.nn.gelu(conv) * bb).astype(BF16)
        acc = acc + _dot(act, wd_ref[c0:c1, :])
        tail_ref[:, cs] = a[2 * ns:4 * ns, :]
    _ffn_tail(acc, pe_ref, gp_ref, wpg_ref, wpp_ref, gf_ref, y_ref, final)


def _ffn_weight_specs(layer):
    return [_layer_spec((1, D_MODEL), layer),
            _layer_spec((D_MODEL, 2 * D_FF), layer),
            _layer_spec((CONV_W, D_FF), layer),
            _layer_spec((1, D_FF), layer),
            _layer_spec((D_FF, D_MODEL), layer),
            _layer_spec((1, D_MODEL), layer),
            _layer_spec((D_MODEL, D_MODEL), layer),
            _layer_spec((D_PLE, D_MODEL), layer),
            _const_spec((1, D_MODEL))]


def _ffn_prompt(x, pe, layer, weights, nb, t, tm, final):
    nt = t // tm
    tok = lambda w: pl.BlockSpec((tm, w), lambda b, i: (b * nt + i, 0))
    return pl.pallas_call(
        functools.partial(_ffn_prompt_kernel, final=final, nt=nt),
        grid=(nb, nt),
        in_specs=[tok(D_MODEL),
                  pl.BlockSpec((None, tm, D_PLE), lambda b, i: (layer, b * nt + i, 0))]
        + _ffn_weight_specs(layer),
        out_specs=[tok(D_MODEL), pl.BlockSpec((8, D_FF), lambda b, i: (b, 0))],
        out_shape=[jax.ShapeDtypeStruct((nb * t, D_MODEL), F32),
                   jax.ShapeDtypeStruct((nb * 8, D_FF), F32)],
        scratch_shapes=[pltpu.VMEM((8, D_FF), F32)],
        compiler_params=_params(("arbitrary", "arbitrary")),
        name="ffn_prompt",
    )(x, pe, *weights)


def _ffn_sample(x, pe, st, layer, weights, final):
    n = x.shape[0]
    return pl.pallas_call(
        functools.partial(_ffn_sample_kernel, final=final),
        grid=(1,),
        in_specs=[pl.BlockSpec((n, D_MODEL), lambda i: (0, 0)),
                  pl.BlockSpec((None, n, D_PLE), lambda i: (layer, 0, 0)),
                  pl.BlockSpec((None, CONV_W - 1, DEC_BATCH, D_FF), lambda i: (layer, 0, 0, 0))]
        + _ffn_weight_specs(layer),
        out_specs=[pl.BlockSpec((n, D_MODEL), lambda i: (0, 0)),
                   pl.BlockSpec((n // 2, D_FF), lambda i: (0, 0))],
        out_shape=[jax.ShapeDtypeStruct((n, D_MODEL), F32),
                   jax.ShapeDtypeStruct((n // 2, D_FF), F32)],
        compiler_params=_params(("arbitrary",)),
        name="ffn_sample",
    )(x, pe, st, *weights)


def _rope_tables(pos, inv_freq):
    half = inv_freq.shape[0]
    ang = pos.astype(np.float64)[:, None] * inv_freq[None, :]
    c, s = np.cos(ang), np.sin(ang)
    rest = SWA_HEAD_DIM - 2 * half
    cos64 = np.concatenate([c, c, np.ones((pos.shape[0], rest))], axis=1)
    sin64 = np.concatenate([-s, s, np.zeros((pos.shape[0], rest))], axis=1)
    return (jnp.asarray(np.concatenate([cos64, cos64], axis=1), F32),
            jnp.asarray(np.concatenate([sin64, sin64], axis=1), F32))


def _pack_w_in(w_in):
    ga0 = sum(IN_SIZES[:4])
    ga = jnp.pad(w_in[:, :, ga0:ga0 + GLA_RANK], ((0, 0), (0, 0), (0, LANES - GLA_RANK)))
    return jnp.concatenate([w_in[:, :, :ga0], w_in[:, :, ga0 + GLA_RANK:], ga], axis=2).astype(BF16)


def kernel(x_prompt, x_sample, state_gla, state_ret, cache_swa_k, cache_swa_v, state_conv, p_prompt,
           p_sample, norm_mix, w_in, w_gla_a, b_gla_a, gla_norm, ret_norm, swa_sink, w_branch, w_out,
           norm_ffn, w_ffn_in, conv_w, conv_b, w_ffn_out, norm_ple, w_ple_gate, w_ple_proj, norm_final):
    nb, t, _ = x_prompt.shape
    ns, ts, _ = x_sample.shape
    n_s = ns * ts

    w_main = _pack_w_in(w_in)
    wa = jnp.pad(w_gla_a, ((0, 0), (0, LANES - GLA_RANK), (0, 0))).astype(BF16)
    ba = b_gla_a[:, None, :]
    wb = w_branch.astype(BF16)
    wo = w_out.astype(BF16)
    ffn_weights = (norm_ffn[:, None, :], w_ffn_in.astype(BF16), conv_w, conv_b[:, None, :],
                   w_ffn_out.astype(BF16), norm_ple[:, None, :], w_ple_gate.astype(BF16),
                   w_ple_proj.astype(BF16), norm_final[None, :])
    g_mix = norm_mix[:, None, :]
    gn_gla = jnp.tile(gla_norm, (1, GLA_HEADS))[:, None, :]
    gn_ret = ret_norm.reshape(DEPTH, 1, RET_W)

    ret_freq = 1.0 / (RET_THETA ** np.linspace(0.0, 1.0, RET_DK // 2))
    swa_freq = 1.0 / (ROPE_THETA ** (np.arange(0, ROPE_DIM, 2, dtype=np.float64) / ROPE_DIM))
    pos_p = np.arange(t)
    pos_s = PAST_LEN + np.arange(n_s) // ns
    tabs_p = _rope_tables(pos_p, ret_freq) + _rope_tables(pos_p, swa_freq)
    tabs_s = _rope_tables(pos_s, ret_freq) + _rope_tables(pos_s, swa_freq)

    gm_np, mk_np = _gla_constants()
    d_np, qdec_np, kdec_np, adec_np = _ret_constants()
    hk_np, hv_np, hs_np = _head_masks()
    mix_consts = (jnp.asarray(gm_np, BF16), jnp.asarray(mk_np), jnp.asarray(d_np), jnp.asarray(qdec_np),
                  jnp.asarray(kdec_np), jnp.asarray(adec_np), jnp.asarray(hk_np, BF16), jnp.asarray(hv_np, BF16),
                  jnp.asarray(hs_np))
    log_gamma = np.log1p(-np.exp2(-5.0 - np.arange(RET_HEADS, dtype=np.float64)))
    la_ret = jnp.asarray(np.broadcast_to(np.repeat(log_gamma, RET_DK)[None, :], (SB, RET_HEADS * RET_DK)), F32)
    mc_np, mn_np = _swa_sample_masks()
    rep_np, bd_np = _rec_sample_constants()
    sample_consts = (jnp.asarray(rep_np, BF16), jnp.asarray(bd_np), jnp.asarray(mc_np), jnp.asarray(mn_np))

    xp = x_prompt.reshape(nb * t, D_MODEL)
    xs = x_sample.transpose(1, 0, 2).reshape(n_s, D_MODEL)
    pe_p = p_prompt.reshape(DEPTH, nb * t, D_PLE)
    pe_s = p_sample.transpose(0, 2, 1, 3).reshape(DEPTH, n_s, D_PLE)
    conv_st = state_conv.transpose(0, 2, 1, 3)
    ck = cache_swa_k.reshape(DEPTH, ns, WINDOW, SWA_KV_HEADS * SWA_HEAD_DIM)
    cv = cache_swa_v.reshape(DEPTH, ns, WINDOW, SWA_KV_HEADS * SWA_HEAD_DIM)

    outs = {k_: [] for k_ in ("gla_p", "gla_s", "ret_p", "ret_s", "kv_p", "kv_s", "conv_p", "conv_s")}
    for l in range(DEPTH):
        final = l == DEPTH - 1
        sink = swa_sink[l]

        xp3, st_a, st_b, kv_tail = _mix_prompt(xp.reshape(nb, t, D_MODEL), sink, l, g_mix, w_main, wa, ba, tabs_p,
                                               mix_consts, gn_gla[l], gn_ret[l], wb, wo)
        xp, tail = _ffn_prompt(xp3.reshape(nb * t, D_MODEL), pe_p, l, ffn_weights, nb, t, 256, final)
        outs["gla_p"].append(st_a)
        outs["ret_p"].append(st_b)
        outs["kv_p"].append(kv_tail)
        outs["conv_p"].append(tail.reshape(nb, 8, D_FF)[:, 8 - (CONV_W - 1):])

        zs = _in_proj(xs, l, g_mix, w_main, wa, ba, tabs_s, 256)
        xs, st_a, st_b = _mix_sample(xs, zs, sink, la_ret, state_gla, state_ret, ck, cv, l, gn_gla[l], gn_ret[l],
                                     sample_consts, wb, wo)
        xs, tail_s = _ffn_sample(xs, pe_s, conv_st, l, ffn_weights, final)
        outs["gla_s"].append(st_a)
        outs["ret_s"].append(st_b)
        outs["kv_s"].append(zs[:, C_SK:C_LA])
        outs["conv_s"].append(tail_s.reshape(CONV_W - 1, ns, D_FF).transpose(1, 0, 2))

    y_prompt = xp.reshape(nb, t, D_MODEL)
    y_sample = xs.reshape(ts, ns, D_MODEL).transpose(1, 0, 2)
    st = {k_: jnp.stack(v_) for k_, v_ in outs.items()}

    def kv_heads(a):
        return a.reshape(a.shape[:-1] + (SWA_KV_HEADS, SWA_HEAD_DIM))

    k_p = kv_heads(st["kv_p"][..., :LANES])
    v_p = kv_heads(st["kv_p"][..., LANES:])
    kv_new = st["kv_s"].reshape(DEPTH, ts, ns, 2 * LANES).transpose(0, 2, 1, 3)
    k_s = jnp.concatenate([cache_swa_k[:, :, ts:], kv_heads(kv_new[..., :LANES])], axis=2)
    v_s = jnp.concatenate([cache_swa_v[:, :, ts:], kv_heads(kv_new[..., LANES:])], axis=2)
    return (y_prompt, y_sample, st["gla_p"], st["gla_s"], st["ret_p"], st["ret_s"],
            k_p, k_s, v_p, v_s, st["conv_p"], st["conv_s"])
```

```python
import functools
import math

import numpy as np
import jax
import jax.numpy as jnp
from jax import lax
from jax.experimental import pallas as pl
from jax.experimental.pallas import tpu as pltpu

F32 = jnp.float32
BF16 = jnp.bfloat16

D_MODEL = 1024
BATCH = 2
SEQ = 8192
DEPTH = 4
DEC_BATCH = 128
DEC_SEQ = 4
PAST_LEN = 8192
D_PLE = 256
GLA_HEADS = 4
GLA_DK = 64
GLA_DV = 128
GLA_RANK = 16
GLA_TAU = 16.0
RET_HEADS = 4
RET_DK = 64
RET_DV = 128
RET_THETA = 10000.0
SWA_Q_HEADS = 8
SWA_KV_HEADS = 2
SWA_HEAD_DIM = 64
SWA_GROUP = SWA_Q_HEADS // SWA_KV_HEADS
WINDOW = 128
ROPE_THETA = 500000.0
ROPE_DIM = SWA_HEAD_DIM // 4
D_FF = 2816
CONV_W = 3
N_BRANCH = 3
EPS = 1e-6
NEG_INF = -1e30

GLA_W = GLA_HEADS * GLA_DV
RET_W = RET_HEADS * RET_DV
SWA_W = SWA_Q_HEADS * SWA_HEAD_DIM
IN_SIZES = (GLA_HEADS * GLA_DK, GLA_HEADS * GLA_DK, GLA_W, GLA_W, GLA_RANK,
            RET_HEADS * RET_DK, RET_HEADS * RET_DK, RET_W, RET_W,
            SWA_W, SWA_KV_HEADS * SWA_HEAD_DIM, SWA_KV_HEADS * SWA_HEAD_DIM,
            N_BRANCH * D_MODEL)

LANES = 128
VMEM_LIMIT = 52 * 1024 * 1024

C_GQ = 0
C_GK = 256
C_GV = 512
C_GR = 1024
C_RQ = 1536
C_RK = 1792
C_RV = 2048
C_RG = 2560
C_SQ = 3072
C_SK = 3584
C_SV = 3712
C_LA = 3840
C_ZG = 4096
Z_W = 7168
W_ZG = 3840
W_GA = 6912
W_MAIN = 7040

TT = 128
N_LEVELS = 7
SB = 16


def _dot(a, b):
    return jnp.dot(a, b, preferred_element_type=F32)


def _dot_nt(a, b):
    return lax.dot_general(a, b, (((1,), (1,)), ((), ())), preferred_element_type=F32)


def _dot_tn(a, b):
    return lax.dot_general(a, b, (((0,), (0,)), ((), ())), preferred_element_type=F32)


def _rms(x, g):
    return x * lax.rsqrt(jnp.mean(x * x, axis=-1, keepdims=True) + EPS) * g


def _layer_spec(shape, layer):
    nd = len(shape)
    return pl.BlockSpec((None,) + tuple(shape), lambda *_: (layer,) + (0,) * nd,
                        pipeline_mode=pl.Buffered(1))


def _const_spec(shape):
    nd = len(shape)
    return pl.BlockSpec(tuple(shape), lambda *_: (0,) * nd, pipeline_mode=pl.Buffered(1))


def _params(sem):
    return pltpu.CompilerParams(dimension_semantics=sem, vmem_limit_bytes=VMEM_LIMIT)


def _rope_block(x, cos, sin_s, first):
    half_mask, half = first
    xr = jnp.where(half_mask, pltpu.roll(x, LANES - half, 1), pltpu.roll(x, half, 1))
    return x * cos + xr * sin_s


def _gate_logits(hb, w_ref, j):
    return _dot(hb, w_ref[:, W_ZG + D_MODEL * j:W_ZG + D_MODEL * (j + 1)])


def _in_stages(x, g_ref, w_ref, wa_ref, ba_ref, tabs, z_ref, with_gates):
    hb = _rms(x, g_ref[...]).astype(BF16)

    def mm(c0, c1):
        return _dot(hb, w_ref[:, c0:c1])

    lane = lax.broadcasted_iota(jnp.int32, (1, LANES), 1) % 64
    ret_first = (lane < RET_DK // 2, RET_DK // 2)
    swa_first = (lane < ROPE_DIM // 2, ROPE_DIM // 2)
    rc, rs, sc, ss = tabs

    ga = mm(W_GA, W_MAIN)
    xa = _dot(ga.astype(BF16), wa_ref[...]) + ba_ref[...]
    log_sig = jnp.minimum(xa, 0.0) - jnp.log1p(jnp.exp(-jnp.abs(xa)))
    z_ref[:, C_LA:C_ZG] = log_sig * (1.0 / GLA_TAU)
    z_ref[:, C_GQ:C_GK] = mm(C_GQ, C_GK) * (GLA_DK ** -0.5)
    yield
    z_ref[:, C_GK:C_RQ] = mm(C_GK, C_RQ)
    yield
    rq = mm(C_RQ, C_RK)
    rk = mm(C_RK, C_RV)
    for j in range(2):
        sl = slice(LANES * j, LANES * (j + 1))
        z_ref[:, C_RQ + LANES * j:C_RQ + LANES * (j + 1)] = _rope_block(rq[:, sl], rc, rs, ret_first)
        z_ref[:, C_RK + LANES * j:C_RK + LANES * (j + 1)] = (
            _rope_block(rk[:, sl], rc, rs, ret_first) * (RET_DK ** -0.5))
    yield
    z_ref[:, C_RV:C_SQ] = mm(C_RV, C_SQ)
    yield
    sqkv = mm(C_SQ, C_LA)
    for j in range(5):
        sl = slice(LANES * j, LANES * (j + 1))
        z_ref[:, C_SQ + LANES * j:C_SQ + LANES * (j + 1)] = _rope_block(sqkv[:, sl], sc, ss, swa_first)
    z_ref[:, C_SV:C_LA] = sqkv[:, C_SV - C_SQ:]
    yield
    if with_gates:
        for j in range(N_BRANCH):
            z_ref[:, C_ZG + D_MODEL * j:C_ZG + D_MODEL * (j + 1)] = _gate_logits(hb, w_ref, j)
            yield


def _in_kernel(x_ref, g_ref, w_ref, wa_ref, ba_ref, rc_ref, rs_ref, sc_ref, ss_ref, z_ref):
    tabs = (rc_ref[...], rs_ref[...], sc_ref[...], ss_ref[...])
    for _ in _in_stages(x_ref[...], g_ref, w_ref, wa_ref, ba_ref, tabs, z_ref, True):
        pass


def _in_proj(x, layer, g_mix, w_main, wa, ba, tabs, tm):
    n = x.shape[0]
    rc, rs, sc, ss = tabs
    nt = rc.shape[0] // tm
    tab = pl.BlockSpec((tm, LANES), lambda i: (i % nt, 0))
    return pl.pallas_call(
        _in_kernel,
        grid=(n // tm,),
        in_specs=[pl.BlockSpec((tm, D_MODEL), lambda i: (i, 0)),
                  _layer_spec((1, D_MODEL), layer),
                  _layer_spec((D_MODEL, W_MAIN), layer),
                  _layer_spec((LANES, 256), layer),
                  _layer_spec((1, 256), layer),
                  tab, tab, tab, tab],
        out_specs=pl.BlockSpec((tm, Z_W), lambda i: (i, 0)),
        out_shape=jax.ShapeDtypeStruct((n, Z_W), F32),
        compiler_params=_params(("arbitrary",)),
        name="in_proj",
    )(x, g_mix, w_main, wa, ba, rc, rs, sc, ss)


def _gla_constants():
    t = np.arange(TT)
    g = np.zeros((2 + N_LEVELS, TT, TT), np.float32)
    g[0] = (t[None, :] <= t[:, None])
    g[1] = (t[None, :] > t[:, None])
    m = np.zeros((1 + N_LEVELS, TT, TT), np.float32)
    m[0] = np.eye(TT)
    for lv in range(1, N_LEVELS + 1):
        bs, hf = 2 ** lv, 2 ** (lv - 1)
        bd = (t // bs) * bs + hf - 1
        upper = (t % bs) >= hf
        u = t[None, :]
        g[1 + lv] = np.where(upper[:, None], (u > bd[:, None]) & (u <= t[:, None]),
                             (u > t[:, None]) & (u <= bd[:, None]))
        same = (t[:, None] // bs) == (t[None, :] // bs)
        m[lv] = same & upper[:, None] & (~upper)[None, :]
    return g.reshape(-1, TT), np.tile(m, (1, 1, GLA_HEADS))


def _head_masks():
    hk = np.arange(GLA_HEADS * TT)[:, None] // TT == np.arange(GLA_HEADS * GLA_DK)[None, :] // GLA_DK
    hv = np.arange(GLA_HEADS * TT)[:, None] // TT == np.arange(GLA_W)[None, :] // GLA_DV
    hs = np.arange(GLA_HEADS * GLA_DK)[:, None] // GLA_DK == np.arange(GLA_W)[None, :] // GLA_DV
    return hk.astype(np.float32), hv.astype(np.float32), hs.astype(np.float32)


def _heads_blockdiag(x, mask):
    return jnp.concatenate([x] * GLA_HEADS, axis=0) * mask


def _col_vector(row):
    n = row.shape[1]
    eye = lax.broadcasted_iota(jnp.int32, (n, n), 0) == lax.broadcasted_iota(jnp.int32, (n, n), 1)
    return jnp.sum(jnp.where(eye, row, 0.0), axis=1, keepdims=True)


def _norm_gate(o, gn, gate):
    outs = [_rms(o[:, GLA_DV * h:GLA_DV * (h + 1)], gn[:, GLA_DV * h:GLA_DV * (h + 1)]) for h in range(GLA_HEADS)]
    return jnp.concatenate(outs, axis=1) * (gate * jax.nn.sigmoid(gate))


def _gla_tile(z_ref, gm_ref, mk_ref, hk_ref, hv_ref, hs_ref, gn_ref, st_ref):
    la = z_ref[:, C_LA:C_ZG]
    la_hi = la.astype(BF16)
    la_lo = (la - la_hi.astype(F32)).astype(BF16)
    gm = gm_ref[...]
    ex = jnp.exp(_dot(gm, la_hi) + _dot(gm, la_lo))
    q = z_ref[:, C_GQ:C_GK]
    k = z_ref[:, C_GK:C_GV]
    e_b = ex[0:TT]
    a_row = e_b[TT - 1:TT, :]
    qd = (q * e_b).astype(BF16)
    kd = (k * ex[TT:2 * TT]).astype(BF16)
    qb = q.astype(BF16)
    kb = k.astype(BF16)
    ql = [(q * ex[(1 + lv) * TT:(2 + lv) * TT]).astype(BF16) for lv in range(1, N_LEVELS + 1)]
    kl = [(k * ex[(1 + lv) * TT:(2 + lv) * TT]).astype(BF16) for lv in range(1, N_LEVELS + 1)]
    hk = hk_ref[...]
    yield
    a = mk_ref[0] * _dot_nt(qb, _heads_blockdiag(kb, hk))
    for lv in range(N_LEVELS):
        yield
        a = a + mk_ref[lv + 1] * _dot_nt(ql[lv], _heads_blockdiag(kl[lv], hk))
    yield
    v = z_ref[:, C_GV:C_GR].astype(BF16)
    s = st_ref[...]
    o = _dot(a.astype(BF16), _heads_blockdiag(v, hv_ref[...])) + _dot(qd, s.astype(BF16))
    yield
    st_ref[...] = _col_vector(a_row) * s + hs_ref[...] * _dot_tn(kd, v)
    yield
    return _norm_gate(o, gn_ref[...], z_ref[:, C_GR:C_RQ])


def _ret_gammas():
    return [1.0 - 2.0 ** (-5.0 - h) for h in range(RET_HEADS)]


def _ret_constants():
    t = np.arange(TT, dtype=np.float64)
    d = np.zeros((RET_HEADS, TT, TT), np.float64)
    qdec = np.zeros((TT, RET_HEADS * RET_DK), np.float64)
    kdec = np.zeros((TT, RET_HEADS * RET_DK), np.float64)
    adec = np.zeros((RET_HEADS * RET_DK, LANES), np.float64)
    for h, gam in enumerate(_ret_gammas()):
        diff = t[:, None] - t[None, :]
        d[h] = np.where(diff >= 0, gam ** np.maximum(diff, 0.0), 0.0)
        qdec[:, h * RET_DK:(h + 1) * RET_DK] = (gam ** (t + 1.0))[:, None]
        kdec[:, h * RET_DK:(h + 1) * RET_DK] = (gam ** (TT - 1.0 - t))[:, None]
        adec[h * RET_DK:(h + 1) * RET_DK, :] = gam ** TT
    d = np.concatenate(list(d), axis=1)
    adec = np.tile(adec, (1, RET_W // LANES))
    return d.astype(np.float32), qdec.astype(np.float32), kdec.astype(np.float32), adec.astype(np.float32)


def _ret_tile(z_ref, d_ref, qdec_ref, kdec_ref, adec_ref, hk_ref, hv_ref, hs_ref, gn_ref, st_ref):
    q = z_ref[:, C_RQ:C_RK]
    k = z_ref[:, C_RK:C_RV]
    qd = (q * qdec_ref[...]).astype(BF16)
    kd = (k * kdec_ref[...]).astype(BF16)
    a = d_ref[...] * _dot_nt(q.astype(BF16), _heads_blockdiag(k.astype(BF16), hk_ref[...]))
    yield
    v = z_ref[:, C_RV:C_RG].astype(BF16)
    s = st_ref[...]
    o = _dot(a.astype(BF16), _heads_blockdiag(v, hv_ref[...])) + _dot(qd, s.astype(BF16))
    yield
    st_ref[...] = adec_ref[...] * s + hs_ref[...] * _dot_tn(kd, v)
    yield
    return _norm_gate(o, gn_ref[...], z_ref[:, C_RG:C_SQ])


def _rec_sample_constants():
    c = np.arange(SB * GLA_DK)
    rep = np.arange(GLA_DK)[:, None] == (c[None, :] % GLA_DK)
    bd = (np.arange(DEC_SEQ * SB)[:, None] % SB) == (c[None, :] // GLA_DK)
    return rep.astype(np.float32), bd.astype(np.float32)


def _block_rows(ref, row0, c0, c1):
    return [ref[pl.ds(t * DEC_BATCH + row0, SB), c0:c1] for t in range(DEC_SEQ)]


def _rec_sample_stages(q, k, v, gate, la, s0_ref, s1_ref, gn_ref, rep_ref, bd_ref):
    b = []
    for t in range(DEC_SEQ):
        b.append(la[t] if t == 0 else b[-1] + la[t])
    qd = jnp.concatenate([q[t] * jnp.exp(b[t]) for t in range(DEC_SEQ)], axis=0).astype(BF16)
    kd = jnp.concatenate([k[t] * jnp.exp(b[-1] - b[t]) for t in range(DEC_SEQ)], axis=0).astype(BF16)
    vv = jnp.concatenate(v, axis=0).astype(BF16)
    a = jnp.exp(b[-1])
    a_hi = a.astype(BF16)
    a_r = a - a_hi.astype(F32)
    a_mid = a_r.astype(BF16)
    a_lo = (a_r - a_mid.astype(F32)).astype(BF16)
    rep = rep_ref[...]
    bd = bd_ref[...]
    ones = jnp.ones((SB, GLA_DV), BF16)

    def expand(x, mask):
        return (_dot(x, rep) * mask).astype(BF16)

    yield
    oi = []
    for h in range(GLA_HEADS):
        hs = slice(GLA_DK * h, GLA_DK * (h + 1))
        vs = slice(GLA_DV * h, GLA_DV * (h + 1))
        s0 = s0_ref[:, h].reshape(SB * GLA_DK, GLA_DV)
        oi.append(_dot(expand(qd[:, hs], bd), s0.astype(BF16)))
        a_col = (_dot_tn(expand(a_hi[:, hs], bd[0:SB]), ones) + _dot_tn(expand(a_mid[:, hs], bd[0:SB]), ones)
                 + _dot_tn(expand(a_lo[:, hs], bd[0:SB]), ones))
        s1 = a_col * s0 + _dot_tn(expand(kd[:, hs], bd), vv[:, vs])
        s1_ref[:, h] = s1.reshape(SB, GLA_DK, GLA_DV)
        yield
    oi = jnp.concatenate(oi, axis=1)

    gn = gn_ref[...]
    outs = []
    for t in range(DEC_SEQ):
        o = oi[t * SB:(t + 1) * SB, :]
        for s in range(t + 1):
            p = q[t] * k[s] * jnp.exp(b[t] - b[s])
            parts = []
            for h in range(GLA_HEADS):
                hs = slice(GLA_DK * h, GLA_DK * (h + 1))
                vs = slice(GLA_DV * h, GLA_DV * (h + 1))
                parts.append(jnp.sum(p[:, hs], axis=1, keepdims=True) * v[s][:, vs])
            o = o + jnp.concatenate(parts, axis=1)
        outs.append(_norm_gate(o, gn, gate[t]))
        yield
    return jnp.concatenate(outs, axis=0)


def _swa_tile(z_ref, sink_ref, kp_ref, vp_ref, first_col):
    kc = z_ref[:, C_SK:C_SV].astype(BF16)
    vc = z_ref[:, C_SV:C_LA].astype(BF16)
    kband = jnp.concatenate([kp_ref[...], kc], axis=0)
    vband = jnp.concatenate([vp_ref[...], vc], axis=0)
    rows = SWA_GROUP * WINDOW
    r = lax.broadcasted_iota(jnp.int32, (rows, 2 * WINDOW), 0) & (WINDOW - 1)
    c = lax.broadcasted_iota(jnp.int32, (rows, 2 * WINDOW), 1)
    valid = (c > r) & (c <= r + WINDOW) & (c >= first_col)
    groups = range(SWA_KV_HEADS)
    dsl = [slice(SWA_HEAD_DIM * g, SWA_HEAD_DIM * (g + 1)) for g in groups]
    s, sk = [], []
    for g in groups:
        heads = range(SWA_GROUP * g, SWA_GROUP * (g + 1))
        q = jnp.concatenate(
            [z_ref[:, C_SQ + SWA_HEAD_DIM * hq:C_SQ + SWA_HEAD_DIM * (hq + 1)] for hq in heads], axis=0)
        sk.append(jnp.concatenate([jnp.full((WINDOW, 1), sink_ref[hq], F32) for hq in heads], axis=0))
        s.append(_dot_nt(q.astype(BF16), kband[:, dsl[g]]) * (SWA_HEAD_DIM ** -0.5))
    yield
    m, p = [], []
    for g in groups:
        sg = jnp.where(valid, s[g], NEG_INF)
        m.append(jnp.maximum(jnp.max(sg, axis=-1, keepdims=True), sk[g]))
        p.append(jnp.exp(sg - m[g]))
    yield
    outs = []
    for g in groups:
        den = jnp.sum(p[g], axis=-1, keepdims=True) + jnp.exp(sk[g] - m[g])
        o = _dot(p[g].astype(BF16), vband[:, dsl[g]]) / den
        outs += [o[WINDOW * j:WINDOW * (j + 1)] for j in range(SWA_GROUP)]
    yield
    return jnp.concatenate(outs, axis=1), kc, vc


def _interleave(stage_fns):
    results = [None] * len(stage_fns)
    live = list(enumerate(stage_fns))
    while live:
        still = []
        for idx, gen in live:
            try:
                next(gen)
                still.append((idx, gen))
            except StopIteration as stop:
                results[idx] = stop.value
        live = still
    return results


def _merge_stages(pend_ref, x_ref, g_ref, w_ref, wb_ref, wo_ref, y_ref):
    nb = x_ref.shape[0]
    hb = _rms(x_ref[...].reshape(nb * TT, D_MODEL), g_ref[...]).astype(BF16)
    gates = []
    for j in range(N_BRANCH):
        gates.append(jax.nn.sigmoid(_gate_logits(hb, w_ref, j)))
        yield
    mixed = None
    for j in range(N_BRANCH):
        term = gates[j] * _dot(pend_ref[:, GLA_W * j:GLA_W * (j + 1)], wb_ref[GLA_W * j:GLA_W * (j + 1), :])
        mixed = term if mixed is None else mixed + term
        yield
    out = _dot(mixed.astype(BF16), wo_ref[...])
    for b in range(x_ref.shape[0]):
        y_ref[b] = x_ref[b] + out[TT * b:TT * (b + 1)]
    yield


def _mix_kernel(sink_ref, xp_ref, xn_ref, tp0, tp1, tp2, tp3, tn0, tn1, tn2, tn3, g_ref, w_ref, wa_ref, ba_ref,
                gm_ref, mk_ref, d_ref, qdec_ref, kdec_ref, adec_ref, hk_ref, hv_ref, hs_ref, gna_ref, gnb_ref,
                wb_ref, wo_ref, y_ref, sta_ref, stb_ref, kv_ref, z_scr, pend_scr, sa_scr, sb_scr, kp_scr, vp_scr,
                *, nt):
    i = pl.program_id(0)
    nb = xp_ref.shape[0]

    def rows_of(ref3):
        return ref3[...].reshape(nb * TT, ref3.shape[2])

    def tiled(tab_refs):
        return tuple(jnp.concatenate([r[...]] * nb, axis=0) for r in tab_refs)

    @pl.when(i == 0)
    def _():
        sa_scr[...] = jnp.zeros_like(sa_scr)
        sb_scr[...] = jnp.zeros_like(sb_scr)
        kp_scr[...] = jnp.zeros_like(kp_scr)
        vp_scr[...] = jnp.zeros_like(vp_scr)
        pend_scr[...] = jnp.zeros_like(pend_scr)
        for _ in _in_stages(rows_of(xp_ref), g_ref, w_ref, wa_ref, ba_ref, tiled((tp0, tp1, tp2, tp3)),
                            z_scr.at[0], False):
            pass

    slot = i % 2
    zc = z_scr.at[slot]
    first_col = jnp.where(i > 0, 0, WINDOW)
    stages = [_merge_stages(pend_scr, xp_ref, g_ref, w_ref, wb_ref, wo_ref, y_ref),
              _in_stages(rows_of(xn_ref), g_ref, w_ref, wa_ref, ba_ref, tiled((tn0, tn1, tn2, tn3)),
                         z_scr.at[1 - slot], False)]
    for b in range(nb):
        zb = zc.at[pl.ds(TT * b, TT)]
        stages += [_gla_tile(zb, gm_ref, mk_ref, hk_ref, hv_ref, hs_ref, gna_ref, sa_scr.at[b]),
                   _swa_tile(zb, sink_ref, kp_scr.at[b], vp_scr.at[b], first_col),
                   _ret_tile(zb, d_ref, qdec_ref, kdec_ref, adec_ref, hk_ref, hv_ref, hs_ref, gnb_ref, sb_scr.at[b])]
    branch = _interleave(stages)[2:]
    swa = [branch[3 * b + 1] for b in range(nb)]
    outs = [[branch[3 * b] for b in range(nb)], [branch[3 * b + 2] for b in range(nb)], [s[0] for s in swa]]
    pend = [jnp.concatenate(o, axis=0).astype(BF16) for o in outs]

    @pl.when(i < nt)
    def _():
        for j in range(N_BRANCH):
            pend_scr[:, GLA_W * j:GLA_W * (j + 1)] = pend[j]
        for b in range(nb):
            kp_scr[b] = swa[b][1]
            vp_scr[b] = swa[b][2]
        for b in range(nb):
            for h in range(GLA_HEADS):
                sta_ref[b, h] = sa_scr[b, GLA_DK * h:GLA_DK * (h + 1), GLA_DV * h:GLA_DV * (h + 1)]
                stb_ref[b, h] = sb_scr[b, RET_DK * h:RET_DK * (h + 1), RET_DV * h:RET_DV * (h + 1)]
            kv_ref[b] = zc[pl.ds(TT * b, TT), C_SK:C_LA]


MIX_VMEM_LIMIT = 60 * 1024 * 1024


def _mix_prompt(x, sink, layer, g_mix, w_main, wa, ba, tabs, consts, gna, gnb, wb, wo):
    nb, t, _ = x.shape
    nt = t // TT
    st_spec = pl.BlockSpec((nb, GLA_HEADS, GLA_DK, GLA_DV), lambda i: (0, 0, 0, 0))
    st_shape = jax.ShapeDtypeStruct((nb, GLA_HEADS, GLA_DK, GLA_DV), F32)
    nxt = lambda i: jnp.minimum(i + 1, nt - 1)
    prv = lambda i: jnp.maximum(i - 1, 0)
    tab_prv = pl.BlockSpec((TT, LANES), lambda i: (prv(i), 0))
    tab_nxt = pl.BlockSpec((TT, LANES), lambda i: (nxt(i), 0))
    return pl.pallas_call(
        functools.partial(_mix_kernel, nt=nt),
        grid=(nt + 1,),
        in_specs=[pl.BlockSpec(memory_space=pltpu.SMEM),
                  pl.BlockSpec((nb, TT, D_MODEL), lambda i: (0, prv(i), 0)),
                  pl.BlockSpec((nb, TT, D_MODEL), lambda i: (0, nxt(i), 0))]
        + [tab_prv] * 4 + [tab_nxt] * 4
        + [_layer_spec((1, D_MODEL), layer), _layer_spec((D_MODEL, W_MAIN), layer),
           _layer_spec((LANES, 256), layer), _layer_spec((1, 256), layer)]
        + [_const_spec(c.shape) for c in consts]
        + [_const_spec((1, GLA_W)), _const_spec((1, RET_W)),
           _layer_spec((GLA_W + RET_W + SWA_W, D_MODEL), layer),
           _layer_spec((D_MODEL, D_MODEL), layer)],
        out_specs=[pl.BlockSpec((nb, TT, D_MODEL), lambda i: (0, prv(i), 0)), st_spec, st_spec,
                   pl.BlockSpec((nb, WINDOW, 2 * LANES), lambda i: (0, 0, 0))],
        out_shape=[jax.ShapeDtypeStruct((nb, t, D_MODEL), F32), st_shape, st_shape,
                   jax.ShapeDtypeStruct((nb, WINDOW, 2 * LANES), F32)],
        scratch_shapes=[pltpu.VMEM((2, nb * TT, C_ZG), F32),
                        pltpu.VMEM((nb * TT, GLA_W + RET_W + SWA_W), BF16),
                        pltpu.VMEM((nb, GLA_HEADS * GLA_DK, GLA_W), F32),
                        pltpu.VMEM((nb, RET_HEADS * RET_DK, RET_W), F32),
                        pltpu.VMEM((nb, WINDOW, 128), BF16), pltpu.VMEM((nb, WINDOW, 128), BF16)],
        compiler_params=pltpu.CompilerParams(dimension_semantics=("arbitrary",), vmem_limit_bytes=MIX_VMEM_LIMIT),
        name="mix_prompt",
    )(sink, x, x, *tabs, *tabs, g_mix, w_main, wa, ba, *consts, gna, gnb, wb, wo)


def _swa_sample_masks():
    r = np.arange(DEC_SEQ * SB)
    rt, rb = r // SB, r % SB
    c = np.arange(SB * WINDOW)
    cb, cj = c // WINDOW, c % WINDOW
    m_cache = (rb[:, None] == cb[None, :]) & (cj[None, :] > rt[:, None])
    m_new = (rb[:, None] == rb[None, :]) & (rt[None, :] <= rt[:, None])
    return m_cache.astype(np.float32), m_new.astype(np.float32)


def _swa_sample_stages(sink_ref, q, kn, vn, kc_ref, vc_ref, mc_ref, mn_ref):
    q = q.astype(BF16)
    kn = kn.astype(BF16)
    vn = vn.astype(BF16)
    kct = [jnp.concatenate([kc_ref[b, g] for b in range(SB)], axis=1).astype(BF16) for g in range(SWA_KV_HEADS)]
    vct = [jnp.concatenate([vc_ref[b, g] for b in range(SB)], axis=1).astype(BF16) for g in range(SWA_KV_HEADS)]
    ok_c = mc_ref[...] > 0.0
    ok_n = mn_ref[...] > 0.0
    outs = []
    for hq in range(SWA_Q_HEADS):
        g = hq // SWA_GROUP
        ds_ = slice(SWA_HEAD_DIM * g, SWA_HEAD_DIM * (g + 1))
        qh = q[:, SWA_HEAD_DIM * hq:SWA_HEAD_DIM * (hq + 1)]
        s1 = jnp.where(ok_c, _dot(qh, kct[g]) * (SWA_HEAD_DIM ** -0.5), NEG_INF)
        s2 = jnp.where(ok_n, _dot_nt(qh, kn[:, ds_]) * (SWA_HEAD_DIM ** -0.5), NEG_INF)
        sk = sink_ref[hq]
        m = jnp.maximum(jnp.maximum(jnp.max(s1, axis=-1, keepdims=True),
                                    jnp.max(s2, axis=-1, keepdims=True)), sk)
        p1 = jnp.exp(s1 - m)
        p2 = jnp.exp(s2 - m)
        den = jnp.sum(p1, axis=-1, keepdims=True) + jnp.sum(p2, axis=-1, keepdims=True) + jnp.exp(sk - m)
        outs.append((_dot_nt(p1.astype(BF16), vct[g]) + _dot(p2.astype(BF16), vn[:, ds_])) / den)
        if hq % 2 == 1:
            yield
    return jnp.concatenate(outs, axis=1)


def _mix_sample_kernel(sink_ref, x_ref, z_ref, lar_ref, sa0_ref, sb0_ref, kc_ref, vc_ref, gna_ref, gnb_ref,
                       rep_ref, bd_ref, mc_ref, mn_ref, wb_ref, wo_ref, y_ref, sa1_ref, sb1_ref):
    row0 = pl.multiple_of(pl.program_id(0) * SB, SB)

    def rows(c0, c1):
        return _block_rows(z_ref, row0, c0, c1)

    def stacked(c0, c1):
        return jnp.concatenate(rows(c0, c1), axis=0)

    la_ret = [lar_ref[...]] * DEC_SEQ
    oa, ob, oc = _interleave([
        _rec_sample_stages(rows(C_GQ, C_GK), rows(C_GK, C_GV), rows(C_GV, C_GR), rows(C_GR, C_RQ),
                           rows(C_LA, C_ZG), sa0_ref, sa1_ref, gna_ref, rep_ref, bd_ref),
        _rec_sample_stages(rows(C_RQ, C_RK), rows(C_RK, C_RV), rows(C_RV, C_RG), rows(C_RG, C_SQ),
                           la_ret, sb0_ref, sb1_ref, gnb_ref, rep_ref, bd_ref),
        _swa_sample_stages(sink_ref, stacked(C_SQ, C_SK), stacked(C_SK, C_SV), stacked(C_SV, C_LA),
                           kc_ref, vc_ref, mc_ref, mn_ref)])
    mixed = sum(jax.nn.sigmoid(stacked(C_ZG + D_MODEL * j, C_ZG + D_MODEL * (j + 1)))
                * _dot(o.astype(BF16), wb_ref[512 * j:512 * (j + 1), :])
                for j, o in enumerate((oa, ob, oc)))
    out = _dot(mixed.astype(BF16), wo_ref[...])
    for t in range(DEC_SEQ):
        r = pl.ds(t * DEC_BATCH + row0, SB)
        y_ref[r, :] = x_ref[r, :] + out[t * SB:(t + 1) * SB, :]


def _mix_sample(x, z, sink, la_ret, state_gla, state_ret, cache_k, cache_v, layer, gna, gnb, consts, wb, wo):
    n = x.shape[0]
    rep, bd, mc, mn = consts
    whole = lambda w: pl.BlockSpec((n, w), lambda j: (0, 0), pipeline_mode=pl.Buffered(1))
    st_in = pl.BlockSpec((None, SB, GLA_HEADS, GLA_DK, GLA_DV), lambda j: (layer, j, 0, 0, 0))
    st_out = pl.BlockSpec((SB, GLA_HEADS, GLA_DK, GLA_DV), lambda j: (j, 0, 0, 0))
    cache = pl.BlockSpec((None, SB, SWA_KV_HEADS, SWA_HEAD_DIM, WINDOW), lambda j: (layer, j, 0, 0, 0))
    st_shape = jax.ShapeDtypeStruct(state_gla.shape[1:], F32)
    return pl.pallas_call(
        _mix_sample_kernel,
        grid=(DEC_BATCH // SB,),
        in_specs=[pl.BlockSpec(memory_space=pltpu.SMEM), whole(D_MODEL), whole(Z_W),
                  _const_spec(la_ret.shape), st_in, st_in, cache, cache,
                  _const_spec((1, GLA_W)), _const_spec((1, RET_W)),
                  _const_spec(rep.shape), _const_spec(bd.shape), _const_spec(mc.shape), _const_spec(mn.shape),
                  _layer_spec((GLA_W + RET_W + SWA_W, D_MODEL), layer),
                  _layer_spec((D_MODEL, D_MODEL), layer)],
        out_specs=[pl.BlockSpec((n, D_MODEL), lambda j: (0, 0)), st_out, st_out],
        out_shape=[jax.ShapeDtypeStruct((n, D_MODEL), F32), st_shape, st_shape],
        compiler_params=_params(("arbitrary",)),
        name="mix_sample",
    )(sink, x, z, la_ret, state_gla, state_ret, cache_k, cache_v, gna, gnb, rep, bd, mc, mn, wb, wo)


def _window_place_matrix():
    r = np.arange(DEC_SEQ * SB)
    c = np.arange(SB * WINDOW)
    return ((r[:, None] % SB == c[None, :] // WINDOW)
            & (c[None, :] % WINDOW == WINDOW - DEC_SEQ + r[:, None] // SB)).astype(np.float32)


def _window_kernel(kv_ref, kc_ref, vc_ref, place_ref, ko_ref, vo_ref):
    row0 = pl.multiple_of(pl.program_id(1) * SB, SB)
    new = jnp.concatenate(_block_rows(kv_ref, row0, 0, 2 * LANES), axis=0)
    hi = new.astype(BF16)
    rest = new - hi.astype(F32)
    mid = rest.astype(BF16)
    lo = (rest - mid.astype(F32)).astype(BF16)
    place = place_ref[...]
    keep = lax.broadcasted_iota(jnp.int32, (SWA_HEAD_DIM, WINDOW), 1) < WINDOW - DEC_SEQ
    for c_ref, o_ref, c0 in ((kc_ref, ko_ref, 0), (vc_ref, vo_ref, LANES)):
        for g in range(SWA_KV_HEADS):
            cols = slice(c0 + SWA_HEAD_DIM * g, c0 + SWA_HEAD_DIM * (g + 1))
            placed = (_dot_tn(hi[:, cols], place) + _dot_tn(mid[:, cols], place)
                      + _dot_tn(lo[:, cols], place))
            for b in range(SB):
                shifted = pltpu.roll(c_ref[b, g], WINDOW - DEC_SEQ, 1)
                o_ref[b, g] = jnp.where(keep, shifted, placed[:, WINDOW * b:WINDOW * (b + 1)])


def _window_update(kv_new, cache_kt, cache_vt, place):
    blk = pl.BlockSpec((None, SB, SWA_KV_HEADS, SWA_HEAD_DIM, WINDOW), lambda l, j: (l, j, 0, 0, 0))
    return pl.pallas_call(
        _window_kernel,
        grid=(DEPTH, DEC_BATCH // SB),
        in_specs=[pl.BlockSpec((None,) + kv_new.shape[1:], lambda l, j: (l, 0, 0)), blk, blk,
                  _const_spec(place.shape)],
        out_specs=[blk, blk],
        out_shape=[jax.ShapeDtypeStruct(cache_kt.shape, F32), jax.ShapeDtypeStruct(cache_vt.shape, F32)],
        compiler_params=_params(("arbitrary", "arbitrary")),
        name="window_update",
    )(kv_new, cache_kt, cache_vt, place)


MXU_N = 256
FF_CHUNKS = ((0, 6 * MXU_N), (6 * MXU_N, D_FF))


def _ffn_tail(x1, pe_ref, gp_ref, wpg_ref, wpp_ref, gf_ref, y_ref, final):
    hp = _rms(x1, gp_ref[...]).astype(BF16)
    x2 = x1 + jax.nn.sigmoid(_dot(hp, wpg_ref[...])) * _dot(pe_ref[...].astype(BF16), wpp_ref[...])
    y_ref[...] = _rms(x2, gf_ref[...]) if final else x2


def _ffn_stages(x, carry, first, gn_ref, wi_ref, cw_ref, cb_ref, wd_ref, keep):
    tm = x.shape[0]
    hb = _rms(x, gn_ref[...]).astype(BF16)
    row = lax.broadcasted_iota(jnp.int32, (tm, 1), 0)
    acc = x
    for c0, c1 in FF_CHUNKS:
        cs = slice(c0, c1)
        a = _dot(hb, wi_ref[:, c0:c1])
        bb = _dot(hb, wi_ref[:, D_FF + c0:D_FF + c1])
        yield
        p0 = carry[6:7, cs] * first
        p1 = carry[7:8, cs] * first
        a1 = jnp.where(row == 0, p1, pltpu.roll(a, 1, 0))
        a2 = jnp.where(row == 0, p0, jnp.where(row == 1, p1, pltpu.roll(a, 2, 0)))
        conv = cb_ref[:, cs] + cw_ref[0:1, cs] * a2 + cw_ref[1:2, cs] * a1 + cw_ref[2:3, cs] * a
        act = (jax.nn.gelu(conv) * bb).astype(BF16)
        keep.append(a[tm - 8:tm, :])
        yield
        acc = acc + _dot(act, wd_ref[c0:c1, :])
        yield
    return acc


def _ffn_prompt_kernel(x_ref, pe_ref, gn_ref, wi_ref, cw_ref, cb_ref, wd_ref, gp_ref, wpg_ref, wpp_ref,
                       gf_ref, y_ref, tail_ref, carry_scr, *, final, nt):
    i = pl.program_id(1)

    @pl.when(i == 0)
    def _():
        carry_scr[...] = jnp.zeros_like(carry_scr)

    keep = []
    (acc,) = _interleave([_ffn_stages(x_ref[...], carry_scr[...], 1.0, gn_ref, wi_ref, cw_ref, cb_ref, wd_ref, keep)])
    _ffn_tail(acc, pe_ref, gp_ref, wpg_ref, wpp_ref, gf_ref, y_ref, final)
    last_rows = jnp.concatenate(keep, axis=1)
    tail_ref[...] = last_rows

    @pl.when(i < nt - 1)
    def _():
        carry_scr[...] = last_rows


def _ffn_sample_kernel(x_ref, pe_ref, st_ref, gn_ref, wi_ref, cw_ref, cb_ref, wd_ref, gp_ref, wpg_ref,
                       wpp_ref, gf_ref, y_ref, tail_ref, *, final):
    ns = DEC_BATCH
    x = x_ref[...]
    hb = _rms(x, gn_ref[...]).astype(BF16)
    acc = x
    for c0, c1 in FF_CHUNKS:
        cs = slice(c0, c1)
        a = _dot(hb, wi_ref[:, c0:c1])
        bb = _dot(hb, wi_ref[:, D_FF + c0:D_FF + c1])
        st0 = st_ref[0, :, cs]
        st1 = st_ref[1, :, cs]
        a1 = jnp.concatenate([st1, a[0:3 * ns]], axis=0)
        a2 = jnp.concatenate([st0, st1, a[0:2 * ns]], axis=0)
        conv = cb_ref[:, cs] + cw_ref[0:1, cs] * a2 + cw_ref[1:2, cs] * a1 + cw_ref[2:3, cs] * a
        act = (jax.nn.gelu(conv) * bb).astype(BF16)
        acc = acc + _dot(act, wd_ref[c0:c1, :])
        tail_ref[:, cs] = a[2 * ns:4 * ns, :]
    _ffn_tail(acc, pe_ref, gp_ref, wpg_ref, wpp_ref, gf_ref, y_ref, final)


def _ffn_weight_specs(layer):
    return [_layer_spec((1, D_MODEL), layer),
            _layer_spec((D_MODEL, 2 * D_FF), layer),
            _layer_spec((CONV_W, D_FF), layer),
            _layer_spec((1, D_FF), layer),
            _layer_spec((D_FF, D_MODEL), layer),
            _layer_spec((1, D_MODEL), layer),
            _layer_spec((D_MODEL, D_MODEL), layer),
            _layer_spec((D_PLE, D_MODEL), layer),
            _const_spec((1, D_MODEL))]


def _ffn_prompt(x, pe, layer, weights, nb, t, tm, final):
    nt = t // tm
    tok = lambda w: pl.BlockSpec((tm, w), lambda b, i: (b * nt + i, 0))
    return pl.pallas_call(
        functools.partial(_ffn_prompt_kernel, final=final, nt=nt),
        grid=(nb, nt),
        in_specs=[tok(D_MODEL),
                  pl.BlockSpec((None, tm, D_PLE), lambda b, i: (layer, b * nt + i, 0))]
        + _ffn_weight_specs(layer),
        out_specs=[tok(D_MODEL), pl.BlockSpec((8, D_FF), lambda b, i: (b, 0))],
        out_shape=[jax.ShapeDtypeStruct((nb * t, D_MODEL), F32),
                   jax.ShapeDtypeStruct((nb * 8, D_FF), F32)],
        scratch_shapes=[pltpu.VMEM((8, D_FF), F32)],
        compiler_params=_params(("arbitrary", "arbitrary")),
        name="ffn_prompt",
    )(x, pe, *weights)


def _ffn_sample(x, pe, st, layer, weights, final):
    n = x.shape[0]
    return pl.pallas_call(
        functools.partial(_ffn_sample_kernel, final=final),
        grid=(1,),
        in_specs=[pl.BlockSpec((n, D_MODEL), lambda i: (0, 0)),
                  pl.BlockSpec((None, n, D_PLE), lambda i: (layer, 0, 0)),
                  pl.BlockSpec((None, CONV_W - 1, DEC_BATCH, D_FF), lambda i: (layer, 0, 0, 0))]
        + _ffn_weight_specs(layer),
        out_specs=[pl.BlockSpec((n, D_MODEL), lambda i: (0, 0)),
                   pl.BlockSpec((n // 2, D_FF), lambda i: (0, 0))],
        out_shape=[jax.ShapeDtypeStruct((n, D_MODEL), F32),
                   jax.ShapeDtypeStruct((n // 2, D_FF), F32)],
        compiler_params=_params(("arbitrary",)),
        name="ffn_sample",
    )(x, pe, st, *weights)


def _rope_tables(pos, inv_freq):
    half = inv_freq.shape[0]
    ang = pos.astype(np.float64)[:, None] * inv_freq[None, :]
    c, s = np.cos(ang), np.sin(ang)
    rest = SWA_HEAD_DIM - 2 * half
    cos64 = np.concatenate([c, c, np.ones((pos.shape[0], rest))], axis=1)
    sin64 = np.concatenate([-s, s, np.zeros((pos.shape[0], rest))], axis=1)
    return (jnp.asarray(np.concatenate([cos64, cos64], axis=1), F32),
            jnp.asarray(np.concatenate([sin64, sin64], axis=1), F32))


def _pack_w_in(w_in):
    ga0 = sum(IN_SIZES[:4])
    ga = jnp.pad(w_in[:, :, ga0:ga0 + GLA_RANK], ((0, 0), (0, 0), (0, LANES - GLA_RANK)))
    return jnp.concatenate([w_in[:, :, :ga0], w_in[:, :, ga0 + GLA_RANK:], ga], axis=2).astype(BF16)


def kernel(x_prompt, x_sample, state_gla, state_ret, cache_swa_k, cache_swa_v, state_conv, p_prompt,
           p_sample, norm_mix, w_in, w_gla_a, b_gla_a, gla_norm, ret_norm, swa_sink, w_branch, w_out,
           norm_ffn, w_ffn_in, conv_w, conv_b, w_ffn_out, norm_ple, w_ple_gate, w_ple_proj, norm_final):
    nb, t, _ = x_prompt.shape
    ns, ts, _ = x_sample.shape
    n_s = ns * ts

    w_main = _pack_w_in(w_in)
    wa = jnp.pad(w_gla_a, ((0, 0), (0, LANES - GLA_RANK), (0, 0))).astype(BF16)
    ba = b_gla_a[:, None, :]
    wb = w_branch.astype(BF16)
    wo = w_out.astype(BF16)
    ffn_weights = (norm_ffn[:, None, :], w_ffn_in.astype(BF16), conv_w, conv_b[:, None, :],
                   w_ffn_out.astype(BF16), norm_ple[:, None, :], w_ple_gate.astype(BF16),
                   w_ple_proj.astype(BF16), norm_final[None, :])
    g_mix = norm_mix[:, None, :]
    gn_gla = jnp.tile(gla_norm, (1, GLA_HEADS))[:, None, :]
    gn_ret = ret_norm.reshape(DEPTH, 1, RET_W)

    ret_freq = 1.0 / (RET_THETA ** np.linspace(0.0, 1.0, RET_DK // 2))
    swa_freq = 1.0 / (ROPE_THETA ** (np.arange(0, ROPE_DIM, 2, dtype=np.float64) / ROPE_DIM))
    pos_p = np.arange(t)
    pos_s = PAST_LEN + np.arange(n_s) // ns
    tabs_p = _rope_tables(pos_p, ret_freq) + _rope_tables(pos_p, swa_freq)
    tabs_s = _rope_tables(pos_s, ret_freq) + _rope_tables(pos_s, swa_freq)

    gm_np, mk_np = _gla_constants()
    d_np, qdec_np, kdec_np, adec_np = _ret_constants()
    hk_np, hv_np, hs_np = _head_masks()
    mix_consts = (jnp.asarray(gm_np, BF16), jnp.asarray(mk_np), jnp.asarray(d_np), jnp.asarray(qdec_np),
                  jnp.asarray(kdec_np), jnp.asarray(adec_np), jnp.asarray(hk_np, BF16), jnp.asarray(hv_np, BF16),
                  jnp.asarray(hs_np))
    log_gamma = np.log1p(-np.exp2(-5.0 - np.arange(RET_HEADS, dtype=np.float64)))
    la_ret = jnp.asarray(np.broadcast_to(np.repeat(log_gamma, RET_DK)[None, :], (SB, RET_HEADS * RET_DK)), F32)
    mc_np, mn_np = _swa_sample_masks()
    rep_np, bd_np = _rec_sample_constants()
    sample_consts = (jnp.asarray(rep_np, BF16), jnp.asarray(bd_np), jnp.asarray(mc_np), jnp.asarray(mn_np))

    xp = x_prompt.reshape(nb * t, D_MODEL)
    xs = x_sample.transpose(1, 0, 2).reshape(n_s, D_MODEL)
    pe_p = p_prompt.reshape(DEPTH, nb * t, D_PLE)
    pe_s = p_sample.transpose(0, 2, 1, 3).reshape(DEPTH, n_s, D_PLE)
    conv_st = state_conv.transpose(0, 2, 1, 3)
    ck = cache_swa_k.transpose(0, 1, 3, 4, 2)
    cv = cache_swa_v.transpose(0, 1, 3, 4, 2)

    outs = {k_: [] for k_ in ("gla_p", "gla_s", "ret_p", "ret_s", "kv_p", "kv_s", "conv_p", "conv_s")}
    for l in range(DEPTH):
        final = l == DEPTH - 1
        sink = swa_sink[l]

        xp3, st_a, st_b, kv_tail = _mix_prompt(xp.reshape(nb, t, D_MODEL), sink, l, g_mix, w_main, wa, ba, tabs_p,
                                               mix_consts, gn_gla[l], gn_ret[l], wb, wo)
        xp, tail = _ffn_prompt(xp3.reshape(nb * t, D_MODEL), pe_p, l, ffn_weights, nb, t, 256, final)
        outs["gla_p"].append(st_a)
        outs["ret_p"].append(st_b)
        outs["kv_p"].append(kv_tail)
        outs["conv_p"].append(tail.reshape(nb, 8, D_FF)[:, 8 - (CONV_W - 1):])

        zs = _in_proj(xs, l, g_mix, w_main, wa, ba, tabs_s, 256)
        xs, st_a, st_b = _mix_sample(xs, zs, sink, la_ret, state_gla, state_ret, ck, cv, l, gn_gla[l], gn_ret[l],
                                     sample_consts, wb, wo)
        xs, tail_s = _ffn_sample(xs, pe_s, conv_st, l, ffn_weights, final)
        outs["gla_s"].append(st_a)
        outs["ret_s"].append(st_b)
        outs["kv_s"].append(zs[:, C_SK:C_LA])
        outs["conv_s"].append(tail_s.reshape(CONV_W - 1, ns, D_FF).transpose(1, 0, 2))

    y_prompt = xp.reshape(nb, t, D_MODEL)
    y_sample = xs.reshape(ts, ns, D_MODEL).transpose(1, 0, 2)
    st = {k_: jnp.stack(v_) for k_, v_ in outs.items()}

    def kv_heads(a):
        return a.reshape(a.shape[:-1] + (SWA_KV_HEADS, SWA_HEAD_DIM))

    k_p = kv_heads(st["kv_p"][..., :LANES])
    v_p = kv_heads(st["kv_p"][..., LANES:])
    k_st, v_st = _window_update(st["kv_s"], ck, cv, jnp.asarray(_window_place_matrix(), BF16))
    k_s = k_st.transpose(0, 1, 4, 2, 3)
    v_s = v_st.transpose(0, 1, 4, 2, 3)
    return (y_prompt, y_sample, st["gla_p"], st["gla_s"], st["ret_p"], st["ret_s"],
            k_p, k_s, v_p, v_s, st["conv_p"], st["conv_s"])
```

```python
import functools
import math

import numpy as np
import jax
import jax.numpy as jnp
from jax import lax
from jax.experimental import pallas as pl
from jax.experimental.pallas import tpu as pltpu

F32 = jnp.float32
BF16 = jnp.bfloat16

D_MODEL = 1024
BATCH = 2
SEQ = 8192
DEPTH = 4
DEC_BATCH = 128
DEC_SEQ = 4
PAST_LEN = 8192
D_PLE = 256
GLA_HEADS = 4
GLA_DK = 64
GLA_DV = 128
GLA_RANK = 16
GLA_TAU = 16.0
RET_HEADS = 4
RET_DK = 64
RET_DV = 128
RET_THETA = 10000.0
SWA_Q_HEADS = 8
SWA_KV_HEADS = 2
SWA_HEAD_DIM = 64
SWA_GROUP = SWA_Q_HEADS // SWA_KV_HEADS
WINDOW = 128
ROPE_THETA = 500000.0
ROPE_DIM = SWA_HEAD_DIM // 4
D_FF = 2816
CONV_W = 3
N_BRANCH = 3
EPS = 1e-6
NEG_INF = -1e30

GLA_W = GLA_HEADS * GLA_DV
RET_W = RET_HEADS * RET_DV
SWA_W = SWA_Q_HEADS * SWA_HEAD_DIM
IN_SIZES = (GLA_HEADS * GLA_DK, GLA_HEADS * GLA_DK, GLA_W, GLA_W, GLA_RANK,
            RET_HEADS * RET_DK, RET_HEADS * RET_DK, RET_W, RET_W,
            SWA_W, SWA_KV_HEADS * SWA_HEAD_DIM, SWA_KV_HEADS * SWA_HEAD_DIM,
            N_BRANCH * D_MODEL)

LANES = 128
VMEM_LIMIT = 52 * 1024 * 1024

C_GQ = 0
C_GK = 256
C_GV = 512
C_GR = 1024
C_RQ = 1536
C_RK = 1792
C_RV = 2048
C_RG = 2560
C_SQ = 3072
C_SK = 3584
C_SV = 3712
C_LA = 3840
C_ZG = 4096
Z_W = 7168
W_ZG = 3840
W_GA = 6912
W_MAIN = 7040

TT = 128
N_LEVELS = 7
MXU_LEVELS = 2
SB = 16


def _dot(a, b):
    return jnp.dot(a, b, preferred_element_type=F32)


def _dot_nt(a, b):
    return lax.dot_general(a, b, (((1,), (1,)), ((), ())), preferred_element_type=F32)


def _dot_tn(a, b):
    return lax.dot_general(a, b, (((0,), (0,)), ((), ())), preferred_element_type=F32)


def _rms(x, g):
    return x * lax.rsqrt(jnp.mean(x * x, axis=-1, keepdims=True) + EPS) * g


def _layer_spec(shape, layer):
    nd = len(shape)
    return pl.BlockSpec((None,) + tuple(shape), lambda *_: (layer,) + (0,) * nd,
                        pipeline_mode=pl.Buffered(1))


def _const_spec(shape):
    nd = len(shape)
    return pl.BlockSpec(tuple(shape), lambda *_: (0,) * nd, pipeline_mode=pl.Buffered(1))


def _params(sem):
    return pltpu.CompilerParams(dimension_semantics=sem, vmem_limit_bytes=VMEM_LIMIT)


def _rope_block(x, cos, sin_s, first):
    half_mask, half = first
    xr = jnp.where(half_mask, pltpu.roll(x, LANES - half, 1), pltpu.roll(x, half, 1))
    return x * cos + xr * sin_s


def _gate_logits(hb, w_ref, j):
    return _dot(hb, w_ref[:, W_ZG + D_MODEL * j:W_ZG + D_MODEL * (j + 1)])


def _in_stages(x, g_ref, w_ref, wa_ref, ba_ref, tabs, z_ref, with_gates):
    hb = _rms(x, g_ref[...]).astype(BF16)

    def mm(c0, c1):
        return _dot(hb, w_ref[:, c0:c1])

    lane = lax.broadcasted_iota(jnp.int32, (1, LANES), 1) % 64
    ret_first = (lane < RET_DK // 2, RET_DK // 2)
    swa_first = (lane < ROPE_DIM // 2, ROPE_DIM // 2)
    rc, rs, sc, ss = tabs

    ga = mm(W_GA, W_MAIN)
    xa = _dot(ga.astype(BF16), wa_ref[...]) + ba_ref[...]
    log_sig = jnp.minimum(xa, 0.0) - jnp.log1p(jnp.exp(-jnp.abs(xa)))
    z_ref[:, C_LA:C_ZG] = log_sig * (1.0 / GLA_TAU)
    z_ref[:, C_GQ:C_GK] = mm(C_GQ, C_GK) * (GLA_DK ** -0.5)
    yield
    z_ref[:, C_GK:C_RQ] = mm(C_GK, C_RQ)
    yield
    rq = mm(C_RQ, C_RK)
    rk = mm(C_RK, C_RV)
    for j in range(2):
        sl = slice(LANES * j, LANES * (j + 1))
        z_ref[:, C_RQ + LANES * j:C_RQ + LANES * (j + 1)] = _rope_block(rq[:, sl], rc, rs, ret_first)
        z_ref[:, C_RK + LANES * j:C_RK + LANES * (j + 1)] = (
            _rope_block(rk[:, sl], rc, rs, ret_first) * (RET_DK ** -0.5))
    yield
    z_ref[:, C_RV:C_SQ] = mm(C_RV, C_SQ)
    yield
    sqkv = mm(C_SQ, C_LA)
    for j in range(5):
        sl = slice(LANES * j, LANES * (j + 1))
        z_ref[:, C_SQ + LANES * j:C_SQ + LANES * (j + 1)] = _rope_block(sqkv[:, sl], sc, ss, swa_first)
    z_ref[:, C_SV:C_LA] = sqkv[:, C_SV - C_SQ:]
    yield
    if with_gates:
        for j in range(N_BRANCH):
            z_ref[:, C_ZG + D_MODEL * j:C_ZG + D_MODEL * (j + 1)] = _gate_logits(hb, w_ref, j)
            yield


def _in_kernel(x_ref, g_ref, w_ref, wa_ref, ba_ref, rc_ref, rs_ref, sc_ref, ss_ref, z_ref):
    tabs = (rc_ref[...], rs_ref[...], sc_ref[...], ss_ref[...])
    for _ in _in_stages(x_ref[...], g_ref, w_ref, wa_ref, ba_ref, tabs, z_ref, True):
        pass


def _in_proj(x, layer, g_mix, w_main, wa, ba, tabs, tm):
    n = x.shape[0]
    rc, rs, sc, ss = tabs
    nt = rc.shape[0] // tm
    tab = pl.BlockSpec((tm, LANES), lambda i: (i % nt, 0))
    return pl.pallas_call(
        _in_kernel,
        grid=(n // tm,),
        in_specs=[pl.BlockSpec((tm, D_MODEL), lambda i: (i, 0)),
                  _layer_spec((1, D_MODEL), layer),
                  _layer_spec((D_MODEL, W_MAIN), layer),
                  _layer_spec((LANES, 256), layer),
                  _layer_spec((1, 256), layer),
                  tab, tab, tab, tab],
        out_specs=pl.BlockSpec((tm, Z_W), lambda i: (i, 0)),
        out_shape=jax.ShapeDtypeStruct((n, Z_W), F32),
        compiler_params=_params(("arbitrary",)),
        name="in_proj",
    )(x, g_mix, w_main, wa, ba, rc, rs, sc, ss)


def _gla_constants():
    t = np.arange(TT)
    g = np.zeros((1 + MXU_LEVELS, TT, TT), np.float32)
    g[0] = (t[None, :] <= t[:, None])
    m = np.zeros((1 + N_LEVELS, TT, TT), np.float32)
    m[0] = np.eye(TT)
    for lv in range(1, N_LEVELS + 1):
        bs, hf = 2 ** lv, 2 ** (lv - 1)
        bd = (t // bs) * bs + hf - 1
        upper = (t % bs) >= hf
        u = t[None, :]
        if lv <= MXU_LEVELS:
            g[lv] = np.where(upper[:, None], (u > bd[:, None]) & (u <= t[:, None]),
                             (u > t[:, None]) & (u <= bd[:, None]))
        same = (t[:, None] // bs) == (t[None, :] // bs)
        m[lv] = same & upper[:, None] & (~upper)[None, :]
    return g.reshape(-1, TT), np.tile(m, (1, 1, GLA_HEADS))


def _head_masks():
    hk = np.arange(GLA_HEADS * TT)[:, None] // TT == np.arange(GLA_HEADS * GLA_DK)[None, :] // GLA_DK
    hv = np.arange(GLA_HEADS * TT)[:, None] // TT == np.arange(GLA_W)[None, :] // GLA_DV
    hs = np.arange(GLA_HEADS * GLA_DK)[:, None] // GLA_DK == np.arange(GLA_W)[None, :] // GLA_DV
    return hk.astype(np.float32), hv.astype(np.float32), hs.astype(np.float32)


def _heads_blockdiag(x, mask):
    return jnp.concatenate([x] * GLA_HEADS, axis=0) * mask


def _col_vector(row):
    n = row.shape[1]
    eye = lax.broadcasted_iota(jnp.int32, (n, n), 0) == lax.broadcasted_iota(jnp.int32, (n, n), 1)
    return jnp.sum(jnp.where(eye, row, 0.0), axis=1, keepdims=True)


def _norm_gate(o, gn, gate):
    outs = [_rms(o[:, GLA_DV * h:GLA_DV * (h + 1)], gn[:, GLA_DV * h:GLA_DV * (h + 1)]) for h in range(GLA_HEADS)]
    return jnp.concatenate(outs, axis=1) * (gate * jax.nn.sigmoid(gate))


def _gla_tile(z_ref, gm_ref, mk_ref, hk_ref, hv_ref, hs_ref, gn_ref, st_ref):
    la = z_ref[:, C_LA:C_ZG]
    la_hi = la.astype(BF16)
    la_r = la - la_hi.astype(F32)
    la_mid = la_r.astype(BF16)
    la_lo = (la_r - la_mid.astype(F32)).astype(BF16)
    gm = gm_ref[...]
    sums = _dot(gm, la_hi) + _dot(gm, la_mid) + _dot(gm, la_lo)
    b = sums[0:TT]
    e_b = jnp.exp(b)
    a_row = e_b[TT - 1:TT, :]
    e_lv = [jnp.exp(sums[lv * TT:(lv + 1) * TT]) for lv in range(1, MXU_LEVELS + 1)]
    for lv in range(MXU_LEVELS + 1, N_LEVELS + 1):
        bs = 2 ** lv
        bound = jnp.concatenate([jnp.broadcast_to(b[r:r + 1, :], (bs, b.shape[1]))
                                 for r in range(bs // 2 - 1, TT, bs)], axis=0)
        e_lv.append(jnp.exp(-jnp.abs(b - bound)))
    q = z_ref[:, C_GQ:C_GK]
    k = z_ref[:, C_GK:C_GV]
    qd = (q * e_b).astype(BF16)
    kd = (k * jnp.exp(b[TT - 1:TT, :] - b)).astype(BF16)
    qb = q.astype(BF16)
    kb = k.astype(BF16)
    ql = [(q * e).astype(BF16) for e in e_lv]
    kl = [(k * e).astype(BF16) for e in e_lv]
    hk = hk_ref[...]
    yield
    a = mk_ref[0] * _dot_nt(qb, _heads_blockdiag(kb, hk))
    for lv in range(N_LEVELS):
        yield
        a = a + mk_ref[lv + 1] * _dot_nt(ql[lv], _heads_blockdiag(kl[lv], hk))
    yield
    v = z_ref[:, C_GV:C_GR].astype(BF16)
    s = st_ref[...]
    o = _dot(a.astype(BF16), _heads_blockdiag(v, hv_ref[...])) + _dot(qd, s.astype(BF16))
    yield
    st_ref[...] = _col_vector(a_row) * s + hs_ref[...] * _dot_tn(kd, v)
    yield
    return _norm_gate(o, gn_ref[...], z_ref[:, C_GR:C_RQ])


def _ret_gammas():
    return [1.0 - 2.0 ** (-5.0 - h) for h in range(RET_HEADS)]


def _ret_constants():
    t = np.arange(TT, dtype=np.float64)
    d = np.zeros((RET_HEADS, TT, TT), np.float64)
    qdec = np.zeros((TT, RET_HEADS * RET_DK), np.float64)
    kdec = np.zeros((TT, RET_HEADS * RET_DK), np.float64)
    adec = np.zeros((RET_HEADS * RET_DK, LANES), np.float64)
    for h, gam in enumerate(_ret_gammas()):
        diff = t[:, None] - t[None, :]
        d[h] = np.where(diff >= 0, gam ** np.maximum(diff, 0.0), 0.0)
        qdec[:, h * RET_DK:(h + 1) * RET_DK] = (gam ** (t + 1.0))[:, None]
        kdec[:, h * RET_DK:(h + 1) * RET_DK] = (gam ** (TT - 1.0 - t))[:, None]
        adec[h * RET_DK:(h + 1) * RET_DK, :] = gam ** TT
    d = np.concatenate(list(d), axis=1)
    adec = np.tile(adec, (1, RET_W // LANES))
    return d.astype(np.float32), qdec.astype(np.float32), kdec.astype(np.float32), adec.astype(np.float32)


def _ret_tile(z_ref, d_ref, qdec_ref, kdec_ref, adec_ref, hk_ref, hv_ref, hs_ref, gn_ref, st_ref):
    q = z_ref[:, C_RQ:C_RK]
    k = z_ref[:, C_RK:C_RV]
    qd = (q * qdec_ref[...]).astype(BF16)
    kd = (k * kdec_ref[...]).astype(BF16)
    a = d_ref[...] * _dot_nt(q.astype(BF16), _heads_blockdiag(k.astype(BF16), hk_ref[...]))
    yield
    v = z_ref[:, C_RV:C_RG].astype(BF16)
    s = st_ref[...]
    o = _dot(a.astype(BF16), _heads_blockdiag(v, hv_ref[...])) + _dot(qd, s.astype(BF16))
    yield
    st_ref[...] = adec_ref[...] * s + hs_ref[...] * _dot_tn(kd, v)
    yield
    return _norm_gate(o, gn_ref[...], z_ref[:, C_RG:C_SQ])


def _rec_sample_constants():
    c = np.arange(SB * GLA_DK)
    rep = np.arange(GLA_DK)[:, None] == (c[None, :] % GLA_DK)
    bd = (np.arange(DEC_SEQ * SB)[:, None] % SB) == (c[None, :] // GLA_DK)
    return rep.astype(np.float32), bd.astype(np.float32)


def _block_rows(ref, row0, c0, c1):
    return [ref[pl.ds(t * DEC_BATCH + row0, SB), c0:c1] for t in range(DEC_SEQ)]


def _rec_sample_stages(q, k, v, gate, la, s0_ref, s1_ref, gn_ref, rep_ref, bd_ref):
    b = []
    for t in range(DEC_SEQ):
        b.append(la[t] if t == 0 else b[-1] + la[t])
    qd = jnp.concatenate([q[t] * jnp.exp(b[t]) for t in range(DEC_SEQ)], axis=0).astype(BF16)
    kd = jnp.concatenate([k[t] * jnp.exp(b[-1] - b[t]) for t in range(DEC_SEQ)], axis=0).astype(BF16)
    vv = jnp.concatenate(v, axis=0).astype(BF16)
    a = jnp.exp(b[-1])
    a_hi = a.astype(BF16)
    a_r = a - a_hi.astype(F32)
    a_mid = a_r.astype(BF16)
    a_lo = (a_r - a_mid.astype(F32)).astype(BF16)
    rep = rep_ref[...]
    bd = bd_ref[...]
    ones = jnp.ones((SB, GLA_DV), BF16)

    def expand(x, mask):
        return (_dot(x, rep) * mask).astype(BF16)

    yield
    oi = []
    for h in range(GLA_HEADS):
        hs = slice(GLA_DK * h, GLA_DK * (h + 1))
        vs = slice(GLA_DV * h, GLA_DV * (h + 1))
        s0 = s0_ref[:, h].reshape(SB * GLA_DK, GLA_DV)
        oi.append(_dot(expand(qd[:, hs], bd), s0.astype(BF16)))
        a_col = (_dot_tn(expand(a_hi[:, hs], bd[0:SB]), ones) + _dot_tn(expand(a_mid[:, hs], bd[0:SB]), ones)
                 + _dot_tn(expand(a_lo[:, hs], bd[0:SB]), ones))
        s1 = a_col * s0 + _dot_tn(expand(kd[:, hs], bd), vv[:, vs])
        s1_ref[:, h] = s1.reshape(SB, GLA_DK, GLA_DV)
        yield
    oi = jnp.concatenate(oi, axis=1)

    gn = gn_ref[...]
    outs = []
    for t in range(DEC_SEQ):
        o = oi[t * SB:(t + 1) * SB, :]
        for s in range(t + 1):
            p = q[t] * k[s] * jnp.exp(b[t] - b[s])
            parts = []
            for h in range(GLA_HEADS):
                hs = slice(GLA_DK * h, GLA_DK * (h + 1))
                vs = slice(GLA_DV * h, GLA_DV * (h + 1))
                parts.append(jnp.sum(p[:, hs], axis=1, keepdims=True) * v[s][:, vs])
            o = o + jnp.concatenate(parts, axis=1)
        outs.append(_norm_gate(o, gn, gate[t]))
        yield
    return jnp.concatenate(outs, axis=0)


def _swa_tile(z_ref, sink_ref, kp_ref, vp_ref, first_col):
    kc = z_ref[:, C_SK:C_SV].astype(BF16)
    vc = z_ref[:, C_SV:C_LA].astype(BF16)
    kband = jnp.concatenate([kp_ref[...], kc], axis=0)
    vband = jnp.concatenate([vp_ref[...], vc], axis=0)
    rows = SWA_GROUP * WINDOW
    r = lax.broadcasted_iota(jnp.int32, (rows, 2 * WINDOW), 0) & (WINDOW - 1)
    c = lax.broadcasted_iota(jnp.int32, (rows, 2 * WINDOW), 1)
    valid = (c > r) & (c <= r + WINDOW) & (c >= first_col)
    groups = range(SWA_KV_HEADS)
    dsl = [slice(SWA_HEAD_DIM * g, SWA_HEAD_DIM * (g + 1)) for g in groups]
    s, sk = [], []
    for g in groups:
        heads = range(SWA_GROUP * g, SWA_GROUP * (g + 1))
        q = jnp.concatenate(
            [z_ref[:, C_SQ + SWA_HEAD_DIM * hq:C_SQ + SWA_HEAD_DIM * (hq + 1)] for hq in heads], axis=0)
        sk.append(jnp.concatenate([jnp.full((WINDOW, 1), sink_ref[hq], F32) for hq in heads], axis=0))
        s.append(_dot_nt(q.astype(BF16), kband[:, dsl[g]]) * (SWA_HEAD_DIM ** -0.5))
    yield
    m, p = [], []
    for g in groups:
        sg = jnp.where(valid, s[g], NEG_INF)
        m.append(jnp.maximum(jnp.max(sg, axis=-1, keepdims=True), sk[g]))
        p.append(jnp.exp(sg - m[g]))
    yield
    outs = []
    for g in groups:
        den = jnp.sum(p[g], axis=-1, keepdims=True) + jnp.exp(sk[g] - m[g])
        o = _dot(p[g].astype(BF16), vband[:, dsl[g]]) / den
        outs += [o[WINDOW * j:WINDOW * (j + 1)] for j in range(SWA_GROUP)]
    yield
    return jnp.concatenate(outs, axis=1), kc, vc


def _interleave(stage_fns):
    results = [None] * len(stage_fns)
    live = list(enumerate(stage_fns))
    while live:
        still = []
        for idx, gen in live:
            try:
                next(gen)
                still.append((idx, gen))
            except StopIteration as stop:
                results[idx] = stop.value
        live = still
    return results


def _merge_stages(pend_ref, x_ref, g_ref, w_ref, wb_ref, wo_ref, y_ref):
    nb = x_ref.shape[0]
    hb = _rms(x_ref[...].reshape(nb * TT, D_MODEL), g_ref[...]).astype(BF16)
    gates = []
    for j in range(N_BRANCH):
        gates.append(jax.nn.sigmoid(_gate_logits(hb, w_ref, j)))
        yield
    mixed = None
    for j in range(N_BRANCH):
        term = gates[j] * _dot(pend_ref[:, GLA_W * j:GLA_W * (j + 1)], wb_ref[GLA_W * j:GLA_W * (j + 1), :])
        mixed = term if mixed is None else mixed + term
        yield
    out = _dot(mixed.astype(BF16), wo_ref[...])
    for b in range(x_ref.shape[0]):
        y_ref[b] = x_ref[b] + out[TT * b:TT * (b + 1)]
    yield


def _mix_kernel(sink_ref, xp_ref, xn_ref, tp0, tp1, tp2, tp3, tn0, tn1, tn2, tn3, g_ref, w_ref, wa_ref, ba_ref,
                gm_ref, mk_ref, d_ref, qdec_ref, kdec_ref, adec_ref, hk_ref, hv_ref, hs_ref, gna_ref, gnb_ref,
                wb_ref, wo_ref, y_ref, sta_ref, stb_ref, kv_ref, z_scr, pend_scr, sa_scr, sb_scr, kp_scr, vp_scr,
                *, nt):
    i = pl.program_id(0)
    nb = xp_ref.shape[0]

    def rows_of(ref3):
        return ref3[...].reshape(nb * TT, ref3.shape[2])

    def tiled(tab_refs):
        return tuple(jnp.concatenate([r[...]] * nb, axis=0) for r in tab_refs)

    @pl.when(i == 0)
    def _():
        sa_scr[...] = jnp.zeros_like(sa_scr)
        sb_scr[...] = jnp.zeros_like(sb_scr)
        kp_scr[...] = jnp.zeros_like(kp_scr)
        vp_scr[...] = jnp.zeros_like(vp_scr)
        pend_scr[...] = jnp.zeros_like(pend_scr)
        for _ in _in_stages(rows_of(xp_ref), g_ref, w_ref, wa_ref, ba_ref, tiled((tp0, tp1, tp2, tp3)),
                            z_scr.at[0], False):
            pass

    slot = i % 2
    zc = z_scr.at[slot]
    first_col = jnp.where(i > 0, 0, WINDOW)
    stages = [_merge_stages(pend_scr, xp_ref, g_ref, w_ref, wb_ref, wo_ref, y_ref),
              _in_stages(rows_of(xn_ref), g_ref, w_ref, wa_ref, ba_ref, tiled((tn0, tn1, tn2, tn3)),
                         z_scr.at[1 - slot], False)]
    for b in range(nb):
        zb = zc.at[pl.ds(TT * b, TT)]
        stages += [_gla_tile(zb, gm_ref, mk_ref, hk_ref, hv_ref, hs_ref, gna_ref, sa_scr.at[b]),
                   _swa_tile(zb, sink_ref, kp_scr.at[b], vp_scr.at[b], first_col),
                   _ret_tile(zb, d_ref, qdec_ref, kdec_ref, adec_ref, hk_ref, hv_ref, hs_ref, gnb_ref, sb_scr.at[b])]
    branch = _interleave(stages)[2:]
    swa = [branch[3 * b + 1] for b in range(nb)]
    outs = [[branch[3 * b] for b in range(nb)], [branch[3 * b + 2] for b in range(nb)], [s[0] for s in swa]]
    pend = [jnp.concatenate(o, axis=0).astype(BF16) for o in outs]

    @pl.when(i < nt)
    def _():
        for j in range(N_BRANCH):
            pend_scr[:, GLA_W * j:GLA_W * (j + 1)] = pend[j]
        for b in range(nb):
            kp_scr[b] = swa[b][1]
            vp_scr[b] = swa[b][2]
        for b in range(nb):
            for h in range(GLA_HEADS):
                sta_ref[b, h] = sa_scr[b, GLA_DK * h:GLA_DK * (h + 1), GLA_DV * h:GLA_DV * (h + 1)]
                stb_ref[b, h] = sb_scr[b, RET_DK * h:RET_DK * (h + 1), RET_DV * h:RET_DV * (h + 1)]
            kv_ref[b] = zc[pl.ds(TT * b, TT), C_SK:C_LA]


MIX_VMEM_LIMIT = 60 * 1024 * 1024


def _mix_prompt(x, sink, layer, g_mix, w_main, wa, ba, tabs, consts, gna, gnb, wb, wo):
    nb, t, _ = x.shape
    nt = t // TT
    st_spec = pl.BlockSpec((nb, GLA_HEADS, GLA_DK, GLA_DV), lambda i: (0, 0, 0, 0))
    st_shape = jax.ShapeDtypeStruct((nb, GLA_HEADS, GLA_DK, GLA_DV), F32)
    nxt = lambda i: jnp.minimum(i + 1, nt - 1)
    prv = lambda i: jnp.maximum(i - 1, 0)
    tab_prv = pl.BlockSpec((TT, LANES), lambda i: (prv(i), 0))
    tab_nxt = pl.BlockSpec((TT, LANES), lambda i: (nxt(i), 0))
    return pl.pallas_call(
        functools.partial(_mix_kernel, nt=nt),
        grid=(nt + 1,),
        in_specs=[pl.BlockSpec(memory_space=pltpu.SMEM),
                  pl.BlockSpec((nb, TT, D_MODEL), lambda i: (0, prv(i), 0)),
                  pl.BlockSpec((nb, TT, D_MODEL), lambda i: (0, nxt(i), 0))]
        + [tab_prv] * 4 + [tab_nxt] * 4
        + [_layer_spec((1, D_MODEL), layer), _layer_spec((D_MODEL, W_MAIN), layer),
           _layer_spec((LANES, 256), layer), _layer_spec((1, 256), layer)]
        + [_const_spec(c.shape) for c in consts]
        + [_const_spec((1, GLA_W)), _const_spec((1, RET_W)),
           _layer_spec((GLA_W + RET_W + SWA_W, D_MODEL), layer),
           _layer_spec((D_MODEL, D_MODEL), layer)],
        out_specs=[pl.BlockSpec((nb, TT, D_MODEL), lambda i: (0, prv(i), 0)), st_spec, st_spec,
                   pl.BlockSpec((nb, WINDOW, 2 * LANES), lambda i: (0, 0, 0))],
        out_shape=[jax.ShapeDtypeStruct((nb, t, D_MODEL), F32), st_shape, st_shape,
                   jax.ShapeDtypeStruct((nb, WINDOW, 2 * LANES), F32)],
        scratch_shapes=[pltpu.VMEM((2, nb * TT, C_ZG), F32),
                        pltpu.VMEM((nb * TT, GLA_W + RET_W + SWA_W), BF16),
                        pltpu.VMEM((nb, GLA_HEADS * GLA_DK, GLA_W), F32),
                        pltpu.VMEM((nb, RET_HEADS * RET_DK, RET_W), F32),
                        pltpu.VMEM((nb, WINDOW, 128), BF16), pltpu.VMEM((nb, WINDOW, 128), BF16)],
        compiler_params=pltpu.CompilerParams(dimension_semantics=("arbitrary",), vmem_limit_bytes=MIX_VMEM_LIMIT),
        name="mix_prompt",
    )(sink, x, x, *tabs, *tabs, g_mix, w_main, wa, ba, *consts, gna, gnb, wb, wo)


def _swa_sample_masks():
    r = np.arange(DEC_SEQ * SB)
    rt, rb = r // SB, r % SB
    c = np.arange(SB * WINDOW)
    cb, cj = c // WINDOW, c % WINDOW
    m_cache = (rb[:, None] == cb[None, :]) & (cj[None, :] > rt[:, None])
    m_new = (rb[:, None] == rb[None, :]) & (rt[None, :] <= rt[:, None])
    return m_cache.astype(np.float32), m_new.astype(np.float32)


def _swa_sample_stages(sink_ref, q, kn, vn, kc_ref, vc_ref, mc_ref, mn_ref):
    q = q.astype(BF16)
    kn = kn.astype(BF16)
    vn = vn.astype(BF16)
    kct = [jnp.concatenate([kc_ref[b, g] for b in range(SB)], axis=1).astype(BF16) for g in range(SWA_KV_HEADS)]
    vct = [jnp.concatenate([vc_ref[b, g] for b in range(SB)], axis=1).astype(BF16) for g in range(SWA_KV_HEADS)]
    ok_c = mc_ref[...] > 0.0
    ok_n = mn_ref[...] > 0.0
    outs = []
    for hq in range(SWA_Q_HEADS):
        g = hq // SWA_GROUP
        ds_ = slice(SWA_HEAD_DIM * g, SWA_HEAD_DIM * (g + 1))
        qh = q[:, SWA_HEAD_DIM * hq:SWA_HEAD_DIM * (hq + 1)]
        s1 = jnp.where(ok_c, _dot(qh, kct[g]) * (SWA_HEAD_DIM ** -0.5), NEG_INF)
        s2 = jnp.where(ok_n, _dot_nt(qh, kn[:, ds_]) * (SWA_HEAD_DIM ** -0.5), NEG_INF)
        sk = sink_ref[hq]
        m = jnp.maximum(jnp.maximum(jnp.max(s1, axis=-1, keepdims=True),
                                    jnp.max(s2, axis=-1, keepdims=True)), sk)
        p1 = jnp.exp(s1 - m)
        p2 = jnp.exp(s2 - m)
        den = jnp.sum(p1, axis=-1, keepdims=True) + jnp.sum(p2, axis=-1, keepdims=True) + jnp.exp(sk - m)
        outs.append((_dot_nt(p1.astype(BF16), vct[g]) + _dot(p2.astype(BF16), vn[:, ds_])) / den)
        if hq % 2 == 1:
            yield
    return jnp.concatenate(outs, axis=1)


def _mix_sample_kernel(sink_ref, x_ref, z_ref, lar_ref, sa0_ref, sb0_ref, kc_ref, vc_ref, gna_ref, gnb_ref,
                       rep_ref, bd_ref, mc_ref, mn_ref, wb_ref, wo_ref, y_ref, sa1_ref, sb1_ref):
    row0 = pl.multiple_of(pl.program_id(0) * SB, SB)

    def rows(c0, c1):
        return _block_rows(z_ref, row0, c0, c1)

    def stacked(c0, c1):
        return jnp.concatenate(rows(c0, c1), axis=0)

    la_ret = [lar_ref[...]] * DEC_SEQ
    oa, ob, oc = _interleave([
        _rec_sample_stages(rows(C_GQ, C_GK), rows(C_GK, C_GV), rows(C_GV, C_GR), rows(C_GR, C_RQ),
                           rows(C_LA, C_ZG), sa0_ref, sa1_ref, gna_ref, rep_ref, bd_ref),
        _rec_sample_stages(rows(C_RQ, C_RK), rows(C_RK, C_RV), rows(C_RV, C_RG), rows(C_RG, C_SQ),
                           la_ret, sb0_ref, sb1_ref, gnb_ref, rep_ref, bd_ref),
        _swa_sample_stages(sink_ref, stacked(C_SQ, C_SK), stacked(C_SK, C_SV), stacked(C_SV, C_LA),
                           kc_ref, vc_ref, mc_ref, mn_ref)])
    mixed = sum(jax.nn.sigmoid(stacked(C_ZG + D_MODEL * j, C_ZG + D_MODEL * (j + 1)))
                * _dot(o.astype(BF16), wb_ref[512 * j:512 * (j + 1), :])
                for j, o in enumerate((oa, ob, oc)))
    out = _dot(mixed.astype(BF16), wo_ref[...])
    for t in range(DEC_SEQ):
        r = pl.ds(t * DEC_BATCH + row0, SB)
        y_ref[r, :] = x_ref[r, :] + out[t * SB:(t + 1) * SB, :]


def _mix_sample(x, z, sink, la_ret, state_gla, state_ret, cache_k, cache_v, layer, gna, gnb, consts, wb, wo):
    n = x.shape[0]
    rep, bd, mc, mn = consts
    whole = lambda w: pl.BlockSpec((n, w), lambda j: (0, 0), pipeline_mode=pl.Buffered(1))
    st_in = pl.BlockSpec((None, SB, GLA_HEADS, GLA_DK, GLA_DV), lambda j: (layer, j, 0, 0, 0))
    st_out = pl.BlockSpec((SB, GLA_HEADS, GLA_DK, GLA_DV), lambda j: (j, 0, 0, 0))
    cache = pl.BlockSpec((None, SB, SWA_KV_HEADS, SWA_HEAD_DIM, WINDOW), lambda j: (layer, j, 0, 0, 0))
    st_shape = jax.ShapeDtypeStruct(state_gla.shape[1:], F32)
    return pl.pallas_call(
        _mix_sample_kernel,
        grid=(DEC_BATCH // SB,),
        in_specs=[pl.BlockSpec(memory_space=pltpu.SMEM), whole(D_MODEL), whole(Z_W),
                  _const_spec(la_ret.shape), st_in, st_in, cache, cache,
                  _const_spec((1, GLA_W)), _const_spec((1, RET_W)),
                  _const_spec(rep.shape), _const_spec(bd.shape), _const_spec(mc.shape), _const_spec(mn.shape),
                  _layer_spec((GLA_W + RET_W + SWA_W, D_MODEL), layer),
                  _layer_spec((D_MODEL, D_MODEL), layer)],
        out_specs=[pl.BlockSpec((n, D_MODEL), lambda j: (0, 0)), st_out, st_out],
        out_shape=[jax.ShapeDtypeStruct((n, D_MODEL), F32), st_shape, st_shape],
        compiler_params=_params(("arbitrary",)),
        name="mix_sample",
    )(sink, x, z, la_ret, state_gla, state_ret, cache_k, cache_v, gna, gnb, rep, bd, mc, mn, wb, wo)


def _window_place_matrix():
    r = np.arange(DEC_SEQ * SB)
    c = np.arange(SB * WINDOW)
    return ((r[:, None] % SB == c[None, :] // WINDOW)
            & (c[None, :] % WINDOW == WINDOW - DEC_SEQ + r[:, None] // SB)).astype(np.float32)


def _window_kernel(kv_ref, kc_ref, vc_ref, place_ref, ko_ref, vo_ref):
    row0 = pl.multiple_of(pl.program_id(1) * SB, SB)
    new = jnp.concatenate(_block_rows(kv_ref, row0, 0, 2 * LANES), axis=0)
    hi = new.astype(BF16)
    rest = new - hi.astype(F32)
    mid = rest.astype(BF16)
    lo = (rest - mid.astype(F32)).astype(BF16)
    place = place_ref[...]
    keep = lax.broadcasted_iota(jnp.int32, (SWA_HEAD_DIM, WINDOW), 1) < WINDOW - DEC_SEQ
    for c_ref, o_ref, c0 in ((kc_ref, ko_ref, 0), (vc_ref, vo_ref, LANES)):
        for g in range(SWA_KV_HEADS):
            cols = slice(c0 + SWA_HEAD_DIM * g, c0 + SWA_HEAD_DIM * (g + 1))
            placed = (_dot_tn(hi[:, cols], place) + _dot_tn(mid[:, cols], place)
                      + _dot_tn(lo[:, cols], place))
            for b in range(SB):
                shifted = pltpu.roll(c_ref[b, g], WINDOW - DEC_SEQ, 1)
                o_ref[b, g] = jnp.where(keep, shifted, placed[:, WINDOW * b:WINDOW * (b + 1)])


def _window_update(kv_new, cache_kt, cache_vt, place):
    blk = pl.BlockSpec((None, SB, SWA_KV_HEADS, SWA_HEAD_DIM, WINDOW), lambda l, j: (l, j, 0, 0, 0))
    return pl.pallas_call(
        _window_kernel,
        grid=(DEPTH, DEC_BATCH // SB),
        in_specs=[pl.BlockSpec((None,) + kv_new.shape[1:], lambda l, j: (l, 0, 0)), blk, blk,
                  _const_spec(place.shape)],
        out_specs=[blk, blk],
        out_shape=[jax.ShapeDtypeStruct(cache_kt.shape, F32), jax.ShapeDtypeStruct(cache_vt.shape, F32)],
        compiler_params=_params(("arbitrary", "arbitrary")),
        name="window_update",
    )(kv_new, cache_kt, cache_vt, place)


MXU_N = 256
FF_CHUNKS = ((0, 6 * MXU_N), (6 * MXU_N, D_FF))


def _ffn_tail(x1, pe_ref, gp_ref, wpg_ref, wpp_ref, gf_ref, y_ref, final):
    hp = _rms(x1, gp_ref[...]).astype(BF16)
    x2 = x1 + jax.nn.sigmoid(_dot(hp, wpg_ref[...])) * _dot(pe_ref[...].astype(BF16), wpp_ref[...])
    y_ref[...] = _rms(x2, gf_ref[...]) if final else x2


def _ffn_stages(x, carry, first, gn_ref, wi_ref, cw_ref, cb_ref, wd_ref, keep):
    tm = x.shape[0]
    hb = _rms(x, gn_ref[...]).astype(BF16)
    row = lax.broadcasted_iota(jnp.int32, (tm, 1), 0)
    acc = x
    for c0, c1 in FF_CHUNKS:
        cs = slice(c0, c1)
        a = _dot(hb, wi_ref[:, c0:c1])
        bb = _dot(hb, wi_ref[:, D_FF + c0:D_FF + c1])
        yield
        p0 = carry[6:7, cs] * first
        p1 = carry[7:8, cs] * first
        a1 = jnp.where(row == 0, p1, pltpu.roll(a, 1, 0))
        a2 = jnp.where(row == 0, p0, jnp.where(row == 1, p1, pltpu.roll(a, 2, 0)))
        conv = cb_ref[:, cs] + cw_ref[0:1, cs] * a2 + cw_ref[1:2, cs] * a1 + cw_ref[2:3, cs] * a
        act = (jax.nn.gelu(conv) * bb).astype(BF16)
        keep.append(a[tm - 8:tm, :])
        yield
        acc = acc + _dot(act, wd_ref[c0:c1, :])
        yield
    return acc


def _ffn_prompt_kernel(x_ref, pe_ref, gn_ref, wi_ref, cw_ref, cb_ref, wd_ref, gp_ref, wpg_ref, wpp_ref,
                       gf_ref, y_ref, tail_ref, carry_scr, *, final, nt):
    i = pl.program_id(1)

    @pl.when(i == 0)
    def _():
        carry_scr[...] = jnp.zeros_like(carry_scr)

    keep = []
    (acc,) = _interleave([_ffn_stages(x_ref[...], carry_scr[...], 1.0, gn_ref, wi_ref, cw_ref, cb_ref, wd_ref, keep)])
    _ffn_tail(acc, pe_ref, gp_ref, wpg_ref, wpp_ref, gf_ref, y_ref, final)
    last_rows = jnp.concatenate(keep, axis=1)
    tail_ref[...] = last_rows

    @pl.when(i < nt - 1)
    def _():
        carry_scr[...] = last_rows


def _ffn_sample_kernel(x_ref, pe_ref, st_ref, gn_ref, wi_ref, cw_ref, cb_ref, wd_ref, gp_ref, wpg_ref,
                       wpp_ref, gf_ref, y_ref, tail_ref, *, final):
    ns = DEC_BATCH
    x = x_ref[...]
    hb = _rms(x, gn_ref[...]).astype(BF16)
    acc = x
    for c0, c1 in FF_CHUNKS:
        cs = slice(c0, c1)
        a = _dot(hb, wi_ref[:, c0:c1])
        bb = _dot(hb, wi_ref[:, D_FF + c0:D_FF + c1])
        st0 = st_ref[0, :, cs]
        st1 = st_ref[1, :, cs]
        a1 = jnp.concatenate([st1, a[0:3 * ns]], axis=0)
        a2 = jnp.concatenate([st0, st1, a[0:2 * ns]], axis=0)
        conv = cb_ref[:, cs] + cw_ref[0:1, cs] * a2 + cw_ref[1:2, cs] * a1 + cw_ref[2:3, cs] * a
        act = (jax.nn.gelu(conv) * bb).astype(BF16)
        acc = acc + _dot(act, wd_ref[c0:c1, :])
        tail_ref[:, cs] = a[2 * ns:4 * ns, :]
    _ffn_tail(acc, pe_ref, gp_ref, wpg_ref, wpp_ref, gf_ref, y_ref, final)


def _ffn_weight_specs(layer):
    return [_layer_spec((1, D_MODEL), layer),
            _layer_spec((D_MODEL, 2 * D_FF), layer),
            _layer_spec((CONV_W, D_FF), layer),
            _layer_spec((1, D_FF), layer),
            _layer_spec((D_FF, D_MODEL), layer),
            _layer_spec((1, D_MODEL), layer),
            _layer_spec((D_MODEL, D_MODEL), layer),
            _layer_spec((D_PLE, D_MODEL), layer),
            _const_spec((1, D_MODEL))]


def _ffn_prompt(x, pe, layer, weights, nb, t, tm, final):
    nt = t // tm
    tok = lambda w: pl.BlockSpec((tm, w), lambda b, i: (b * nt + i, 0))
    return pl.pallas_call(
        functools.partial(_ffn_prompt_kernel, final=final, nt=nt),
        grid=(nb, nt),
        in_specs=[tok(D_MODEL),
                  pl.BlockSpec((None, tm, D_PLE), lambda b, i: (layer, b * nt + i, 0))]
        + _ffn_weight_specs(layer),
        out_specs=[tok(D_MODEL), pl.BlockSpec((8, D_FF), lambda b, i: (b, 0))],
        out_shape=[jax.ShapeDtypeStruct((nb * t, D_MODEL), F32),
                   jax.ShapeDtypeStruct((nb * 8, D_FF), F32)],
        scratch_shapes=[pltpu.VMEM((8, D_FF), F32)],
        compiler_params=_params(("arbitrary", "arbitrary")),
        name="ffn_prompt",
    )(x, pe, *weights)


def _ffn_sample(x, pe, st, layer, weights, final):
    n = x.shape[0]
    return pl.pallas_call(
        functools.partial(_ffn_sample_kernel, final=final),
        grid=(1,),
        in_specs=[pl.BlockSpec((n, D_MODEL), lambda i: (0, 0)),
                  pl.BlockSpec((None, n, D_PLE), lambda i: (layer, 0, 0)),
                  pl.BlockSpec((None, CONV_W - 1, DEC_BATCH, D_FF), lambda i: (layer, 0, 0, 0))]
        + _ffn_weight_specs(layer),
        out_specs=[pl.BlockSpec((n, D_MODEL), lambda i: (0, 0)),
                   pl.BlockSpec((n // 2, D_FF), lambda i: (0, 0))],
        out_shape=[jax.ShapeDtypeStruct((n, D_MODEL), F32),
                   jax.ShapeDtypeStruct((n // 2, D_FF), F32)],
        compiler_params=_params(("arbitrary",)),
        name="ffn_sample",
    )(x, pe, st, *weights)


def _rope_tables(pos, inv_freq):
    half = inv_freq.shape[0]
    ang = pos.astype(np.float64)[:, None] * inv_freq[None, :]
    c, s = np.cos(ang), np.sin(ang)
    rest = SWA_HEAD_DIM - 2 * half
    cos64 = np.concatenate([c, c, np.ones((pos.shape[0], rest))], axis=1)
    sin64 = np.concatenate([-s, s, np.zeros((pos.shape[0], rest))], axis=1)
    return (jnp.asarray(np.concatenate([cos64, cos64], axis=1), F32),
            jnp.asarray(np.concatenate([sin64, sin64], axis=1), F32))


def _pack_w_in(w_in):
    ga0 = sum(IN_SIZES[:4])
    ga = jnp.pad(w_in[:, :, ga0:ga0 + GLA_RANK], ((0, 0), (0, 0), (0, LANES - GLA_RANK)))
    return jnp.concatenate([w_in[:, :, :ga0], w_in[:, :, ga0 + GLA_RANK:], ga], axis=2).astype(BF16)


def kernel(x_prompt, x_sample, state_gla, state_ret, cache_swa_k, cache_swa_v, state_conv, p_prompt,
           p_sample, norm_mix, w_in, w_gla_a, b_gla_a, gla_norm, ret_norm, swa_sink, w_branch, w_out,
           norm_ffn, w_ffn_in, conv_w, conv_b, w_ffn_out, norm_ple, w_ple_gate, w_ple_proj, norm_final):
    nb, t, _ = x_prompt.shape
    ns, ts, _ = x_sample.shape
    n_s = ns * ts

    w_main = _pack_w_in(w_in)
    wa = jnp.pad(w_gla_a, ((0, 0), (0, LANES - GLA_RANK), (0, 0))).astype(BF16)
    ba = b_gla_a[:, None, :]
    wb = w_branch.astype(BF16)
    wo = w_out.astype(BF16)
    ffn_weights = (norm_ffn[:, None, :], w_ffn_in.astype(BF16), conv_w, conv_b[:, None, :],
                   w_ffn_out.astype(BF16), norm_ple[:, None, :], w_ple_gate.astype(BF16),
                   w_ple_proj.astype(BF16), norm_final[None, :])
    g_mix = norm_mix[:, None, :]
    gn_gla = jnp.tile(gla_norm, (1, GLA_HEADS))[:, None, :]
    gn_ret = ret_norm.reshape(DEPTH, 1, RET_W)

    ret_freq = 1.0 / (RET_THETA ** np.linspace(0.0, 1.0, RET_DK // 2))
    swa_freq = 1.0 / (ROPE_THETA ** (np.arange(0, ROPE_DIM, 2, dtype=np.float64) / ROPE_DIM))
    pos_p = np.arange(t)
    pos_s = PAST_LEN + np.arange(n_s) // ns
    tabs_p = _rope_tables(pos_p, ret_freq) + _rope_tables(pos_p, swa_freq)
    tabs_s = _rope_tables(pos_s, ret_freq) + _rope_tables(pos_s, swa_freq)

    gm_np, mk_np = _gla_constants()
    d_np, qdec_np, kdec_np, adec_np = _ret_constants()
    hk_np, hv_np, hs_np = _head_masks()
    mix_consts = (jnp.asarray(gm_np, BF16), jnp.asarray(mk_np), jnp.asarray(d_np), jnp.asarray(qdec_np),
                  jnp.asarray(kdec_np), jnp.asarray(adec_np), jnp.asarray(hk_np, BF16), jnp.asarray(hv_np, BF16),
                  jnp.asarray(hs_np))
    log_gamma = np.log1p(-np.exp2(-5.0 - np.arange(RET_HEADS, dtype=np.float64)))
    la_ret = jnp.asarray(np.broadcast_to(np.repeat(log_gamma, RET_DK)[None, :], (SB, RET_HEADS * RET_DK)), F32)
    mc_np, mn_np = _swa_sample_masks()
    rep_np, bd_np = _rec_sample_constants()
    sample_consts = (jnp.asarray(rep_np, BF16), jnp.asarray(bd_np), jnp.asarray(mc_np), jnp.asarray(mn_np))

    xp = x_prompt.reshape(nb * t, D_MODEL)
    xs = x_sample.transpose(1, 0, 2).reshape(n_s, D_MODEL)
    pe_p = p_prompt.reshape(DEPTH, nb * t, D_PLE)
    pe_s = p_sample.transpose(0, 2, 1, 3).reshape(DEPTH, n_s, D_PLE)
    conv_st = state_conv.transpose(0, 2, 1, 3)
    ck = cache_swa_k.transpose(0, 1, 3, 4, 2)
    cv = cache_swa_v.transpose(0, 1, 3, 4, 2)

    outs = {k_: [] for k_ in ("gla_p", "gla_s", "ret_p", "ret_s", "kv_p", "kv_s", "conv_p", "conv_s")}
    for l in range(DEPTH):
        final = l == DEPTH - 1
        sink = swa_sink[l]

        xp3, st_a, st_b, kv_tail = _mix_prompt(xp.reshape(nb, t, D_MODEL), sink, l, g_mix, w_main, wa, ba, tabs_p,
                                               mix_consts, gn_gla[l], gn_ret[l], wb, wo)
        xp, tail = _ffn_prompt(xp3.reshape(nb * t, D_MODEL), pe_p, l, ffn_weights, nb, t, 256, final)
        outs["gla_p"].append(st_a)
        outs["ret_p"].append(st_b)
        outs["kv_p"].append(kv_tail)
        outs["conv_p"].append(tail.reshape(nb, 8, D_FF)[:, 8 - (CONV_W - 1):])

        zs = _in_proj(xs, l, g_mix, w_main, wa, ba, tabs_s, 256)
        xs, st_a, st_b = _mix_sample(xs, zs, sink, la_ret, state_gla, state_ret, ck, cv, l, gn_gla[l], gn_ret[l],
                                     sample_consts, wb, wo)
        xs, tail_s = _ffn_sample(xs, pe_s, conv_st, l, ffn_weights, final)
        outs["gla_s"].append(st_a)
        outs["ret_s"].append(st_b)
        outs["kv_s"].append(zs[:, C_SK:C_LA])
        outs["conv_s"].append(tail_s.reshape(CONV_W - 1, ns, D_FF).transpose(1, 0, 2))

    y_prompt = xp.reshape(nb, t, D_MODEL)
    y_sample = xs.reshape(ts, ns, D_MODEL).transpose(1, 0, 2)
    st = {k_: jnp.stack(v_) for k_, v_ in outs.items()}

    def kv_heads(a):
        return a.reshape(a.shape[:-1] + (SWA_KV_HEADS, SWA_HEAD_DIM))

    k_p = kv_heads(st["kv_p"][..., :LANES])
    v_p = kv_heads(st["kv_p"][..., LANES:])
    k_st, v_st = _window_update(st["kv_s"], ck, cv, jnp.asarray(_window_place_matrix(), BF16))
    k_s = k_st.transpose(0, 1, 4, 2, 3)
    v_s = v_st.transpose(0, 1, 4, 2, 3)
    return (y_prompt, y_sample, st["gla_p"], st["gla_s"], st["ret_p"], st["ret_s"],
            k_p, k_s, v_p, v_s, st["conv_p"], st["conv_s"])
```

```python
import functools
import math

import numpy as np
import jax
import jax.numpy as jnp
from jax import lax
from jax.experimental import pallas as pl
from jax.experimental.pallas import tpu as pltpu

F32 = jnp.float32
BF16 = jnp.bfloat16

D_MODEL = 1024
BATCH = 2
SEQ = 8192
DEPTH = 4
DEC_BATCH = 128
DEC_SEQ = 4
PAST_LEN = 8192
D_PLE = 256
GLA_HEADS = 4
GLA_DK = 64
GLA_DV = 128
GLA_RANK = 16
GLA_TAU = 16.0
RET_HEADS = 4
RET_DK = 64
RET_DV = 128
RET_THETA = 10000.0
SWA_Q_HEADS = 8
SWA_KV_HEADS = 2
SWA_HEAD_DIM = 64
SWA_GROUP = SWA_Q_HEADS // SWA_KV_HEADS
WINDOW = 128
ROPE_THETA = 500000.0
ROPE_DIM = SWA_HEAD_DIM // 4
D_FF = 2816
CONV_W = 3
N_BRANCH = 3
EPS = 1e-6
NEG_INF = -1e30

GLA_W = GLA_HEADS * GLA_DV
RET_W = RET_HEADS * RET_DV
SWA_W = SWA_Q_HEADS * SWA_HEAD_DIM
IN_SIZES = (GLA_HEADS * GLA_DK, GLA_HEADS * GLA_DK, GLA_W, GLA_W, GLA_RANK,
            RET_HEADS * RET_DK, RET_HEADS * RET_DK, RET_W, RET_W,
            SWA_W, SWA_KV_HEADS * SWA_HEAD_DIM, SWA_KV_HEADS * SWA_HEAD_DIM,
            N_BRANCH * D_MODEL)

LANES = 128
VMEM_LIMIT = 52 * 1024 * 1024

C_GQ = 0
C_GK = 256
C_GV = 512
C_GR = 1024
C_RQ = 1536
C_RK = 1792
C_RV = 2048
C_RG = 2560
C_SQ = 3072
C_SK = 3584
C_SV = 3712
C_LA = 3840
C_ZG = 4096
Z_W = 7168
W_ZG = 3840
W_GA = 6912
W_MAIN = 7040

TT = 128
N_LEVELS = 7
SB = 16


def _dot(a, b):
    return jnp.dot(a, b, preferred_element_type=F32)


def _dot_nt(a, b):
    return lax.dot_general(a, b, (((1,), (1,)), ((), ())), preferred_element_type=F32)


def _dot_tn(a, b):
    return lax.dot_general(a, b, (((0,), (0,)), ((), ())), preferred_element_type=F32)


def _rms(x, g):
    return x * lax.rsqrt(jnp.mean(x * x, axis=-1, keepdims=True) + EPS) * g


def _layer_spec(shape, layer):
    nd = len(shape)
    return pl.BlockSpec((None,) + tuple(shape), lambda *_: (layer,) + (0,) * nd,
                        pipeline_mode=pl.Buffered(1))


def _const_spec(shape):
    nd = len(shape)
    return pl.BlockSpec(tuple(shape), lambda *_: (0,) * nd, pipeline_mode=pl.Buffered(1))


def _params(sem):
    return pltpu.CompilerParams(dimension_semantics=sem, vmem_limit_bytes=VMEM_LIMIT)


def _rope_block(x, cos, sin_s, first):
    half_mask, half = first
    xr = jnp.where(half_mask, pltpu.roll(x, LANES - half, 1), pltpu.roll(x, half, 1))
    return x * cos + xr * sin_s


def _gate_logits(hb, w_ref, j):
    return _dot(hb, w_ref[:, W_ZG + D_MODEL * j:W_ZG + D_MODEL * (j + 1)])


def _in_stages(x, g_ref, w_ref, wa_ref, ba_ref, tabs, z_ref, with_gates):
    hb = _rms(x, g_ref[...]).astype(BF16)

    def mm(c0, c1):
        return _dot(hb, w_ref[:, c0:c1])

    lane = lax.broadcasted_iota(jnp.int32, (1, LANES), 1) % 64
    ret_first = (lane < RET_DK // 2, RET_DK // 2)
    swa_first = (lane < ROPE_DIM // 2, ROPE_DIM // 2)
    rc, rs, sc, ss = tabs

    ga = mm(W_GA, W_MAIN)
    xa = _dot(ga.astype(BF16), wa_ref[...]) + ba_ref[...]
    log_sig = jnp.minimum(xa, 0.0) - jnp.log1p(jnp.exp(-jnp.abs(xa)))
    z_ref[:, C_LA:C_ZG] = log_sig * (1.0 / GLA_TAU)
    z_ref[:, C_GQ:C_GK] = mm(C_GQ, C_GK) * (GLA_DK ** -0.5)
    yield
    z_ref[:, C_GK:C_RQ] = mm(C_GK, C_RQ)
    yield
    rq = mm(C_RQ, C_RK)
    rk = mm(C_RK, C_RV)
    for j in range(2):
        sl = slice(LANES * j, LANES * (j + 1))
        z_ref[:, C_RQ + LANES * j:C_RQ + LANES * (j + 1)] = _rope_block(rq[:, sl], rc, rs, ret_first)
        z_ref[:, C_RK + LANES * j:C_RK + LANES * (j + 1)] = (
            _rope_block(rk[:, sl], rc, rs, ret_first) * (RET_DK ** -0.5))
    yield
    z_ref[:, C_RV:C_SQ] = mm(C_RV, C_SQ)
    yield
    sqkv = mm(C_SQ, C_LA)
    for j in range(5):
        sl = slice(LANES * j, LANES * (j + 1))
        z_ref[:, C_SQ + LANES * j:C_SQ + LANES * (j + 1)] = _rope_block(sqkv[:, sl], sc, ss, swa_first)
    z_ref[:, C_SV:C_LA] = sqkv[:, C_SV - C_SQ:]
    yield
    if with_gates:
        for j in range(N_BRANCH):
            z_ref[:, C_ZG + D_MODEL * j:C_ZG + D_MODEL * (j + 1)] = _gate_logits(hb, w_ref, j)
            yield


def _in_kernel(x_ref, g_ref, w_ref, wa_ref, ba_ref, rc_ref, rs_ref, sc_ref, ss_ref, z_ref):
    tabs = (rc_ref[...], rs_ref[...], sc_ref[...], ss_ref[...])
    for _ in _in_stages(x_ref[...], g_ref, w_ref, wa_ref, ba_ref, tabs, z_ref, True):
        pass


def _in_proj(x, layer, g_mix, w_main, wa, ba, tabs, tm):
    n = x.shape[0]
    rc, rs, sc, ss = tabs
    nt = rc.shape[0] // tm
    tab = pl.BlockSpec((tm, LANES), lambda i: (i % nt, 0))
    return pl.pallas_call(
        _in_kernel,
        grid=(n // tm,),
        in_specs=[pl.BlockSpec((tm, D_MODEL), lambda i: (i, 0)),
                  _layer_spec((1, D_MODEL), layer),
                  _layer_spec((D_MODEL, W_MAIN), layer),
                  _layer_spec((LANES, 256), layer),
                  _layer_spec((1, 256), layer),
                  tab, tab, tab, tab],
        out_specs=pl.BlockSpec((tm, Z_W), lambda i: (i, 0)),
        out_shape=jax.ShapeDtypeStruct((n, Z_W), F32),
        compiler_params=_params(("arbitrary",)),
        name="in_proj",
    )(x, g_mix, w_main, wa, ba, rc, rs, sc, ss)


def _gla_constants():
    t = np.arange(TT)
    g = np.zeros((2 + N_LEVELS, TT, TT), np.float32)
    g[0] = (t[None, :] <= t[:, None])
    g[1] = (t[None, :] > t[:, None])
    m = np.zeros((1 + N_LEVELS, TT, TT), np.float32)
    m[0] = np.eye(TT)
    for lv in range(1, N_LEVELS + 1):
        bs, hf = 2 ** lv, 2 ** (lv - 1)
        bd = (t // bs) * bs + hf - 1
        upper = (t % bs) >= hf
        u = t[None, :]
        g[1 + lv] = np.where(upper[:, None], (u > bd[:, None]) & (u <= t[:, None]),
                             (u > t[:, None]) & (u <= bd[:, None]))
        same = (t[:, None] // bs) == (t[None, :] // bs)
        m[lv] = same & upper[:, None] & (~upper)[None, :]
    return g.reshape(-1, TT), np.tile(m, (1, 1, GLA_HEADS))


def _head_masks():
    hk = np.arange(GLA_HEADS * TT)[:, None] // TT == np.arange(GLA_HEADS * GLA_DK)[None, :] // GLA_DK
    hv = np.arange(GLA_HEADS * TT)[:, None] // TT == np.arange(GLA_W)[None, :] // GLA_DV
    hs = np.arange(GLA_HEADS * GLA_DK)[:, None] // GLA_DK == np.arange(GLA_W)[None, :] // GLA_DV
    return hk.astype(np.float32), hv.astype(np.float32), hs.astype(np.float32)


def _heads_blockdiag(x, mask):
    return jnp.concatenate([x] * GLA_HEADS, axis=0) * mask


def _col_vector(row):
    n = row.shape[1]
    eye = lax.broadcasted_iota(jnp.int32, (n, n), 0) == lax.broadcasted_iota(jnp.int32, (n, n), 1)
    return jnp.sum(jnp.where(eye, row, 0.0), axis=1, keepdims=True)


def _norm_gate(o, gn, gate):
    outs = [_rms(o[:, GLA_DV * h:GLA_DV * (h + 1)], gn[:, GLA_DV * h:GLA_DV * (h + 1)]) for h in range(GLA_HEADS)]
    return jnp.concatenate(outs, axis=1) * (gate * jax.nn.sigmoid(gate))


def _gla_tile(z_ref, gm_ref, mk_ref, hk_ref, hv_ref, hs_ref, gn_ref, st_ref):
    la = z_ref[:, C_LA:C_ZG]
    la_hi = la.astype(BF16)
    la_lo = (la - la_hi.astype(F32)).astype(BF16)
    gm = gm_ref[...]
    ex = jnp.exp(_dot(gm, la_hi) + _dot(gm, la_lo))
    q = z_ref[:, C_GQ:C_GK]
    k = z_ref[:, C_GK:C_GV]
    e_b = ex[0:TT]
    a_row = e_b[TT - 1:TT, :]
    qd = (q * e_b).astype(BF16)
    kd = (k * ex[TT:2 * TT]).astype(BF16)
    qb = q.astype(BF16)
    kb = k.astype(BF16)
    ql = [(q * ex[(1 + lv) * TT:(2 + lv) * TT]).astype(BF16) for lv in range(1, N_LEVELS + 1)]
    kl = [(k * ex[(1 + lv) * TT:(2 + lv) * TT]).astype(BF16) for lv in range(1, N_LEVELS + 1)]
    hk = hk_ref[...]
    yield
    a = mk_ref[0] * _dot_nt(qb, _heads_blockdiag(kb, hk))
    for lv in range(N_LEVELS):
        yield
        a = a + mk_ref[lv + 1] * _dot_nt(ql[lv], _heads_blockdiag(kl[lv], hk))
    yield
    v = z_ref[:, C_GV:C_GR].astype(BF16)
    s = st_ref[...]
    o = _dot(a.astype(BF16), _heads_blockdiag(v, hv_ref[...])) + _dot(qd, s.astype(BF16))
    yield
    st_ref[...] = _col_vector(a_row) * s + hs_ref[...] * _dot_tn(kd, v)
    yield
    return _norm_gate(o, gn_ref[...], z_ref[:, C_GR:C_RQ])


def _ret_gammas():
    return [1.0 - 2.0 ** (-5.0 - h) for h in range(RET_HEADS)]


def _ret_constants():
    t = np.arange(TT, dtype=np.float64)
    d = np.zeros((RET_HEADS, TT, TT), np.float64)
    qdec = np.zeros((TT, RET_HEADS * RET_DK), np.float64)
    kdec = np.zeros((TT, RET_HEADS * RET_DK), np.float64)
    adec = np.zeros((RET_HEADS * RET_DK, LANES), np.float64)
    for h, gam in enumerate(_ret_gammas()):
        diff = t[:, None] - t[None, :]
        d[h] = np.where(diff >= 0, gam ** np.maximum(diff, 0.0), 0.0)
        qdec[:, h * RET_DK:(h + 1) * RET_DK] = (gam ** (t + 1.0))[:, None]
        kdec[:, h * RET_DK:(h + 1) * RET_DK] = (gam ** (TT - 1.0 - t))[:, None]
        adec[h * RET_DK:(h + 1) * RET_DK, :] = gam ** TT
    d = np.concatenate(list(d), axis=1)
    adec = np.tile(adec, (1, RET_W // LANES))
    return d.astype(np.float32), qdec.astype(np.float32), kdec.astype(np.float32), adec.astype(np.float32)


def _ret_tile(z_ref, d_ref, qdec_ref, kdec_ref, adec_ref, hk_ref, hv_ref, hs_ref, gn_ref, st_ref):
    q = z_ref[:, C_RQ:C_RK]
    k = z_ref[:, C_RK:C_RV]
    qd = (q * qdec_ref[...]).astype(BF16)
    kd = (k * kdec_ref[...]).astype(BF16)
    a = d_ref[...] * _dot_nt(q.astype(BF16), _heads_blockdiag(k.astype(BF16), hk_ref[...]))
    yield
    v = z_ref[:, C_RV:C_RG].astype(BF16)
    s = st_ref[...]
    o = _dot(a.astype(BF16), _heads_blockdiag(v, hv_ref[...])) + _dot(qd, s.astype(BF16))
    yield
    st_ref[...] = adec_ref[...] * s + hs_ref[...] * _dot_tn(kd, v)
    yield
    return _norm_gate(o, gn_ref[...], z_ref[:, C_RG:C_SQ])


def _rec_sample_constants():
    c = np.arange(SB * GLA_DK)
    rep = np.arange(GLA_DK)[:, None] == (c[None, :] % GLA_DK)
    bd = (np.arange(DEC_SEQ * SB)[:, None] % SB) == (c[None, :] // GLA_DK)
    return rep.astype(np.float32), bd.astype(np.float32)


def _block_rows(ref, row0, c0, c1):
    return [ref[pl.ds(t * DEC_BATCH + row0, SB), c0:c1] for t in range(DEC_SEQ)]


def _rec_sample_stages(q, k, v, gate, la, s0_ref, s1_ref, gn_ref, rep_ref, bd_ref):
    b = []
    for t in range(DEC_SEQ):
        b.append(la[t] if t == 0 else b[-1] + la[t])
    qd = jnp.concatenate([q[t] * jnp.exp(b[t]) for t in range(DEC_SEQ)], axis=0).astype(BF16)
    kd = jnp.concatenate([k[t] * jnp.exp(b[-1] - b[t]) for t in range(DEC_SEQ)], axis=0).astype(BF16)
    vv = jnp.concatenate(v, axis=0).astype(BF16)
    a = jnp.exp(b[-1])
    a_hi = a.astype(BF16)
    a_r = a - a_hi.astype(F32)
    a_mid = a_r.astype(BF16)
    a_lo = (a_r - a_mid.astype(F32)).astype(BF16)
    rep = rep_ref[...]
    bd = bd_ref[...]
    ones = jnp.ones((SB, GLA_DV), BF16)

    def expand(x, mask):
        return (_dot(x, rep) * mask).astype(BF16)

    yield
    oi = []
    for h in range(GLA_HEADS):
        hs = slice(GLA_DK * h, GLA_DK * (h + 1))
        vs = slice(GLA_DV * h, GLA_DV * (h + 1))
        s0 = s0_ref[:, h].reshape(SB * GLA_DK, GLA_DV)
        oi.append(_dot(expand(qd[:, hs], bd), s0.astype(BF16)))
        a_col = (_dot_tn(expand(a_hi[:, hs], bd[0:SB]), ones) + _dot_tn(expand(a_mid[:, hs], bd[0:SB]), ones)
                 + _dot_tn(expand(a_lo[:, hs], bd[0:SB]), ones))
        s1 = a_col * s0 + _dot_tn(expand(kd[:, hs], bd), vv[:, vs])
        s1_ref[:, h] = s1.reshape(SB, GLA_DK, GLA_DV)
        yield
    oi = jnp.concatenate(oi, axis=1)

    gn = gn_ref[...]
    outs = []
    for t in range(DEC_SEQ):
        o = oi[t * SB:(t + 1) * SB, :]
        for s in range(t + 1):
            p = q[t] * k[s] * jnp.exp(b[t] - b[s])
            parts = []
            for h in range(GLA_HEADS):
                hs = slice(GLA_DK * h, GLA_DK * (h + 1))
                vs = slice(GLA_DV * h, GLA_DV * (h + 1))
                parts.append(jnp.sum(p[:, hs], axis=1, keepdims=True) * v[s][:, vs])
            o = o + jnp.concatenate(parts, axis=1)
        outs.append(_norm_gate(o, gn, gate[t]))
        yield
    return jnp.concatenate(outs, axis=0)


def _swa_tile(z_ref, sink_ref, kp_ref, vp_ref, first_col):
    kc = z_ref[:, C_SK:C_SV].astype(BF16)
    vc = z_ref[:, C_SV:C_LA].astype(BF16)
    kband = jnp.concatenate([kp_ref[...], kc], axis=0)
    vband = jnp.concatenate([vp_ref[...], vc], axis=0)
    rows = SWA_GROUP * WINDOW
    r = lax.broadcasted_iota(jnp.int32, (rows, 2 * WINDOW), 0) & (WINDOW - 1)
    c = lax.broadcasted_iota(jnp.int32, (rows, 2 * WINDOW), 1)
    valid = (c > r) & (c <= r + WINDOW) & (c >= first_col)
    groups = range(SWA_KV_HEADS)
    dsl = [slice(SWA_HEAD_DIM * g, SWA_HEAD_DIM * (g + 1)) for g in groups]
    s, sk = [], []
    for g in groups:
        heads = range(SWA_GROUP * g, SWA_GROUP * (g + 1))
        q = jnp.concatenate(
            [z_ref[:, C_SQ + SWA_HEAD_DIM * hq:C_SQ + SWA_HEAD_DIM * (hq + 1)] for hq in heads], axis=0)
        sk.append(jnp.concatenate([jnp.full((WINDOW, 1), sink_ref[hq], F32) for hq in heads], axis=0))
        s.append(_dot_nt(q.astype(BF16), kband[:, dsl[g]]) * (SWA_HEAD_DIM ** -0.5))
    yield
    m, p = [], []
    for g in groups:
        sg = jnp.where(valid, s[g], NEG_INF)
        m.append(jnp.maximum(jnp.max(sg, axis=-1, keepdims=True), sk[g]))
        p.append(jnp.exp(sg - m[g]))
    yield
    outs = []
    for g in groups:
        den = jnp.sum(p[g], axis=-1, keepdims=True) + jnp.exp(sk[g] - m[g])
        o = _dot(p[g].astype(BF16), vband[:, dsl[g]]) / den
        outs += [o[WINDOW * j:WINDOW * (j + 1)] for j in range(SWA_GROUP)]
    yield
    return jnp.concatenate(outs, axis=1), kc, vc


def _interleave(stage_fns):
    results = [None] * len(stage_fns)
    live = list(enumerate(stage_fns))
    while live:
        still = []
        for idx, gen in live:
            try:
                next(gen)
                still.append((idx, gen))
            except StopIteration as stop:
                results[idx] = stop.value
        live = still
    return results


def _merge_stages(pend_ref, x_ref, g_ref, w_ref, wb_ref, wo_ref, y_ref):
    nb = x_ref.shape[0]
    hb = _rms(x_ref[...].reshape(nb * TT, D_MODEL), g_ref[...]).astype(BF16)
    gates = []
    for j in range(N_BRANCH):
        gates.append(jax.nn.sigmoid(_gate_logits(hb, w_ref, j)))
        yield
    mixed = None
    for j in range(N_BRANCH):
        term = gates[j] * _dot(pend_ref[:, GLA_W * j:GLA_W * (j + 1)], wb_ref[GLA_W * j:GLA_W * (j + 1), :])
        mixed = term if mixed is None else mixed + term
        yield
    out = _dot(mixed.astype(BF16), wo_ref[...])
    for b in range(x_ref.shape[0]):
        y_ref[b] = x_ref[b] + out[TT * b:TT * (b + 1)]
    yield


def _mix_kernel(sink_ref, xp_ref, xn_ref, tp0, tp1, tp2, tp3, tn0, tn1, tn2, tn3, g_ref, w_ref, wa_ref, ba_ref,
                gm_ref, mk_ref, d_ref, qdec_ref, kdec_ref, adec_ref, hk_ref, hv_ref, hs_ref, gna_ref, gnb_ref,
                wb_ref, wo_ref, y_ref, sta_ref, stb_ref, kv_ref, z_scr, pend_scr, sa_scr, sb_scr, kp_scr, vp_scr,
                *, nt):
    i = pl.program_id(0)
    nb = xp_ref.shape[0]

    def rows_of(ref3):
        return ref3[...].reshape(nb * TT, ref3.shape[2])

    def tiled(tab_refs):
        return tuple(jnp.concatenate([r[...]] * nb, axis=0) for r in tab_refs)

    @pl.when(i == 0)
    def _():
        sa_scr[...] = jnp.zeros_like(sa_scr)
        sb_scr[...] = jnp.zeros_like(sb_scr)
        kp_scr[...] = jnp.zeros_like(kp_scr)
        vp_scr[...] = jnp.zeros_like(vp_scr)
        pend_scr[...] = jnp.zeros_like(pend_scr)
        for _ in _in_stages(rows_of(xp_ref), g_ref, w_ref, wa_ref, ba_ref, tiled((tp0, tp1, tp2, tp3)),
                            z_scr.at[0], False):
            pass

    slot = i % 2
    zc = z_scr.at[slot]
    first_col = jnp.where(i > 0, 0, WINDOW)
    stages = [_merge_stages(pend_scr, xp_ref, g_ref, w_ref, wb_ref, wo_ref, y_ref),
              _in_stages(rows_of(xn_ref), g_ref, w_ref, wa_ref, ba_ref, tiled((tn0, tn1, tn2, tn3)),
                         z_scr.at[1 - slot], False)]
    for b in range(nb):
        zb = zc.at[pl.ds(TT * b, TT)]
        stages += [_gla_tile(zb, gm_ref, mk_ref, hk_ref, hv_ref, hs_ref, gna_ref, sa_scr.at[b]),
                   _swa_tile(zb, sink_ref, kp_scr.at[b], vp_scr.at[b], first_col),
                   _ret_tile(zb, d_ref, qdec_ref, kdec_ref, adec_ref, hk_ref, hv_ref, hs_ref, gnb_ref, sb_scr.at[b])]
    branch = _interleave(stages)[2:]
    swa = [branch[3 * b + 1] for b in range(nb)]
    outs = [[branch[3 * b] for b in range(nb)], [branch[3 * b + 2] for b in range(nb)], [s[0] for s in swa]]
    pend = [jnp.concatenate(o, axis=0).astype(BF16) for o in outs]

    @pl.when(i < nt)
    def _():
        for j in range(N_BRANCH):
            pend_scr[:, GLA_W * j:GLA_W * (j + 1)] = pend[j]
        for b in range(nb):
            kp_scr[b] = swa[b][1]
            vp_scr[b] = swa[b][2]
        for b in range(nb):
            for h in range(GLA_HEADS):
                sta_ref[b, h] = sa_scr[b, GLA_DK * h:GLA_DK * (h + 1), GLA_DV * h:GLA_DV * (h + 1)]
                stb_ref[b, h] = sb_scr[b, RET_DK * h:RET_DK * (h + 1), RET_DV * h:RET_DV * (h + 1)]
            kv_ref[b] = zc[pl.ds(TT * b, TT), C_SK:C_LA]


MIX_VMEM_LIMIT = 60 * 1024 * 1024


def _mix_prompt(x, sink, layer, g_mix, w_main, wa, ba, tabs, consts, gna, gnb, wb, wo):
    nb, t, _ = x.shape
    nt = t // TT
    st_spec = pl.BlockSpec((nb, GLA_HEADS, GLA_DK, GLA_DV), lambda i: (0, 0, 0, 0))
    st_shape = jax.ShapeDtypeStruct((nb, GLA_HEADS, GLA_DK, GLA_DV), F32)
    nxt = lambda i: jnp.minimum(i + 1, nt - 1)
    prv = lambda i: jnp.maximum(i - 1, 0)
    tab_prv = pl.BlockSpec((TT, LANES), lambda i: (prv(i), 0))
    tab_nxt = pl.BlockSpec((TT, LANES), lambda i: (nxt(i), 0))
    return pl.pallas_call(
        functools.partial(_mix_kernel, nt=nt),
        grid=(nt + 1,),
        in_specs=[pl.BlockSpec(memory_space=pltpu.SMEM),
                  pl.BlockSpec((nb, TT, D_MODEL), lambda i: (0, prv(i), 0)),
                  pl.BlockSpec((nb, TT, D_MODEL), lambda i: (0, nxt(i), 0))]
        + [tab_prv] * 4 + [tab_nxt] * 4
        + [_layer_spec((1, D_MODEL), layer), _layer_spec((D_MODEL, W_MAIN), layer),
           _layer_spec((LANES, 256), layer), _layer_spec((1, 256), layer)]
        + [_const_spec(c.shape) for c in consts]
        + [_const_spec((1, GLA_W)), _const_spec((1, RET_W)),
           _layer_spec((GLA_W + RET_W + SWA_W, D_MODEL), layer),
           _layer_spec((D_MODEL, D_MODEL), layer)],
        out_specs=[pl.BlockSpec((nb, TT, D_MODEL), lambda i: (0, prv(i), 0)), st_spec, st_spec,
                   pl.BlockSpec((nb, WINDOW, 2 * LANES), lambda i: (0, 0, 0))],
        out_shape=[jax.ShapeDtypeStruct((nb, t, D_MODEL), F32), st_shape, st_shape,
                   jax.ShapeDtypeStruct((nb, WINDOW, 2 * LANES), F32)],
        scratch_shapes=[pltpu.VMEM((2, nb * TT, C_ZG), F32),
                        pltpu.VMEM((nb * TT, GLA_W + RET_W + SWA_W), BF16),
                        pltpu.VMEM((nb, GLA_HEADS * GLA_DK, GLA_W), F32),
                        pltpu.VMEM((nb, RET_HEADS * RET_DK, RET_W), F32),
                        pltpu.VMEM((nb, WINDOW, 128), BF16), pltpu.VMEM((nb, WINDOW, 128), BF16)],
        compiler_params=pltpu.CompilerParams(dimension_semantics=("arbitrary",), vmem_limit_bytes=MIX_VMEM_LIMIT),
        name="mix_prompt",
    )(sink, x, x, *tabs, *tabs, g_mix, w_main, wa, ba, *consts, gna, gnb, wb, wo)


def _swa_sample_masks():
    r = np.arange(DEC_SEQ * SB)
    rt, rb = r // SB, r % SB
    c = np.arange(SB * WINDOW)
    cb, cj = c // WINDOW, c % WINDOW
    m_cache = (rb[:, None] == cb[None, :]) & (cj[None, :] > rt[:, None])
    m_new = (rb[:, None] == rb[None, :]) & (rt[None, :] <= rt[:, None])
    return m_cache.astype(np.float32), m_new.astype(np.float32)


def _swa_sample_stages(sink_ref, q, kn, vn, kc_ref, vc_ref, mc_ref, mn_ref):
    q = q.astype(BF16)
    kn = kn.astype(BF16)
    vn = vn.astype(BF16)
    kct = [jnp.concatenate([kc_ref[b, g] for b in range(SB)], axis=1).astype(BF16) for g in range(SWA_KV_HEADS)]
    vct = [jnp.concatenate([vc_ref[b, g] for b in range(SB)], axis=1).astype(BF16) for g in range(SWA_KV_HEADS)]
    ok_c = mc_ref[...] > 0.0
    ok_n = mn_ref[...] > 0.0
    outs = []
    for hq in range(SWA_Q_HEADS):
        g = hq // SWA_GROUP
        ds_ = slice(SWA_HEAD_DIM * g, SWA_HEAD_DIM * (g + 1))
        qh = q[:, SWA_HEAD_DIM * hq:SWA_HEAD_DIM * (hq + 1)]
        s1 = jnp.where(ok_c, _dot(qh, kct[g]) * (SWA_HEAD_DIM ** -0.5), NEG_INF)
        s2 = jnp.where(ok_n, _dot_nt(qh, kn[:, ds_]) * (SWA_HEAD_DIM ** -0.5), NEG_INF)
        sk = sink_ref[hq]
        m = jnp.maximum(jnp.maximum(jnp.max(s1, axis=-1, keepdims=True),
                                    jnp.max(s2, axis=-1, keepdims=True)), sk)
        p1 = jnp.exp(s1 - m)
        p2 = jnp.exp(s2 - m)
        den = jnp.sum(p1, axis=-1, keepdims=True) + jnp.sum(p2, axis=-1, keepdims=True) + jnp.exp(sk - m)
        outs.append((_dot_nt(p1.astype(BF16), vct[g]) + _dot(p2.astype(BF16), vn[:, ds_])) / den)
        if hq % 2 == 1:
            yield
    return jnp.concatenate(outs, axis=1)


def _mix_sample_kernel(sink_ref, x_ref, z_ref, lar_ref, sa0_ref, sb0_ref, kc_ref, vc_ref, gna_ref, gnb_ref,
                       rep_ref, bd_ref, mc_ref, mn_ref, wb_ref, wo_ref, sa_all_ref, sb_all_ref,
                       y_ref, sa1_ref, sb1_ref):
    del sa_all_ref, sb_all_ref
    row0 = pl.multiple_of(pl.program_id(0) * SB, SB)

    def rows(c0, c1):
        return _block_rows(z_ref, row0, c0, c1)

    def stacked(c0, c1):
        return jnp.concatenate(rows(c0, c1), axis=0)

    la_ret = [lar_ref[...]] * DEC_SEQ
    oa, ob, oc = _interleave([
        _rec_sample_stages(rows(C_GQ, C_GK), rows(C_GK, C_GV), rows(C_GV, C_GR), rows(C_GR, C_RQ),
                           rows(C_LA, C_ZG), sa0_ref, sa1_ref, gna_ref, rep_ref, bd_ref),
        _rec_sample_stages(rows(C_RQ, C_RK), rows(C_RK, C_RV), rows(C_RV, C_RG), rows(C_RG, C_SQ),
                           la_ret, sb0_ref, sb1_ref, gnb_ref, rep_ref, bd_ref),
        _swa_sample_stages(sink_ref, stacked(C_SQ, C_SK), stacked(C_SK, C_SV), stacked(C_SV, C_LA),
                           kc_ref, vc_ref, mc_ref, mn_ref)])
    mixed = sum(jax.nn.sigmoid(stacked(C_ZG + D_MODEL * j, C_ZG + D_MODEL * (j + 1)))
                * _dot(o.astype(BF16), wb_ref[512 * j:512 * (j + 1), :])
                for j, o in enumerate((oa, ob, oc)))
    out = _dot(mixed.astype(BF16), wo_ref[...])
    for t in range(DEC_SEQ):
        r = pl.ds(t * DEC_BATCH + row0, SB)
        y_ref[r, :] = x_ref[r, :] + out[t * SB:(t + 1) * SB, :]


def _mix_sample(x, z, sink, la_ret, state_gla, state_ret, new_gla, new_ret, cache_k, cache_v, layer, gna, gnb,
                consts, wb, wo):
    n = x.shape[0]
    rep, bd, mc, mn = consts
    whole = lambda w: pl.BlockSpec((n, w), lambda j: (0, 0), pipeline_mode=pl.Buffered(1))
    st_blk = pl.BlockSpec((None, SB, GLA_HEADS, GLA_DK, GLA_DV), lambda j: (layer, j, 0, 0, 0))
    cache = pl.BlockSpec((None, SB, SWA_KV_HEADS, SWA_HEAD_DIM, WINDOW), lambda j: (layer, j, 0, 0, 0))
    st_shape = jax.ShapeDtypeStruct(state_gla.shape, F32)
    untouched = pl.BlockSpec(memory_space=pl.ANY)
    operands = (sink, x, z, la_ret, state_gla, state_ret, cache_k, cache_v, gna, gnb, rep, bd, mc, mn, wb, wo,
                new_gla, new_ret)
    return pl.pallas_call(
        _mix_sample_kernel,
        grid=(DEC_BATCH // SB,),
        in_specs=[pl.BlockSpec(memory_space=pltpu.SMEM), whole(D_MODEL), whole(Z_W),
                  _const_spec(la_ret.shape), st_blk, st_blk, cache, cache,
                  _const_spec((1, GLA_W)), _const_spec((1, RET_W)),
                  _const_spec(rep.shape), _const_spec(bd.shape), _const_spec(mc.shape), _const_spec(mn.shape),
                  _layer_spec((GLA_W + RET_W + SWA_W, D_MODEL), layer),
                  _layer_spec((D_MODEL, D_MODEL), layer), untouched, untouched],
        out_specs=[pl.BlockSpec((n, D_MODEL), lambda j: (0, 0)), st_blk, st_blk],
        out_shape=[jax.ShapeDtypeStruct((n, D_MODEL), F32), st_shape, st_shape],
        input_output_aliases={len(operands) - 2: 1, len(operands) - 1: 2},
        compiler_params=_params(("arbitrary",)),
        name="mix_sample",
    )(*operands)


def _window_place_matrix():
    r = np.arange(DEC_SEQ * SB)
    c = np.arange(SB * WINDOW)
    return ((r[:, None] % SB == c[None, :] // WINDOW)
            & (c[None, :] % WINDOW == WINDOW - DEC_SEQ + r[:, None] // SB)).astype(np.float32)


def _window_kernel(kv_ref, kc_ref, vc_ref, place_ref, ko_ref, vo_ref):
    row0 = pl.multiple_of(pl.program_id(1) * SB, SB)
    new = jnp.concatenate(_block_rows(kv_ref, row0, 0, 2 * LANES), axis=0)
    hi = new.astype(BF16)
    rest = new - hi.astype(F32)
    mid = rest.astype(BF16)
    lo = (rest - mid.astype(F32)).astype(BF16)
    place = place_ref[...]
    keep = lax.broadcasted_iota(jnp.int32, (SWA_HEAD_DIM, WINDOW), 1) < WINDOW - DEC_SEQ
    for c_ref, o_ref, c0 in ((kc_ref, ko_ref, 0), (vc_ref, vo_ref, LANES)):
        for g in range(SWA_KV_HEADS):
            cols = slice(c0 + SWA_HEAD_DIM * g, c0 + SWA_HEAD_DIM * (g + 1))
            placed = (_dot_tn(hi[:, cols], place) + _dot_tn(mid[:, cols], place)
                      + _dot_tn(lo[:, cols], place))
            for b in range(SB):
                shifted = pltpu.roll(c_ref[b, g], WINDOW - DEC_SEQ, 1)
                o_ref[b, g] = jnp.where(keep, shifted, placed[:, WINDOW * b:WINDOW * (b + 1)])


def _window_update(kv_new, cache_kt, cache_vt, place):
    blk = pl.BlockSpec((None, SB, SWA_KV_HEADS, SWA_HEAD_DIM, WINDOW), lambda l, j: (l, j, 0, 0, 0))
    return pl.pallas_call(
        _window_kernel,
        grid=(DEPTH, DEC_BATCH // SB),
        in_specs=[pl.BlockSpec((None,) + kv_new.shape[1:], lambda l, j: (l, 0, 0)), blk, blk,
                  _const_spec(place.shape)],
        out_specs=[blk, blk],
        out_shape=[jax.ShapeDtypeStruct(cache_kt.shape, F32), jax.ShapeDtypeStruct(cache_vt.shape, F32)],
        compiler_params=_params(("arbitrary", "arbitrary")),
        name="window_update",
    )(kv_new, cache_kt, cache_vt, place)


MXU_N = 256
FF_CHUNKS = ((0, 6 * MXU_N), (6 * MXU_N, D_FF))


def _ffn_tail(x1, pe_ref, gp_ref, wpg_ref, wpp_ref, gf_ref, y_ref, final):
    hp = _rms(x1, gp_ref[...]).astype(BF16)
    x2 = x1 + jax.nn.sigmoid(_dot(hp, wpg_ref[...])) * _dot(pe_ref[...].astype(BF16), wpp_ref[...])
    y_ref[...] = _rms(x2, gf_ref[...]) if final else x2


def _ffn_stages(x, carry, first, gn_ref, wi_ref, cw_ref, cb_ref, wd_ref, keep):
    tm = x.shape[0]
    hb = _rms(x, gn_ref[...]).astype(BF16)
    row = lax.broadcasted_iota(jnp.int32, (tm, 1), 0)
    acc = x
    for c0, c1 in FF_CHUNKS:
        cs = slice(c0, c1)
        a = _dot(hb, wi_ref[:, c0:c1])
        bb = _dot(hb, wi_ref[:, D_FF + c0:D_FF + c1])
        yield
        p0 = carry[6:7, cs] * first
        p1 = carry[7:8, cs] * first
        a1 = jnp.where(row == 0, p1, pltpu.roll(a, 1, 0))
        a2 = jnp.where(row == 0, p0, jnp.where(row == 1, p1, pltpu.roll(a, 2, 0)))
        conv = cb_ref[:, cs] + cw_ref[0:1, cs] * a2 + cw_ref[1:2, cs] * a1 + cw_ref[2:3, cs] * a
        act = (jax.nn.gelu(conv) * bb).astype(BF16)
        keep.append(a[tm - 8:tm, :])
        yield
        acc = acc + _dot(act, wd_ref[c0:c1, :])
        yield
    return acc


def _ffn_prompt_kernel(x_ref, pe_ref, gn_ref, wi_ref, cw_ref, cb_ref, wd_ref, gp_ref, wpg_ref, wpp_ref,
                       gf_ref, y_ref, tail_ref, carry_scr, *, final, nt):
    i = pl.program_id(1)

    @pl.when(i == 0)
    def _():
        carry_scr[...] = jnp.zeros_like(carry_scr)

    keep = []
    (acc,) = _interleave([_ffn_stages(x_ref[...], carry_scr[...], 1.0, gn_ref, wi_ref, cw_ref, cb_ref, wd_ref, keep)])
    _ffn_tail(acc, pe_ref, gp_ref, wpg_ref, wpp_ref, gf_ref, y_ref, final)
    last_rows = jnp.concatenate(keep, axis=1)
    tail_ref[...] = last_rows

    @pl.when(i < nt - 1)
    def _():
        carry_scr[...] = last_rows


def _ffn_sample_kernel(x_ref, pe_ref, st_ref, gn_ref, wi_ref, cw_ref, cb_ref, wd_ref, gp_ref, wpg_ref,
                       wpp_ref, gf_ref, y_ref, tail_ref, *, final):
    ns = DEC_BATCH
    x = x_ref[...]
    hb = _rms(x, gn_ref[...]).astype(BF16)
    acc = x
    for c0, c1 in FF_CHUNKS:
        cs = slice(c0, c1)
        a = _dot(hb, wi_ref[:, c0:c1])
        bb = _dot(hb, wi_ref[:, D_FF + c0:D_FF + c1])
        st0 = st_ref[0, :, cs]
        st1 = st_ref[1, :, cs]
        a1 = jnp.concatenate([st1, a[0:3 * ns]], axis=0)
        a2 = jnp.concatenate([st0, st1, a[0:2 * ns]], axis=0)
        conv = cb_ref[:, cs] + cw_ref[0:1, cs] * a2 + cw_ref[1:2, cs] * a1 + cw_ref[2:3, cs] * a
        act = (jax.nn.gelu(conv) * bb).astype(BF16)
        acc = acc + _dot(act, wd_ref[c0:c1, :])
        tail_ref[:, cs] = a[2 * ns:4 * ns, :]
    _ffn_tail(acc, pe_ref, gp_ref, wpg_ref, wpp_ref, gf_ref, y_ref, final)


def _ffn_weight_specs(layer):
    return [_layer_spec((1, D_MODEL), layer),
            _layer_spec((D_MODEL, 2 * D_FF), layer),
            _layer_spec((CONV_W, D_FF), layer),
            _layer_spec((1, D_FF), layer),
            _layer_spec((D_FF, D_MODEL), layer),
            _layer_spec((1, D_MODEL), layer),
            _layer_spec((D_MODEL, D_MODEL), layer),
            _layer_spec((D_PLE, D_MODEL), layer),
            _const_spec((1, D_MODEL))]


def _ffn_prompt(x, pe, layer, weights, nb, t, tm, final):
    nt = t // tm
    tok = lambda w: pl.BlockSpec((tm, w), lambda b, i: (b * nt + i, 0))
    return pl.pallas_call(
        functools.partial(_ffn_prompt_kernel, final=final, nt=nt),
        grid=(nb, nt),
        in_specs=[tok(D_MODEL),
                  pl.BlockSpec((None, tm, D_PLE), lambda b, i: (layer, b * nt + i, 0))]
        + _ffn_weight_specs(layer),
        out_specs=[tok(D_MODEL), pl.BlockSpec((8, D_FF), lambda b, i: (b, 0))],
        out_shape=[jax.ShapeDtypeStruct((nb * t, D_MODEL), F32),
                   jax.ShapeDtypeStruct((nb * 8, D_FF), F32)],
        scratch_shapes=[pltpu.VMEM((8, D_FF), F32)],
        compiler_params=_params(("arbitrary", "arbitrary")),
        name="ffn_prompt",
    )(x, pe, *weights)


def _ffn_sample(x, pe, st, layer, weights, final):
    n = x.shape[0]
    return pl.pallas_call(
        functools.partial(_ffn_sample_kernel, final=final),
        grid=(1,),
        in_specs=[pl.BlockSpec((n, D_MODEL), lambda i: (0, 0)),
                  pl.BlockSpec((None, n, D_PLE), lambda i: (layer, 0, 0)),
                  pl.BlockSpec((None, CONV_W - 1, DEC_BATCH, D_FF), lambda i: (layer, 0, 0, 0))]
        + _ffn_weight_specs(layer),
        out_specs=[pl.BlockSpec((n, D_MODEL), lambda i: (0, 0)),
                   pl.BlockSpec((n // 2, D_FF), lambda i: (0, 0))],
        out_shape=[jax.ShapeDtypeStruct((n, D_MODEL), F32),
                   jax.ShapeDtypeStruct((n // 2, D_FF), F32)],
        compiler_params=_params(("arbitrary",)),
        name="ffn_sample",
    )(x, pe, st, *weights)


def _rope_tables(pos, inv_freq):
    half = inv_freq.shape[0]
    ang = pos.astype(np.float64)[:, None] * inv_freq[None, :]
    c, s = np.cos(ang), np.sin(ang)
    rest = SWA_HEAD_DIM - 2 * half
    cos64 = np.concatenate([c, c, np.ones((pos.shape[0], rest))], axis=1)
    sin64 = np.concatenate([-s, s, np.zeros((pos.shape[0], rest))], axis=1)
    return (jnp.asarray(np.concatenate([cos64, cos64], axis=1), F32),
            jnp.asarray(np.concatenate([sin64, sin64], axis=1), F32))


def _pack_w_in(w_in):
    ga0 = sum(IN_SIZES[:4])
    ga = jnp.pad(w_in[:, :, ga0:ga0 + GLA_RANK], ((0, 0), (0, 0), (0, LANES - GLA_RANK)))
    return jnp.concatenate([w_in[:, :, :ga0], w_in[:, :, ga0 + GLA_RANK:], ga], axis=2).astype(BF16)


def kernel(x_prompt, x_sample, state_gla, state_ret, cache_swa_k, cache_swa_v, state_conv, p_prompt,
           p_sample, norm_mix, w_in, w_gla_a, b_gla_a, gla_norm, ret_norm, swa_sink, w_branch, w_out,
           norm_ffn, w_ffn_in, conv_w, conv_b, w_ffn_out, norm_ple, w_ple_gate, w_ple_proj, norm_final):
    nb, t, _ = x_prompt.shape
    ns, ts, _ = x_sample.shape
    n_s = ns * ts

    w_main = _pack_w_in(w_in)
    wa = jnp.pad(w_gla_a, ((0, 0), (0, LANES - GLA_RANK), (0, 0))).astype(BF16)
    ba = b_gla_a[:, None, :]
    wb = w_branch.astype(BF16)
    wo = w_out.astype(BF16)
    ffn_weights = (norm_ffn[:, None, :], w_ffn_in.astype(BF16), conv_w, conv_b[:, None, :],
                   w_ffn_out.astype(BF16), norm_ple[:, None, :], w_ple_gate.astype(BF16),
                   w_ple_proj.astype(BF16), norm_final[None, :])
    g_mix = norm_mix[:, None, :]
    gn_gla = jnp.tile(gla_norm, (1, GLA_HEADS))[:, None, :]
    gn_ret = ret_norm.reshape(DEPTH, 1, RET_W)

    ret_freq = 1.0 / (RET_THETA ** np.linspace(0.0, 1.0, RET_DK // 2))
    swa_freq = 1.0 / (ROPE_THETA ** (np.arange(0, ROPE_DIM, 2, dtype=np.float64) / ROPE_DIM))
    pos_p = np.arange(t)
    pos_s = PAST_LEN + np.arange(n_s) // ns
    tabs_p = _rope_tables(pos_p, ret_freq) + _rope_tables(pos_p, swa_freq)
    tabs_s = _rope_tables(pos_s, ret_freq) + _rope_tables(pos_s, swa_freq)

    gm_np, mk_np = _gla_constants()
    d_np, qdec_np, kdec_np, adec_np = _ret_constants()
    hk_np, hv_np, hs_np = _head_masks()
    mix_consts = (jnp.asarray(gm_np, BF16), jnp.asarray(mk_np), jnp.asarray(d_np), jnp.asarray(qdec_np),
                  jnp.asarray(kdec_np), jnp.asarray(adec_np), jnp.asarray(hk_np, BF16), jnp.asarray(hv_np, BF16),
                  jnp.asarray(hs_np))
    log_gamma = np.log1p(-np.exp2(-5.0 - np.arange(RET_HEADS, dtype=np.float64)))
    la_ret = jnp.asarray(np.broadcast_to(np.repeat(log_gamma, RET_DK)[None, :], (SB, RET_HEADS * RET_DK)), F32)
    mc_np, mn_np = _swa_sample_masks()
    rep_np, bd_np = _rec_sample_constants()
    sample_consts = (jnp.asarray(rep_np, BF16), jnp.asarray(bd_np), jnp.asarray(mc_np), jnp.asarray(mn_np))

    xp = x_prompt.reshape(nb * t, D_MODEL)
    xs = x_sample.transpose(1, 0, 2).reshape(n_s, D_MODEL)
    pe_p = p_prompt.reshape(DEPTH, nb * t, D_PLE)
    pe_s = p_sample.transpose(0, 2, 1, 3).reshape(DEPTH, n_s, D_PLE)
    conv_st = state_conv.transpose(0, 2, 1, 3)
    ck = cache_swa_k.transpose(0, 1, 3, 4, 2)
    cv = cache_swa_v.transpose(0, 1, 3, 4, 2)

    outs = {k_: [] for k_ in ("gla_p", "ret_p", "kv_p", "kv_s", "conv_p", "conv_s")}
    gla_s = jnp.zeros(state_gla.shape, F32)
    ret_s = jnp.zeros(state_ret.shape, F32)
    for l in range(DEPTH):
        final = l == DEPTH - 1
        sink = swa_sink[l]

        xp3, st_a, st_b, kv_tail = _mix_prompt(xp.reshape(nb, t, D_MODEL), sink, l, g_mix, w_main, wa, ba, tabs_p,
                                               mix_consts, gn_gla[l], gn_ret[l], wb, wo)
        xp, tail = _ffn_prompt(xp3.reshape(nb * t, D_MODEL), pe_p, l, ffn_weights, nb, t, 256, final)
        outs["gla_p"].append(st_a)
        outs["ret_p"].append(st_b)
        outs["kv_p"].append(kv_tail)
        outs["conv_p"].append(tail.reshape(nb, 8, D_FF)[:, 8 - (CONV_W - 1):])

        zs = _in_proj(xs, l, g_mix, w_main, wa, ba, tabs_s, 256)
        xs, gla_s, ret_s = _mix_sample(xs, zs, sink, la_ret, state_gla, state_ret, gla_s, ret_s, ck, cv, l,
                                       gn_gla[l], gn_ret[l], sample_consts, wb, wo)
        xs, tail_s = _ffn_sample(xs, pe_s, conv_st, l, ffn_weights, final)
        outs["kv_s"].append(zs[:, C_SK:C_LA])
        outs["conv_s"].append(tail_s.reshape(CONV_W - 1, ns, D_FF).transpose(1, 0, 2))

    y_prompt = xp.reshape(nb, t, D_MODEL)
    y_sample = xs.reshape(ts, ns, D_MODEL).transpose(1, 0, 2)
    st = {k_: jnp.stack(v_) for k_, v_ in outs.items()}

    def kv_heads(a):
        return a.reshape(a.shape[:-1] + (SWA_KV_HEADS, SWA_HEAD_DIM))

    k_p = kv_heads(st["kv_p"][..., :LANES])
    v_p = kv_heads(st["kv_p"][..., LANES:])
    k_st, v_st = _window_update(st["kv_s"], ck, cv, jnp.asarray(_window_place_matrix(), BF16))
    k_s = k_st.transpose(0, 1, 4, 2, 3)
    v_s = v_st.transpose(0, 1, 4, 2, 3)
    return (y_prompt, y_sample, st["gla_p"], gla_s, st["ret_p"], ret_s,
            k_p, k_s, v_p, v_s, st["conv_p"], st["conv_s"])
```

```python
import functools

import numpy as np
import jax
import jax.numpy as jnp
from jax import lax
from jax.experimental import pallas as pl
from jax.experimental.pallas import tpu as pltpu

F32 = jnp.float32
BF16 = jnp.bfloat16

D_MODEL = 1024
BATCH = 2
SEQ = 8192
DEPTH = 4
DEC_BATCH = 128
DEC_SEQ = 4
PAST_LEN = 8192
D_PLE = 256
GLA_HEADS = 4
GLA_DK = 64
GLA_DV = 128
GLA_RANK = 16
GLA_TAU = 16.0
RET_HEADS = 4
RET_DK = 64
RET_DV = 128
RET_THETA = 10000.0
SWA_Q_HEADS = 8
SWA_KV_HEADS = 2
SWA_HEAD_DIM = 64
SWA_GROUP = SWA_Q_HEADS // SWA_KV_HEADS
WINDOW = 128
ROPE_THETA = 500000.0
ROPE_DIM = SWA_HEAD_DIM // 4
D_FF = 2816
CONV_W = 3
N_BRANCH = 3
EPS = 1e-6
NEG_INF = -1e30

GLA_W = GLA_HEADS * GLA_DV
RET_W = RET_HEADS * RET_DV
SWA_W = SWA_Q_HEADS * SWA_HEAD_DIM
IN_SIZES = (GLA_HEADS * GLA_DK, GLA_HEADS * GLA_DK, GLA_W, GLA_W, GLA_RANK,
            RET_HEADS * RET_DK, RET_HEADS * RET_DK, RET_W, RET_W,
            SWA_W, SWA_KV_HEADS * SWA_HEAD_DIM, SWA_KV_HEADS * SWA_HEAD_DIM,
            N_BRANCH * D_MODEL)

LANES = 128
VMEM_LIMIT = 52 * 1024 * 1024

C_GQ = 0
C_GK = 256
C_GV = 512
C_GR = 1024
C_RQ = 1536
C_RK = 1792
C_RV = 2048
C_RG = 2560
C_SQ = 3072
C_SK = 3584
C_SV = 3712
C_LA = 3840
C_ZG = 4096
Z_W = 7168
W_ZG = 3840
W_GA = 6912
W_MAIN = 7040

TT = 128
N_LEVELS = 7
SB = 16


def _dot(a, b):
    return jnp.dot(a, b, preferred_element_type=F32)


def _dot_nt(a, b):
    return lax.dot_general(a, b, (((1,), (1,)), ((), ())), preferred_element_type=F32)


def _dot_tn(a, b):
    return lax.dot_general(a, b, (((0,), (0,)), ((), ())), preferred_element_type=F32)


def _rms(x, g):
    return x * lax.rsqrt(jnp.mean(x * x, axis=-1, keepdims=True) + EPS) * g


def _layer_spec(shape, layer):
    nd = len(shape)
    return pl.BlockSpec((None,) + tuple(shape), lambda *_: (layer,) + (0,) * nd,
                        pipeline_mode=pl.Buffered(1))


def _const_spec(shape):
    nd = len(shape)
    return pl.BlockSpec(tuple(shape), lambda *_: (0,) * nd, pipeline_mode=pl.Buffered(1))


def _params(sem):
    return pltpu.CompilerParams(dimension_semantics=sem, vmem_limit_bytes=VMEM_LIMIT)


def _rope_block(x, cos, sin_s, first):
    half_mask, half = first
    xr = jnp.where(half_mask, pltpu.roll(x, LANES - half, 1), pltpu.roll(x, half, 1))
    return x * cos + xr * sin_s


def _gate_logits(hb, w_ref, j):
    return _dot(hb, w_ref[:, W_ZG + D_MODEL * j:W_ZG + D_MODEL * (j + 1)])


def _in_stages(x, g_ref, w_ref, wa_ref, ba_ref, tabs, z_ref, with_gates):
    hb = _rms(x, g_ref[...]).astype(BF16)

    def mm(c0, c1):
        return _dot(hb, w_ref[:, c0:c1])

    lane = lax.broadcasted_iota(jnp.int32, (1, LANES), 1) % 64
    ret_first = (lane < RET_DK // 2, RET_DK // 2)
    swa_first = (lane < ROPE_DIM // 2, ROPE_DIM // 2)
    rc, rs, sc, ss = tabs

    ga = mm(W_GA, W_MAIN)
    xa = _dot(ga.astype(BF16), wa_ref[...]) + ba_ref[...]
    log_sig = jnp.minimum(xa, 0.0) - jnp.log1p(jnp.exp(-jnp.abs(xa)))
    z_ref[:, C_LA:C_ZG] = log_sig * (1.0 / GLA_TAU)
    z_ref[:, C_GQ:C_GK] = mm(C_GQ, C_GK) * (GLA_DK ** -0.5)
    yield
    z_ref[:, C_GK:C_RQ] = mm(C_GK, C_RQ)
    yield
    rq = mm(C_RQ, C_RK)
    rk = mm(C_RK, C_RV)
    for j in range(2):
        sl = slice(LANES * j, LANES * (j + 1))
        z_ref[:, C_RQ + LANES * j:C_RQ + LANES * (j + 1)] = _rope_block(rq[:, sl], rc, rs, ret_first)
        z_ref[:, C_RK + LANES * j:C_RK + LANES * (j + 1)] = (
            _rope_block(rk[:, sl], rc, rs, ret_first) * (RET_DK ** -0.5))
    yield
    z_ref[:, C_RV:C_SQ] = mm(C_RV, C_SQ)
    yield
    sqkv = mm(C_SQ, C_LA)
    for j in range(5):
        sl = slice(LANES * j, LANES * (j + 1))
        z_ref[:, C_SQ + LANES * j:C_SQ + LANES * (j + 1)] = _rope_block(sqkv[:, sl], sc, ss, swa_first)
    z_ref[:, C_SV:C_LA] = sqkv[:, C_SV - C_SQ:]
    yield
    if with_gates:
        for j in range(N_BRANCH):
            z_ref[:, C_ZG + D_MODEL * j:C_ZG + D_MODEL * (j + 1)] = _gate_logits(hb, w_ref, j)
            yield


def _in_kernel(x_ref, g_ref, w_ref, wa_ref, ba_ref, rc_ref, rs_ref, sc_ref, ss_ref, z_ref):
    tabs = (rc_ref[...], rs_ref[...], sc_ref[...], ss_ref[...])
    for _ in _in_stages(x_ref[...], g_ref, w_ref, wa_ref, ba_ref, tabs, z_ref, True):
        pass


def _in_proj(x, layer, g_mix, w_main, wa, ba, tabs, tm):
    n = x.shape[0]
    rc, rs, sc, ss = tabs
    nt = rc.shape[0] // tm
    tab = pl.BlockSpec((tm, LANES), lambda i: (i % nt, 0))
    return pl.pallas_call(
        _in_kernel,
        grid=(n // tm,),
        in_specs=[pl.BlockSpec((tm, D_MODEL), lambda i: (i, 0)),
                  _layer_spec((1, D_MODEL), layer),
                  _layer_spec((D_MODEL, W_MAIN), layer),
                  _layer_spec((LANES, 256), layer),
                  _layer_spec((1, 256), layer),
                  tab, tab, tab, tab],
        out_specs=pl.BlockSpec((tm, Z_W), lambda i: (i, 0)),
        out_shape=jax.ShapeDtypeStruct((n, Z_W), F32),
        compiler_params=_params(("arbitrary",)),
        name="in_proj",
    )(x, g_mix, w_main, wa, ba, rc, rs, sc, ss)


def _gla_constants():
    t = np.arange(TT)
    g = np.zeros((2 + N_LEVELS, TT, TT), np.float32)
    g[0] = (t[None, :] <= t[:, None])
    g[1] = (t[None, :] > t[:, None])
    m = np.zeros((1 + N_LEVELS, TT, TT), np.float32)
    m[0] = np.eye(TT)
    for lv in range(1, N_LEVELS + 1):
        bs, hf = 2 ** lv, 2 ** (lv - 1)
        bd = (t // bs) * bs + hf - 1
        upper = (t % bs) >= hf
        u = t[None, :]
        g[1 + lv] = np.where(upper[:, None], (u > bd[:, None]) & (u <= t[:, None]),
                             (u > t[:, None]) & (u <= bd[:, None]))
        same = (t[:, None] // bs) == (t[None, :] // bs)
        m[lv] = same & upper[:, None] & (~upper)[None, :]
    return g.reshape(-1, TT), np.tile(m, (1, 1, GLA_HEADS))


def _head_masks():
    hk = np.arange(GLA_HEADS * TT)[:, None] // TT == np.arange(GLA_HEADS * GLA_DK)[None, :] // GLA_DK
    hv = np.arange(GLA_HEADS * TT)[:, None] // TT == np.arange(GLA_W)[None, :] // GLA_DV
    hs = np.arange(GLA_HEADS * GLA_DK)[:, None] // GLA_DK == np.arange(GLA_W)[None, :] // GLA_DV
    return hk.astype(np.float32), hv.astype(np.float32), hs.astype(np.float32)


def _heads_blockdiag(x, mask):
    return jnp.concatenate([x] * GLA_HEADS, axis=0) * mask


def _col_vector(row):
    n = row.shape[1]
    eye = lax.broadcasted_iota(jnp.int32, (n, n), 0) == lax.broadcasted_iota(jnp.int32, (n, n), 1)
    return jnp.sum(jnp.where(eye, row, 0.0), axis=1, keepdims=True)


def _norm_gate(o, gn, gate):
    outs = [_rms(o[:, GLA_DV * h:GLA_DV * (h + 1)], gn[:, GLA_DV * h:GLA_DV * (h + 1)]) for h in range(GLA_HEADS)]
    return jnp.concatenate(outs, axis=1) * (gate * jax.nn.sigmoid(gate))


def _gla_tile(z_ref, gm_ref, mk_ref, hk_ref, hv_ref, hs_ref, gn_ref, st_ref):
    la = z_ref[:, C_LA:C_ZG]
    la_hi = la.astype(BF16)
    la_lo = (la - la_hi.astype(F32)).astype(BF16)
    gm = gm_ref[...]
    ex = jnp.exp(_dot(gm, la_hi) + _dot(gm, la_lo))
    q = z_ref[:, C_GQ:C_GK]
    k = z_ref[:, C_GK:C_GV]
    e_b = ex[0:TT]
    a_row = e_b[TT - 1:TT, :]
    qd = (q * e_b).astype(BF16)
    kd = (k * ex[TT:2 * TT]).astype(BF16)
    qb = q.astype(BF16)
    kb = k.astype(BF16)
    ql = [(q * ex[(1 + lv) * TT:(2 + lv) * TT]).astype(BF16) for lv in range(1, N_LEVELS + 1)]
    kl = [(k * ex[(1 + lv) * TT:(2 + lv) * TT]).astype(BF16) for lv in range(1, N_LEVELS + 1)]
    hk = hk_ref[...]
    yield
    a = mk_ref[0] * _dot_nt(qb, _heads_blockdiag(kb, hk))
    for lv in range(N_LEVELS):
        yield
        a = a + mk_ref[lv + 1] * _dot_nt(ql[lv], _heads_blockdiag(kl[lv], hk))
    yield
    v = z_ref[:, C_GV:C_GR].astype(BF16)
    s = st_ref[...]
    o = _dot(a.astype(BF16), _heads_blockdiag(v, hv_ref[...])) + _dot(qd, s.astype(BF16))
    yield
    st_ref[...] = _col_vector(a_row) * s + hs_ref[...] * _dot_tn(kd, v)
    yield
    return _norm_gate(o, gn_ref[...], z_ref[:, C_GR:C_RQ])


def _ret_gammas():
    return [1.0 - 2.0 ** (-5.0 - h) for h in range(RET_HEADS)]


def _ret_constants():
    t = np.arange(TT, dtype=np.float64)
    d = np.zeros((RET_HEADS, TT, TT), np.float64)
    qdec = np.zeros((TT, RET_HEADS * RET_DK), np.float64)
    kdec = np.zeros((TT, RET_HEADS * RET_DK), np.float64)
    adec = np.zeros((RET_HEADS * RET_DK, LANES), np.float64)
    for h, gam in enumerate(_ret_gammas()):
        diff = t[:, None] - t[None, :]
        d[h] = np.where(diff >= 0, gam ** np.maximum(diff, 0.0), 0.0)
        qdec[:, h * RET_DK:(h + 1) * RET_DK] = (gam ** (t + 1.0))[:, None]
        kdec[:, h * RET_DK:(h + 1) * RET_DK] = (gam ** (TT - 1.0 - t))[:, None]
        adec[h * RET_DK:(h + 1) * RET_DK, :] = gam ** TT
    d = np.concatenate(list(d), axis=1)
    adec = np.tile(adec, (1, RET_W // LANES))
    return d.astype(np.float32), qdec.astype(np.float32), kdec.astype(np.float32), adec.astype(np.float32)


def _ret_tile(z_ref, d_ref, qdec_ref, kdec_ref, adec_ref, hk_ref, hv_ref, hs_ref, gn_ref, st_ref):
    q = z_ref[:, C_RQ:C_RK]
    k = z_ref[:, C_RK:C_RV]
    qd = (q * qdec_ref[...]).astype(BF16)
    kd = (k * kdec_ref[...]).astype(BF16)
    a = d_ref[...] * _dot_nt(q.astype(BF16), _heads_blockdiag(k.astype(BF16), hk_ref[...]))
    yield
    v = z_ref[:, C_RV:C_RG].astype(BF16)
    s = st_ref[...]
    o = _dot(a.astype(BF16), _heads_blockdiag(v, hv_ref[...])) + _dot(qd, s.astype(BF16))
    yield
    st_ref[...] = adec_ref[...] * s + hs_ref[...] * _dot_tn(kd, v)
    yield
    return _norm_gate(o, gn_ref[...], z_ref[:, C_RG:C_SQ])


def _rec_sample_constants():
    c = np.arange(SB * GLA_DK)
    rep = np.arange(GLA_DK)[:, None] == (c[None, :] % GLA_DK)
    bd = (np.arange(DEC_SEQ * SB)[:, None] % SB) == (c[None, :] // GLA_DK)
    return rep.astype(np.float32), bd.astype(np.float32)


def _block_rows(ref, row0, c0, c1):
    return [ref[pl.ds(t * DEC_BATCH + row0, SB), c0:c1] for t in range(DEC_SEQ)]


def _rec_sample_stages(q, k, v, gate, la, s0_ref, s1_ref, gn_ref, rep_ref, bd_ref):
    b = []
    for t in range(DEC_SEQ):
        b.append(la[t] if t == 0 else b[-1] + la[t])
    qd = jnp.concatenate([q[t] * jnp.exp(b[t]) for t in range(DEC_SEQ)], axis=0).astype(BF16)
    kd = jnp.concatenate([k[t] * jnp.exp(b[-1] - b[t]) for t in range(DEC_SEQ)], axis=0).astype(BF16)
    vv = jnp.concatenate(v, axis=0).astype(BF16)
    a = jnp.exp(b[-1])
    a_hi = a.astype(BF16)
    a_r = a - a_hi.astype(F32)
    a_mid = a_r.astype(BF16)
    a_lo = (a_r - a_mid.astype(F32)).astype(BF16)
    rep = rep_ref[...]
    bd = bd_ref[...]
    ones = jnp.ones((SB, GLA_DV), BF16)

    def expand(x, mask):
        return (_dot(x, rep) * mask).astype(BF16)

    yield
    oi = []
    for h in range(GLA_HEADS):
        hs = slice(GLA_DK * h, GLA_DK * (h + 1))
        vs = slice(GLA_DV * h, GLA_DV * (h + 1))
        s0 = s0_ref[:, h].reshape(SB * GLA_DK, GLA_DV)
        oi.append(_dot(expand(qd[:, hs], bd), s0.astype(BF16)))
        a_col = (_dot_tn(expand(a_hi[:, hs], bd[0:SB]), ones) + _dot_tn(expand(a_mid[:, hs], bd[0:SB]), ones)
                 + _dot_tn(expand(a_lo[:, hs], bd[0:SB]), ones))
        s1 = a_col * s0 + _dot_tn(expand(kd[:, hs], bd), vv[:, vs])
        s1_ref[:, h] = s1.reshape(SB, GLA_DK, GLA_DV)
        yield
    oi = jnp.concatenate(oi, axis=1)

    gn = gn_ref[...]
    outs = []
    for t in range(DEC_SEQ):
        o = oi[t * SB:(t + 1) * SB, :]
        for s in range(t + 1):
            p = q[t] * k[s] * jnp.exp(b[t] - b[s])
            parts = []
            for h in range(GLA_HEADS):
                hs = slice(GLA_DK * h, GLA_DK * (h + 1))
                vs = slice(GLA_DV * h, GLA_DV * (h + 1))
                parts.append(jnp.sum(p[:, hs], axis=1, keepdims=True) * v[s][:, vs])
            o = o + jnp.concatenate(parts, axis=1)
        outs.append(_norm_gate(o, gn, gate[t]))
        yield
    return jnp.concatenate(outs, axis=0)


def _swa_tile(z_ref, sink_ref, kp_ref, vp_ref, first_col):
    kc = z_ref[:, C_SK:C_SV].astype(BF16)
    vc = z_ref[:, C_SV:C_LA].astype(BF16)
    kband = jnp.concatenate([kp_ref[...], kc], axis=0)
    vband = jnp.concatenate([vp_ref[...], vc], axis=0)
    rows = SWA_GROUP * WINDOW
    r = lax.broadcasted_iota(jnp.int32, (rows, 2 * WINDOW), 0) & (WINDOW - 1)
    c = lax.broadcasted_iota(jnp.int32, (rows, 2 * WINDOW), 1)
    valid = (c > r) & (c <= r + WINDOW) & (c >= first_col)
    groups = range(SWA_KV_HEADS)
    dsl = [slice(SWA_HEAD_DIM * g, SWA_HEAD_DIM * (g + 1)) for g in groups]
    s, sk = [], []
    for g in groups:
        heads = range(SWA_GROUP * g, SWA_GROUP * (g + 1))
        q = jnp.concatenate(
            [z_ref[:, C_SQ + SWA_HEAD_DIM * hq:C_SQ + SWA_HEAD_DIM * (hq + 1)] for hq in heads], axis=0)
        sk.append(jnp.concatenate([jnp.full((WINDOW, 1), sink_ref[hq], F32) for hq in heads], axis=0))
        s.append(_dot_nt(q.astype(BF16), kband[:, dsl[g]]) * (SWA_HEAD_DIM ** -0.5))
    yield
    m, p = [], []
    for g in groups:
        sg = jnp.where(valid, s[g], NEG_INF)
        m.append(jnp.maximum(jnp.max(sg, axis=-1, keepdims=True), sk[g]))
        p.append(jnp.exp(sg - m[g]))
    yield
    outs = []
    for g in groups:
        den = jnp.sum(p[g], axis=-1, keepdims=True) + jnp.exp(sk[g] - m[g])
        o = _dot(p[g].astype(BF16), vband[:, dsl[g]]) / den
        outs += [o[WINDOW * j:WINDOW * (j + 1)] for j in range(SWA_GROUP)]
    yield
    return jnp.concatenate(outs, axis=1), kc, vc


def _interleave(stage_fns):
    results = [None] * len(stage_fns)
    live = list(enumerate(stage_fns))
    while live:
        still = []
        for idx, gen in live:
            try:
                next(gen)
                still.append((idx, gen))
            except StopIteration as stop:
                results[idx] = stop.value
        live = still
    return results


def _merge_stages(pend_ref, x_ref, g_ref, w_ref, wb_ref, wo_ref, y_ref):
    nb = x_ref.shape[0]
    hb = _rms(x_ref[...].reshape(nb * TT, D_MODEL), g_ref[...]).astype(BF16)
    gates = []
    for j in range(N_BRANCH):
        gates.append(jax.nn.sigmoid(_gate_logits(hb, w_ref, j)))
        yield
    mixed = None
    for j in range(N_BRANCH):
        term = gates[j] * _dot(pend_ref[:, GLA_W * j:GLA_W * (j + 1)], wb_ref[GLA_W * j:GLA_W * (j + 1), :])
        mixed = term if mixed is None else mixed + term
        yield
    out = _dot(mixed.astype(BF16), wo_ref[...])
    for b in range(x_ref.shape[0]):
        y_ref[b] = x_ref[b] + out[TT * b:TT * (b + 1)]
    yield


def _mix_kernel(sink_ref, xp_ref, xn_ref, tp0, tp1, tp2, tp3, tn0, tn1, tn2, tn3, g_ref, w_ref, wa_ref, ba_ref,
                gm_ref, mk_ref, d_ref, qdec_ref, kdec_ref, adec_ref, hk_ref, hv_ref, hs_ref, gna_ref, gnb_ref,
                wb_ref, wo_ref, y_ref, sta_ref, stb_ref, kv_ref, z_scr, pend_scr, sa_scr, sb_scr, kp_scr, vp_scr,
                *, nt):
    i = pl.program_id(0)
    nb = xp_ref.shape[0]

    def rows_of(ref3):
        return ref3[...].reshape(nb * TT, ref3.shape[2])

    def tiled(tab_refs):
        return tuple(jnp.concatenate([r[...]] * nb, axis=0) for r in tab_refs)

    @pl.when(i == 0)
    def _():
        sa_scr[...] = jnp.zeros_like(sa_scr)
        sb_scr[...] = jnp.zeros_like(sb_scr)
        kp_scr[...] = jnp.zeros_like(kp_scr)
        vp_scr[...] = jnp.zeros_like(vp_scr)
        pend_scr[...] = jnp.zeros_like(pend_scr)
        for _ in _in_stages(rows_of(xp_ref), g_ref, w_ref, wa_ref, ba_ref, tiled((tp0, tp1, tp2, tp3)),
                            z_scr.at[0], False):
            pass

    slot = i % 2
    zc = z_scr.at[slot]
    first_col = jnp.where(i > 0, 0, WINDOW)
    stages = [_merge_stages(pend_scr, xp_ref, g_ref, w_ref, wb_ref, wo_ref, y_ref),
              _in_stages(rows_of(xn_ref), g_ref, w_ref, wa_ref, ba_ref, tiled((tn0, tn1, tn2, tn3)),
                         z_scr.at[1 - slot], False)]
    for b in range(nb):
        zb = zc.at[pl.ds(TT * b, TT)]
        stages += [_gla_tile(zb, gm_ref, mk_ref, hk_ref, hv_ref, hs_ref, gna_ref, sa_scr.at[b]),
                   _swa_tile(zb, sink_ref, kp_scr.at[b], vp_scr.at[b], first_col),
                   _ret_tile(zb, d_ref, qdec_ref, kdec_ref, adec_ref, hk_ref, hv_ref, hs_ref, gnb_ref, sb_scr.at[b])]
    branch = _interleave(stages)[2:]
    swa = [branch[3 * b + 1] for b in range(nb)]
    outs = [[branch[3 * b] for b in range(nb)], [branch[3 * b + 2] for b in range(nb)], [s[0] for s in swa]]
    pend = [jnp.concatenate(o, axis=0).astype(BF16) for o in outs]

    @pl.when(i < nt)
    def _():
        for j in range(N_BRANCH):
            pend_scr[:, GLA_W * j:GLA_W * (j + 1)] = pend[j]
        for b in range(nb):
            kp_scr[b] = swa[b][1]
            vp_scr[b] = swa[b][2]
        for b in range(nb):
            for h in range(GLA_HEADS):
                sta_ref[b, h] = sa_scr[b, GLA_DK * h:GLA_DK * (h + 1), GLA_DV * h:GLA_DV * (h + 1)]
                stb_ref[b, h] = sb_scr[b, RET_DK * h:RET_DK * (h + 1), RET_DV * h:RET_DV * (h + 1)]
            kv_ref[b] = zc[pl.ds(TT * b, TT), C_SK:C_LA]


MIX_VMEM_LIMIT = 60 * 1024 * 1024


def _mix_prompt(x, sink, layer, g_mix, w_main, wa, ba, tabs, consts, gna, gnb, wb, wo):
    nb, t, _ = x.shape
    nt = t // TT
    st_spec = pl.BlockSpec((nb, GLA_HEADS, GLA_DK, GLA_DV), lambda i: (0, 0, 0, 0))
    st_shape = jax.ShapeDtypeStruct((nb, GLA_HEADS, GLA_DK, GLA_DV), F32)
    nxt = lambda i: jnp.minimum(i + 1, nt - 1)
    prv = lambda i: jnp.maximum(i - 1, 0)
    tab_prv = pl.BlockSpec((TT, LANES), lambda i: (prv(i), 0))
    tab_nxt = pl.BlockSpec((TT, LANES), lambda i: (nxt(i), 0))
    return pl.pallas_call(
        functools.partial(_mix_kernel, nt=nt),
        grid=(nt + 1,),
        in_specs=[pl.BlockSpec(memory_space=pltpu.SMEM),
                  pl.BlockSpec((nb, TT, D_MODEL), lambda i: (0, prv(i), 0)),
                  pl.BlockSpec((nb, TT, D_MODEL), lambda i: (0, nxt(i), 0))]
        + [tab_prv] * 4 + [tab_nxt] * 4
        + [_layer_spec((1, D_MODEL), layer), _layer_spec((D_MODEL, W_MAIN), layer),
           _layer_spec((LANES, 256), layer), _layer_spec((1, 256), layer)]
        + [_const_spec(c.shape) for c in consts]
        + [_const_spec((1, GLA_W)), _const_spec((1, RET_W)),
           _layer_spec((GLA_W + RET_W + SWA_W, D_MODEL), layer),
           _layer_spec((D_MODEL, D_MODEL), layer)],
        out_specs=[pl.BlockSpec((nb, TT, D_MODEL), lambda i: (0, prv(i), 0)), st_spec, st_spec,
                   pl.BlockSpec((nb, WINDOW, 2 * LANES), lambda i: (0, 0, 0))],
        out_shape=[jax.ShapeDtypeStruct((nb, t, D_MODEL), F32), st_shape, st_shape,
                   jax.ShapeDtypeStruct((nb, WINDOW, 2 * LANES), F32)],
        scratch_shapes=[pltpu.VMEM((2, nb * TT, C_ZG), F32),
                        pltpu.VMEM((nb * TT, GLA_W + RET_W + SWA_W), BF16),
                        pltpu.VMEM((nb, GLA_HEADS * GLA_DK, GLA_W), F32),
                        pltpu.VMEM((nb, RET_HEADS * RET_DK, RET_W), F32),
                        pltpu.VMEM((nb, WINDOW, 128), BF16), pltpu.VMEM((nb, WINDOW, 128), BF16)],
        compiler_params=pltpu.CompilerParams(dimension_semantics=("arbitrary",), vmem_limit_bytes=MIX_VMEM_LIMIT),
        name="mix_prompt",
    )(sink, x, x, *tabs, *tabs, g_mix, w_main, wa, ba, *consts, gna, gnb, wb, wo)


def _swa_sample_masks():
    r = np.arange(DEC_SEQ * SB)
    rt, rb = r // SB, r % SB
    c = np.arange(SB * WINDOW)
    cb, cj = c // WINDOW, c % WINDOW
    m_cache = (rb[:, None] == cb[None, :]) & (cj[None, :] > rt[:, None])
    m_new = (rb[:, None] == rb[None, :]) & (rt[None, :] <= rt[:, None])
    return m_cache.astype(np.float32), m_new.astype(np.float32)


def _swa_sample_stages(sink_ref, q, kn, vn, kc_ref, vc_ref, mc_ref, mn_ref):
    q = q.astype(BF16)
    kn = kn.astype(BF16)
    vn = vn.astype(BF16)
    kct = [jnp.concatenate([kc_ref[b, g] for b in range(SB)], axis=1).astype(BF16) for g in range(SWA_KV_HEADS)]
    vct = [jnp.concatenate([vc_ref[b, g] for b in range(SB)], axis=1).astype(BF16) for g in range(SWA_KV_HEADS)]
    ok_c = mc_ref[...] > 0.0
    ok_n = mn_ref[...] > 0.0
    outs = []
    for hq in range(SWA_Q_HEADS):
        g = hq // SWA_GROUP
        ds_ = slice(SWA_HEAD_DIM * g, SWA_HEAD_DIM * (g + 1))
        qh = q[:, SWA_HEAD_DIM * hq:SWA_HEAD_DIM * (hq + 1)]
        s1 = jnp.where(ok_c, _dot(qh, kct[g]) * (SWA_HEAD_DIM ** -0.5), NEG_INF)
        s2 = jnp.where(ok_n, _dot_nt(qh, kn[:, ds_]) * (SWA_HEAD_DIM ** -0.5), NEG_INF)
        sk = sink_ref[hq]
        m = jnp.maximum(jnp.maximum(jnp.max(s1, axis=-1, keepdims=True),
                                    jnp.max(s2, axis=-1, keepdims=True)), sk)
        p1 = jnp.exp(s1 - m)
        p2 = jnp.exp(s2 - m)
        den = jnp.sum(p1, axis=-1, keepdims=True) + jnp.sum(p2, axis=-1, keepdims=True) + jnp.exp(sk - m)
        outs.append((_dot_nt(p1.astype(BF16), vct[g]) + _dot(p2.astype(BF16), vn[:, ds_])) / den)
        if hq % 2 == 1:
            yield
    return jnp.concatenate(outs, axis=1)


def _mix_sample_kernel(sink_ref, x_ref, z_ref, lar_ref, sa0_ref, sb0_ref, kc_ref, vc_ref, gna_ref, gnb_ref,
                       rep_ref, bd_ref, mc_ref, mn_ref, wb_ref, wo_ref, sa_all_ref, sb_all_ref,
                       y_ref, sa1_ref, sb1_ref):
    del sa_all_ref, sb_all_ref
    row0 = pl.multiple_of(pl.program_id(0) * SB, SB)

    def rows(c0, c1):
        return _block_rows(z_ref, row0, c0, c1)

    def stacked(c0, c1):
        return jnp.concatenate(rows(c0, c1), axis=0)

    la_ret = [lar_ref[...]] * DEC_SEQ
    oa, ob, oc = _interleave([
        _rec_sample_stages(rows(C_GQ, C_GK), rows(C_GK, C_GV), rows(C_GV, C_GR), rows(C_GR, C_RQ),
                           rows(C_LA, C_ZG), sa0_ref, sa1_ref, gna_ref, rep_ref, bd_ref),
        _rec_sample_stages(rows(C_RQ, C_RK), rows(C_RK, C_RV), rows(C_RV, C_RG), rows(C_RG, C_SQ),
                           la_ret, sb0_ref, sb1_ref, gnb_ref, rep_ref, bd_ref),
        _swa_sample_stages(sink_ref, stacked(C_SQ, C_SK), stacked(C_SK, C_SV), stacked(C_SV, C_LA),
                           kc_ref, vc_ref, mc_ref, mn_ref)])
    mixed = sum(jax.nn.sigmoid(stacked(C_ZG + D_MODEL * j, C_ZG + D_MODEL * (j + 1)))
                * _dot(o.astype(BF16), wb_ref[512 * j:512 * (j + 1), :])
                for j, o in enumerate((oa, ob, oc)))
    out = _dot(mixed.astype(BF16), wo_ref[...])
    for t in range(DEC_SEQ):
        r = pl.ds(t * DEC_BATCH + row0, SB)
        y_ref[r, :] = x_ref[r, :] + out[t * SB:(t + 1) * SB, :]


def _mix_sample(x, z, sink, la_ret, state_gla, state_ret, new_gla, new_ret, cache_k, cache_v, layer, gna, gnb,
                consts, wb, wo):
    n = x.shape[0]
    rep, bd, mc, mn = consts
    whole = lambda w: pl.BlockSpec((n, w), lambda j: (0, 0), pipeline_mode=pl.Buffered(1))
    st_blk = pl.BlockSpec((None, SB, GLA_HEADS, GLA_DK, GLA_DV), lambda j: (layer, j, 0, 0, 0))
    cache = pl.BlockSpec((None, SB, SWA_KV_HEADS, SWA_HEAD_DIM, WINDOW), lambda j: (layer, j, 0, 0, 0))
    st_shape = jax.ShapeDtypeStruct(state_gla.shape, F32)
    untouched = pl.BlockSpec(memory_space=pl.ANY)
    operands = (sink, x, z, la_ret, state_gla, state_ret, cache_k, cache_v, gna, gnb, rep, bd, mc, mn, wb, wo,
                new_gla, new_ret)
    return pl.pallas_call(
        _mix_sample_kernel,
        grid=(DEC_BATCH // SB,),
        in_specs=[pl.BlockSpec(memory_space=pltpu.SMEM), whole(D_MODEL), whole(Z_W),
                  _const_spec(la_ret.shape), st_blk, st_blk, cache, cache,
                  _const_spec((1, GLA_W)), _const_spec((1, RET_W)),
                  _const_spec(rep.shape), _const_spec(bd.shape), _const_spec(mc.shape), _const_spec(mn.shape),
                  _layer_spec((GLA_W + RET_W + SWA_W, D_MODEL), layer),
                  _layer_spec((D_MODEL, D_MODEL), layer), untouched, untouched],
        out_specs=[pl.BlockSpec((n, D_MODEL), lambda j: (0, 0)), st_blk, st_blk],
        out_shape=[jax.ShapeDtypeStruct((n, D_MODEL), F32), st_shape, st_shape],
        input_output_aliases={len(operands) - 2: 1, len(operands) - 1: 2},
        compiler_params=_params(("arbitrary",)),
        name="mix_sample",
    )(*operands)


def _window_place_matrix():
    r = np.arange(DEC_SEQ * SB)
    c = np.arange(SB * WINDOW)
    return ((r[:, None] % SB == c[None, :] // WINDOW)
            & (c[None, :] % WINDOW == WINDOW - DEC_SEQ + r[:, None] // SB)).astype(np.float32)


def _window_kernel(kv_ref, kc_ref, vc_ref, place_ref, ko_ref, vo_ref):
    row0 = pl.multiple_of(pl.program_id(1) * SB, SB)
    new = jnp.concatenate(_block_rows(kv_ref, row0, 0, 2 * LANES), axis=0)
    hi = new.astype(BF16)
    rest = new - hi.astype(F32)
    mid = rest.astype(BF16)
    lo = (rest - mid.astype(F32)).astype(BF16)
    place = place_ref[...]
    keep = lax.broadcasted_iota(jnp.int32, (SWA_HEAD_DIM, WINDOW), 1) < WINDOW - DEC_SEQ
    for c_ref, o_ref, c0 in ((kc_ref, ko_ref, 0), (vc_ref, vo_ref, LANES)):
        for g in range(SWA_KV_HEADS):
            cols = slice(c0 + SWA_HEAD_DIM * g, c0 + SWA_HEAD_DIM * (g + 1))
            placed = (_dot_tn(hi[:, cols], place) + _dot_tn(mid[:, cols], place)
                      + _dot_tn(lo[:, cols], place))
            for b in range(SB):
                shifted = pltpu.roll(c_ref[b, g], WINDOW - DEC_SEQ, 1)
                o_ref[b, g] = jnp.where(keep, shifted, placed[:, WINDOW * b:WINDOW * (b + 1)])


def _window_update(kv_new, cache_kt, cache_vt, place):
    blk = pl.BlockSpec((None, SB, SWA_KV_HEADS, SWA_HEAD_DIM, WINDOW), lambda l, j: (l, j, 0, 0, 0))
    return pl.pallas_call(
        _window_kernel,
        grid=(DEPTH, DEC_BATCH // SB),
        in_specs=[pl.BlockSpec((None,) + kv_new.shape[1:], lambda l, j: (l, 0, 0)), blk, blk,
                  _const_spec(place.shape)],
        out_specs=[blk, blk],
        out_shape=[jax.ShapeDtypeStruct(cache_kt.shape, F32), jax.ShapeDtypeStruct(cache_vt.shape, F32)],
        compiler_params=_params(("arbitrary", "arbitrary")),
        name="window_update",
    )(kv_new, cache_kt, cache_vt, place)


MXU_N = 256
FF_CHUNKS = ((0, 6 * MXU_N), (6 * MXU_N, D_FF))


def _ffn_tail(x1, pe_ref, gp_ref, wpg_ref, wpp_ref, gf_ref, y_ref, final):
    hp = _rms(x1, gp_ref[...]).astype(BF16)
    x2 = x1 + jax.nn.sigmoid(_dot(hp, wpg_ref[...])) * _dot(pe_ref[...].astype(BF16), wpp_ref[...])
    y_ref[...] = _rms(x2, gf_ref[...]) if final else x2


def _ffn_stages(x, carry, gn_ref, wi_ref, cw_ref, cb_ref, wd_ref, keep):
    tm = x.shape[0]
    hb = _rms(x, gn_ref[...]).astype(BF16)
    row = lax.broadcasted_iota(jnp.int32, (tm, 1), 0)
    acc = x
    for c0, c1 in FF_CHUNKS:
        cs = slice(c0, c1)
        a = _dot(hb, wi_ref[:, c0:c1])
        bb = _dot(hb, wi_ref[:, D_FF + c0:D_FF + c1])
        yield
        p0 = carry[6:7, cs]
        p1 = carry[7:8, cs]
        a1 = jnp.where(row == 0, p1, pltpu.roll(a, 1, 0))
        a2 = jnp.where(row == 0, p0, jnp.where(row == 1, p1, pltpu.roll(a, 2, 0)))
        conv = cb_ref[:, cs] + cw_ref[0:1, cs] * a2 + cw_ref[1:2, cs] * a1 + cw_ref[2:3, cs] * a
        act = (jax.nn.gelu(conv) * bb).astype(BF16)
        keep.append(a[tm - 8:tm, :])
        yield
        acc = acc + _dot(act, wd_ref[c0:c1, :])
        yield
    return acc


def _ffn_prompt_kernel(x_ref, pe_ref, gn_ref, wi_ref, cw_ref, cb_ref, wd_ref, gp_ref, wpg_ref, wpp_ref,
                       gf_ref, y_ref, tail_ref, carry_scr, *, final, nt):
    i = pl.program_id(1)

    @pl.when(i == 0)
    def _():
        carry_scr[...] = jnp.zeros_like(carry_scr)

    keep = []
    (acc,) = _interleave([_ffn_stages(x_ref[...], carry_scr[...], gn_ref, wi_ref, cw_ref, cb_ref, wd_ref, keep)])
    _ffn_tail(acc, pe_ref, gp_ref, wpg_ref, wpp_ref, gf_ref, y_ref, final)
    last_rows = jnp.concatenate(keep, axis=1)
    tail_ref[...] = last_rows

    @pl.when(i < nt - 1)
    def _():
        carry_scr[...] = last_rows


def _ffn_sample_kernel(x_ref, pe_ref, st_ref, gn_ref, wi_ref, cw_ref, cb_ref, wd_ref, gp_ref, wpg_ref,
                       wpp_ref, gf_ref, y_ref, tail_ref, *, final):
    ns = DEC_BATCH
    x = x_ref[...]
    hb = _rms(x, gn_ref[...]).astype(BF16)
    acc = x
    for c0, c1 in FF_CHUNKS:
        cs = slice(c0, c1)
        a = _dot(hb, wi_ref[:, c0:c1])
        bb = _dot(hb, wi_ref[:, D_FF + c0:D_FF + c1])
        st0 = st_ref[0, :, cs]
        st1 = st_ref[1, :, cs]
        a1 = jnp.concatenate([st1, a[0:3 * ns]], axis=0)
        a2 = jnp.concatenate([st0, st1, a[0:2 * ns]], axis=0)
        conv = cb_ref[:, cs] + cw_ref[0:1, cs] * a2 + cw_ref[1:2, cs] * a1 + cw_ref[2:3, cs] * a
        act = (jax.nn.gelu(conv) * bb).astype(BF16)
        acc = acc + _dot(act, wd_ref[c0:c1, :])
        tail_ref[:, cs] = a[2 * ns:4 * ns, :]
    _ffn_tail(acc, pe_ref, gp_ref, wpg_ref, wpp_ref, gf_ref, y_ref, final)


def _ffn_weight_specs(layer):
    return [_layer_spec((1, D_MODEL), layer),
            _layer_spec((D_MODEL, 2 * D_FF), layer),
            _layer_spec((CONV_W, D_FF), layer),
            _layer_spec((1, D_FF), layer),
            _layer_spec((D_FF, D_MODEL), layer),
            _layer_spec((1, D_MODEL), layer),
            _layer_spec((D_MODEL, D_MODEL), layer),
            _layer_spec((D_PLE, D_MODEL), layer),
            _const_spec((1, D_MODEL))]


def _ffn_prompt(x, pe, layer, weights, nb, t, tm, final):
    nt = t // tm
    tok = lambda w: pl.BlockSpec((tm, w), lambda b, i: (b * nt + i, 0))
    return pl.pallas_call(
        functools.partial(_ffn_prompt_kernel, final=final, nt=nt),
        grid=(nb, nt),
        in_specs=[tok(D_MODEL),
                  pl.BlockSpec((None, tm, D_PLE), lambda b, i: (layer, b * nt + i, 0))]
        + _ffn_weight_specs(layer),
        out_specs=[tok(D_MODEL), pl.BlockSpec((8, D_FF), lambda b, i: (b, 0))],
        out_shape=[jax.ShapeDtypeStruct((nb * t, D_MODEL), F32),
                   jax.ShapeDtypeStruct((nb * 8, D_FF), F32)],
        scratch_shapes=[pltpu.VMEM((8, D_FF), F32)],
        compiler_params=_params(("arbitrary", "arbitrary")),
        name="ffn_prompt",
    )(x, pe, *weights)


def _ffn_sample(x, pe, st, layer, weights, final):
    n = x.shape[0]
    return pl.pallas_call(
        functools.partial(_ffn_sample_kernel, final=final),
        grid=(1,),
        in_specs=[pl.BlockSpec((n, D_MODEL), lambda i: (0, 0)),
                  pl.BlockSpec((None, n, D_PLE), lambda i: (layer, 0, 0)),
                  pl.BlockSpec((None, CONV_W - 1, DEC_BATCH, D_FF), lambda i: (layer, 0, 0, 0))]
        + _ffn_weight_specs(layer),
        out_specs=[pl.BlockSpec((n, D_MODEL), lambda i: (0, 0)),
                   pl.BlockSpec((n // 2, D_FF), lambda i: (0, 0))],
        out_shape=[jax.ShapeDtypeStruct((n, D_MODEL), F32),
                   jax.ShapeDtypeStruct((n // 2, D_FF), F32)],
        compiler_params=_params(("arbitrary",)),
        name="ffn_sample",
    )(x, pe, st, *weights)


def _rope_tables(pos, inv_freq):
    half = inv_freq.shape[0]
    ang = pos.astype(np.float64)[:, None] * inv_freq[None, :]
    c, s = np.cos(ang), np.sin(ang)
    rest = SWA_HEAD_DIM - 2 * half
    cos64 = np.concatenate([c, c, np.ones((pos.shape[0], rest))], axis=1)
    sin64 = np.concatenate([-s, s, np.zeros((pos.shape[0], rest))], axis=1)
    return (jnp.asarray(np.concatenate([cos64, cos64], axis=1), F32),
            jnp.asarray(np.concatenate([sin64, sin64], axis=1), F32))


def _pack_w_in(w_in):
    ga0 = sum(IN_SIZES[:4])
    ga = jnp.pad(w_in[:, :, ga0:ga0 + GLA_RANK], ((0, 0), (0, 0), (0, LANES - GLA_RANK)))
    return jnp.concatenate([w_in[:, :, :ga0], w_in[:, :, ga0 + GLA_RANK:], ga], axis=2).astype(BF16)


def kernel(x_prompt, x_sample, state_gla, state_ret, cache_swa_k, cache_swa_v, state_conv, p_prompt,
           p_sample, norm_mix, w_in, w_gla_a, b_gla_a, gla_norm, ret_norm, swa_sink, w_branch, w_out,
           norm_ffn, w_ffn_in, conv_w, conv_b, w_ffn_out, norm_ple, w_ple_gate, w_ple_proj, norm_final):
    nb, t, _ = x_prompt.shape
    ns, ts, _ = x_sample.shape
    n_s = ns * ts

    w_main = _pack_w_in(w_in)
    wa = jnp.pad(w_gla_a, ((0, 0), (0, LANES - GLA_RANK), (0, 0))).astype(BF16)
    ba = b_gla_a[:, None, :]
    wb = w_branch.astype(BF16)
    wo = w_out.astype(BF16)
    ffn_weights = (norm_ffn[:, None, :], w_ffn_in.astype(BF16), conv_w, conv_b[:, None, :],
                   w_ffn_out.astype(BF16), norm_ple[:, None, :], w_ple_gate.astype(BF16),
                   w_ple_proj.astype(BF16), norm_final[None, :])
    g_mix = norm_mix[:, None, :]
    gn_gla = jnp.tile(gla_norm, (1, GLA_HEADS))[:, None, :]
    gn_ret = ret_norm.reshape(DEPTH, 1, RET_W)

    ret_freq = 1.0 / (RET_THETA ** np.linspace(0.0, 1.0, RET_DK // 2))
    swa_freq = 1.0 / (ROPE_THETA ** (np.arange(0, ROPE_DIM, 2, dtype=np.float64) / ROPE_DIM))
    pos_p = np.arange(t)
    pos_s = PAST_LEN + np.arange(n_s) // ns
    tabs_p = _rope_tables(pos_p, ret_freq) + _rope_tables(pos_p, swa_freq)
    tabs_s = _rope_tables(pos_s, ret_freq) + _rope_tables(pos_s, swa_freq)

    gm_np, mk_np = _gla_constants()
    d_np, qdec_np, kdec_np, adec_np = _ret_constants()
    hk_np, hv_np, hs_np = _head_masks()
    mix_consts = (jnp.asarray(gm_np, BF16), jnp.asarray(mk_np), jnp.asarray(d_np), jnp.asarray(qdec_np),
                  jnp.asarray(kdec_np), jnp.asarray(adec_np), jnp.asarray(hk_np, BF16), jnp.asarray(hv_np, BF16),
                  jnp.asarray(hs_np))
    log_gamma = np.log1p(-np.exp2(-5.0 - np.arange(RET_HEADS, dtype=np.float64)))
    la_ret = jnp.asarray(np.broadcast_to(np.repeat(log_gamma, RET_DK)[None, :], (SB, RET_HEADS * RET_DK)), F32)
    mc_np, mn_np = _swa_sample_masks()
    rep_np, bd_np = _rec_sample_constants()
    sample_consts = (jnp.asarray(rep_np, BF16), jnp.asarray(bd_np), jnp.asarray(mc_np), jnp.asarray(mn_np))

    xp = x_prompt.reshape(nb * t, D_MODEL)
    xs = x_sample.transpose(1, 0, 2).reshape(n_s, D_MODEL)
    pe_p = p_prompt.reshape(DEPTH, nb * t, D_PLE)
    pe_s = p_sample.transpose(0, 2, 1, 3).reshape(DEPTH, n_s, D_PLE)
    conv_st = state_conv.transpose(0, 2, 1, 3)
    ck = cache_swa_k.transpose(0, 1, 3, 4, 2)
    cv = cache_swa_v.transpose(0, 1, 3, 4, 2)

    outs = {k_: [] for k_ in ("gla_p", "ret_p", "kv_p", "kv_s", "conv_p", "conv_s")}
    gla_s = jnp.zeros(state_gla.shape, F32)
    ret_s = jnp.zeros(state_ret.shape, F32)
    for l in range(DEPTH):
        final = l == DEPTH - 1
        sink = swa_sink[l]

        xp3, st_a, st_b, kv_tail = _mix_prompt(xp.reshape(nb, t, D_MODEL), sink, l, g_mix, w_main, wa, ba, tabs_p,
                                               mix_consts, gn_gla[l], gn_ret[l], wb, wo)
        xp, tail = _ffn_prompt(xp3.reshape(nb * t, D_MODEL), pe_p, l, ffn_weights, nb, t, 256, final)
        outs["gla_p"].append(st_a)
        outs["ret_p"].append(st_b)
        outs["kv_p"].append(kv_tail)
        outs["conv_p"].append(tail.reshape(nb, 8, D_FF)[:, 8 - (CONV_W - 1):])

        zs = _in_proj(xs, l, g_mix, w_main, wa, ba, tabs_s, 256)
        xs, gla_s, ret_s = _mix_sample(xs, zs, sink, la_ret, state_gla, state_ret, gla_s, ret_s, ck, cv, l,
                                       gn_gla[l], gn_ret[l], sample_consts, wb, wo)
        xs, tail_s = _ffn_sample(xs, pe_s, conv_st, l, ffn_weights, final)
        outs["kv_s"].append(zs[:, C_SK:C_LA])
        outs["conv_s"].append(tail_s.reshape(CONV_W - 1, ns, D_FF).transpose(1, 0, 2))

    y_prompt = xp.reshape(nb, t, D_MODEL)
    y_sample = xs.reshape(ts, ns, D_MODEL).transpose(1, 0, 2)
    st = {k_: jnp.stack(v_) for k_, v_ in outs.items()}

    def kv_heads(a):
        return a.reshape(a.shape[:-1] + (SWA_KV_HEADS, SWA_HEAD_DIM))

    k_p = kv_heads(st["kv_p"][..., :LANES])
    v_p = kv_heads(st["kv_p"][..., LANES:])
    k_st, v_st = _window_update(st["kv_s"], ck, cv, jnp.asarray(_window_place_matrix(), BF16))
    k_s = k_st.transpose(0, 1, 4, 2, 3)
    v_s = v_st.transpose(0, 1, 4, 2, 3)
    return (y_prompt, y_sample, st["gla_p"], gla_s, st["ret_p"], ret_s,
            k_p, k_s, v_p, v_s, st["conv_p"], st["conv_s"])
```

```python
import functools

import numpy as np
import jax
import jax.numpy as jnp
from jax import lax
from jax.experimental import pallas as pl
from jax.experimental.pallas import tpu as pltpu

F32 = jnp.float32
BF16 = jnp.bfloat16

D_MODEL = 1024
BATCH = 2
SEQ = 8192
DEPTH = 4
DEC_BATCH = 128
DEC_SEQ = 4
PAST_LEN = 8192
D_PLE = 256
GLA_HEADS = 4
GLA_DK = 64
GLA_DV = 128
GLA_RANK = 16
GLA_TAU = 16.0
RET_HEADS = 4
RET_DK = 64
RET_DV = 128
RET_THETA = 10000.0
SWA_Q_HEADS = 8
SWA_KV_HEADS = 2
SWA_HEAD_DIM = 64
SWA_GROUP = SWA_Q_HEADS // SWA_KV_HEADS
WINDOW = 128
ROPE_THETA = 500000.0
ROPE_DIM = SWA_HEAD_DIM // 4
D_FF = 2816
CONV_W = 3
N_BRANCH = 3
EPS = 1e-6
NEG_INF = -1e30

GLA_W = GLA_HEADS * GLA_DV
RET_W = RET_HEADS * RET_DV
SWA_W = SWA_Q_HEADS * SWA_HEAD_DIM
IN_SIZES = (GLA_HEADS * GLA_DK, GLA_HEADS * GLA_DK, GLA_W, GLA_W, GLA_RANK,
            RET_HEADS * RET_DK, RET_HEADS * RET_DK, RET_W, RET_W,
            SWA_W, SWA_KV_HEADS * SWA_HEAD_DIM, SWA_KV_HEADS * SWA_HEAD_DIM,
            N_BRANCH * D_MODEL)

LANES = 128
VMEM_LIMIT = 52 * 1024 * 1024

C_GQ = 0
C_GK = 256
C_GV = 512
C_GR = 1024
C_RQ = 1536
C_RK = 1792
C_RV = 2048
C_RG = 2560
C_SQ = 3072
C_SK = 3584
C_SV = 3712
C_LA = 3840
C_ZG = 4096
Z_W = 7168
W_ZG = 3840
W_GA = 6912
W_MAIN = 7040

TT = 128
N_LEVELS = 7
SB = 16


def _dot(a, b):
    return jnp.dot(a, b, preferred_element_type=F32)


def _dot_nt(a, b):
    return lax.dot_general(a, b, (((1,), (1,)), ((), ())), preferred_element_type=F32)


def _dot_tn(a, b):
    return lax.dot_general(a, b, (((0,), (0,)), ((), ())), preferred_element_type=F32)


def _rms(x, g):
    return x * lax.rsqrt(jnp.mean(x * x, axis=-1, keepdims=True) + EPS) * g


def _layer_spec(shape, layer):
    nd = len(shape)
    return pl.BlockSpec((None,) + tuple(shape), lambda *_: (layer,) + (0,) * nd,
                        pipeline_mode=pl.Buffered(1))


def _const_spec(shape):
    nd = len(shape)
    return pl.BlockSpec(tuple(shape), lambda *_: (0,) * nd, pipeline_mode=pl.Buffered(1))


def _params(sem):
    return pltpu.CompilerParams(dimension_semantics=sem, vmem_limit_bytes=VMEM_LIMIT)


def _rope_block(x, cos, sin_s, first):
    half_mask, half = first
    xr = jnp.where(half_mask, pltpu.roll(x, LANES - half, 1), pltpu.roll(x, half, 1))
    return x * cos + xr * sin_s


def _gate_logits(hb, w_ref, j):
    return _dot(hb, w_ref[:, W_ZG + D_MODEL * j:W_ZG + D_MODEL * (j + 1)])


def _in_stages(x, g_ref, w_ref, wa_ref, ba_ref, tabs, z_ref, with_gates):
    hb = _rms(x, g_ref[...]).astype(BF16)

    def mm(c0, c1):
        return _dot(hb, w_ref[:, c0:c1])

    lane = lax.broadcasted_iota(jnp.int32, (1, LANES), 1) % 64
    ret_first = (lane < RET_DK // 2, RET_DK // 2)
    swa_first = (lane < ROPE_DIM // 2, ROPE_DIM // 2)
    rc, rs, sc, ss = tabs

    ga = mm(W_GA, W_MAIN)
    xa = _dot(ga.astype(BF16), wa_ref[...]) + ba_ref[...]
    log_sig = jnp.minimum(xa, 0.0) - jnp.log1p(jnp.exp(-jnp.abs(xa)))
    z_ref[:, C_LA:C_ZG] = log_sig * (1.0 / GLA_TAU)
    z_ref[:, C_GQ:C_GK] = mm(C_GQ, C_GK) * (GLA_DK ** -0.5)
    yield
    z_ref[:, C_GK:C_RQ] = mm(C_GK, C_RQ)
    yield
    rq = mm(C_RQ, C_RK)
    rk = mm(C_RK, C_RV)
    for j in range(2):
        sl = slice(LANES * j, LANES * (j + 1))
        z_ref[:, C_RQ + LANES * j:C_RQ + LANES * (j + 1)] = _rope_block(rq[:, sl], rc, rs, ret_first)
        z_ref[:, C_RK + LANES * j:C_RK + LANES * (j + 1)] = (
            _rope_block(rk[:, sl], rc, rs, ret_first) * (RET_DK ** -0.5))
    yield
    z_ref[:, C_RV:C_SQ] = mm(C_RV, C_SQ)
    yield
    sqkv = mm(C_SQ, C_LA)
    for j in range(5):
        sl = slice(LANES * j, LANES * (j + 1))
        z_ref[:, C_SQ + LANES * j:C_SQ + LANES * (j + 1)] = _rope_block(sqkv[:, sl], sc, ss, swa_first)
    z_ref[:, C_SV:C_LA] = sqkv[:, C_SV - C_SQ:]
    yield
    if with_gates:
        for j in range(N_BRANCH):
            z_ref[:, C_ZG + D_MODEL * j:C_ZG + D_MODEL * (j + 1)] = _gate_logits(hb, w_ref, j)
            yield


def _in_kernel(x_ref, g_ref, w_ref, wa_ref, ba_ref, rc_ref, rs_ref, sc_ref, ss_ref, z_ref):
    tabs = (rc_ref[...], rs_ref[...], sc_ref[...], ss_ref[...])
    for _ in _in_stages(x_ref[...], g_ref, w_ref, wa_ref, ba_ref, tabs, z_ref, True):
        pass


def _in_proj(x, layer, g_mix, w_main, wa, ba, tabs, tm):
    n = x.shape[0]
    rc, rs, sc, ss = tabs
    nt = rc.shape[0] // tm
    tab = pl.BlockSpec((tm, LANES), lambda i: (i % nt, 0))
    return pl.pallas_call(
        _in_kernel,
        grid=(n // tm,),
        in_specs=[pl.BlockSpec((tm, D_MODEL), lambda i: (i, 0)),
                  _layer_spec((1, D_MODEL), layer),
                  _layer_spec((D_MODEL, W_MAIN), layer),
                  _layer_spec((LANES, 256), layer),
                  _layer_spec((1, 256), layer),
                  tab, tab, tab, tab],
        out_specs=pl.BlockSpec((tm, Z_W), lambda i: (i, 0)),
        out_shape=jax.ShapeDtypeStruct((n, Z_W), F32),
        compiler_params=_params(("arbitrary",)),
        name="in_proj",
    )(x, g_mix, w_main, wa, ba, rc, rs, sc, ss)


def _gla_constants():
    t = np.arange(TT)
    g = np.zeros((2 + N_LEVELS, TT, TT), np.float32)
    g[0] = (t[None, :] <= t[:, None])
    g[1] = (t[None, :] > t[:, None])
    m = np.zeros((1 + N_LEVELS, TT, TT), np.float32)
    m[0] = np.eye(TT)
    for lv in range(1, N_LEVELS + 1):
        bs, hf = 2 ** lv, 2 ** (lv - 1)
        bd = (t // bs) * bs + hf - 1
        upper = (t % bs) >= hf
        u = t[None, :]
        g[1 + lv] = np.where(upper[:, None], (u > bd[:, None]) & (u <= t[:, None]),
                             (u > t[:, None]) & (u <= bd[:, None]))
        same = (t[:, None] // bs) == (t[None, :] // bs)
        m[lv] = same & upper[:, None] & (~upper)[None, :]
    return g.reshape(-1, TT), np.tile(m, (1, 1, GLA_HEADS))


def _head_masks():
    hk = np.arange(GLA_HEADS * TT)[:, None] // TT == np.arange(GLA_HEADS * GLA_DK)[None, :] // GLA_DK
    hv = np.arange(GLA_HEADS * TT)[:, None] // TT == np.arange(GLA_W)[None, :] // GLA_DV
    hs = np.arange(GLA_HEADS * GLA_DK)[:, None] // GLA_DK == np.arange(GLA_W)[None, :] // GLA_DV
    return hk.astype(np.float32), hv.astype(np.float32), hs.astype(np.float32)


def _heads_blockdiag(x, mask):
    return jnp.concatenate([x] * GLA_HEADS, axis=0) * mask


def _col_vector(row):
    n = row.shape[1]
    eye = lax.broadcasted_iota(jnp.int32, (n, n), 0) == lax.broadcasted_iota(jnp.int32, (n, n), 1)
    return jnp.sum(jnp.where(eye, row, 0.0), axis=1, keepdims=True)


def _norm_gate(o, gn, gate):
    outs = [_rms(o[:, GLA_DV * h:GLA_DV * (h + 1)], gn[:, GLA_DV * h:GLA_DV * (h + 1)]) for h in range(GLA_HEADS)]
    return jnp.concatenate(outs, axis=1) * (gate * jax.nn.sigmoid(gate))


def _gla_tile(z_ref, gm_ref, mk_ref, hk_ref, hv_ref, hs_ref, gn_ref, st_ref):
    la = z_ref[:, C_LA:C_ZG]
    la_hi = la.astype(BF16)
    la_lo = (la - la_hi.astype(F32)).astype(BF16)
    gm = gm_ref[...]
    ex = jnp.exp(_dot(gm, la_hi) + _dot(gm, la_lo))
    q = z_ref[:, C_GQ:C_GK]
    k = z_ref[:, C_GK:C_GV]
    e_b = ex[0:TT]
    a_row = e_b[TT - 1:TT, :]
    qd = (q * e_b).astype(BF16)
    kd = (k * ex[TT:2 * TT]).astype(BF16)
    qb = q.astype(BF16)
    kb = k.astype(BF16)
    ql = [(q * ex[(1 + lv) * TT:(2 + lv) * TT]).astype(BF16) for lv in range(1, N_LEVELS + 1)]
    kl = [(k * ex[(1 + lv) * TT:(2 + lv) * TT]).astype(BF16) for lv in range(1, N_LEVELS + 1)]
    hk = hk_ref[...]
    yield
    a = mk_ref[0] * _dot_nt(qb, _heads_blockdiag(kb, hk))
    for lv in range(N_LEVELS):
        yield
        a = a + mk_ref[lv + 1] * _dot_nt(ql[lv], _heads_blockdiag(kl[lv], hk))
    yield
    v = z_ref[:, C_GV:C_GR].astype(BF16)
    s = st_ref[...]
    o = _dot(a.astype(BF16), _heads_blockdiag(v, hv_ref[...])) + _dot(qd, s.astype(BF16))
    yield
    st_ref[...] = _col_vector(a_row) * s + hs_ref[...] * _dot_tn(kd, v)
    yield
    return _norm_gate(o, gn_ref[...], z_ref[:, C_GR:C_RQ])


def _ret_gammas():
    return [1.0 - 2.0 ** (-5.0 - h) for h in range(RET_HEADS)]


def _ret_constants():
    t = np.arange(TT, dtype=np.float64)
    d = np.zeros((RET_HEADS, TT, TT), np.float64)
    qdec = np.zeros((TT, RET_HEADS * RET_DK), np.float64)
    kdec = np.zeros((TT, RET_HEADS * RET_DK), np.float64)
    adec = np.zeros((RET_HEADS * RET_DK, LANES), np.float64)
    for h, gam in enumerate(_ret_gammas()):
        diff = t[:, None] - t[None, :]
        d[h] = np.where(diff >= 0, gam ** np.maximum(diff, 0.0), 0.0)
        qdec[:, h * RET_DK:(h + 1) * RET_DK] = (gam ** (t + 1.0))[:, None]
        kdec[:, h * RET_DK:(h + 1) * RET_DK] = (gam ** (TT - 1.0 - t))[:, None]
        adec[h * RET_DK:(h + 1) * RET_DK, :] = gam ** TT
    d = np.concatenate(list(d), axis=1)
    adec = np.tile(adec, (1, RET_W // LANES))
    return d.astype(np.float32), qdec.astype(np.float32), kdec.astype(np.float32), adec.astype(np.float32)


def _ret_tile(z_ref, d_ref, qdec_ref, kdec_ref, adec_ref, hk_ref, hv_ref, hs_ref, gn_ref, st_ref):
    q = z_ref[:, C_RQ:C_RK]
    k = z_ref[:, C_RK:C_RV]
    qd = (q * qdec_ref[...]).astype(BF16)
    kd = (k * kdec_ref[...]).astype(BF16)
    a = d_ref[...] * _dot_nt(q.astype(BF16), _heads_blockdiag(k.astype(BF16), hk_ref[...]))
    yield
    v = z_ref[:, C_RV:C_RG].astype(BF16)
    s = st_ref[...]
    o = _dot(a.astype(BF16), _heads_blockdiag(v, hv_ref[...])) + _dot(qd, s.astype(BF16))
    yield
    st_ref[...] = adec_ref[...] * s + hs_ref[...] * _dot_tn(kd, v)
    yield
    return _norm_gate(o, gn_ref[...], z_ref[:, C_RG:C_SQ])


def _rec_sample_constants():
    c = np.arange(SB * GLA_DK)
    rep = np.arange(GLA_DK)[:, None] == (c[None, :] % GLA_DK)
    bd = (np.arange(DEC_SEQ * SB)[:, None] % SB) == (c[None, :] // GLA_DK)
    return rep.astype(np.float32), bd.astype(np.float32)


def _block_rows(ref, row0, c0, c1):
    return [ref[pl.ds(t * DEC_BATCH + row0, SB), c0:c1] for t in range(DEC_SEQ)]


def _rec_sample_stages(q, k, v, gate, la, s0_ref, s1_ref, gn_ref, rep_ref, bd_ref):
    b = []
    for t in range(DEC_SEQ):
        b.append(la[t] if t == 0 else b[-1] + la[t])
    qd = jnp.concatenate([q[t] * jnp.exp(b[t]) for t in range(DEC_SEQ)], axis=0).astype(BF16)
    kd = jnp.concatenate([k[t] * jnp.exp(b[-1] - b[t]) for t in range(DEC_SEQ)], axis=0).astype(BF16)
    vv = jnp.concatenate(v, axis=0).astype(BF16)
    a = jnp.exp(b[-1])
    a_hi = a.astype(BF16)
    a_r = a - a_hi.astype(F32)
    a_mid = a_r.astype(BF16)
    a_lo = (a_r - a_mid.astype(F32)).astype(BF16)
    rep = rep_ref[...]
    bd = bd_ref[...]
    ones = jnp.ones((SB, GLA_DV), BF16)

    def expand(x, mask):
        return (_dot(x, rep) * mask).astype(BF16)

    yield
    oi = []
    for h in range(GLA_HEADS):
        hs = slice(GLA_DK * h, GLA_DK * (h + 1))
        vs = slice(GLA_DV * h, GLA_DV * (h + 1))
        s0 = s0_ref[:, h].reshape(SB * GLA_DK, GLA_DV)
        oi.append(_dot(expand(qd[:, hs], bd), s0.astype(BF16)))
        a_col = (_dot_tn(expand(a_hi[:, hs], bd[0:SB]), ones) + _dot_tn(expand(a_mid[:, hs], bd[0:SB]), ones)
                 + _dot_tn(expand(a_lo[:, hs], bd[0:SB]), ones))
        s1 = a_col * s0 + _dot_tn(expand(kd[:, hs], bd), vv[:, vs])
        s1_ref[:, h] = s1.reshape(SB, GLA_DK, GLA_DV)
        yield
    oi = jnp.concatenate(oi, axis=1)

    gn = gn_ref[...]
    outs = []
    for t in range(DEC_SEQ):
        o = oi[t * SB:(t + 1) * SB, :]
        for s in range(t + 1):
            p = q[t] * k[s] * jnp.exp(b[t] - b[s])
            parts = []
            for h in range(GLA_HEADS):
                hs = slice(GLA_DK * h, GLA_DK * (h + 1))
                vs = slice(GLA_DV * h, GLA_DV * (h + 1))
                parts.append(jnp.sum(p[:, hs], axis=1, keepdims=True) * v[s][:, vs])
            o = o + jnp.concatenate(parts, axis=1)
        outs.append(_norm_gate(o, gn, gate[t]))
        yield
    return jnp.concatenate(outs, axis=0)


def _swa_tile(z_ref, sink_ref, kp_ref, vp_ref, first_col):
    kc = z_ref[:, C_SK:C_SV].astype(BF16)
    vc = z_ref[:, C_SV:C_LA].astype(BF16)
    kband = jnp.concatenate([kp_ref[...], kc], axis=0)
    vband = jnp.concatenate([vp_ref[...], vc], axis=0)
    rows = SWA_GROUP * WINDOW
    r = lax.broadcasted_iota(jnp.int32, (rows, 2 * WINDOW), 0) & (WINDOW - 1)
    c = lax.broadcasted_iota(jnp.int32, (rows, 2 * WINDOW), 1)
    valid = (c > r) & (c <= r + WINDOW) & (c >= first_col)
    groups = range(SWA_KV_HEADS)
    dsl = [slice(SWA_HEAD_DIM * g, SWA_HEAD_DIM * (g + 1)) for g in groups]
    s, sk = [], []
    for g in groups:
        heads = range(SWA_GROUP * g, SWA_GROUP * (g + 1))
        q = jnp.concatenate(
            [z_ref[:, C_SQ + SWA_HEAD_DIM * hq:C_SQ + SWA_HEAD_DIM * (hq + 1)] for hq in heads], axis=0)
        sk.append(jnp.concatenate([jnp.full((WINDOW, 1), sink_ref[hq], F32) for hq in heads], axis=0))
        s.append(_dot_nt(q.astype(BF16), kband[:, dsl[g]]) * (SWA_HEAD_DIM ** -0.5))
    yield
    m, p = [], []
    for g in groups:
        sg = jnp.where(valid, s[g], NEG_INF)
        m.append(jnp.maximum(jnp.max(sg, axis=-1, keepdims=True), sk[g]))
        p.append(jnp.exp(sg - m[g]))
    yield
    outs = []
    for g in groups:
        den = jnp.sum(p[g], axis=-1, keepdims=True) + jnp.exp(sk[g] - m[g])
        o = _dot(p[g].astype(BF16), vband[:, dsl[g]]) / den
        outs += [o[WINDOW * j:WINDOW * (j + 1)] for j in range(SWA_GROUP)]
    yield
    return jnp.concatenate(outs, axis=1), kc, vc


def _interleave(stage_fns):
    results = [None] * len(stage_fns)
    live = list(enumerate(stage_fns))
    while live:
        still = []
        for idx, gen in live:
            try:
                next(gen)
                still.append((idx, gen))
            except StopIteration as stop:
                results[idx] = stop.value
        live = still
    return results


def _merge_stages(pend_ref, x_ref, g_ref, w_ref, wb_ref, wo_ref, y_ref):
    nb = x_ref.shape[0]
    hb = _rms(x_ref[...].reshape(nb * TT, D_MODEL), g_ref[...]).astype(BF16)
    gates = []
    for j in range(N_BRANCH):
        gates.append(jax.nn.sigmoid(_gate_logits(hb, w_ref, j)))
        yield
    mixed = None
    for j in range(N_BRANCH):
        term = gates[j] * _dot(pend_ref[:, GLA_W * j:GLA_W * (j + 1)], wb_ref[GLA_W * j:GLA_W * (j + 1), :])
        mixed = term if mixed is None else mixed + term
        yield
    out = _dot(mixed.astype(BF16), wo_ref[...])
    for b in range(x_ref.shape[0]):
        y_ref[b] = x_ref[b] + out[TT * b:TT * (b + 1)]
    yield


def _mix_kernel(sink_ref, xp_ref, xn_ref, tp0, tp1, tp2, tp3, tn0, tn1, tn2, tn3, g_ref, w_ref, wa_ref, ba_ref,
                gm_ref, mk_ref, d_ref, qdec_ref, kdec_ref, adec_ref, hk_ref, hv_ref, hs_ref, gna_ref, gnb_ref,
                wb_ref, wo_ref, y_ref, sta_ref, stb_ref, kv_ref, z_scr, pend_scr, sa_scr, sb_scr, kp_scr, vp_scr,
                *, nt):
    i = pl.program_id(0)
    nb = xp_ref.shape[0]

    def rows_of(ref3):
        return ref3[...].reshape(nb * TT, ref3.shape[2])

    def tiled(tab_refs):
        return tuple(jnp.concatenate([r[...]] * nb, axis=0) for r in tab_refs)

    @pl.when(i == 0)
    def _():
        sa_scr[...] = jnp.zeros_like(sa_scr)
        sb_scr[...] = jnp.zeros_like(sb_scr)
        kp_scr[...] = jnp.zeros_like(kp_scr)
        vp_scr[...] = jnp.zeros_like(vp_scr)
        pend_scr[...] = jnp.zeros_like(pend_scr)
        for _ in _in_stages(rows_of(xp_ref), g_ref, w_ref, wa_ref, ba_ref, tiled((tp0, tp1, tp2, tp3)),
                            z_scr.at[0], False):
            pass

    slot = i % 2
    zc = z_scr.at[slot]
    first_col = jnp.where(i > 0, 0, WINDOW)
    stages = [_merge_stages(pend_scr, xp_ref, g_ref, w_ref, wb_ref, wo_ref, y_ref),
              _in_stages(rows_of(xn_ref), g_ref, w_ref, wa_ref, ba_ref, tiled((tn0, tn1, tn2, tn3)),
                         z_scr.at[1 - slot], False)]
    for b in range(nb):
        zb = zc.at[pl.ds(TT * b, TT)]
        stages += [_gla_tile(zb, gm_ref, mk_ref, hk_ref, hv_ref, hs_ref, gna_ref, sa_scr.at[b]),
                   _swa_tile(zb, sink_ref, kp_scr.at[b], vp_scr.at[b], first_col),
                   _ret_tile(zb, d_ref, qdec_ref, kdec_ref, adec_ref, hk_ref, hv_ref, hs_ref, gnb_ref, sb_scr.at[b])]
    branch = _interleave(stages)[2:]
    swa = [branch[3 * b + 1] for b in range(nb)]
    outs = [[branch[3 * b] for b in range(nb)], [branch[3 * b + 2] for b in range(nb)], [s[0] for s in swa]]
    pend = [jnp.concatenate(o, axis=0).astype(BF16) for o in outs]

    @pl.when(i < nt)
    def _():
        for j in range(N_BRANCH):
            pend_scr[:, GLA_W * j:GLA_W * (j + 1)] = pend[j]
        for b in range(nb):
            kp_scr[b] = swa[b][1]
            vp_scr[b] = swa[b][2]
        for b in range(nb):
            for h in range(GLA_HEADS):
                sta_ref[b, h] = sa_scr[b, GLA_DK * h:GLA_DK * (h + 1), GLA_DV * h:GLA_DV * (h + 1)]
                stb_ref[b, h] = sb_scr[b, RET_DK * h:RET_DK * (h + 1), RET_DV * h:RET_DV * (h + 1)]
            kv_ref[b] = zc[pl.ds(TT * b, TT), C_SK:C_LA]


MIX_VMEM_LIMIT = 60 * 1024 * 1024


def _mix_prompt(x, sink, layer, g_mix, w_main, wa, ba, tabs, consts, gna, gnb, wb, wo):
    nb, t, _ = x.shape
    nt = t // TT
    st_spec = pl.BlockSpec((nb, GLA_HEADS, GLA_DK, GLA_DV), lambda i: (0, 0, 0, 0))
    st_shape = jax.ShapeDtypeStruct((nb, GLA_HEADS, GLA_DK, GLA_DV), F32)
    nxt = lambda i: jnp.minimum(i + 1, nt - 1)
    prv = lambda i: jnp.maximum(i - 1, 0)
    tab_prv = pl.BlockSpec((TT, LANES), lambda i: (prv(i), 0))
    tab_nxt = pl.BlockSpec((TT, LANES), lambda i: (nxt(i), 0))
    return pl.pallas_call(
        functools.partial(_mix_kernel, nt=nt),
        grid=(nt + 1,),
        in_specs=[pl.BlockSpec(memory_space=pltpu.SMEM),
                  pl.BlockSpec((nb, TT, D_MODEL), lambda i: (0, prv(i), 0)),
                  pl.BlockSpec((nb, TT, D_MODEL), lambda i: (0, nxt(i), 0))]
        + [tab_prv] * 4 + [tab_nxt] * 4
        + [_layer_spec((1, D_MODEL), layer), _layer_spec((D_MODEL, W_MAIN), layer),
           _layer_spec((LANES, 256), layer), _layer_spec((1, 256), layer)]
        + [_const_spec(c.shape) for c in consts]
        + [_const_spec((1, GLA_W)), _const_spec((1, RET_W)),
           _layer_spec((GLA_W + RET_W + SWA_W, D_MODEL), layer),
           _layer_spec((D_MODEL, D_MODEL), layer)],
        out_specs=[pl.BlockSpec((nb, TT, D_MODEL), lambda i: (0, prv(i), 0)), st_spec, st_spec,
                   pl.BlockSpec((nb, WINDOW, 2 * LANES), lambda i: (0, 0, 0))],
        out_shape=[jax.ShapeDtypeStruct((nb, t, D_MODEL), F32), st_shape, st_shape,
                   jax.ShapeDtypeStruct((nb, WINDOW, 2 * LANES), F32)],
        scratch_shapes=[pltpu.VMEM((2, nb * TT, C_ZG), F32),
                        pltpu.VMEM((nb * TT, GLA_W + RET_W + SWA_W), BF16),
                        pltpu.VMEM((nb, GLA_HEADS * GLA_DK, GLA_W), F32),
                        pltpu.VMEM((nb, RET_HEADS * RET_DK, RET_W), F32),
                        pltpu.VMEM((nb, WINDOW, 128), BF16), pltpu.VMEM((nb, WINDOW, 128), BF16)],
        compiler_params=pltpu.CompilerParams(dimension_semantics=("arbitrary",), vmem_limit_bytes=MIX_VMEM_LIMIT),
        name="mix_prompt",
    )(sink, x, x, *tabs, *tabs, g_mix, w_main, wa, ba, *consts, gna, gnb, wb, wo)


def _swa_sample_masks():
    r = np.arange(DEC_SEQ * SB)
    rt, rb = r // SB, r % SB
    c = np.arange(SB * WINDOW)
    cb, cj = c // WINDOW, c % WINDOW
    m_cache = (rb[:, None] == cb[None, :]) & (cj[None, :] > rt[:, None])
    m_new = (rb[:, None] == rb[None, :]) & (rt[None, :] <= rt[:, None])
    return m_cache.astype(np.float32), m_new.astype(np.float32)


def _swa_sample_stages(sink_ref, q, kn, vn, kc_ref, vc_ref, mc_ref, mn_ref):
    q = q.astype(BF16)
    kn = kn.astype(BF16)
    vn = vn.astype(BF16)
    kct = [jnp.concatenate([kc_ref[b, g] for b in range(SB)], axis=1).astype(BF16) for g in range(SWA_KV_HEADS)]
    vct = [jnp.concatenate([vc_ref[b, g] for b in range(SB)], axis=1).astype(BF16) for g in range(SWA_KV_HEADS)]
    ok_c = mc_ref[...] > 0.0
    ok_n = mn_ref[...] > 0.0
    outs = []
    for hq in range(SWA_Q_HEADS):
        g = hq // SWA_GROUP
        ds_ = slice(SWA_HEAD_DIM * g, SWA_HEAD_DIM * (g + 1))
        qh = q[:, SWA_HEAD_DIM * hq:SWA_HEAD_DIM * (hq + 1)]
        s1 = jnp.where(ok_c, _dot(qh, kct[g]) * (SWA_HEAD_DIM ** -0.5), NEG_INF)
        s2 = jnp.where(ok_n, _dot_nt(qh, kn[:, ds_]) * (SWA_HEAD_DIM ** -0.5), NEG_INF)
        sk = sink_ref[hq]
        m = jnp.maximum(jnp.maximum(jnp.max(s1, axis=-1, keepdims=True),
                                    jnp.max(s2, axis=-1, keepdims=True)), sk)
        p1 = jnp.exp(s1 - m)
        p2 = jnp.exp(s2 - m)
        den = jnp.sum(p1, axis=-1, keepdims=True) + jnp.sum(p2, axis=-1, keepdims=True) + jnp.exp(sk - m)
        outs.append((_dot_nt(p1.astype(BF16), vct[g]) + _dot(p2.astype(BF16), vn[:, ds_])) / den)
        if hq % 2 == 1:
            yield
    return jnp.concatenate(outs, axis=1)


def _mix_sample_kernel(sink_ref, x_ref, z_ref, lar_ref, sa0_ref, sb0_ref, kc_ref, vc_ref, gna_ref, gnb_ref,
                       rep_ref, bd_ref, mc_ref, mn_ref, wb_ref, wo_ref, sa_all_ref, sb_all_ref,
                       y_ref, sa1_ref, sb1_ref):
    del sa_all_ref, sb_all_ref
    row0 = pl.multiple_of(pl.program_id(0) * SB, SB)

    def rows(c0, c1):
        return _block_rows(z_ref, row0, c0, c1)

    def stacked(c0, c1):
        return jnp.concatenate(rows(c0, c1), axis=0)

    la_ret = [lar_ref[...]] * DEC_SEQ
    oa, ob, oc = _interleave([
        _rec_sample_stages(rows(C_GQ, C_GK), rows(C_GK, C_GV), rows(C_GV, C_GR), rows(C_GR, C_RQ),
                           rows(C_LA, C_ZG), sa0_ref, sa1_ref, gna_ref, rep_ref, bd_ref),
        _rec_sample_stages(rows(C_RQ, C_RK), rows(C_RK, C_RV), rows(C_RV, C_RG), rows(C_RG, C_SQ),
                           la_ret, sb0_ref, sb1_ref, gnb_ref, rep_ref, bd_ref),
        _swa_sample_stages(sink_ref, stacked(C_SQ, C_SK), stacked(C_SK, C_SV), stacked(C_SV, C_LA),
                           kc_ref, vc_ref, mc_ref, mn_ref)])
    mixed = sum(jax.nn.sigmoid(stacked(C_ZG + D_MODEL * j, C_ZG + D_MODEL * (j + 1)))
                * _dot(o.astype(BF16), wb_ref[512 * j:512 * (j + 1), :])
                for j, o in enumerate((oa, ob, oc)))
    out = _dot(mixed.astype(BF16), wo_ref[...])
    for t in range(DEC_SEQ):
        r = pl.ds(t * DEC_BATCH + row0, SB)
        y_ref[r, :] = x_ref[r, :] + out[t * SB:(t + 1) * SB, :]


def _mix_sample(x, z, sink, la_ret, state_gla, state_ret, new_gla, new_ret, cache_k, cache_v, layer, gna, gnb,
                consts, wb, wo):
    n = x.shape[0]
    rep, bd, mc, mn = consts
    whole = lambda w: pl.BlockSpec((n, w), lambda j: (0, 0), pipeline_mode=pl.Buffered(1))
    st_blk = pl.BlockSpec((None, SB, GLA_HEADS, GLA_DK, GLA_DV), lambda j: (layer, j, 0, 0, 0))
    cache = pl.BlockSpec((None, SB, SWA_KV_HEADS, SWA_HEAD_DIM, WINDOW), lambda j: (layer, j, 0, 0, 0))
    st_shape = jax.ShapeDtypeStruct(state_gla.shape, F32)
    untouched = pl.BlockSpec(memory_space=pl.ANY)
    operands = (sink, x, z, la_ret, state_gla, state_ret, cache_k, cache_v, gna, gnb, rep, bd, mc, mn, wb, wo,
                new_gla, new_ret)
    return pl.pallas_call(
        _mix_sample_kernel,
        grid=(DEC_BATCH // SB,),
        in_specs=[pl.BlockSpec(memory_space=pltpu.SMEM), whole(D_MODEL), whole(Z_W),
                  _const_spec(la_ret.shape), st_blk, st_blk, cache, cache,
                  _const_spec((1, GLA_W)), _const_spec((1, RET_W)),
                  _const_spec(rep.shape), _const_spec(bd.shape), _const_spec(mc.shape), _const_spec(mn.shape),
                  _layer_spec((GLA_W + RET_W + SWA_W, D_MODEL), layer),
                  _layer_spec((D_MODEL, D_MODEL), layer), untouched, untouched],
        out_specs=[pl.BlockSpec((n, D_MODEL), lambda j: (0, 0)), st_blk, st_blk],
        out_shape=[jax.ShapeDtypeStruct((n, D_MODEL), F32), st_shape, st_shape],
        input_output_aliases={len(operands) - 2: 1, len(operands) - 1: 2},
        compiler_params=_params(("arbitrary",)),
        name="mix_sample",
    )(*operands)


def _window_place_matrix():
    r = np.arange(DEC_SEQ * SB)
    c = np.arange(SB * WINDOW)
    return ((r[:, None] % SB == c[None, :] // WINDOW)
            & (c[None, :] % WINDOW == WINDOW - DEC_SEQ + r[:, None] // SB)).astype(np.float32)


def _window_kernel(kv_ref, kc_ref, vc_ref, place_ref, ko_ref, vo_ref):
    row0 = pl.multiple_of(pl.program_id(1) * SB, SB)
    new = jnp.concatenate(_block_rows(kv_ref, row0, 0, 2 * LANES), axis=0)
    hi = new.astype(BF16)
    rest = new - hi.astype(F32)
    mid = rest.astype(BF16)
    lo = (rest - mid.astype(F32)).astype(BF16)
    place = place_ref[...]
    keep = lax.broadcasted_iota(jnp.int32, (SWA_HEAD_DIM, WINDOW), 1) < WINDOW - DEC_SEQ
    for c_ref, o_ref, c0 in ((kc_ref, ko_ref, 0), (vc_ref, vo_ref, LANES)):
        for g in range(SWA_KV_HEADS):
            cols = slice(c0 + SWA_HEAD_DIM * g, c0 + SWA_HEAD_DIM * (g + 1))
            placed = (_dot_tn(hi[:, cols], place) + _dot_tn(mid[:, cols], place)
                      + _dot_tn(lo[:, cols], place))
            for b in range(SB):
                shifted = pltpu.roll(c_ref[b, g], WINDOW - DEC_SEQ, 1)
                o_ref[b, g] = jnp.where(keep, shifted, placed[:, WINDOW * b:WINDOW * (b + 1)])


def _window_update(kv_new, cache_kt, cache_vt, place):
    blk = pl.BlockSpec((None, SB, SWA_KV_HEADS, SWA_HEAD_DIM, WINDOW), lambda l, j: (l, j, 0, 0, 0))
    return pl.pallas_call(
        _window_kernel,
        grid=(DEPTH, DEC_BATCH // SB),
        in_specs=[pl.BlockSpec((None,) + kv_new.shape[1:], lambda l, j: (l, 0, 0)), blk, blk,
                  _const_spec(place.shape)],
        out_specs=[blk, blk],
        out_shape=[jax.ShapeDtypeStruct(cache_kt.shape, F32), jax.ShapeDtypeStruct(cache_vt.shape, F32)],
        compiler_params=_params(("arbitrary", "arbitrary")),
        name="window_update",
    )(kv_new, cache_kt, cache_vt, place)


MXU_N = 256
FF_CHUNKS = ((0, 6 * MXU_N), (6 * MXU_N, D_FF))


def _ffn_tail(x1, pe_ref, gp_ref, wpg_ref, wpp_ref, gf_ref, y_ref, final):
    hp = _rms(x1, gp_ref[...]).astype(BF16)
    x2 = x1 + jax.nn.sigmoid(_dot(hp, wpg_ref[...])) * _dot(pe_ref[...].astype(BF16), wpp_ref[...])
    y_ref[...] = _rms(x2, gf_ref[...]) if final else x2


def _ffn_stages(x, carry, gn_ref, wi_ref, cw_ref, cb_ref, wd_ref, keep):
    tm = x.shape[0]
    hb = _rms(x, gn_ref[...]).astype(BF16)
    row = lax.broadcasted_iota(jnp.int32, (tm, 1), 0)
    acc = x
    for c0, c1 in FF_CHUNKS:
        cs = slice(c0, c1)
        a = _dot(hb, wi_ref[:, c0:c1])
        bb = _dot(hb, wi_ref[:, D_FF + c0:D_FF + c1])
        yield
        p0 = carry[6:7, cs]
        p1 = carry[7:8, cs]
        a1 = jnp.where(row == 0, p1, pltpu.roll(a, 1, 0))
        a2 = jnp.where(row == 0, p0, jnp.where(row == 1, p1, pltpu.roll(a, 2, 0)))
        conv = cb_ref[:, cs] + cw_ref[0:1, cs] * a2 + cw_ref[1:2, cs] * a1 + cw_ref[2:3, cs] * a
        act = (jax.nn.gelu(conv) * bb).astype(BF16)
        keep.append(a[tm - 8:tm, :])
        yield
        acc = acc + _dot(act, wd_ref[c0:c1, :])
        yield
    return acc


def _ffn_prompt_kernel(x_ref, pe_ref, gn_ref, wi_ref, cw_ref, cb_ref, wd_ref, gp_ref, wpg_ref, wpp_ref,
                       gf_ref, y_ref, tail_ref, carry_scr, *, final, nt):
    i = pl.program_id(1)

    @pl.when(i == 0)
    def _():
        carry_scr[...] = jnp.zeros_like(carry_scr)

    keep = []
    (acc,) = _interleave([_ffn_stages(x_ref[...], carry_scr[...], gn_ref, wi_ref, cw_ref, cb_ref, wd_ref, keep)])
    _ffn_tail(acc, pe_ref, gp_ref, wpg_ref, wpp_ref, gf_ref, y_ref, final)
    last_rows = jnp.concatenate(keep, axis=1)
    tail_ref[...] = last_rows

    @pl.when(i < nt - 1)
    def _():
        carry_scr[...] = last_rows


def _ffn_sample_kernel(x_ref, pe_ref, st_ref, gn_ref, wi_ref, cw_ref, cb_ref, wd_ref, gp_ref, wpg_ref,
                       wpp_ref, gf_ref, y_ref, tail_ref, *, final):
    ns = DEC_BATCH
    x = x_ref[...]
    hb = _rms(x, gn_ref[...]).astype(BF16)
    acc = x
    for c0, c1 in FF_CHUNKS:
        cs = slice(c0, c1)
        a = _dot(hb, wi_ref[:, c0:c1])
        bb = _dot(hb, wi_ref[:, D_FF + c0:D_FF + c1])
        st0 = st_ref[0, :, cs]
        st1 = st_ref[1, :, cs]
        a1 = jnp.concatenate([st1, a[0:3 * ns]], axis=0)
        a2 = jnp.concatenate([st0, st1, a[0:2 * ns]], axis=0)
        conv = cb_ref[:, cs] + cw_ref[0:1, cs] * a2 + cw_ref[1:2, cs] * a1 + cw_ref[2:3, cs] * a
        act = (jax.nn.gelu(conv) * bb).astype(BF16)
        acc = acc + _dot(act, wd_ref[c0:c1, :])
        tail_ref[:, cs] = a[2 * ns:4 * ns, :]
    _ffn_tail(acc, pe_ref, gp_ref, wpg_ref, wpp_ref, gf_ref, y_ref, final)


def _ffn_weight_specs(layer):
    return [_layer_spec((1, D_MODEL), layer),
            _layer_spec((D_MODEL, 2 * D_FF), layer),
            _layer_spec((CONV_W, D_FF), layer),
            _layer_spec((1, D_FF), layer),
            _layer_spec((D_FF, D_MODEL), layer),
            _layer_spec((1, D_MODEL), layer),
            _layer_spec((D_MODEL, D_MODEL), layer),
            _layer_spec((D_PLE, D_MODEL), layer),
            _const_spec((1, D_MODEL))]


def _ffn_prompt(x, pe, layer, weights, nb, t, tm, final):
    nt = t // tm
    tok = lambda w: pl.BlockSpec((tm, w), lambda b, i: (b * nt + i, 0))
    return pl.pallas_call(
        functools.partial(_ffn_prompt_kernel, final=final, nt=nt),
        grid=(nb, nt),
        in_specs=[tok(D_MODEL),
                  pl.BlockSpec((None, tm, D_PLE), lambda b, i: (layer, b * nt + i, 0))]
        + _ffn_weight_specs(layer),
        out_specs=[tok(D_MODEL), pl.BlockSpec((8, D_FF), lambda b, i: (b, 0))],
        out_shape=[jax.ShapeDtypeStruct((nb * t, D_MODEL), F32),
                   jax.ShapeDtypeStruct((nb * 8, D_FF), F32)],
        scratch_shapes=[pltpu.VMEM((8, D_FF), F32)],
        compiler_params=pltpu.CompilerParams(dimension_semantics=("arbitrary", "arbitrary"),
                                             vmem_limit_bytes=MIX_VMEM_LIMIT),
        name="ffn_prompt",
    )(x, pe, *weights)


def _ffn_sample(x, pe, st, layer, weights, final):
    n = x.shape[0]
    return pl.pallas_call(
        functools.partial(_ffn_sample_kernel, final=final),
        grid=(1,),
        in_specs=[pl.BlockSpec((n, D_MODEL), lambda i: (0, 0)),
                  pl.BlockSpec((None, n, D_PLE), lambda i: (layer, 0, 0)),
                  pl.BlockSpec((None, CONV_W - 1, DEC_BATCH, D_FF), lambda i: (layer, 0, 0, 0))]
        + _ffn_weight_specs(layer),
        out_specs=[pl.BlockSpec((n, D_MODEL), lambda i: (0, 0)),
                   pl.BlockSpec((n // 2, D_FF), lambda i: (0, 0))],
        out_shape=[jax.ShapeDtypeStruct((n, D_MODEL), F32),
                   jax.ShapeDtypeStruct((n // 2, D_FF), F32)],
        compiler_params=_params(("arbitrary",)),
        name="ffn_sample",
    )(x, pe, st, *weights)


def _rope_tables(pos, inv_freq):
    half = inv_freq.shape[0]
    ang = pos.astype(np.float64)[:, None] * inv_freq[None, :]
    c, s = np.cos(ang), np.sin(ang)
    rest = SWA_HEAD_DIM - 2 * half
    cos64 = np.concatenate([c, c, np.ones((pos.shape[0], rest))], axis=1)
    sin64 = np.concatenate([-s, s, np.zeros((pos.shape[0], rest))], axis=1)
    return (jnp.asarray(np.concatenate([cos64, cos64], axis=1), F32),
            jnp.asarray(np.concatenate([sin64, sin64], axis=1), F32))


def _pack_w_in(w_in):
    ga0 = sum(IN_SIZES[:4])
    ga = jnp.pad(w_in[:, :, ga0:ga0 + GLA_RANK], ((0, 0), (0, 0), (0, LANES - GLA_RANK)))
    return jnp.concatenate([w_in[:, :, :ga0], w_in[:, :, ga0 + GLA_RANK:], ga], axis=2).astype(BF16)


def kernel(x_prompt, x_sample, state_gla, state_ret, cache_swa_k, cache_swa_v, state_conv, p_prompt,
           p_sample, norm_mix, w_in, w_gla_a, b_gla_a, gla_norm, ret_norm, swa_sink, w_branch, w_out,
           norm_ffn, w_ffn_in, conv_w, conv_b, w_ffn_out, norm_ple, w_ple_gate, w_ple_proj, norm_final):
    nb, t, _ = x_prompt.shape
    ns, ts, _ = x_sample.shape
    n_s = ns * ts

    w_main = _pack_w_in(w_in)
    wa = jnp.pad(w_gla_a, ((0, 0), (0, LANES - GLA_RANK), (0, 0))).astype(BF16)
    ba = b_gla_a[:, None, :]
    wb = w_branch.astype(BF16)
    wo = w_out.astype(BF16)
    ffn_weights = (norm_ffn[:, None, :], w_ffn_in.astype(BF16), conv_w, conv_b[:, None, :],
                   w_ffn_out.astype(BF16), norm_ple[:, None, :], w_ple_gate.astype(BF16),
                   w_ple_proj.astype(BF16), norm_final[None, :])
    g_mix = norm_mix[:, None, :]
    gn_gla = jnp.tile(gla_norm, (1, GLA_HEADS))[:, None, :]
    gn_ret = ret_norm.reshape(DEPTH, 1, RET_W)

    ret_freq = 1.0 / (RET_THETA ** np.linspace(0.0, 1.0, RET_DK // 2))
    swa_freq = 1.0 / (ROPE_THETA ** (np.arange(0, ROPE_DIM, 2, dtype=np.float64) / ROPE_DIM))
    pos_p = np.arange(t)
    pos_s = PAST_LEN + np.arange(n_s) // ns
    tabs_p = _rope_tables(pos_p, ret_freq) + _rope_tables(pos_p, swa_freq)
    tabs_s = _rope_tables(pos_s, ret_freq) + _rope_tables(pos_s, swa_freq)

    gm_np, mk_np = _gla_constants()
    d_np, qdec_np, kdec_np, adec_np = _ret_constants()
    hk_np, hv_np, hs_np = _head_masks()
    mix_consts = (jnp.asarray(gm_np, BF16), jnp.asarray(mk_np), jnp.asarray(d_np), jnp.asarray(qdec_np),
                  jnp.asarray(kdec_np), jnp.asarray(adec_np), jnp.asarray(hk_np, BF16), jnp.asarray(hv_np, BF16),
                  jnp.asarray(hs_np))
    log_gamma = np.log1p(-np.exp2(-5.0 - np.arange(RET_HEADS, dtype=np.float64)))
    la_ret = jnp.asarray(np.broadcast_to(np.repeat(log_gamma, RET_DK)[None, :], (SB, RET_HEADS * RET_DK)), F32)
    mc_np, mn_np = _swa_sample_masks()
    rep_np, bd_np = _rec_sample_constants()
    sample_consts = (jnp.asarray(rep_np, BF16), jnp.asarray(bd_np), jnp.asarray(mc_np), jnp.asarray(mn_np))

    xp = x_prompt.reshape(nb * t, D_MODEL)
    xs = x_sample.transpose(1, 0, 2).reshape(n_s, D_MODEL)
    pe_p = p_prompt.reshape(DEPTH, nb * t, D_PLE)
    pe_s = p_sample.transpose(0, 2, 1, 3).reshape(DEPTH, n_s, D_PLE)
    conv_st = state_conv.transpose(0, 2, 1, 3)
    ck = cache_swa_k.transpose(0, 1, 3, 4, 2)
    cv = cache_swa_v.transpose(0, 1, 3, 4, 2)

    outs = {k_: [] for k_ in ("gla_p", "ret_p", "kv_p", "kv_s", "conv_p", "conv_s")}
    gla_s = jnp.zeros(state_gla.shape, F32)
    ret_s = jnp.zeros(state_ret.shape, F32)
    for l in range(DEPTH):
        final = l == DEPTH - 1
        sink = swa_sink[l]

        xp3, st_a, st_b, kv_tail = _mix_prompt(xp.reshape(nb, t, D_MODEL), sink, l, g_mix, w_main, wa, ba, tabs_p,
                                               mix_consts, gn_gla[l], gn_ret[l], wb, wo)
        xp, tail = _ffn_prompt(xp3.reshape(nb * t, D_MODEL), pe_p, l, ffn_weights, nb, t, 512, final)
        outs["gla_p"].append(st_a)
        outs["ret_p"].append(st_b)
        outs["kv_p"].append(kv_tail)
        outs["conv_p"].append(tail.reshape(nb, 8, D_FF)[:, 8 - (CONV_W - 1):])

        zs = _in_proj(xs, l, g_mix, w_main, wa, ba, tabs_s, 256)
        xs, gla_s, ret_s = _mix_sample(xs, zs, sink, la_ret, state_gla, state_ret, gla_s, ret_s, ck, cv, l,
                                       gn_gla[l], gn_ret[l], sample_consts, wb, wo)
        xs, tail_s = _ffn_sample(xs, pe_s, conv_st, l, ffn_weights, final)
        outs["kv_s"].append(zs[:, C_SK:C_LA])
        outs["conv_s"].append(tail_s.reshape(CONV_W - 1, ns, D_FF).transpose(1, 0, 2))

    y_prompt = xp.reshape(nb, t, D_MODEL)
    y_sample = xs.reshape(ts, ns, D_MODEL).transpose(1, 0, 2)
    st = {k_: jnp.stack(v_) for k_, v_ in outs.items()}

    def kv_heads(a):
        return a.reshape(a.shape[:-1] + (SWA_KV_HEADS, SWA_HEAD_DIM))

    k_p = kv_heads(st["kv_p"][..., :LANES])
    v_p = kv_heads(st["kv_p"][..., LANES:])
    k_st, v_st = _window_update(st["kv_s"], ck, cv, jnp.asarray(_window_place_matrix(), BF16))
    k_s = k_st.transpose(0, 1, 4, 2, 3)
    v_s = v_st.transpose(0, 1, 4, 2, 3)
    return (y_prompt, y_sample, st["gla_p"], gla_s, st["ret_p"], ret_s,
            k_p, k_s, v_p, v_s, st["conv_p"], st["conv_s"])
```

```python
import functools

import numpy as np
import jax
import jax.numpy as jnp
from jax import lax
from jax.experimental import pallas as pl
from jax.experimental.pallas import tpu as pltpu

F32 = jnp.float32
BF16 = jnp.bfloat16

D_MODEL = 1024
BATCH = 2
SEQ = 8192
DEPTH = 4
DEC_BATCH = 128
DEC_SEQ = 4
PAST_LEN = 8192
D_PLE = 256
GLA_HEADS = 4
GLA_DK = 64
GLA_DV = 128
GLA_RANK = 16
GLA_TAU = 16.0
RET_HEADS = 4
RET_DK = 64
RET_DV = 128
RET_THETA = 10000.0
SWA_Q_HEADS = 8
SWA_KV_HEADS = 2
SWA_HEAD_DIM = 64
SWA_GROUP = SWA_Q_HEADS // SWA_KV_HEADS
WINDOW = 128
ROPE_THETA = 500000.0
ROPE_DIM = SWA_HEAD_DIM // 4
D_FF = 2816
CONV_W = 3
N_BRANCH = 3
EPS = 1e-6
NEG_INF = -1e30

GLA_W = GLA_HEADS * GLA_DV
RET_W = RET_HEADS * RET_DV
SWA_W = SWA_Q_HEADS * SWA_HEAD_DIM
IN_SIZES = (GLA_HEADS * GLA_DK, GLA_HEADS * GLA_DK, GLA_W, GLA_W, GLA_RANK,
            RET_HEADS * RET_DK, RET_HEADS * RET_DK, RET_W, RET_W,
            SWA_W, SWA_KV_HEADS * SWA_HEAD_DIM, SWA_KV_HEADS * SWA_HEAD_DIM,
            N_BRANCH * D_MODEL)

LANES = 128
VMEM_LIMIT = 52 * 1024 * 1024

C_GQ = 0
C_GK = 256
C_GV = 512
C_GR = 1024
C_RQ = 1536
C_RK = 1792
C_RV = 2048
C_RG = 2560
C_SQ = 3072
C_SK = 3584
C_SV = 3712
C_LA = 3840
C_ZG = 4096
Z_W = 7168
W_ZG = 3840
W_GA = 6912
W_MAIN = 7040

TT = 128
N_LEVELS = 7
SB = 16


def _dot(a, b):
    return jnp.dot(a, b, preferred_element_type=F32)


def _dot_nt(a, b):
    return lax.dot_general(a, b, (((1,), (1,)), ((), ())), preferred_element_type=F32)


def _dot_tn(a, b):
    return lax.dot_general(a, b, (((0,), (0,)), ((), ())), preferred_element_type=F32)


def _rms(x, g):
    return x * lax.rsqrt(jnp.mean(x * x, axis=-1, keepdims=True) + EPS) * g


def _layer_spec(shape, layer):
    nd = len(shape)
    return pl.BlockSpec((None,) + tuple(shape), lambda *_: (layer,) + (0,) * nd,
                        pipeline_mode=pl.Buffered(1))


def _const_spec(shape):
    nd = len(shape)
    return pl.BlockSpec(tuple(shape), lambda *_: (0,) * nd, pipeline_mode=pl.Buffered(1))


def _params(sem):
    return pltpu.CompilerParams(dimension_semantics=sem, vmem_limit_bytes=VMEM_LIMIT)


def _rope_block(x, cos, sin_s, first):
    half_mask, half = first
    xr = jnp.where(half_mask, pltpu.roll(x, LANES - half, 1), pltpu.roll(x, half, 1))
    return x * cos + xr * sin_s


def _gate_logits(hb, w_ref, j):
    return _dot(hb, w_ref[:, W_ZG + D_MODEL * j:W_ZG + D_MODEL * (j + 1)])


def _in_stages(x, g_ref, w_ref, wa_ref, ba_ref, tabs, z_ref, with_gates):
    hb = _rms(x, g_ref[...]).astype(BF16)

    def mm(c0, c1):
        return _dot(hb, w_ref[:, c0:c1])

    lane = lax.broadcasted_iota(jnp.int32, (1, LANES), 1) % 64
    ret_first = (lane < RET_DK // 2, RET_DK // 2)
    swa_first = (lane < ROPE_DIM // 2, ROPE_DIM // 2)
    rc, rs, sc, ss = tabs

    ga = mm(W_GA, W_MAIN)
    xa = _dot(ga.astype(BF16), wa_ref[...]) + ba_ref[...]
    log_sig = jnp.minimum(xa, 0.0) - jnp.log1p(jnp.exp(-jnp.abs(xa)))
    z_ref[:, C_LA:C_ZG] = log_sig * (1.0 / GLA_TAU)
    z_ref[:, C_GQ:C_GK] = mm(C_GQ, C_GK) * (GLA_DK ** -0.5)
    yield
    z_ref[:, C_GK:C_RQ] = mm(C_GK, C_RQ)
    yield
    rq = mm(C_RQ, C_RK)
    rk = mm(C_RK, C_RV)
    for j in range(2):
        sl = slice(LANES * j, LANES * (j + 1))
        z_ref[:, C_RQ + LANES * j:C_RQ + LANES * (j + 1)] = _rope_block(rq[:, sl], rc, rs, ret_first)
        z_ref[:, C_RK + LANES * j:C_RK + LANES * (j + 1)] = (
            _rope_block(rk[:, sl], rc, rs, ret_first) * (RET_DK ** -0.5))
    yield
    z_ref[:, C_RV:C_SQ] = mm(C_RV, C_SQ)
    yield
    sqkv = mm(C_SQ, C_LA)
    for j in range(5):
        sl = slice(LANES * j, LANES * (j + 1))
        z_ref[:, C_SQ + LANES * j:C_SQ + LANES * (j + 1)] = _rope_block(sqkv[:, sl], sc, ss, swa_first)
    z_ref[:, C_SV:C_LA] = sqkv[:, C_SV - C_SQ:]
    yield
    if with_gates:
        for j in range(N_BRANCH):
            z_ref[:, C_ZG + D_MODEL * j:C_ZG + D_MODEL * (j + 1)] = _gate_logits(hb, w_ref, j)
            yield


def _in_kernel(x_ref, g_ref, w_ref, wa_ref, ba_ref, rc_ref, rs_ref, sc_ref, ss_ref, z_ref):
    tabs = (rc_ref[...], rs_ref[...], sc_ref[...], ss_ref[...])
    for _ in _in_stages(x_ref[...], g_ref, w_ref, wa_ref, ba_ref, tabs, z_ref, True):
        pass


def _in_proj(x, layer, g_mix, w_main, wa, ba, tabs, tm):
    n = x.shape[0]
    rc, rs, sc, ss = tabs
    nt = rc.shape[0] // tm
    tab = pl.BlockSpec((tm, LANES), lambda i: (i % nt, 0))
    return pl.pallas_call(
        _in_kernel,
        grid=(n // tm,),
        in_specs=[pl.BlockSpec((tm, D_MODEL), lambda i: (i, 0)),
                  _layer_spec((1, D_MODEL), layer),
                  _layer_spec((D_MODEL, W_MAIN), layer),
                  _layer_spec((LANES, 256), layer),
                  _layer_spec((1, 256), layer),
                  tab, tab, tab, tab],
        out_specs=pl.BlockSpec((tm, Z_W), lambda i: (i, 0)),
        out_shape=jax.ShapeDtypeStruct((n, Z_W), F32),
        compiler_params=_params(("arbitrary",)),
        name="in_proj",
    )(x, g_mix, w_main, wa, ba, rc, rs, sc, ss)


def _gla_constants():
    t = np.arange(TT)
    g = np.zeros((2 + N_LEVELS, TT, TT), np.float32)
    g[0] = (t[None, :] <= t[:, None])
    g[1] = (t[None, :] > t[:, None])
    m = np.zeros((1 + N_LEVELS, TT, TT), np.float32)
    m[0] = np.eye(TT)
    for lv in range(1, N_LEVELS + 1):
        bs, hf = 2 ** lv, 2 ** (lv - 1)
        bd = (t // bs) * bs + hf - 1
        upper = (t % bs) >= hf
        u = t[None, :]
        g[1 + lv] = np.where(upper[:, None], (u > bd[:, None]) & (u <= t[:, None]),
                             (u > t[:, None]) & (u <= bd[:, None]))
        same = (t[:, None] // bs) == (t[None, :] // bs)
        m[lv] = same & upper[:, None] & (~upper)[None, :]
    return g.reshape(-1, TT), np.tile(m, (1, 1, GLA_HEADS))


def _head_masks():
    hk = np.arange(GLA_HEADS * TT)[:, None] // TT == np.arange(GLA_HEADS * GLA_DK)[None, :] // GLA_DK
    hv = np.arange(GLA_HEADS * TT)[:, None] // TT == np.arange(GLA_W)[None, :] // GLA_DV
    hs = np.arange(GLA_HEADS * GLA_DK)[:, None] // GLA_DK == np.arange(GLA_W)[None, :] // GLA_DV
    return hk.astype(np.float32), hv.astype(np.float32), hs.astype(np.float32)


def _heads_blockdiag(x, mask):
    return jnp.concatenate([x] * GLA_HEADS, axis=0) * mask


def _col_vector(row):
    n = row.shape[1]
    eye = lax.broadcasted_iota(jnp.int32, (n, n), 0) == lax.broadcasted_iota(jnp.int32, (n, n), 1)
    return jnp.sum(jnp.where(eye, row, 0.0), axis=1, keepdims=True)


def _norm_gate(o, gn, gate):
    outs = [_rms(o[:, GLA_DV * h:GLA_DV * (h + 1)], gn[:, GLA_DV * h:GLA_DV * (h + 1)]) for h in range(GLA_HEADS)]
    return jnp.concatenate(outs, axis=1) * (gate * jax.nn.sigmoid(gate))


def _gla_tile(z_ref, gm_ref, mk_ref, hk_ref, hv_ref, hs_ref, gn_ref, st_ref):
    la = z_ref[:, C_LA:C_ZG]
    la_hi = la.astype(BF16)
    la_lo = (la - la_hi.astype(F32)).astype(BF16)
    gm = gm_ref[...]
    ex = jnp.exp(_dot(gm, la_hi) + _dot(gm, la_lo))
    q = z_ref[:, C_GQ:C_GK]
    k = z_ref[:, C_GK:C_GV]
    e_b = ex[0:TT]
    a_row = e_b[TT - 1:TT, :]
    qd = (q * e_b).astype(BF16)
    kd = (k * ex[TT:2 * TT]).astype(BF16)
    qb = q.astype(BF16)
    kb = k.astype(BF16)
    ql = [(q * ex[(1 + lv) * TT:(2 + lv) * TT]).astype(BF16) for lv in range(1, N_LEVELS + 1)]
    kl = [(k * ex[(1 + lv) * TT:(2 + lv) * TT]).astype(BF16) for lv in range(1, N_LEVELS + 1)]
    hk = hk_ref[...]
    yield
    a = mk_ref[0] * _dot_nt(qb, _heads_blockdiag(kb, hk))
    for lv in range(N_LEVELS):
        yield
        a = a + mk_ref[lv + 1] * _dot_nt(ql[lv], _heads_blockdiag(kl[lv], hk))
    yield
    v = z_ref[:, C_GV:C_GR].astype(BF16)
    s = st_ref[...]
    o = _dot(a.astype(BF16), _heads_blockdiag(v, hv_ref[...])) + _dot(qd, s.astype(BF16))
    yield
    st_ref[...] = _col_vector(a_row) * s + hs_ref[...] * _dot_tn(kd, v)
    yield
    return _norm_gate(o, gn_ref[...], z_ref[:, C_GR:C_RQ])


def _ret_gammas():
    return [1.0 - 2.0 ** (-5.0 - h) for h in range(RET_HEADS)]


def _ret_constants():
    t = np.arange(TT, dtype=np.float64)
    d = np.zeros((RET_HEADS, TT, TT), np.float64)
    qdec = np.zeros((TT, RET_HEADS * RET_DK), np.float64)
    kdec = np.zeros((TT, RET_HEADS * RET_DK), np.float64)
    adec = np.zeros((RET_HEADS * RET_DK, LANES), np.float64)
    for h, gam in enumerate(_ret_gammas()):
        diff = t[:, None] - t[None, :]
        d[h] = np.where(diff >= 0, gam ** np.maximum(diff, 0.0), 0.0)
        qdec[:, h * RET_DK:(h + 1) * RET_DK] = (gam ** (t + 1.0))[:, None]
        kdec[:, h * RET_DK:(h + 1) * RET_DK] = (gam ** (TT - 1.0 - t))[:, None]
        adec[h * RET_DK:(h + 1) * RET_DK, :] = gam ** TT
    d = np.concatenate(list(d), axis=1)
    adec = np.tile(adec, (1, RET_W // LANES))
    return d.astype(np.float32), qdec.astype(np.float32), kdec.astype(np.float32), adec.astype(np.float32)


def _ret_tile(z_ref, d_ref, qdec_ref, kdec_ref, adec_ref, hk_ref, hv_ref, hs_ref, gn_ref, st_ref):
    q = z_ref[:, C_RQ:C_RK]
    k = z_ref[:, C_RK:C_RV]
    qd = (q * qdec_ref[...]).astype(BF16)
    kd = (k * kdec_ref[...]).astype(BF16)
    a = d_ref[...] * _dot_nt(q.astype(BF16), _heads_blockdiag(k.astype(BF16), hk_ref[...]))
    yield
    v = z_ref[:, C_RV:C_RG].astype(BF16)
    s = st_ref[...]
    o = _dot(a.astype(BF16), _heads_blockdiag(v, hv_ref[...])) + _dot(qd, s.astype(BF16))
    yield
    st_ref[...] = adec_ref[...] * s + hs_ref[...] * _dot_tn(kd, v)
    yield
    return _norm_gate(o, gn_ref[...], z_ref[:, C_RG:C_SQ])


def _rec_sample_constants():
    c = np.arange(SB * GLA_DK)
    rep = np.arange(GLA_DK)[:, None] == (c[None, :] % GLA_DK)
    bd = (np.arange(DEC_SEQ * SB)[:, None] % SB) == (c[None, :] // GLA_DK)
    return rep.astype(np.float32), bd.astype(np.float32)


def _block_rows(ref, row0, c0, c1):
    return [ref[pl.ds(t * DEC_BATCH + row0, SB), c0:c1] for t in range(DEC_SEQ)]


def _rec_sample_stages(q, k, v, gate, la, s0_ref, s1_ref, gn_ref, rep_ref, bd_ref):
    b = []
    for t in range(DEC_SEQ):
        b.append(la[t] if t == 0 else b[-1] + la[t])
    qd = jnp.concatenate([q[t] * jnp.exp(b[t]) for t in range(DEC_SEQ)], axis=0).astype(BF16)
    kd = jnp.concatenate([k[t] * jnp.exp(b[-1] - b[t]) for t in range(DEC_SEQ)], axis=0).astype(BF16)
    vv = jnp.concatenate(v, axis=0).astype(BF16)
    a = jnp.exp(b[-1])
    a_hi = a.astype(BF16)
    a_r = a - a_hi.astype(F32)
    a_mid = a_r.astype(BF16)
    a_lo = (a_r - a_mid.astype(F32)).astype(BF16)
    rep = rep_ref[...]
    bd = bd_ref[...]
    ones = jnp.ones((SB, GLA_DV), BF16)

    def expand(x, mask):
        return (_dot(x, rep) * mask).astype(BF16)

    yield
    oi = []
    for h in range(GLA_HEADS):
        hs = slice(GLA_DK * h, GLA_DK * (h + 1))
        vs = slice(GLA_DV * h, GLA_DV * (h + 1))
        s0 = s0_ref[:, h].reshape(SB * GLA_DK, GLA_DV)
        oi.append(_dot(expand(qd[:, hs], bd), s0.astype(BF16)))
        a_col = (_dot_tn(expand(a_hi[:, hs], bd[0:SB]), ones) + _dot_tn(expand(a_mid[:, hs], bd[0:SB]), ones)
                 + _dot_tn(expand(a_lo[:, hs], bd[0:SB]), ones))
        s1 = a_col * s0 + _dot_tn(expand(kd[:, hs], bd), vv[:, vs])
        s1_ref[:, h] = s1.reshape(SB, GLA_DK, GLA_DV)
        yield
    oi = jnp.concatenate(oi, axis=1)

    gn = gn_ref[...]
    outs = []
    for t in range(DEC_SEQ):
        o = oi[t * SB:(t + 1) * SB, :]
        for s in range(t + 1):
            p = q[t] * k[s] * jnp.exp(b[t] - b[s])
            parts = []
            for h in range(GLA_HEADS):
                hs = slice(GLA_DK * h, GLA_DK * (h + 1))
                vs = slice(GLA_DV * h, GLA_DV * (h + 1))
                parts.append(jnp.sum(p[:, hs], axis=1, keepdims=True) * v[s][:, vs])
            o = o + jnp.concatenate(parts, axis=1)
        outs.append(_norm_gate(o, gn, gate[t]))
        yield
    return jnp.concatenate(outs, axis=0)


def _swa_tile(z_ref, sink_ref, kp_ref, vp_ref, first_col):
    kc = z_ref[:, C_SK:C_SV].astype(BF16)
    vc = z_ref[:, C_SV:C_LA].astype(BF16)
    kband = jnp.concatenate([kp_ref[...], kc], axis=0)
    vband = jnp.concatenate([vp_ref[...], vc], axis=0)
    rows = SWA_GROUP * WINDOW
    r = lax.broadcasted_iota(jnp.int32, (rows, 2 * WINDOW), 0) & (WINDOW - 1)
    c = lax.broadcasted_iota(jnp.int32, (rows, 2 * WINDOW), 1)
    valid = (c > r) & (c <= r + WINDOW) & (c >= first_col)
    groups = range(SWA_KV_HEADS)
    dsl = [slice(SWA_HEAD_DIM * g, SWA_HEAD_DIM * (g + 1)) for g in groups]
    s, sk = [], []
    for g in groups:
        heads = range(SWA_GROUP * g, SWA_GROUP * (g + 1))
        q = jnp.concatenate(
            [z_ref[:, C_SQ + SWA_HEAD_DIM * hq:C_SQ + SWA_HEAD_DIM * (hq + 1)] for hq in heads], axis=0)
        sk.append(jnp.concatenate([jnp.full((WINDOW, 1), sink_ref[hq], F32) for hq in heads], axis=0))
        s.append(_dot_nt(q.astype(BF16), kband[:, dsl[g]]) * (SWA_HEAD_DIM ** -0.5))
    yield
    m, p = [], []
    for g in groups:
        sg = jnp.where(valid, s[g], NEG_INF)
        m.append(jnp.maximum(jnp.max(sg, axis=-1, keepdims=True), sk[g]))
        p.append(jnp.exp(sg - m[g]))
    yield
    outs = []
    for g in groups:
        den = jnp.sum(p[g], axis=-1, keepdims=True) + jnp.exp(sk[g] - m[g])
        o = _dot(p[g].astype(BF16), vband[:, dsl[g]]) / den
        outs += [o[WINDOW * j:WINDOW * (j + 1)] for j in range(SWA_GROUP)]
    yield
    return jnp.concatenate(outs, axis=1), kc, vc


def _interleave(stage_fns):
    results = [None] * len(stage_fns)
    live = list(enumerate(stage_fns))
    while live:
        still = []
        for idx, gen in live:
            try:
                next(gen)
                still.append((idx, gen))
            except StopIteration as stop:
                results[idx] = stop.value
        live = still
    return results


def _merge_stages(pend_ref, x_ref, g_ref, w_ref, wb_ref, wo_ref, y_ref):
    nb = x_ref.shape[0]
    hb = _rms(x_ref[...].reshape(nb * TT, D_MODEL), g_ref[...]).astype(BF16)
    gates = []
    for j in range(N_BRANCH):
        gates.append(jax.nn.sigmoid(_gate_logits(hb, w_ref, j)))
        yield
    mixed = None
    for j in range(N_BRANCH):
        term = gates[j] * _dot(pend_ref[:, GLA_W * j:GLA_W * (j + 1)], wb_ref[GLA_W * j:GLA_W * (j + 1), :])
        mixed = term if mixed is None else mixed + term
        yield
    out = _dot(mixed.astype(BF16), wo_ref[...])
    for b in range(x_ref.shape[0]):
        y_ref[b] = x_ref[b] + out[TT * b:TT * (b + 1)]
    yield


def _mix_kernel(sink_ref, xp_ref, xn_ref, tp0, tp1, tp2, tp3, tn0, tn1, tn2, tn3, g_ref, w_ref, wa_ref, ba_ref,
                gm_ref, mk_ref, d_ref, qdec_ref, kdec_ref, adec_ref, hk_ref, hv_ref, hs_ref, gna_ref, gnb_ref,
                wb_ref, wo_ref, y_ref, sta_ref, stb_ref, kv_ref, z_scr, pend_scr, sa_scr, sb_scr, kp_scr, vp_scr,
                *, nt):
    i = pl.program_id(0)
    nb = xp_ref.shape[0]

    def rows_of(ref3):
        return ref3[...].reshape(nb * TT, ref3.shape[2])

    def tiled(tab_refs):
        return tuple(jnp.concatenate([r[...]] * nb, axis=0) for r in tab_refs)

    @pl.when(i == 0)
    def _():
        sa_scr[...] = jnp.zeros_like(sa_scr)
        sb_scr[...] = jnp.zeros_like(sb_scr)
        kp_scr[...] = jnp.zeros_like(kp_scr)
        vp_scr[...] = jnp.zeros_like(vp_scr)
        pend_scr[...] = jnp.zeros_like(pend_scr)
        for _ in _in_stages(rows_of(xp_ref), g_ref, w_ref, wa_ref, ba_ref, tiled((tp0, tp1, tp2, tp3)),
                            z_scr.at[0], False):
            pass

    slot = i % 2
    zc = z_scr.at[slot]
    first_col = jnp.where(i > 0, 0, WINDOW)
    stages = []
    for b in range(nb):
        zb = zc.at[pl.ds(TT * b, TT)]
        stages += [_gla_tile(zb, gm_ref, mk_ref, hk_ref, hv_ref, hs_ref, gna_ref, sa_scr.at[b]),
                   _swa_tile(zb, sink_ref, kp_scr.at[b], vp_scr.at[b], first_col),
                   _ret_tile(zb, d_ref, qdec_ref, kdec_ref, adec_ref, hk_ref, hv_ref, hs_ref, gnb_ref, sb_scr.at[b])]
    stages += [_merge_stages(pend_scr, xp_ref, g_ref, w_ref, wb_ref, wo_ref, y_ref),
               _in_stages(rows_of(xn_ref), g_ref, w_ref, wa_ref, ba_ref, tiled((tn0, tn1, tn2, tn3)),
                          z_scr.at[1 - slot], False)]
    branch = _interleave(stages)[:3 * nb]
    swa = [branch[3 * b + 1] for b in range(nb)]
    outs = [[branch[3 * b] for b in range(nb)], [branch[3 * b + 2] for b in range(nb)], [s[0] for s in swa]]
    pend = [jnp.concatenate(o, axis=0).astype(BF16) for o in outs]

    @pl.when(i < nt)
    def _():
        for j in range(N_BRANCH):
            pend_scr[:, GLA_W * j:GLA_W * (j + 1)] = pend[j]
        for b in range(nb):
            kp_scr[b] = swa[b][1]
            vp_scr[b] = swa[b][2]
        for b in range(nb):
            for h in range(GLA_HEADS):
                sta_ref[b, h] = sa_scr[b, GLA_DK * h:GLA_DK * (h + 1), GLA_DV * h:GLA_DV * (h + 1)]
                stb_ref[b, h] = sb_scr[b, RET_DK * h:RET_DK * (h + 1), RET_DV * h:RET_DV * (h + 1)]
            kv_ref[b] = zc[pl.ds(TT * b, TT), C_SK:C_LA]


MIX_VMEM_LIMIT = 60 * 1024 * 1024


def _mix_prompt(x, sink, layer, g_mix, w_main, wa, ba, tabs, consts, gna, gnb, wb, wo):
    nb, t, _ = x.shape
    nt = t // TT
    st_spec = pl.BlockSpec((nb, GLA_HEADS, GLA_DK, GLA_DV), lambda i: (0, 0, 0, 0))
    st_shape = jax.ShapeDtypeStruct((nb, GLA_HEADS, GLA_DK, GLA_DV), F32)
    nxt = lambda i: jnp.minimum(i + 1, nt - 1)
    prv = lambda i: jnp.maximum(i - 1, 0)
    tab_prv = pl.BlockSpec((TT, LANES), lambda i: (prv(i), 0))
    tab_nxt = pl.BlockSpec((TT, LANES), lambda i: (nxt(i), 0))
    return pl.pallas_call(
        functools.partial(_mix_kernel, nt=nt),
        grid=(nt + 1,),
        in_specs=[pl.BlockSpec(memory_space=pltpu.SMEM),
                  pl.BlockSpec((nb, TT, D_MODEL), lambda i: (0, prv(i), 0)),
                  pl.BlockSpec((nb, TT, D_MODEL), lambda i: (0, nxt(i), 0))]
        + [tab_prv] * 4 + [tab_nxt] * 4
        + [_layer_spec((1, D_MODEL), layer), _layer_spec((D_MODEL, W_MAIN), layer),
           _layer_spec((LANES, 256), layer), _layer_spec((1, 256), layer)]
        + [_const_spec(c.shape) for c in consts]
        + [_const_spec((1, GLA_W)), _const_spec((1, RET_W)),
           _layer_spec((GLA_W + RET_W + SWA_W, D_MODEL), layer),
           _layer_spec((D_MODEL, D_MODEL), layer)],
        out_specs=[pl.BlockSpec((nb, TT, D_MODEL), lambda i: (0, prv(i), 0)), st_spec, st_spec,
                   pl.BlockSpec((nb, WINDOW, 2 * LANES), lambda i: (0, 0, 0))],
        out_shape=[jax.ShapeDtypeStruct((nb, t, D_MODEL), F32), st_shape, st_shape,
                   jax.ShapeDtypeStruct((nb, WINDOW, 2 * LANES), F32)],
        scratch_shapes=[pltpu.VMEM((2, nb * TT, C_ZG), F32),
                        pltpu.VMEM((nb * TT, GLA_W + RET_W + SWA_W), BF16),
                        pltpu.VMEM((nb, GLA_HEADS * GLA_DK, GLA_W), F32),
                        pltpu.VMEM((nb, RET_HEADS * RET_DK, RET_W), F32),
                        pltpu.VMEM((nb, WINDOW, 128), BF16), pltpu.VMEM((nb, WINDOW, 128), BF16)],
        compiler_params=pltpu.CompilerParams(dimension_semantics=("arbitrary",), vmem_limit_bytes=MIX_VMEM_LIMIT),
        name="mix_prompt",
    )(sink, x, x, *tabs, *tabs, g_mix, w_main, wa, ba, *consts, gna, gnb, wb, wo)


def _swa_sample_masks():
    r = np.arange(DEC_SEQ * SB)
    rt, rb = r // SB, r % SB
    c = np.arange(SB * WINDOW)
    cb, cj = c // WINDOW, c % WINDOW
    m_cache = (rb[:, None] == cb[None, :]) & (cj[None, :] > rt[:, None])
    m_new = (rb[:, None] == rb[None, :]) & (rt[None, :] <= rt[:, None])
    return m_cache.astype(np.float32), m_new.astype(np.float32)


def _swa_sample_stages(sink_ref, q, kn, vn, kc_ref, vc_ref, mc_ref, mn_ref):
    q = q.astype(BF16)
    kn = kn.astype(BF16)
    vn = vn.astype(BF16)
    kct = [jnp.concatenate([kc_ref[b, g] for b in range(SB)], axis=1).astype(BF16) for g in range(SWA_KV_HEADS)]
    vct = [jnp.concatenate([vc_ref[b, g] for b in range(SB)], axis=1).astype(BF16) for g in range(SWA_KV_HEADS)]
    ok_c = mc_ref[...] > 0.0
    ok_n = mn_ref[...] > 0.0
    outs = []
    for hq in range(SWA_Q_HEADS):
        g = hq // SWA_GROUP
        ds_ = slice(SWA_HEAD_DIM * g, SWA_HEAD_DIM * (g + 1))
        qh = q[:, SWA_HEAD_DIM * hq:SWA_HEAD_DIM * (hq + 1)]
        s1 = jnp.where(ok_c, _dot(qh, kct[g]) * (SWA_HEAD_DIM ** -0.5), NEG_INF)
        s2 = jnp.where(ok_n, _dot_nt(qh, kn[:, ds_]) * (SWA_HEAD_DIM ** -0.5), NEG_INF)
        sk = sink_ref[hq]
        m = jnp.maximum(jnp.maximum(jnp.max(s1, axis=-1, keepdims=True),
                                    jnp.max(s2, axis=-1, keepdims=True)), sk)
        p1 = jnp.exp(s1 - m)
        p2 = jnp.exp(s2 - m)
        den = jnp.sum(p1, axis=-1, keepdims=True) + jnp.sum(p2, axis=-1, keepdims=True) + jnp.exp(sk - m)
        outs.append((_dot_nt(p1.astype(BF16), vct[g]) + _dot(p2.astype(BF16), vn[:, ds_])) / den)
        if hq % 2 == 1:
            yield
    return jnp.concatenate(outs, axis=1)


def _mix_sample_kernel(sink_ref, x_ref, z_ref, lar_ref, sa0_ref, sb0_ref, kc_ref, vc_ref, gna_ref, gnb_ref,
                       rep_ref, bd_ref, mc_ref, mn_ref, wb_ref, wo_ref, sa_all_ref, sb_all_ref,
                       y_ref, sa1_ref, sb1_ref):
    del sa_all_ref, sb_all_ref
    row0 = pl.multiple_of(pl.program_id(0) * SB, SB)

    def rows(c0, c1):
        return _block_rows(z_ref, row0, c0, c1)

    def stacked(c0, c1):
        return jnp.concatenate(rows(c0, c1), axis=0)

    la_ret = [lar_ref[...]] * DEC_SEQ
    oa, ob, oc = _interleave([
        _rec_sample_stages(rows(C_GQ, C_GK), rows(C_GK, C_GV), rows(C_GV, C_GR), rows(C_GR, C_RQ),
                           rows(C_LA, C_ZG), sa0_ref, sa1_ref, gna_ref, rep_ref, bd_ref),
        _rec_sample_stages(rows(C_RQ, C_RK), rows(C_RK, C_RV), rows(C_RV, C_RG), rows(C_RG, C_SQ),
                           la_ret, sb0_ref, sb1_ref, gnb_ref, rep_ref, bd_ref),
        _swa_sample_stages(sink_ref, stacked(C_SQ, C_SK), stacked(C_SK, C_SV), stacked(C_SV, C_LA),
                           kc_ref, vc_ref, mc_ref, mn_ref)])
    mixed = sum(jax.nn.sigmoid(stacked(C_ZG + D_MODEL * j, C_ZG + D_MODEL * (j + 1)))
                * _dot(o.astype(BF16), wb_ref[512 * j:512 * (j + 1), :])
                for j, o in enumerate((oa, ob, oc)))
    out = _dot(mixed.astype(BF16), wo_ref[...])
    for t in range(DEC_SEQ):
        r = pl.ds(t * DEC_BATCH + row0, SB)
        y_ref[r, :] = x_ref[r, :] + out[t * SB:(t + 1) * SB, :]


def _mix_sample(x, z, sink, la_ret, state_gla, state_ret, new_gla, new_ret, cache_k, cache_v, layer, gna, gnb,
                consts, wb, wo):
    n = x.shape[0]
    rep, bd, mc, mn = consts
    whole = lambda w: pl.BlockSpec((n, w), lambda j: (0, 0), pipeline_mode=pl.Buffered(1))
    st_blk = pl.BlockSpec((None, SB, GLA_HEADS, GLA_DK, GLA_DV), lambda j: (layer, j, 0, 0, 0))
    cache = pl.BlockSpec((None, SB, SWA_KV_HEADS, SWA_HEAD_DIM, WINDOW), lambda j: (layer, j, 0, 0, 0))
    st_shape = jax.ShapeDtypeStruct(state_gla.shape, F32)
    untouched = pl.BlockSpec(memory_space=pl.ANY)
    operands = (sink, x, z, la_ret, state_gla, state_ret, cache_k, cache_v, gna, gnb, rep, bd, mc, mn, wb, wo,
                new_gla, new_ret)
    return pl.pallas_call(
        _mix_sample_kernel,
        grid=(DEC_BATCH // SB,),
        in_specs=[pl.BlockSpec(memory_space=pltpu.SMEM), whole(D_MODEL), whole(Z_W),
                  _const_spec(la_ret.shape), st_blk, st_blk, cache, cache,
                  _const_spec((1, GLA_W)), _const_spec((1, RET_W)),
                  _const_spec(rep.shape), _const_spec(bd.shape), _const_spec(mc.shape), _const_spec(mn.shape),
                  _layer_spec((GLA_W + RET_W + SWA_W, D_MODEL), layer),
                  _layer_spec((D_MODEL, D_MODEL), layer), untouched, untouched],
        out_specs=[pl.BlockSpec((n, D_MODEL), lambda j: (0, 0)), st_blk, st_blk],
        out_shape=[jax.ShapeDtypeStruct((n, D_MODEL), F32), st_shape, st_shape],
        input_output_aliases={len(operands) - 2: 1, len(operands) - 1: 2},
        compiler_params=_params(("arbitrary",)),
        name="mix_sample",
    )(*operands)


def _window_place_matrix():
    r = np.arange(DEC_SEQ * SB)
    c = np.arange(SB * WINDOW)
    return ((r[:, None] % SB == c[None, :] // WINDOW)
            & (c[None, :] % WINDOW == WINDOW - DEC_SEQ + r[:, None] // SB)).astype(np.float32)


def _window_kernel(kv_ref, kc_ref, vc_ref, place_ref, ko_ref, vo_ref):
    row0 = pl.multiple_of(pl.program_id(1) * SB, SB)
    new = jnp.concatenate(_block_rows(kv_ref, row0, 0, 2 * LANES), axis=0)
    hi = new.astype(BF16)
    rest = new - hi.astype(F32)
    mid = rest.astype(BF16)
    lo = (rest - mid.astype(F32)).astype(BF16)
    place = place_ref[...]
    keep = lax.broadcasted_iota(jnp.int32, (SWA_HEAD_DIM, WINDOW), 1) < WINDOW - DEC_SEQ
    for c_ref, o_ref, c0 in ((kc_ref, ko_ref, 0), (vc_ref, vo_ref, LANES)):
        for g in range(SWA_KV_HEADS):
            cols = slice(c0 + SWA_HEAD_DIM * g, c0 + SWA_HEAD_DIM * (g + 1))
            placed = (_dot_tn(hi[:, cols], place) + _dot_tn(mid[:, cols], place)
                      + _dot_tn(lo[:, cols], place))
            for b in range(SB):
                shifted = pltpu.roll(c_ref[b, g], WINDOW - DEC_SEQ, 1)
                o_ref[b, g] = jnp.where(keep, shifted, placed[:, WINDOW * b:WINDOW * (b + 1)])


def _window_update(kv_new, cache_kt, cache_vt, place):
    blk = pl.BlockSpec((None, SB, SWA_KV_HEADS, SWA_HEAD_DIM, WINDOW), lambda l, j: (l, j, 0, 0, 0))
    return pl.pallas_call(
        _window_kernel,
        grid=(DEPTH, DEC_BATCH // SB),
        in_specs=[pl.BlockSpec((None,) + kv_new.shape[1:], lambda l, j: (l, 0, 0)), blk, blk,
                  _const_spec(place.shape)],
        out_specs=[blk, blk],
        out_shape=[jax.ShapeDtypeStruct(cache_kt.shape, F32), jax.ShapeDtypeStruct(cache_vt.shape, F32)],
        compiler_params=_params(("arbitrary", "arbitrary")),
        name="window_update",
    )(kv_new, cache_kt, cache_vt, place)


MXU_N = 256
FF_CHUNKS = ((0, 6 * MXU_N), (6 * MXU_N, D_FF))


def _ffn_tail(x1, pe_ref, gp_ref, wpg_ref, wpp_ref, gf_ref, y_ref, final):
    hp = _rms(x1, gp_ref[...]).astype(BF16)
    x2 = x1 + jax.nn.sigmoid(_dot(hp, wpg_ref[...])) * _dot(pe_ref[...].astype(BF16), wpp_ref[...])
    y_ref[...] = _rms(x2, gf_ref[...]) if final else x2


def _ffn_stages(x, carry, gn_ref, wi_ref, cw_ref, cb_ref, wd_ref, keep):
    tm = x.shape[0]
    hb = _rms(x, gn_ref[...]).astype(BF16)
    row = lax.broadcasted_iota(jnp.int32, (tm, 1), 0)
    acc = x
    for c0, c1 in FF_CHUNKS:
        cs = slice(c0, c1)
        a = _dot(hb, wi_ref[:, c0:c1])
        bb = _dot(hb, wi_ref[:, D_FF + c0:D_FF + c1])
        yield
        p0 = carry[6:7, cs]
        p1 = carry[7:8, cs]
        a1 = jnp.where(row == 0, p1, pltpu.roll(a, 1, 0))
        a2 = jnp.where(row == 0, p0, jnp.where(row == 1, p1, pltpu.roll(a, 2, 0)))
        conv = cb_ref[:, cs] + cw_ref[0:1, cs] * a2 + cw_ref[1:2, cs] * a1 + cw_ref[2:3, cs] * a
        act = (jax.nn.gelu(conv) * bb).astype(BF16)
        keep.append(a[tm - 8:tm, :])
        yield
        acc = acc + _dot(act, wd_ref[c0:c1, :])
        yield
    return acc


def _ffn_prompt_kernel(x_ref, pe_ref, gn_ref, wi_ref, cw_ref, cb_ref, wd_ref, gp_ref, wpg_ref, wpp_ref,
                       gf_ref, y_ref, tail_ref, carry_scr, *, final, nt):
    i = pl.program_id(1)

    @pl.when(i == 0)
    def _():
        carry_scr[...] = jnp.zeros_like(carry_scr)

    keep = []
    (acc,) = _interleave([_ffn_stages(x_ref[...], carry_scr[...], gn_ref, wi_ref, cw_ref, cb_ref, wd_ref, keep)])
    _ffn_tail(acc, pe_ref, gp_ref, wpg_ref, wpp_ref, gf_ref, y_ref, final)
    last_rows = jnp.concatenate(keep, axis=1)
    tail_ref[...] = last_rows

    @pl.when(i < nt - 1)
    def _():
        carry_scr[...] = last_rows


def _ffn_sample_kernel(x_ref, pe_ref, st_ref, gn_ref, wi_ref, cw_ref, cb_ref, wd_ref, gp_ref, wpg_ref,
                       wpp_ref, gf_ref, y_ref, tail_ref, *, final):
    ns = DEC_BATCH
    x = x_ref[...]
    hb = _rms(x, gn_ref[...]).astype(BF16)
    acc = x
    for c0, c1 in FF_CHUNKS:
        cs = slice(c0, c1)
        a = _dot(hb, wi_ref[:, c0:c1])
        bb = _dot(hb, wi_ref[:, D_FF + c0:D_FF + c1])
        st0 = st_ref[0, :, cs]
        st1 = st_ref[1, :, cs]
        a1 = jnp.concatenate([st1, a[0:3 * ns]], axis=0)
        a2 = jnp.concatenate([st0, st1, a[0:2 * ns]], axis=0)
        conv = cb_ref[:, cs] + cw_ref[0:1, cs] * a2 + cw_ref[1:2, cs] * a1 + cw_ref[2:3, cs] * a
        act = (jax.nn.gelu(conv) * bb).astype(BF16)
        acc = acc + _dot(act, wd_ref[c0:c1, :])
        tail_ref[:, cs] = a[2 * ns:4 * ns, :]
    _ffn_tail(acc, pe_ref, gp_ref, wpg_ref, wpp_ref, gf_ref, y_ref, final)


def _ffn_weight_specs(layer):
    return [_layer_spec((1, D_MODEL), layer),
            _layer_spec((D_MODEL, 2 * D_FF), layer),
            _layer_spec((CONV_W, D_FF), layer),
            _layer_spec((1, D_FF), layer),
            _layer_spec((D_FF, D_MODEL), layer),
            _layer_spec((1, D_MODEL), layer),
            _layer_spec((D_MODEL, D_MODEL), layer),
            _layer_spec((D_PLE, D_MODEL), layer),
            _const_spec((1, D_MODEL))]


def _ffn_prompt(x, pe, layer, weights, nb, t, tm, final):
    nt = t // tm
    tok = lambda w: pl.BlockSpec((tm, w), lambda b, i: (b * nt + i, 0))
    return pl.pallas_call(
        functools.partial(_ffn_prompt_kernel, final=final, nt=nt),
        grid=(nb, nt),
        in_specs=[tok(D_MODEL),
                  pl.BlockSpec((None, tm, D_PLE), lambda b, i: (layer, b * nt + i, 0))]
        + _ffn_weight_specs(layer),
        out_specs=[tok(D_MODEL), pl.BlockSpec((8, D_FF), lambda b, i: (b, 0))],
        out_shape=[jax.ShapeDtypeStruct((nb * t, D_MODEL), F32),
                   jax.ShapeDtypeStruct((nb * 8, D_FF), F32)],
        scratch_shapes=[pltpu.VMEM((8, D_FF), F32)],
        compiler_params=pltpu.CompilerParams(dimension_semantics=("arbitrary", "arbitrary"),
                                             vmem_limit_bytes=MIX_VMEM_LIMIT),
        name="ffn_prompt",
    )(x, pe, *weights)


def _ffn_sample(x, pe, st, layer, weights, final):
    n = x.shape[0]
    return pl.pallas_call(
        functools.partial(_ffn_sample_kernel, final=final),
        grid=(1,),
        in_specs=[pl.BlockSpec((n, D_MODEL), lambda i: (0, 0)),
                  pl.BlockSpec((None, n, D_PLE), lambda i: (layer, 0, 0)),
                  pl.BlockSpec((None, CONV_W - 1, DEC_BATCH, D_FF), lambda i: (layer, 0, 0, 0))]
        + _ffn_weight_specs(layer),
        out_specs=[pl.BlockSpec((n, D_MODEL), lambda i: (0, 0)),
                   pl.BlockSpec((n // 2, D_FF), lambda i: (0, 0))],
        out_shape=[jax.ShapeDtypeStruct((n, D_MODEL), F32),
                   jax.ShapeDtypeStruct((n // 2, D_FF), F32)],
        compiler_params=_params(("arbitrary",)),
        name="ffn_sample",
    )(x, pe, st, *weights)


def _rope_tables(pos, inv_freq):
    half = inv_freq.shape[0]
    ang = pos.astype(np.float64)[:, None] * inv_freq[None, :]
    c, s = np.cos(ang), np.sin(ang)
    rest = SWA_HEAD_DIM - 2 * half
    cos64 = np.concatenate([c, c, np.ones((pos.shape[0], rest))], axis=1)
    sin64 = np.concatenate([-s, s, np.zeros((pos.shape[0], rest))], axis=1)
    return (jnp.asarray(np.concatenate([cos64, cos64], axis=1), F32),
            jnp.asarray(np.concatenate([sin64, sin64], axis=1), F32))


def _pack_w_in(w_in):
    ga0 = sum(IN_SIZES[:4])
    ga = jnp.pad(w_in[:, :, ga0:ga0 + GLA_RANK], ((0, 0), (0, 0), (0, LANES - GLA_RANK)))
    return jnp.concatenate([w_in[:, :, :ga0], w_in[:, :, ga0 + GLA_RANK:], ga], axis=2).astype(BF16)


def kernel(x_prompt, x_sample, state_gla, state_ret, cache_swa_k, cache_swa_v, state_conv, p_prompt,
           p_sample, norm_mix, w_in, w_gla_a, b_gla_a, gla_norm, ret_norm, swa_sink, w_branch, w_out,
           norm_ffn, w_ffn_in, conv_w, conv_b, w_ffn_out, norm_ple, w_ple_gate, w_ple_proj, norm_final):
    nb, t, _ = x_prompt.shape
    ns, ts, _ = x_sample.shape
    n_s = ns * ts

    w_main = _pack_w_in(w_in)
    wa = jnp.pad(w_gla_a, ((0, 0), (0, LANES - GLA_RANK), (0, 0))).astype(BF16)
    ba = b_gla_a[:, None, :]
    wb = w_branch.astype(BF16)
    wo = w_out.astype(BF16)
    ffn_weights = (norm_ffn[:, None, :], w_ffn_in.astype(BF16), conv_w, conv_b[:, None, :],
                   w_ffn_out.astype(BF16), norm_ple[:, None, :], w_ple_gate.astype(BF16),
                   w_ple_proj.astype(BF16), norm_final[None, :])
    g_mix = norm_mix[:, None, :]
    gn_gla = jnp.tile(gla_norm, (1, GLA_HEADS))[:, None, :]
    gn_ret = ret_norm.reshape(DEPTH, 1, RET_W)

    ret_freq = 1.0 / (RET_THETA ** np.linspace(0.0, 1.0, RET_DK // 2))
    swa_freq = 1.0 / (ROPE_THETA ** (np.arange(0, ROPE_DIM, 2, dtype=np.float64) / ROPE_DIM))
    pos_p = np.arange(t)
    pos_s = PAST_LEN + np.arange(n_s) // ns
    tabs_p = _rope_tables(pos_p, ret_freq) + _rope_tables(pos_p, swa_freq)
    tabs_s = _rope_tables(pos_s, ret_freq) + _rope_tables(pos_s, swa_freq)

    gm_np, mk_np = _gla_constants()
    d_np, qdec_np, kdec_np, adec_np = _ret_constants()
    hk_np, hv_np, hs_np = _head_masks()
    mix_consts = (jnp.asarray(gm_np, BF16), jnp.asarray(mk_np), jnp.asarray(d_np), jnp.asarray(qdec_np),
                  jnp.asarray(kdec_np), jnp.asarray(adec_np), jnp.asarray(hk_np, BF16), jnp.asarray(hv_np, BF16),
                  jnp.asarray(hs_np))
    log_gamma = np.log1p(-np.exp2(-5.0 - np.arange(RET_HEADS, dtype=np.float64)))
    la_ret = jnp.asarray(np.broadcast_to(np.repeat(log_gamma, RET_DK)[None, :], (SB, RET_HEADS * RET_DK)), F32)
    mc_np, mn_np = _swa_sample_masks()
    rep_np, bd_np = _rec_sample_constants()
    sample_consts = (jnp.asarray(rep_np, BF16), jnp.asarray(bd_np), jnp.asarray(mc_np), jnp.asarray(mn_np))

    xp = x_prompt.reshape(nb * t, D_MODEL)
    xs = x_sample.transpose(1, 0, 2).reshape(n_s, D_MODEL)
    pe_p = p_prompt.reshape(DEPTH, nb * t, D_PLE)
    pe_s = p_sample.transpose(0, 2, 1, 3).reshape(DEPTH, n_s, D_PLE)
    conv_st = state_conv.transpose(0, 2, 1, 3)
    ck = cache_swa_k.transpose(0, 1, 3, 4, 2)
    cv = cache_swa_v.transpose(0, 1, 3, 4, 2)

    outs = {k_: [] for k_ in ("gla_p", "ret_p", "kv_p", "kv_s", "conv_p", "conv_s")}
    gla_s = jnp.zeros(state_gla.shape, F32)
    ret_s = jnp.zeros(state_ret.shape, F32)
    for l in range(DEPTH):
        final = l == DEPTH - 1
        sink = swa_sink[l]

        xp3, st_a, st_b, kv_tail = _mix_prompt(xp.reshape(nb, t, D_MODEL), sink, l, g_mix, w_main, wa, ba, tabs_p,
                                               mix_consts, gn_gla[l], gn_ret[l], wb, wo)
        xp, tail = _ffn_prompt(xp3.reshape(nb * t, D_MODEL), pe_p, l, ffn_weights, nb, t, 512, final)
        outs["gla_p"].append(st_a)
        outs["ret_p"].append(st_b)
        outs["kv_p"].append(kv_tail)
        outs["conv_p"].append(tail.reshape(nb, 8, D_FF)[:, 8 - (CONV_W - 1):])

        zs = _in_proj(xs, l, g_mix, w_main, wa, ba, tabs_s, 256)
        xs, gla_s, ret_s = _mix_sample(xs, zs, sink, la_ret, state_gla, state_ret, gla_s, ret_s, ck, cv, l,
                                       gn_gla[l], gn_ret[l], sample_consts, wb, wo)
        xs, tail_s = _ffn_sample(xs, pe_s, conv_st, l, ffn_weights, final)
        outs["kv_s"].append(zs[:, C_SK:C_LA])
        outs["conv_s"].append(tail_s.reshape(CONV_W - 1, ns, D_FF).transpose(1, 0, 2))

    y_prompt = xp.reshape(nb, t, D_MODEL)
    y_sample = xs.reshape(ts, ns, D_MODEL).transpose(1, 0, 2)
    st = {k_: jnp.stack(v_) for k_, v_ in outs.items()}

    def kv_heads(a):
        return a.reshape(a.shape[:-1] + (SWA_KV_HEADS, SWA_HEAD_DIM))

    k_p = kv_heads(st["kv_p"][..., :LANES])
    v_p = kv_heads(st["kv_p"][..., LANES:])
    k_st, v_st = _window_update(st["kv_s"], ck, cv, jnp.asarray(_window_place_matrix(), BF16))
    k_s = k_st.transpose(0, 1, 4, 2, 3)
    v_s = v_st.transpose(0, 1, 4, 2, 3)
    return (y_prompt, y_sample, st["gla_p"], gla_s, st["ret_p"], ret_s,
            k_p, k_s, v_p, v_s, st["conv_p"], st["conv_s"])
```
